```python
import jax, jax.numpy as jnp
from jax import lax
import numpy as np

D_MODEL = 1024
BATCH = 16
SEQ = 2048
DEPTH = 1

HEAD_DIM = 64
ROPE_DIM = HEAD_DIM // 4
ROPE_THETA = 500000.0
NORM_EPS = 1e-6
ATTN_SCALE = HEAD_DIM ** -0.5

NSA_HEADS = 8
NSA_GROUPS = 2
NSA_REP = NSA_HEADS // NSA_GROUPS
CMP_LEN = 32
CMP_STRIDE = 16
CMP_HIDDEN = 256
SEL_BLOCK = 64
SEL_TOPN = 8
WINDOW = 512
WIN_QBLOCK = 128
NSA_QCHUNK = 64

MOBA_HEADS = 8
MOBA_BLOCK = 256
MOBA_TOPK = 3
MOBA_QCHUNK = 16

PAD_MULT = 256
D_FF = ((-(-8 * D_MODEL // 3)) + 255) // 256 * 256
PLE_DIM = 256

IN_SPLITS = ((NSA_HEADS * HEAD_DIM,) + (NSA_GROUPS * HEAD_DIM,) * 6 + (3 * NSA_HEADS,)
             + (MOBA_HEADS * HEAD_DIM,) * 3 + (D_MODEL, D_MODEL))
IN_COLS = sum(IN_SPLITS)
IN_CUTS = tuple(int(c) for c in np.cumsum(IN_SPLITS)[:-1])

kernel_name = 'hybrid_nsa_moba_swiglu_ple'


def rms_norm(x, g):
    xf = x.astype(jnp.float32)
    y = xf * lax.rsqrt(jnp.mean(xf * xf, axis=-1, keepdims=True) + NORM_EPS)
    return (y * g.astype(jnp.float32)).astype(x.dtype)


def partial_rope(t, pos):
    half = ROPE_DIM // 2
    inv_freq = ROPE_THETA ** (-jnp.arange(half, dtype=jnp.float32) / half)
    ang = pos.astype(jnp.float32)[..., None] * inv_freq
    cos, sin = jnp.cos(ang), jnp.sin(ang)
    tr = t[..., :ROPE_DIM].astype(jnp.float32)
    t1, t2 = tr[..., :half], tr[..., half:]
    rot = jnp.concatenate([t1 * cos - t2 * sin, t2 * cos + t1 * sin], axis=-1)
    return jnp.concatenate([rot.astype(t.dtype), t[..., ROPE_DIM:]], axis=-1)


def masked_softmax(scores, mask):
    s = jnp.where(mask, scores.astype(jnp.float32), -jnp.inf)
    m = jnp.max(s, axis=-1, keepdims=True)
    m = jnp.where(jnp.isfinite(m), m, 0.0)
    e = jnp.where(mask, jnp.exp(s - m), 0.0)
    return e / jnp.maximum(jnp.sum(e, axis=-1, keepdims=True), 1e-30)


def to_heads(t, n):
    b, s, _ = t.shape
    return t.reshape(b, s, n, HEAD_DIM).transpose(0, 2, 1, 3)


def from_heads(t):
    b, n, s, d = t.shape
    return t.transpose(0, 2, 1, 3).reshape(b, s, n * d)


def compress_blocks(t, pe, w1, w2):
    b, g, sp, d = t.shape
    n_sub, per = sp // CMP_STRIDE, CMP_LEN // CMP_STRIDE
    sub = t.reshape(b, g, n_sub, CMP_STRIDE, d)
    nc = n_sub - per + 1
    blocks = jnp.concatenate([sub[:, :, i:i + nc] for i in range(per)], axis=3)
    flat = (blocks + pe).reshape(b, g, nc, CMP_LEN * d)
    return jax.nn.silu(flat @ w1) @ w2


def nsa_mixer(q_in, kc_in, vc_in, ks_in, vs_in, kw_in, vw_in, gate_in, pos_pad,
              q_gain, kc_gain, ks_gain, kw_gain, pe_k, pe_v, ck_w1, ck_w2, cv_w1, cv_w2):
    b, sp, _ = q_in.shape
    G, R, dh = NSA_GROUPS, NSA_REP, HEAD_DIM
    pos_h = pos_pad[:, None, :]
    t_idx = jnp.arange(sp)
    q = partial_rope(rms_norm(to_heads(q_in, NSA_HEADS), q_gain), pos_h).reshape(b, G, R, sp, dh)

    kc = rms_norm(compress_blocks(to_heads(kc_in, G), pe_k, ck_w1, ck_w2), kc_gain)
    vc = compress_blocks(to_heads(vc_in, G), pe_v, cv_w1, cv_w2)
    nc = kc.shape[2]
    cmp_start = jnp.arange(nc) * CMP_STRIDE
    cmp_end = cmp_start + CMP_LEN - 1
    kc = partial_rope(kc, pos_pad[:, cmp_end][:, None, :])
    s_c = jnp.einsum('bgrtd,bgcd->bgrtc', q, kc) * ATTN_SCALE
    p_c = masked_softmax(s_c, cmp_end[None, :] <= t_idx[:, None])
    o_c = jnp.einsum('bgrtc,bgcd->bgrtd', p_c.astype(vc.dtype), vc)

    ns = sp // SEL_BLOCK
    j = jnp.arange(ns)
    overlap = ((cmp_start[:, None] <= j[None, :] * SEL_BLOCK + SEL_BLOCK - 1)
               & (cmp_end[:, None] >= j[None, :] * SEL_BLOCK)).astype(jnp.float32)
    imp = jnp.einsum('bgrtc,cj->bgtj', p_c, overlap)
    cur = (t_idx // SEL_BLOCK)[:, None]
    forced = (j[None, :] == 0) | (j[None, :] == cur) | (j[None, :] == cur - 1)
    imp = jnp.where(forced, jnp.inf, jnp.where(j[None, :] > cur, -jnp.inf, imp))
    n_top = min(SEL_TOPN, ns)
    _, sel_idx = lax.top_k(imp, n_top)

    ks = partial_rope(rms_norm(to_heads(ks_in, G), ks_gain), pos_h)
    vs = to_heads(vs_in, G)
    ks_blk = ks.reshape(b, G, ns, SEL_BLOCK, dh)
    vs_blk = vs.reshape(b, G, ns, SEL_BLOCK, dh)
    nq = sp // NSA_QCHUNK
    bi = jnp.arange(b)[:, None, None, None]
    gi = jnp.arange(G)[None, :, None, None]
    m_sel = n_top * SEL_BLOCK

    def sel_chunk(args):
        qc, ic, tc = args
        kg = ks_blk[bi, gi, ic].reshape(b, G, NSA_QCHUNK, m_sel, dh)
        vg = vs_blk[bi, gi, ic].reshape(b, G, NSA_QCHUNK, m_sel, dh)
        kpos = (ic[..., None] * SEL_BLOCK + jnp.arange(SEL_BLOCK)).reshape(b, G, NSA_QCHUNK, m_sel)
        mask = (kpos <= tc[None, None, :, None])[:, :, None]
        s = jnp.einsum('bgrcd,bgcmd->bgrcm', qc, kg) * ATTN_SCALE
        pr = masked_softmax(s, mask)
        return jnp.einsum('bgrcm,bgcmd->bgrcd', pr.astype(vg.dtype), vg)

    q_ch = q.reshape(b, G, R, nq, NSA_QCHUNK, dh).transpose(3, 0, 1, 2, 4, 5)
    idx_ch = sel_idx.reshape(b, G, nq, NSA_QCHUNK, n_top).transpose(2, 0, 1, 3, 4)
    o_s = lax.map(sel_chunk, (q_ch, idx_ch, t_idx.reshape(nq, NSA_QCHUNK)))
    o_s = o_s.transpose(1, 2, 3, 0, 4, 5).reshape(b, G, R, sp, dh)

    kw = partial_rope(rms_norm(to_heads(kw_in, G), kw_gain), pos_h)
    vw = to_heads(vw_in, G)
    nw, nprev = sp // WIN_QBLOCK, WINDOW // WIN_QBLOCK
    band = (nprev + 1) * WIN_QBLOCK

    def banded(t):
        tb = jnp.pad(t.reshape(b, G, nw, WIN_QBLOCK, dh), ((0, 0), (0, 0), (nprev, 0), (0, 0), (0, 0)))
        return jnp.concatenate([tb[:, :, i:i + nw] for i in range(nprev + 1)], axis=3).transpose(2, 0, 1, 3, 4)

    def win_block(args):
        qb, kb, vb, w = args
        qpos = w * WIN_QBLOCK + jnp.arange(WIN_QBLOCK)
        kpos = (w - nprev) * WIN_QBLOCK + jnp.arange(band)
        diff = qpos[:, None] - kpos[None, :]
        mask = (diff >= 0) & (diff < WINDOW) & (kpos[None, :] >= 0)
        s = jnp.einsum('bgrqd,bgkd->bgrqk', qb, kb) * ATTN_SCALE
        pr = masked_softmax(s, mask)
        return jnp.einsum('bgrqk,bgkd->bgrqd', pr.astype(vb.dtype), vb)

    q_w = q.reshape(b, G, R, nw, WIN_QBLOCK, dh).transpose(3, 0, 1, 2, 4, 5)
    o_w = lax.map(win_block, (q_w, banded(kw), banded(vw), jnp.arange(nw)))
    o_w = o_w.transpose(1, 2, 3, 0, 4, 5).reshape(b, G, R, sp, dh)

    g = jax.nn.sigmoid(gate_in.reshape(b, sp, NSA_HEADS, 3)).transpose(0, 2, 1, 3).reshape(b, G, R, sp, 3)
    o = g[..., 0:1] * o_c + g[..., 1:2] * o_s + g[..., 2:3] * o_w
    return from_heads(o.reshape(b, NSA_HEADS, sp, dh))


def moba_mixer(q_in, k_in, v_in, pos_pad, q_gain, k_gain):
    b, sp, _ = q_in.shape
    H, dh, BS, C = MOBA_HEADS, HEAD_DIM, MOBA_BLOCK, MOBA_QCHUNK
    pos_h = pos_pad[:, None, :]
    t_idx = jnp.arange(sp)
    q = partial_rope(rms_norm(to_heads(q_in, H), q_gain), pos_h)
    k = partial_rope(rms_norm(to_heads(k_in, H), k_gain), pos_h)
    v = to_heads(v_in, H)
    nb = sp // BS
    k_blk = k.reshape(b, H, nb, BS, dh)
    v_blk = v.reshape(b, H, nb, BS, dh)
    k_mean = jnp.mean(k_blk.astype(jnp.float32), axis=3)
    score = jnp.einsum('bhtd,bhnd->bhtn', q.astype(jnp.float32), k_mean)
    own = (t_idx // BS)[:, None]
    score = jnp.where(jnp.arange(nb)[None, :] < own, score, -jnp.inf)
    n_top = min(MOBA_TOPK, nb)
    _, sel_idx = lax.top_k(score, n_top)
    bi = jnp.arange(b)[:, None, None, None]
    hi = jnp.arange(H)[None, :, None, None]
    nq = sp // C
    m_sel = n_top * BS

    def chunk(args):
        qc, ic, c = args
        start = c * C
        blk = start // BS
        k_own = lax.dynamic_slice_in_dim(k, blk * BS, BS, axis=2)
        v_own = lax.dynamic_slice_in_dim(v, blk * BS, BS, axis=2)
        qpos = start + jnp.arange(C)
        m_own = jnp.broadcast_to(blk * BS + jnp.arange(BS)[None, :] <= qpos[:, None], (b, H, C, BS))
        m_sel_mask = jnp.broadcast_to((ic < blk)[..., None], (b, H, C, n_top, BS)).reshape(b, H, C, m_sel)
        kg = k_blk[bi, hi, ic].reshape(b, H, C, m_sel, dh)
        vg = v_blk[bi, hi, ic].reshape(b, H, C, m_sel, dh)
        s = jnp.concatenate([jnp.einsum('bhcd,bhpd->bhcp', qc, k_own),
                             jnp.einsum('bhcd,bhcmd->bhcm', qc, kg)], axis=-1) * ATTN_SCALE
        pr = masked_softmax(s, jnp.concatenate([m_own, m_sel_mask], axis=-1)).astype(v.dtype)
        return (jnp.einsum('bhcp,bhpd->bhcd', pr[..., :BS], v_own)
                + jnp.einsum('bhcm,bhcmd->bhcd', pr[..., BS:], vg))

    q_ch = q.reshape(b, H, nq, C, dh).transpose(2, 0, 1, 3, 4)
    idx_ch = sel_idx.reshape(b, H, nq, C, n_top).transpose(2, 0, 1, 3, 4)
    o = lax.map(chunk, (q_ch, idx_ch, jnp.arange(nq)))
    o = o.transpose(1, 2, 0, 3, 4).reshape(b, H, sp, dh)
    return from_heads(o)


def setup_inputs(seed: int = 0) -> dict:
    key = jax.random.key(seed)
    ks = jax.random.split(key, 25)

    def normal(k, shape, scale):
        return jax.random.normal(k, shape, jnp.float32) * scale

    def gain(k, n):
        return 1.0 + 0.1 * jax.random.normal(k, (DEPTH, n), jnp.float32)

    L, dh = DEPTH, HEAD_DIM
    nsa_w, moba_w = NSA_HEADS * dh, MOBA_HEADS * dh
    return {
        'x': normal(ks[0], (BATCH, SEQ, D_MODEL), 1.0),
        'p': normal(ks[1], (DEPTH, BATCH, SEQ, PLE_DIM), 1.0),
        'positions': jnp.tile(jnp.arange(SEQ, dtype=jnp.int32)[None, :], (BATCH, 1)),
        'g_mix': gain(ks[2], D_MODEL),
        'w_in': normal(ks[3], (L, D_MODEL, IN_COLS), D_MODEL ** -0.5),
        'nsa_q_gain': gain(ks[4], dh),
        'nsa_kc_gain': gain(ks[5], dh),
        'nsa_ks_gain': gain(ks[6], dh),
        'nsa_kw_gain': gain(ks[7], dh),
        'nsa_pe_k': normal(ks[8], (L, CMP_LEN, dh), 0.1),
        'nsa_pe_v': normal(ks[9], (L, CMP_LEN, dh), 0.1),
        'nsa_ck_w1': normal(ks[10], (L, CMP_LEN * dh, CMP_HIDDEN), (CMP_LEN * dh) ** -0.5),
        'nsa_ck_w2': normal(ks[11], (L, CMP_HIDDEN, dh), CMP_HIDDEN ** -0.5),
        'nsa_cv_w1': normal(ks[12], (L, CMP_LEN * dh, CMP_HIDDEN), (CMP_LEN * dh) ** -0.5),
        'nsa_cv_w2': normal(ks[13], (L, CMP_HIDDEN, dh), CMP_HIDDEN ** -0.5),
        'moba_q_gain': gain(ks[14], dh),
        'moba_k_gain': gain(ks[15], dh),
        'w_up_nsa': normal(ks[16], (L, nsa_w, D_MODEL), nsa_w ** -0.5),
        'w_up_moba': normal(ks[17], (L, moba_w, D_MODEL), moba_w ** -0.5),
        'w_out': normal(ks[18], (L, D_MODEL, D_MODEL), D_MODEL ** -0.5),
        'g_ffn': gain(ks[19], D_MODEL),
        'w_ffn_in': normal(ks[20], (L, D_MODEL, 2 * D_FF), D_MODEL ** -0.5),
        'w_ffn_out': normal(ks[21], (L, D_FF, D_MODEL), D_FF ** -0.5),
        'g_ple': gain(ks[22], D_MODEL),
        'w_ple_gate': normal(ks[23], (L, D_MODEL, D_MODEL), D_MODEL ** -0.5),
        'w_ple_proj': normal(ks[24], (L, PLE_DIM, D_MODEL), PLE_DIM ** -0.5),
    }


def reference(x, p, positions, g_mix, w_in, nsa_q_gain, nsa_kc_gain, nsa_ks_gain, nsa_kw_gain,
              nsa_pe_k, nsa_pe_v, nsa_ck_w1, nsa_ck_w2, nsa_cv_w1, nsa_cv_w2,
              moba_q_gain, moba_k_gain, w_up_nsa, w_up_moba, w_out,
              g_ffn, w_ffn_in, w_ffn_out, g_ple, w_ple_gate, w_ple_proj):
    b, s, _ = x.shape
    sp = -(-s // PAD_MULT) * PAD_MULT
    extra = jnp.arange(1, sp - s + 1, dtype=positions.dtype)
    pos_pad = jnp.concatenate([positions, positions[:, -1:] + extra[None, :]], axis=1)
    for i in range(DEPTH):
        proj = rms_norm(x, g_mix[i]) @ w_in[i]
        proj = jnp.pad(proj, ((0, 0), (0, sp - s), (0, 0)))
        (q_n, kc_n, vc_n, ks_n, vs_n, kw_n, vw_n, gate_n,
         q_m, k_m, v_m, gate_a, gate_b) = jnp.split(proj, IN_CUTS, axis=-1)
        y_nsa = nsa_mixer(q_n, kc_n, vc_n, ks_n, vs_n, kw_n, vw_n, gate_n, pos_pad,
                          nsa_q_gain[i], nsa_kc_gain[i], nsa_ks_gain[i], nsa_kw_gain[i],
                          nsa_pe_k[i], nsa_pe_v[i], nsa_ck_w1[i], nsa_ck_w2[i],
                          nsa_cv_w1[i], nsa_cv_w2[i])[:, :s]
        y_moba = moba_mixer(q_m, k_m, v_m, pos_pad, moba_q_gain[i], moba_k_gain[i])[:, :s]
        merged = (jax.nn.sigmoid(gate_a[:, :s]) * (y_nsa @ w_up_nsa[i])
                  + jax.nn.sigmoid(gate_b[:, :s]) * (y_moba @ w_up_moba[i]))
        x = x + merged @ w_out[i]
        gate, up = jnp.split(rms_norm(x, g_ffn[i]) @ w_ffn_in[i], 2, axis=-1)
        x = x + (jax.nn.silu(gate) * up) @ w_ffn_out[i]
        ple_gate = jax.nn.sigmoid(rms_norm(x, g_ple[i]) @ w_ple_gate[i])
        x = x + ple_gate * (p[i] @ w_ple_proj[i])
    return x
```

```python
import functools

import jax
import jax.numpy as jnp
import numpy as np
from jax import lax
from jax.experimental import pallas as pl
from jax.experimental.pallas import tpu as pltpu

F32 = jnp.float32
BF16 = jnp.bfloat16

D_MODEL = 1024
HEAD_DIM = 64
ROPE_DIM = HEAD_DIM // 4
ROPE_HALF = ROPE_DIM // 2
ROPE_THETA = 500000.0
NORM_EPS = 1e-6
ATTN_SCALE = HEAD_DIM ** -0.5

NSA_HEADS = 8
NSA_GROUPS = 2
NSA_REP = NSA_HEADS // NSA_GROUPS
CMP_LEN = 32
CMP_STRIDE = 16
CMP_HIDDEN = 256
SEL_BLOCK = 64
SEL_TOPN = 8
WINDOW = 512

MOBA_HEADS = 8
MOBA_BLOCK = 256
MOBA_TOPK = 3

PAD_MULT = 256
D_FF = ((-(-8 * D_MODEL // 3)) + 255) // 256 * 256
PLE_DIM = 256

IN_SPLITS = ((NSA_HEADS * HEAD_DIM,) + (NSA_GROUPS * HEAD_DIM,) * 6 + (3 * NSA_HEADS,)
             + (MOBA_HEADS * HEAD_DIM,) * 3 + (D_MODEL, D_MODEL))
IN_CUTS = tuple(int(c) for c in np.cumsum(IN_SPLITS)[:-1])

LANES = 128
NEG_BIG = -1e30
VMEM_LIMIT = 56 * 1024 * 1024

NORMED_COLS = 1792
PROJ_COLS = 2944

PROJ_TM = 256
NSA_TQ = 128
NSA_TK = 256
WIN_TK = 128
MERGE_TM = 256
FFN_TM = 256
FFN_CHUNK = D_FF // 2


def _dot(a, b):
    return jnp.dot(a, b, preferred_element_type=F32)


def _dot_nt(a, b):
    return lax.dot_general(a, b, (((1,), (1,)), ((), ())), preferred_element_type=F32)


def _dot_split(a_f32, b_bf16):
    hi = a_f32.astype(BF16)
    lo = (a_f32 - hi.astype(F32)).astype(BF16)
    return _dot(hi, b_bf16) + _dot(lo, b_bf16)


def _sigmoid(x):
    return 1.0 / (1.0 + jnp.exp(-x))


def _rms_rows(x, g):
    return x * lax.rsqrt(jnp.mean(x * x, axis=-1, keepdims=True) + NORM_EPS) * g


def _head_norm(t, bd, gain):
    ss = _dot_split(t * t, bd)
    return t * lax.rsqrt(ss * (1.0 / HEAD_DIM) + NORM_EPS) * gain


def _rope_tables(pos_col, invf):
    ang = pos_col.astype(F32) * invf
    cos_a, sin_a = jnp.cos(ang), jnp.sin(ang)
    d = lax.broadcasted_iota(jnp.int32, ang.shape, 1) & (HEAD_DIM - 1)
    s_lo = jnp.where(d < ROPE_HALF, -sin_a, 0.0)
    s_hi = jnp.where((d >= ROPE_HALF) & (d < ROPE_DIM), sin_a, 0.0)
    return cos_a, s_lo, s_hi


def _rope(y, tables):
    cos_a, s_lo, s_hi = tables
    w = y.shape[1]
    return (y * cos_a + pltpu.roll(y, w - ROPE_HALF, 1) * s_lo
            + pltpu.roll(y, ROPE_HALF, 1) * s_hi)


def _proj_kernel(x_ref, pos_ref, gmix_ref, w_ref, gain_ref, invf_ref, bd_ref,
                 qn_ref, qm_ref, km_ref, ks_ref, kw_ref, vm_ref, vs_ref, vw_ref,
                 kc_ref, vc_ref, gn_ref):
    h = _rms_rows(x_ref[...], gmix_ref[...])
    acc = _dot(h.astype(BF16), w_ref[...])
    tables = _rope_tables(pos_ref[...], invf_ref[...])
    bd = bd_ref[...]
    outs = (qn_ref, qn_ref, qm_ref, qm_ref, km_ref, km_ref)
    for c in range(NORMED_COLS // 256):
        t = acc[:, c * 256:(c + 1) * 256]
        y = _rope(_head_norm(t, bd, gain_ref[:, c * 256:(c + 1) * 256]), tables).astype(BF16)
        if c < 6:
            outs[c][:, (c % 2) * 256:(c % 2 + 1) * 256] = y
        else:
            ks_ref[...] = y[:, :LANES]
            kw_ref[...] = y[:, LANES:]
    o = NORMED_COLS
    vm_ref[...] = acc[:, o:o + 512].astype(BF16)
    vs_ref[...] = acc[:, o + 512:o + 640].astype(BF16)
    vw_ref[...] = acc[:, o + 640:o + 768].astype(BF16)
    kc_ref[...] = acc[:, o + 768:o + 896]
    vc_ref[...] = acc[:, o + 896:o + 1024]
    gn_ref[...] = acc[:, o + 1024:o + 1152]


def _proj_call(x2, pos2, gmix, w, gains, invf, bd):
    t = x2.shape[0]
    tm = PROJ_TM
    row = lambda w_: pl.BlockSpec((tm, w_), lambda i: (i, 0))
    full = lambda a: pl.BlockSpec(a.shape, lambda i: (0, 0))
    out_shapes = [jax.ShapeDtypeStruct((t, 512), BF16)] * 3 + [jax.ShapeDtypeStruct((t, LANES), BF16)] * 2 \
        + [jax.ShapeDtypeStruct((t, 512), BF16)] + [jax.ShapeDtypeStruct((t, LANES), BF16)] * 2 \
        + [jax.ShapeDtypeStruct((t, LANES), F32)] * 3
    out_specs = [row(s.shape[1]) for s in out_shapes]
    return pl.pallas_call(
        _proj_kernel,
        grid=(t // tm,),
        in_specs=[row(D_MODEL), row(1), full(gmix), full(w), full(gains), full(invf), full(bd)],
        out_specs=out_specs,
        out_shape=out_shapes,
        compiler_params=pltpu.CompilerParams(dimension_semantics=("parallel",),
                                             vmem_limit_bytes=VMEM_LIMIT),
    )(x2, pos2, gmix, w, gains, invf, bd)


def _compress_kernel(kc4_ref, vc4_ref, pe_ref, wk_ref, wv_ref, w2k_ref, w2v_ref, gain_ref,
                     pose_ref, invf_ref, bd_ref, kc_out, vc_out):
    half = 2 * CMP_HIDDEN

    def comp(x, pe_a, pe_b, w_ref, w2_ref):
        a = _dot((x + pe_a).astype(BF16), w_ref[:, :half])
        b = _dot((x + pe_b).astype(BF16), w_ref[:, half:])
        hid = a + pltpu.roll(b, b.shape[0] - 1, 0)
        act = hid * _sigmoid(hid)
        return _dot(act.astype(BF16), w2_ref[...])

    kc = comp(kc4_ref[0], pe_ref[0:1, :], pe_ref[1:2, :], wk_ref, w2k_ref)
    vc = comp(vc4_ref[0], pe_ref[2:3, :], pe_ref[3:4, :], wv_ref, w2v_ref)
    kc = _head_norm(kc, bd_ref[...], gain_ref[...])
    kc = _rope(kc, _rope_tables(pose_ref[0], invf_ref[...]))
    kc_out[0] = kc.astype(BF16)
    vc_out[0] = vc.astype(BF16)


def _compress_call(kc4, vc4, pe4, wk, wv, w2k, w2v, gain, pos_end, invf, bd):
    b, n_sub, width = kc4.shape
    blk = lambda shp: pl.BlockSpec((1,) + shp, lambda i: (i, 0, 0))
    full = lambda a: pl.BlockSpec(a.shape, lambda i: (0,) * a.ndim)
    out = jax.ShapeDtypeStruct((b, n_sub, LANES), BF16)
    return pl.pallas_call(
        _compress_kernel,
        grid=(b,),
        in_specs=[blk((n_sub, width)), blk((n_sub, width)), full(pe4), full(wk), full(wv),
                  full(w2k), full(w2v), full(gain), blk((n_sub, 1)), full(invf), full(bd)],
        out_specs=[blk((n_sub, LANES)), blk((n_sub, LANES))],
        out_shape=[out, out],
        compiler_params=pltpu.CompilerParams(dimension_semantics=("parallel",),
                                             vmem_limit_bytes=VMEM_LIMIT),
    )(kc4, vc4, pe4, wk, wv, w2k, w2v, gain, pos_end, invf, bd)


def _online_update(carry, s, v):
    m, l, acc = carry
    m_new = jnp.maximum(m, jnp.max(s, axis=-1, keepdims=True))
    alpha = jnp.exp(m - m_new)
    e = jnp.exp(s - m_new)
    l = alpha * l + jnp.sum(e, axis=-1, keepdims=True)
    acc = alpha * acc + _dot(e.astype(BF16), v)
    return m_new, l, acc


def _nsa_kernel(q_ref, kc_ref, vc_ref, ks_ref, vs_ref, kw_ref, vw_ref, gate_ref, ovl_ref, out_ref):
    tq = q_ref.shape[1]
    n_cmp = kc_ref.shape[1]
    nh = 2 * NSA_REP
    rows = nh * tq
    i = pl.program_id(1)
    t0 = i * tq

    q = q_ref[0]
    lane_q = lax.broadcasted_iota(jnp.int32, (tq, LANES), 1)
    lo = lane_q < HEAD_DIM
    zero = jnp.zeros((tq, LANES), BF16)
    tiles = [q[:, r * LANES:(r + 1) * LANES] for r in range(NSA_REP)]
    lhs = jnp.concatenate([jnp.where(lo, t, zero) for t in tiles]
                          + [jnp.where(lo, zero, t) for t in tiles], axis=0)

    t_q = t0 + lax.broadcasted_iota(jnp.int32, (tq, 1), 0)

    def per_head(a):
        return jnp.concatenate([a] * nh, axis=0)

    def per_group(a0, a1):
        return jnp.concatenate([a0] * NSA_REP + [a1] * NSA_REP, axis=0)

    c_idx = lax.broadcasted_iota(jnp.int32, (tq, n_cmp), 1)
    valid_c = (c_idx * CMP_STRIDE + (CMP_LEN - 1) <= t_q) & (c_idx < n_cmp - 1)
    s_c = _dot_nt(lhs, kc_ref[0]) + per_head(jnp.where(valid_c, 0.0, NEG_BIG))
    m_c = jnp.max(s_c, axis=-1, keepdims=True)
    e_c = jnp.where(per_head(valid_c), jnp.exp(s_c - m_c), 0.0)
    p_c = e_c / jnp.maximum(jnp.sum(e_c, axis=-1, keepdims=True), 1e-30)
    o_c = _dot(p_c.astype(BF16), vc_ref[0])

    cur = t_q // SEL_BLOCK
    j_idx = lax.broadcasted_iota(jnp.int32, (tq, LANES), 1)
    forced = (j_idx == 0) | (j_idx == cur) | (j_idx == cur - 1)
    sel = []
    for g in range(NSA_GROUPS):
        p_sum = p_c[(g * NSA_REP) * tq:(g * NSA_REP + 1) * tq]
        for r in range(1, NSA_REP):
            p_sum = p_sum + p_c[(g * NSA_REP + r) * tq:(g * NSA_REP + r + 1) * tq]
        imp = _dot_split(p_sum, ovl_ref[...])
        imp = jnp.where(forced, jnp.inf, jnp.where(j_idx > cur, -jnp.inf, imp))
        rank = jnp.zeros((tq, LANES), F32)
        for jp in range(S_BLOCKS_MAX):
            col = imp[:, jp:jp + 1]
            beats = (col > imp) | ((col == imp) & (j_idx > jp))
            rank = rank + jnp.where(beats, 1.0, 0.0)
        sel.append(jnp.where(rank < SEL_TOPN, 1.0, 0.0).astype(BF16))

    init = (jnp.full((rows, 1), NEG_BIG, F32), jnp.zeros((rows, 1), F32),
            jnp.zeros((rows, LANES), F32))

    def sel_body(kt, carry):
        k0 = pl.multiple_of(kt * NSA_TK, NSA_TK)
        k_t = ks_ref[0, pl.ds(k0, NSA_TK), :]
        v_t = vs_ref[0, pl.ds(k0, NSA_TK), :]
        kpos = k0 + lax.broadcasted_iota(jnp.int32, (tq, NSA_TK), 1)
        causal = kpos <= t_q
        blk_of_key = (k0 + lax.broadcasted_iota(jnp.int32, (LANES, NSA_TK), 1)) // SEL_BLOCK
        expand = jnp.where(blk_of_key == lax.broadcasted_iota(jnp.int32, (LANES, NSA_TK), 0),
                           1.0, 0.0).astype(BF16)
        bias = [jnp.where(causal & (_dot(sel[g], expand) > 0.5), 0.0, NEG_BIG)
                for g in range(NSA_GROUPS)]
        s = _dot_nt(lhs, k_t) + per_group(bias[0], bias[1])
        return _online_update(carry, s, v_t)

    n_sel_tiles = (t0 + tq + NSA_TK - 1) // NSA_TK
    _, l_s, acc_s = lax.fori_loop(0, n_sel_tiles, sel_body, init)

    def win_body(kt, carry):
        k0 = pl.multiple_of(kt * WIN_TK, WIN_TK)
        k_t = kw_ref[0, pl.ds(k0, WIN_TK), :]
        v_t = vw_ref[0, pl.ds(k0, WIN_TK), :]
        diff = t_q - (k0 + lax.broadcasted_iota(jnp.int32, (tq, WIN_TK), 1))
        bias = jnp.where((diff >= 0) & (diff < WINDOW), 0.0, NEG_BIG)
        s = _dot_nt(lhs, k_t) + per_head(bias)
        return _online_update(carry, s, v_t)

    first_win = jnp.maximum((t0 - (WINDOW - 1)) // WIN_TK, 0)
    last_win = (t0 + tq - 1) // WIN_TK
    _, l_w, acc_w = lax.fori_loop(first_win, last_win + 1, win_body, init)

    gsig = _sigmoid(gate_ref[0])

    def gate_col(branch):
        cols = []
        for g in range(NSA_GROUPS):
            for r in range(NSA_REP):
                c = 3 * (g * NSA_REP + r) + branch
                cols.append(gsig[:, c:c + 1])
        return jnp.concatenate(cols, axis=0)

    o = (o_c * gate_col(0)
         + acc_s * (gate_col(1) / jnp.maximum(l_s, 1e-30))
         + acc_w * (gate_col(2) / jnp.maximum(l_w, 1e-30)))
    for r in range(NSA_REP):
        o_lo = o[r * tq:(r + 1) * tq]
        o_hi = o[(NSA_REP + r) * tq:(NSA_REP + r + 1) * tq]
        out_ref[0, :, r * LANES:(r + 1) * LANES] = jnp.where(lo, o_lo, o_hi).astype(BF16)


S_BLOCKS_MAX = 32


def _nsa_call(qn, kc, vc, ks, vs, kw, vw, gn, ovl):
    b, s, _ = qn.shape
    tq = NSA_TQ
    assert s // SEL_BLOCK <= S_BLOCKS_MAX
    qblk = lambda w_: pl.BlockSpec((1, tq, w_), lambda bi, i: (bi, i, 0))
    seq = lambda a: pl.BlockSpec((1,) + a.shape[1:], lambda bi, i: (bi, 0, 0))
    full = lambda a: pl.BlockSpec(a.shape, lambda bi, i: (0,) * a.ndim)
    return pl.pallas_call(
        _nsa_kernel,
        grid=(b, s // tq),
        in_specs=[qblk(512), seq(kc), seq(vc), seq(ks), seq(vs), seq(kw), seq(vw), qblk(LANES),
                  full(ovl)],
        out_specs=qblk(512),
        out_shape=jax.ShapeDtypeStruct((b, s, 512), BF16),
        compiler_params=pltpu.CompilerParams(dimension_semantics=("parallel", "parallel"),
                                             vmem_limit_bytes=VMEM_LIMIT),
    )(qn, kc, vc, ks, vs, kw, vw, gn, ovl)


def _moba_kernel(q_ref, k_ref, v_ref, out_ref, s_scr):
    bs = MOBA_BLOCK
    rows = 2 * bs
    i = pl.program_id(2)

    q = q_ref[0]
    lane_q = lax.broadcasted_iota(jnp.int32, (bs, LANES), 1)
    lo = lane_q < HEAD_DIM
    zero = jnp.zeros((bs, LANES), BF16)
    lhs = jnp.concatenate([jnp.where(lo, q, zero), jnp.where(lo, zero, q)], axis=0)
    lane = lax.broadcasted_iota(jnp.int32, (rows, LANES), 1)

    def score_body(n, sc):
        k0 = pl.multiple_of(n * bs, bs)
        s = _dot_nt(lhs, k_ref[0, pl.ds(k0, bs), :])
        s_scr[n] = s
        mean_s = jnp.sum(s, axis=-1, keepdims=True) * (1.0 / bs)
        return jnp.where(lane == n, mean_s, sc)

    sc = lax.fori_loop(0, i, score_body, jnp.full((rows, LANES), -jnp.inf, F32))
    rank = jnp.zeros((rows, LANES), F32)
    for n in range(N_BLOCKS_MAX):
        col = sc[:, n:n + 1]
        beats = (col > sc) | ((col == sc) & (lane > n))
        rank = rank + jnp.where(beats, 1.0, 0.0)
    sel = jnp.where((rank < MOBA_TOPK) & (lane < i), 1.0, 0.0)

    k0 = pl.multiple_of(i * bs, bs)
    r_idx = lax.broadcasted_iota(jnp.int32, (bs, bs), 0)
    c_idx = lax.broadcasted_iota(jnp.int32, (bs, bs), 1)
    bias_own = jnp.where(c_idx <= r_idx, 0.0, NEG_BIG)
    s_own = _dot_nt(lhs, k_ref[0, pl.ds(k0, bs), :]) + jnp.concatenate([bias_own, bias_own], axis=0)
    init = (jnp.full((rows, 1), NEG_BIG, F32), jnp.zeros((rows, 1), F32),
            jnp.zeros((rows, LANES), F32))
    carry = _online_update(init, s_own, v_ref[0, pl.ds(k0, bs), :])

    def past_body(n, carry):
        kn = pl.multiple_of(n * bs, bs)
        picked = jnp.sum(jnp.where(lane == n, sel, 0.0), axis=-1, keepdims=True)
        s = s_scr[n] + (picked - 1.0) * (-NEG_BIG)
        return _online_update(carry, s, v_ref[0, pl.ds(kn, bs), :])

    _, l, acc = lax.fori_loop(0, i, past_body, carry)
    o = acc / jnp.maximum(l, 1e-30)
    out_ref[0] = jnp.where(lo, o[:bs], o[bs:]).astype(BF16)


N_BLOCKS_MAX = 8


def _moba_call(qm, km, vm):
    b, s, w = qm.shape
    bs = MOBA_BLOCK
    assert s // bs <= N_BLOCKS_MAX
    qblk = pl.BlockSpec((1, bs, LANES), lambda bi, p, i: (bi, i, p))
    seq = pl.BlockSpec((1, s, LANES), lambda bi, p, i: (bi, 0, p))
    return pl.pallas_call(
        _moba_kernel,
        grid=(b, w // LANES, s // bs),
        in_specs=[qblk, seq, seq],
        out_specs=qblk,
        out_shape=jax.ShapeDtypeStruct((b, s, w), BF16),
        scratch_shapes=[pltpu.VMEM((N_BLOCKS_MAX, 2 * bs, bs), F32)],
        compiler_params=pltpu.CompilerParams(
            dimension_semantics=("parallel", "parallel", "parallel"), vmem_limit_bytes=VMEM_LIMIT),
    )(qm, km, vm)


def _merge_kernel(x_ref, yn_ref, ym_ref, gmix_ref, wg_ref, wun_ref, wum_ref, wo_ref, out_ref):
    x = x_ref[...]
    h = _rms_rows(x, gmix_ref[...]).astype(BF16)
    ga = _sigmoid(_dot(h, wg_ref[:, :D_MODEL]))
    gb = _sigmoid(_dot(h, wg_ref[:, D_MODEL:]))
    merged = ga * _dot(yn_ref[...], wun_ref[...]) + gb * _dot(ym_ref[...], wum_ref[...])
    out_ref[...] = x + _dot(merged.astype(BF16), wo_ref[...])


def _merge_call(x2, yn, ym, gmix, wg, wun, wum, wo):
    t = x2.shape[0]
    tm = MERGE_TM
    row = lambda w_: pl.BlockSpec((tm, w_), lambda i: (i, 0))
    full = lambda a: pl.BlockSpec(a.shape, lambda i: (0, 0))
    return pl.pallas_call(
        _merge_kernel,
        grid=(t // tm,),
        in_specs=[row(D_MODEL), row(512), row(512), full(gmix), full(wg), full(wun), full(wum),
                  full(wo)],
        out_specs=row(D_MODEL),
        out_shape=jax.ShapeDtypeStruct((t, D_MODEL), F32),
        compiler_params=pltpu.CompilerParams(dimension_semantics=("parallel",),
                                             vmem_limit_bytes=VMEM_LIMIT),
    )(x2, yn, ym, gmix, wg, wun, wum, wo)


def _ffn_kernel(x_ref, p_ref, gffn_ref, wfi_ref, wfo_ref, gple_ref, wpg_ref, wpp_ref, out_ref):
    x = x_ref[...]
    h = _rms_rows(x, gffn_ref[...]).astype(BF16)
    y = x
    for c in range(D_FF // FFN_CHUNK):
        gate = _dot(h, wfi_ref[:, c * FFN_CHUNK:(c + 1) * FFN_CHUNK])
        up = _dot(h, wfi_ref[:, D_FF + c * FFN_CHUNK:D_FF + (c + 1) * FFN_CHUNK])
        act = (gate * _sigmoid(gate) * up).astype(BF16)
        y = y + _dot(act, wfo_ref[c * FFN_CHUNK:(c + 1) * FFN_CHUNK, :])
    h2 = _rms_rows(y, gple_ref[...]).astype(BF16)
    ple_gate = _sigmoid(_dot(h2, wpg_ref[...]))
    out_ref[...] = y + ple_gate * _dot(p_ref[...].astype(BF16), wpp_ref[...])


def _ffn_call(x1, p2, gffn, wfi, wfo, gple, wpg, wpp):
    t = x1.shape[0]
    tm = FFN_TM
    row = lambda w_: pl.BlockSpec((tm, w_), lambda i: (i, 0))
    full = lambda a: pl.BlockSpec(a.shape, lambda i: (0, 0))
    return pl.pallas_call(
        _ffn_kernel,
        grid=(t // tm,),
        in_specs=[row(D_MODEL), row(PLE_DIM), full(gffn), full(wfi), full(wfo), full(gple),
                  full(wpg), full(wpp)],
        out_specs=row(D_MODEL),
        out_shape=jax.ShapeDtypeStruct((t, D_MODEL), F32),
        compiler_params=pltpu.CompilerParams(dimension_semantics=("parallel",),
                                             vmem_limit_bytes=VMEM_LIMIT),
    )(x1, p2, gffn, wfi, wfo, gple, wpg, wpp)


def _block_diag_ones(width):
    idx = np.arange(width) // HEAD_DIM
    return jnp.asarray(idx[:, None] == idx[None, :], dtype=BF16)


def _inv_freq_lanes(width):
    inv_freq = ROPE_THETA ** (-jnp.arange(ROPE_HALF, dtype=F32) / ROPE_HALF)
    per_head = jnp.concatenate([inv_freq, inv_freq, jnp.zeros((HEAD_DIM - ROPE_DIM,), F32)])
    return jnp.tile(per_head, width // HEAD_DIM)[None, :]


def _split_w_in(w):
    parts = jnp.split(w, IN_CUTS, axis=-1)
    (q_n, kc, vc, ks, vs, kw, vw, gate_n, q_m, k_m, v_m, gate_a, gate_b) = parts
    d = w.shape[0]
    q_n = q_n.reshape(d, NSA_GROUPS, NSA_REP, HEAD_DIM).transpose(0, 2, 1, 3).reshape(d, -1)
    gate_n = jnp.pad(gate_n, ((0, 0), (0, LANES - gate_n.shape[1])))
    w_attn = jnp.concatenate([q_n, q_m, k_m, ks, kw, v_m, vs, vw, kc, vc, gate_n], axis=1)
    w_gate = jnp.concatenate([gate_a, gate_b], axis=1)
    return w_attn.astype(BF16), w_gate.astype(BF16)


def _compress_weights(w1, w2, pe):
    eye = jnp.eye(NSA_GROUPS, dtype=F32)
    halves = []
    for part in (w1[:CMP_STRIDE * HEAD_DIM], w1[CMP_STRIDE * HEAD_DIM:]):
        p3 = part.reshape(CMP_STRIDE, HEAD_DIM, CMP_HIDDEN)
        halves.append(jnp.einsum('idh,gk->igdkh', p3, eye)
                      .reshape(CMP_STRIDE * NSA_GROUPS * HEAD_DIM, NSA_GROUPS * CMP_HIDDEN))
    w_big = jnp.concatenate(halves, axis=1).astype(BF16)
    w2_bd = jnp.einsum('hd,kg->khgd', w2, eye).reshape(NSA_GROUPS * CMP_HIDDEN,
                                                        NSA_GROUPS * HEAD_DIM).astype(BF16)
    pe_rows = [jnp.broadcast_to(pe[a:a + CMP_STRIDE, None, :], (CMP_STRIDE, NSA_GROUPS, HEAD_DIM))
               .reshape(1, -1) for a in (0, CMP_STRIDE)]
    return w_big, w2_bd, pe_rows


def _overlap_matrix(n_cmp):
    c = np.arange(n_cmp)
    j = np.arange(LANES)
    start, end = c * CMP_STRIDE, c * CMP_STRIDE + CMP_LEN - 1
    ov = (start[:, None] <= j[None, :] * SEL_BLOCK + SEL_BLOCK - 1) & (end[:, None] >= j[None, :] * SEL_BLOCK)
    return jnp.asarray(ov, dtype=BF16)


def kernel(x, p, positions, g_mix, w_in, nsa_q_gain, nsa_kc_gain, nsa_ks_gain, nsa_kw_gain, nsa_pe_k, nsa_pe_v, nsa_ck_w1, nsa_ck_w2, nsa_cv_w1, nsa_cv_w2, moba_q_gain, moba_k_gain, w_up_nsa, w_up_moba, w_out, g_ffn, w_ffn_in, w_ffn_out, g_ple, w_ple_gate, w_ple_proj):
    b, s, d = x.shape
    depth = w_in.shape[0]
    assert s % PAD_MULT == 0 and d == D_MODEL
    t = b * s
    n_sub = s // CMP_STRIDE

    pos2 = positions.reshape(t, 1)
    pos_end = jnp.concatenate([positions[:, CMP_LEN - 1::CMP_STRIDE], positions[:, -1:]], axis=1)[:, :, None]
    invf256 = _inv_freq_lanes(256)
    bd256, bd128 = _block_diag_ones(256), _block_diag_ones(LANES)
    ovl = _overlap_matrix(n_sub)
    tile = lambda g, n: jnp.tile(g, n)
    xi = x.reshape(t, d)

    for i in range(depth):
        w_attn, w_gate = _split_w_in(w_in[i])
        gains = jnp.concatenate([tile(nsa_q_gain[i], 8) * ATTN_SCALE, tile(moba_q_gain[i], 8) * ATTN_SCALE,
                                 tile(moba_k_gain[i], 8), tile(nsa_ks_gain[i], 2),
                                 tile(nsa_kw_gain[i], 2)])[None, :]
        gmix = g_mix[i][None, :]
        (qn, qm, km, ks, kw, vm, vs, vw, kc_raw, vc_raw, gn) = _proj_call(
            xi, pos2, gmix, w_attn, gains, invf256, bd256)

        wk_big, w2k_bd, pe_k = _compress_weights(nsa_ck_w1[i], nsa_ck_w2[i], nsa_pe_k[i])
        wv_big, w2v_bd, pe_v = _compress_weights(nsa_cv_w1[i], nsa_cv_w2[i], nsa_pe_v[i])
        pe4 = jnp.concatenate(pe_k + pe_v, axis=0)
        kc, vc = _compress_call(kc_raw.reshape(b, n_sub, CMP_STRIDE * LANES),
                                vc_raw.reshape(b, n_sub, CMP_STRIDE * LANES),
                                pe4, wk_big, wv_big, w2k_bd, w2v_bd,
                                tile(nsa_kc_gain[i], 2)[None, :], pos_end, invf256[:, :LANES], bd128)

        r3 = lambda a: a.reshape(b, s, a.shape[-1])
        y_nsa = _nsa_call(r3(qn), kc, vc, r3(ks), r3(vs), r3(kw), r3(vw), r3(gn), ovl)
        y_moba = _moba_call(r3(qm), r3(km), r3(vm))

        w_un = (w_up_nsa[i].reshape(NSA_GROUPS, NSA_REP, HEAD_DIM, d).transpose(1, 0, 2, 3)
                .reshape(NSA_HEADS * HEAD_DIM, d).astype(BF16))
        x1 = _merge_call(xi, y_nsa.reshape(t, -1), y_moba.reshape(t, -1), gmix, w_gate, w_un,
                         w_up_moba[i].astype(BF16), w_out[i].astype(BF16))
        xi = _ffn_call(x1, p[i].reshape(t, PLE_DIM), g_ffn[i][None, :], w_ffn_in[i].astype(BF16),
                       w_ffn_out[i].astype(BF16), g_ple[i][None, :], w_ple_gate[i].astype(BF16),
                       w_ple_proj[i].astype(BF16))
    return xi.reshape(b, s, d)
```

```python
import jax
import jax.numpy as jnp
import numpy as np
from jax import lax
from jax.experimental import pallas as pl
from jax.experimental.pallas import tpu as pltpu

F32 = jnp.float32
BF16 = jnp.bfloat16

D_MODEL = 1024
HEAD_DIM = 64
ROPE_DIM = HEAD_DIM // 4
ROPE_HALF = ROPE_DIM // 2
ROPE_THETA = 500000.0
NORM_EPS = 1e-6
ATTN_SCALE = HEAD_DIM ** -0.5

NSA_HEADS = 8
NSA_GROUPS = 2
NSA_REP = NSA_HEADS // NSA_GROUPS
CMP_LEN = 32
CMP_STRIDE = 16
CMP_HIDDEN = 256
SEL_BLOCK = 64
SEL_TOPN = 8
WINDOW = 512

MOBA_HEADS = 8
MOBA_BLOCK = 256
MOBA_TOPK = 3

PAD_MULT = 256
D_FF = ((-(-8 * D_MODEL // 3)) + 255) // 256 * 256
PLE_DIM = 256

IN_SPLITS = ((NSA_HEADS * HEAD_DIM,) + (NSA_GROUPS * HEAD_DIM,) * 6 + (3 * NSA_HEADS,)
             + (MOBA_HEADS * HEAD_DIM,) * 3 + (D_MODEL, D_MODEL))
IN_CUTS = tuple(int(c) for c in np.cumsum(IN_SPLITS)[:-1])

LANES = 128
SUBLANES = 8
NEG_BIG = -1e30
VMEM_LIMIT = 56 * 1024 * 1024

Q_WIDTH = NSA_HEADS * HEAD_DIM
KV_WIDTH = NSA_GROUPS * HEAD_DIM
GATE_ROWS = 32
TOK_NORMED = Q_WIDTH + 2 * KV_WIDTH
TOK_COLS = TOK_NORMED + 2 * KV_WIDTH
FEAT_ROWS = 3 * Q_WIDTH + 2 * KV_WIDTH + GATE_ROWS

PROJ_TM = MOBA_BLOCK
NSA_TQ = 128
NSA_TK = 256
WIN_TK = 128
MERGE_TM = 256
FFN_TM = 256
FFN_CHUNK = D_FF // 2
SEL_BLOCKS_MAX = 32
MOBA_BLOCKS_MAX = 8


def _dot(a, b):
    return jnp.dot(a, b, preferred_element_type=F32)


def _dot_nt(a, b):
    return lax.dot_general(a, b, (((1,), (1,)), ((), ())), preferred_element_type=F32)


def _split_bf16(a_f32):
    hi = a_f32.astype(BF16)
    return hi, (a_f32 - hi.astype(F32)).astype(BF16)


def _sigmoid(x):
    return 1.0 / (1.0 + jnp.exp(-x))


def _rms_rows(x, g):
    return x * lax.rsqrt(jnp.mean(x * x, axis=-1, keepdims=True) + NORM_EPS) * g


def _head_norm(t, bd, gain):
    hi, lo = _split_bf16(t * t)
    ss = _dot(hi, bd) + _dot(lo, bd)
    return t * lax.rsqrt(ss * (1.0 / HEAD_DIM) + NORM_EPS) * gain


def _rope_tables(pos_col, invf):
    ang = pos_col.astype(F32) * invf
    cos_a, sin_a = jnp.cos(ang), jnp.sin(ang)
    d = lax.broadcasted_iota(jnp.int32, ang.shape, 1) & (HEAD_DIM - 1)
    s_lo = jnp.where(d < ROPE_HALF, -sin_a, 0.0)
    s_hi = jnp.where((d >= ROPE_HALF) & (d < ROPE_DIM), sin_a, 0.0)
    return cos_a, s_lo, s_hi


def _rope(y, tables):
    cos_a, s_lo, s_hi = tables
    w = y.shape[1]
    return (y * cos_a + pltpu.roll(y, w - ROPE_HALF, 1) * s_lo
            + pltpu.roll(y, ROPE_HALF, 1) * s_hi)


def _online_update(m, l, acc, x, v_rows):
    m_new = jnp.maximum(m, jnp.max(x, axis=0, keepdims=True))
    alpha = jnp.exp(m - m_new)
    e = jnp.exp(x - m_new)
    l = alpha * l + jnp.sum(e, axis=0, keepdims=True)
    acc = alpha * acc + _dot(v_rows, e.astype(BF16))
    return m_new, l, acc


def _rank_rows(v, n_rows):
    j_idx = lax.broadcasted_iota(jnp.int32, v.shape, 0)
    rank = jnp.zeros(v.shape, F32)
    for jp in range(n_rows):
        row = v[jp:jp + 1, :]
        beats = (row > v) | ((row == v) & (j_idx > jp))
        rank = rank + jnp.where(beats, 1.0, 0.0)
    return rank


def _proj_kernel(x_ref, posc_ref, posr_ref, gmix_ref, wtok_ref, wfeat_ref, gtok_ref, gfeat_ref,
                 invf_ref, invf8_ref, bd_ref,
                 qn_ref, qm_ref, gn_ref, vm_ref, vs_ref, vw_ref, km_ref, ks_ref, kw_ref,
                 kc_ref, vc_ref):
    h = _rms_rows(x_ref[...], gmix_ref[...]).astype(BF16)

    acc = _dot(h, wtok_ref[...])
    tables = _rope_tables(posc_ref[...], invf_ref[...])
    bd = bd_ref[...]
    for c in range(TOK_NORMED // 256):
        t = acc[:, c * 256:(c + 1) * 256]
        y = _rope(_head_norm(t, bd, gtok_ref[:, c * 256:(c + 1) * 256]), tables).astype(BF16)
        if c < 2:
            km_ref[:, c * 256:(c + 1) * 256] = y
        else:
            ks_ref[...] = y[:, :KV_WIDTH]
            kw_ref[...] = y[:, KV_WIDTH:]
    kc_ref[...] = acc[:, TOK_NORMED:TOK_NORMED + KV_WIDTH]
    vc_ref[...] = acc[:, TOK_NORMED + KV_WIDTH:]

    acc_t = _dot_nt(wfeat_ref[...], h)
    ang = invf8_ref[...] * posr_ref[0].astype(F32)
    cos_a, sin_a = jnp.cos(ang), jnp.sin(ang)
    for hh in range(2 * NSA_HEADS):
        t = acc_t[hh * HEAD_DIM:(hh + 1) * HEAD_DIM, :]
        ss = jnp.sum(t * t, axis=0, keepdims=True)
        y = t * lax.rsqrt(ss * (1.0 / HEAD_DIM) + NORM_EPS) * gfeat_ref[hh * HEAD_DIM:(hh + 1) * HEAD_DIM, :]
        a, b = y[:ROPE_HALF], y[ROPE_HALF:ROPE_DIM]
        y = jnp.concatenate([a * cos_a - b * sin_a, b * cos_a + a * sin_a, y[ROPE_DIM:]], axis=0)
        dst = qn_ref if hh < NSA_HEADS else qm_ref
        r0 = (hh % NSA_HEADS) * HEAD_DIM
        dst[0, r0:r0 + HEAD_DIM, :] = y.astype(BF16)
    o = 2 * Q_WIDTH
    vm_ref[0] = acc_t[o:o + Q_WIDTH].astype(BF16)
    vs_ref[0] = acc_t[o + Q_WIDTH:o + Q_WIDTH + KV_WIDTH].astype(BF16)
    vw = acc_t[o + Q_WIDTH + KV_WIDTH:o + Q_WIDTH + 2 * KV_WIDTH].astype(BF16)
    for j in range(PROJ_TM // WIN_TK):
        vw_ref[0, j] = vw[:, j * WIN_TK:(j + 1) * WIN_TK]
    gn_ref[0] = acc_t[o + Q_WIDTH + 2 * KV_WIDTH:]


def _proj_call(x2, posc, posr, gmix, wtok, wfeat, gtok, gfeat, invf, invf8, bd, b, s):
    t = x2.shape[0]
    tm = PROJ_TM
    nt = s // tm
    row = lambda w_: pl.BlockSpec((tm, w_), lambda i: (i, 0))
    full = lambda a: pl.BlockSpec(a.shape, lambda i: (0,) * a.ndim)
    feat = lambda r: pl.BlockSpec((1, r, tm), lambda i: (i // nt, 0, i % nt))
    tile = lambda r: pl.BlockSpec((1, r, tm), lambda i: (i, 0, 0))
    out_shape = [
        jax.ShapeDtypeStruct((b, Q_WIDTH, s), BF16),
        jax.ShapeDtypeStruct((b, Q_WIDTH, s), BF16),
        jax.ShapeDtypeStruct((b, GATE_ROWS, s), F32),
        jax.ShapeDtypeStruct((t // tm, Q_WIDTH, tm), BF16),
        jax.ShapeDtypeStruct((t // tm, KV_WIDTH, tm), BF16),
        jax.ShapeDtypeStruct((t // tm, tm // WIN_TK, KV_WIDTH, WIN_TK), BF16),
        jax.ShapeDtypeStruct((t, Q_WIDTH), BF16),
        jax.ShapeDtypeStruct((t, KV_WIDTH), BF16),
        jax.ShapeDtypeStruct((t, KV_WIDTH), BF16),
        jax.ShapeDtypeStruct((t, KV_WIDTH), F32),
        jax.ShapeDtypeStruct((t, KV_WIDTH), F32),
    ]
    out_specs = [feat(Q_WIDTH), feat(Q_WIDTH), feat(GATE_ROWS), tile(Q_WIDTH), tile(KV_WIDTH),
                 pl.BlockSpec((1, tm // WIN_TK, KV_WIDTH, WIN_TK), lambda i: (i, 0, 0, 0)),
                 row(Q_WIDTH), row(KV_WIDTH), row(KV_WIDTH), row(KV_WIDTH), row(KV_WIDTH)]
    return pl.pallas_call(
        _proj_kernel,
        grid=(t // tm,),
        in_specs=[row(D_MODEL), row(1), pl.BlockSpec((1, 1, tm), lambda i: (i, 0, 0)), full(gmix),
                  full(wtok), full(wfeat), full(gtok), full(gfeat), full(invf), full(invf8), full(bd)],
        out_specs=out_specs,
        out_shape=out_shape,
        compiler_params=pltpu.CompilerParams(dimension_semantics=("parallel",),
                                             vmem_limit_bytes=VMEM_LIMIT),
    )(x2, posc, posr, gmix, wtok, wfeat, gtok, gfeat, invf, invf8, bd)


def _compress_kernel(kc4_ref, vc4_ref, pe_ref, wk_ref, wv_ref, w2k_ref, w2v_ref, gain_ref,
                     pose_ref, invf_ref, bd_ref, kc_out, vct_out):
    half = NSA_GROUPS * CMP_HIDDEN

    def comp(x, pe_a, pe_b, w_ref, w2_ref):
        a = _dot((x + pe_a).astype(BF16), w_ref[:, :half])
        b = _dot((x + pe_b).astype(BF16), w_ref[:, half:])
        hid = a + pltpu.roll(b, b.shape[0] - 1, 0)
        act = hid * _sigmoid(hid)
        return _dot(act.astype(BF16), w2_ref[...])

    kc = comp(kc4_ref[0], pe_ref[0:1, :], pe_ref[1:2, :], wk_ref, w2k_ref)
    vc = comp(vc4_ref[0], pe_ref[2:3, :], pe_ref[3:4, :], wv_ref, w2v_ref)
    kc = _head_norm(kc, bd_ref[...], gain_ref[...])
    kc = _rope(kc, _rope_tables(pose_ref[0], invf_ref[...]))
    kc_out[0] = kc.astype(BF16)
    vct_out[0] = vc.T.astype(BF16)


def _compress_call(kc4, vc4, pe4, wk, wv, w2k, w2v, gain, pos_end, invf, bd):
    b, n_sub, width = kc4.shape
    blk = lambda shp: pl.BlockSpec((1,) + shp, lambda i: (i, 0, 0))
    full = lambda a: pl.BlockSpec(a.shape, lambda i: (0,) * a.ndim)
    return pl.pallas_call(
        _compress_kernel,
        grid=(b,),
        in_specs=[blk((n_sub, width)), blk((n_sub, width)), full(pe4), full(wk), full(wv),
                  full(w2k), full(w2v), full(gain), blk((n_sub, 1)), full(invf), full(bd)],
        out_specs=[blk((n_sub, KV_WIDTH)), blk((KV_WIDTH, n_sub))],
        out_shape=[jax.ShapeDtypeStruct((b, n_sub, KV_WIDTH), BF16),
                   jax.ShapeDtypeStruct((b, KV_WIDTH, n_sub), BF16)],
        compiler_params=pltpu.CompilerParams(dimension_semantics=("parallel",),
                                             vmem_limit_bytes=VMEM_LIMIT),
    )(kc4, vc4, pe4, wk, wv, w2k, w2v, gain, pos_end, invf, bd)


def _nsa_kernel(q_ref, kc_ref, vct_ref, ks_ref, vs_ref, kw_ref, vw_ref, gate_ref, ovl_ref, out_ref,
                sel_scr):
    tq = q_ref.shape[2]
    n_cmp = kc_ref.shape[1]
    n_slots = NSA_GROUPS * NSA_REP
    i = pl.program_id(1)
    t0 = i * tq
    t_row = t0 + lax.broadcasted_iota(jnp.int32, (1, tq), 1)

    zero = jnp.zeros((HEAD_DIM, tq), BF16)
    slots = []
    for g in range(NSA_GROUPS):
        for r in range(NSA_REP):
            rows = q_ref[0, r * KV_WIDTH + g * HEAD_DIM:r * KV_WIDTH + (g + 1) * HEAD_DIM, :]
            slots.append(jnp.concatenate([rows, zero] if g == 0 else [zero, rows], axis=0))
    lhs_t = jnp.concatenate(slots, axis=1)

    def slot_cols(a, hs):
        return a[:, hs * tq:(hs + 1) * tq]

    c_idx = lax.broadcasted_iota(jnp.int32, (n_cmp, tq), 0)
    valid_c = (c_idx * CMP_STRIDE + (CMP_LEN - 1) <= t_row) & (c_idx < n_cmp - 1)
    bias_c = jnp.where(valid_c, 0.0, NEG_BIG)
    s_c = _dot(kc_ref[0], lhs_t)
    o_c, p_sum = [], []
    for hs in range(n_slots):
        g = hs // NSA_REP
        x = slot_cols(s_c, hs) + bias_c
        m = jnp.max(x, axis=0, keepdims=True)
        e = jnp.where(valid_c, jnp.exp(x - m), 0.0)
        p = e / jnp.maximum(jnp.sum(e, axis=0, keepdims=True), 1e-30)
        o_c.append(_dot(vct_ref[0, g * HEAD_DIM:(g + 1) * HEAD_DIM, :], p.astype(BF16)))
        if hs % NSA_REP == 0:
            p_sum.append(p)
        else:
            p_sum[g] = p_sum[g] + p

    cur = t_row // SEL_BLOCK
    j_idx = lax.broadcasted_iota(jnp.int32, (SEL_BLOCKS_MAX, tq), 0)
    forced = (j_idx == 0) | (j_idx == cur) | (j_idx == cur - 1)
    for g in range(NSA_GROUPS):
        hi, lo = _split_bf16(p_sum[g])
        imp = _dot(ovl_ref[...], hi) + _dot(ovl_ref[...], lo)
        imp = jnp.where(forced, jnp.inf, jnp.where(j_idx > cur, -jnp.inf, imp))
        sel_scr[g] = jnp.where(_rank_rows(imp, SEL_BLOCKS_MAX) < SEL_TOPN, 1.0, 0.0)

    def init():
        return tuple((jnp.full((1, tq), NEG_BIG, F32), jnp.zeros((1, tq), F32),
                      jnp.zeros((HEAD_DIM, tq), F32)) for _ in range(n_slots))

    def update_slots(carry, s, bias, v_tile):
        out = []
        for hs in range(n_slots):
            g = hs // NSA_REP
            m, l, acc = carry[hs]
            out.append(_online_update(m, l, acc, slot_cols(s, hs) + bias[g],
                                      v_tile[g * HEAD_DIM:(g + 1) * HEAD_DIM, :]))
        return tuple(out)

    blocks_per_tile = NSA_TK // SEL_BLOCK

    def sel_body(kt, carry):
        k0 = pl.multiple_of(kt * NSA_TK, NSA_TK)
        s = _dot(ks_ref[0, pl.ds(k0, NSA_TK), :], lhs_t)
        bias = []
        for g in range(NSA_GROUPS):
            pieces = []
            for a in range(blocks_per_tile):
                kpos = k0 + a * SEL_BLOCK + lax.broadcasted_iota(jnp.int32, (SEL_BLOCK, tq), 0)
                picked = sel_scr[g, pl.ds(kt * blocks_per_tile + a, 1), :] > 0.5
                pieces.append(jnp.where((kpos <= t_row) & picked, 0.0, NEG_BIG))
            bias.append(jnp.concatenate(pieces, axis=0))
        return update_slots(carry, s, bias, vs_ref[0, kt])

    n_sel_tiles = (t0 + tq + NSA_TK - 1) // NSA_TK
    res_s = lax.fori_loop(0, n_sel_tiles, sel_body, init())

    def win_body(kt, carry):
        k0 = pl.multiple_of(kt * WIN_TK, WIN_TK)
        s = _dot(kw_ref[0, pl.ds(k0, WIN_TK), :], lhs_t)
        diff = t_row - (k0 + lax.broadcasted_iota(jnp.int32, (WIN_TK, tq), 0))
        bias = jnp.where((diff >= 0) & (diff < WINDOW), 0.0, NEG_BIG)
        return update_slots(carry, s, [bias] * NSA_GROUPS, vw_ref[0, kt])

    first_win = jnp.maximum((t0 - (WINDOW - 1)) // WIN_TK, 0)
    last_win = (t0 + tq - 1) // WIN_TK
    res_w = lax.fori_loop(first_win, last_win + 1, win_body, init())

    gsig = _sigmoid(gate_ref[0])
    o_t = []
    for hs in range(n_slots):
        c = 3 * hs
        _, l_s, acc_s = res_s[hs]
        _, l_w, acc_w = res_w[hs]
        o_t.append(o_c[hs] * gsig[c:c + 1, :]
                   + acc_s * (gsig[c + 1:c + 2, :] / jnp.maximum(l_s, 1e-30))
                   + acc_w * (gsig[c + 2:c + 3, :] / jnp.maximum(l_w, 1e-30)))
    for r in range(NSA_REP):
        tile_t = jnp.concatenate([o_t[r], o_t[NSA_REP + r]], axis=0)
        out_ref[0, :, r * KV_WIDTH:(r + 1) * KV_WIDTH] = tile_t.T.astype(BF16)


def _nsa_call(qn, kc, vct, ks, vs, kw, vw, gn, ovl):
    b, _, s = qn.shape
    tq = NSA_TQ
    assert s // SEL_BLOCK <= SEL_BLOCKS_MAX
    seq = lambda a: pl.BlockSpec((1,) + a.shape[1:], lambda bi, i: (bi,) + (0,) * (a.ndim - 1))
    full = lambda a: pl.BlockSpec(a.shape, lambda bi, i: (0,) * a.ndim)
    return pl.pallas_call(
        _nsa_kernel,
        grid=(b, s // tq),
        in_specs=[pl.BlockSpec((1, Q_WIDTH, tq), lambda bi, i: (bi, 0, i)),
                  seq(kc), seq(vct), seq(ks), seq(vs), seq(kw), seq(vw),
                  pl.BlockSpec((1, GATE_ROWS, tq), lambda bi, i: (bi, 0, i)), full(ovl)],
        out_specs=pl.BlockSpec((1, tq, Q_WIDTH), lambda bi, i: (bi, i, 0)),
        out_shape=jax.ShapeDtypeStruct((b, s, Q_WIDTH), BF16),
        scratch_shapes=[pltpu.VMEM((NSA_GROUPS, SEL_BLOCKS_MAX, tq), F32)],
        compiler_params=pltpu.CompilerParams(dimension_semantics=("parallel", "parallel"),
                                             vmem_limit_bytes=VMEM_LIMIT),
    )(qn, kc, vct, ks, vs, kw, vw, gn, ovl)


def _moba_kernel(q_ref, k_ref, v_ref, out_ref, km_scr, sel_scr):
    bs = MOBA_BLOCK
    n_blocks = k_ref.shape[1] // bs
    i = pl.program_id(2)

    @pl.when(i == 0)
    def _():
        km_scr[...] = jnp.zeros(km_scr.shape, F32)
        for n in range(n_blocks):
            km_scr[n:n + 1, :] = jnp.mean(k_ref[0, n * bs:(n + 1) * bs, :].astype(F32), axis=0,
                                          keepdims=True)

    zero = jnp.zeros((HEAD_DIM, bs), BF16)
    lhs_t = jnp.concatenate(
        [jnp.concatenate([q_ref[0, :HEAD_DIM, :], zero], axis=0),
         jnp.concatenate([zero, q_ref[0, HEAD_DIM:, :]], axis=0)], axis=1)

    km_hi, km_lo = _split_bf16(km_scr[...])
    sc = _dot(km_hi, lhs_t) + _dot(km_lo, lhs_t)
    n_idx = lax.broadcasted_iota(jnp.int32, sc.shape, 0)
    sc = jnp.where(n_idx < i, sc, -jnp.inf)
    sel_scr[...] = jnp.where((_rank_rows(sc, MOBA_BLOCKS_MAX) < MOBA_TOPK) & (n_idx < i), 1.0, 0.0)

    def update_slots(carry, s, bias, v_tile):
        out = []
        for hs in range(2):
            m, l, acc = carry[hs]
            out.append(_online_update(m, l, acc, s[:, hs * bs:(hs + 1) * bs] + bias[hs],
                                      v_tile[hs * HEAD_DIM:(hs + 1) * HEAD_DIM, :]))
        return tuple(out)

    k0 = pl.multiple_of(i * bs, bs)
    key_i = lax.broadcasted_iota(jnp.int32, (bs, bs), 0)
    qry_i = lax.broadcasted_iota(jnp.int32, (bs, bs), 1)
    bias_own = jnp.where(key_i <= qry_i, 0.0, NEG_BIG)
    init = tuple((jnp.full((1, bs), NEG_BIG, F32), jnp.zeros((1, bs), F32),
                  jnp.zeros((HEAD_DIM, bs), F32)) for _ in range(2))
    carry = update_slots(init, _dot(k_ref[0, pl.ds(k0, bs), :], lhs_t), [bias_own, bias_own],
                         v_ref[0, i])

    def past_body(n, carry):
        kn = pl.multiple_of(n * bs, bs)
        s = _dot(k_ref[0, pl.ds(kn, bs), :], lhs_t)
        bias_row = (sel_scr[pl.ds(n, 1), :] - 1.0) * (-NEG_BIG)
        return update_slots(carry, s, [bias_row[:, :bs], bias_row[:, bs:]], v_ref[0, n])

    res = lax.fori_loop(0, i, past_body, carry)
    o_t = jnp.concatenate([res[hs][2] / jnp.maximum(res[hs][1], 1e-30) for hs in range(2)], axis=0)
    out_ref[0] = o_t.T.astype(BF16)


def _moba_call(qm, km, vm):
    b, w, s = qm.shape
    bs = MOBA_BLOCK
    nb = s // bs
    assert nb <= MOBA_BLOCKS_MAX
    return pl.pallas_call(
        _moba_kernel,
        grid=(b, w // LANES, nb),
        in_specs=[pl.BlockSpec((1, LANES, bs), lambda bi, p, i: (bi, p, i)),
                  pl.BlockSpec((1, s, LANES), lambda bi, p, i: (bi, 0, p)),
                  pl.BlockSpec((1, nb, LANES, bs), lambda bi, p, i: (bi, 0, p, 0))],
        out_specs=pl.BlockSpec((1, bs, LANES), lambda bi, p, i: (bi, i, p)),
        out_shape=jax.ShapeDtypeStruct((b, s, w), BF16),
        scratch_shapes=[pltpu.VMEM((MOBA_BLOCKS_MAX, LANES), F32),
                        pltpu.VMEM((MOBA_BLOCKS_MAX, 2 * bs), F32)],
        compiler_params=pltpu.CompilerParams(
            dimension_semantics=("parallel", "parallel", "arbitrary"), vmem_limit_bytes=VMEM_LIMIT),
    )(qm, km, vm)


def _merge_kernel(x_ref, yn_ref, ym_ref, gmix_ref, wg_ref, wun_ref, wum_ref, wo_ref, out_ref):
    x = x_ref[...]
    h = _rms_rows(x, gmix_ref[...]).astype(BF16)
    ga = _sigmoid(_dot(h, wg_ref[:, :D_MODEL]))
    gb = _sigmoid(_dot(h, wg_ref[:, D_MODEL:]))
    merged = ga * _dot(yn_ref[...], wun_ref[...]) + gb * _dot(ym_ref[...], wum_ref[...])
    out_ref[...] = x + _dot(merged.astype(BF16), wo_ref[...])


def _merge_call(x2, yn, ym, gmix, wg, wun, wum, wo):
    t = x2.shape[0]
    tm = MERGE_TM
    row = lambda w_: pl.BlockSpec((tm, w_), lambda i: (i, 0))
    full = lambda a: pl.BlockSpec(a.shape, lambda i: (0, 0))
    return pl.pallas_call(
        _merge_kernel,
        grid=(t // tm,),
        in_specs=[row(D_MODEL), row(Q_WIDTH), row(Q_WIDTH), full(gmix), full(wg), full(wun),
                  full(wum), full(wo)],
        out_specs=row(D_MODEL),
        out_shape=jax.ShapeDtypeStruct((t, D_MODEL), F32),
        compiler_params=pltpu.CompilerParams(dimension_semantics=("parallel",),
                                             vmem_limit_bytes=VMEM_LIMIT),
    )(x2, yn, ym, gmix, wg, wun, wum, wo)


def _ffn_kernel(x_ref, p_ref, gffn_ref, wfi_ref, wfo_ref, gple_ref, wpg_ref, wpp_ref, out_ref):
    x = x_ref[...]
    h = _rms_rows(x, gffn_ref[...]).astype(BF16)
    y = x
    for c in range(D_FF // FFN_CHUNK):
        gate = _dot(h, wfi_ref[:, c * FFN_CHUNK:(c + 1) * FFN_CHUNK])
        up = _dot(h, wfi_ref[:, D_FF + c * FFN_CHUNK:D_FF + (c + 1) * FFN_CHUNK])
        act = (gate * _sigmoid(gate) * up).astype(BF16)
        y = y + _dot(act, wfo_ref[c * FFN_CHUNK:(c + 1) * FFN_CHUNK, :])
    h2 = _rms_rows(y, gple_ref[...]).astype(BF16)
    ple_gate = _sigmoid(_dot(h2, wpg_ref[...]))
    out_ref[...] = y + ple_gate * _dot(p_ref[...].astype(BF16), wpp_ref[...])


def _ffn_call(x1, p2, gffn, wfi, wfo, gple, wpg, wpp):
    t = x1.shape[0]
    tm = FFN_TM
    row = lambda w_: pl.BlockSpec((tm, w_), lambda i: (i, 0))
    full = lambda a: pl.BlockSpec(a.shape, lambda i: (0, 0))
    return pl.pallas_call(
        _ffn_kernel,
        grid=(t // tm,),
        in_specs=[row(D_MODEL), row(PLE_DIM), full(gffn), full(wfi), full(wfo), full(gple),
                  full(wpg), full(wpp)],
        out_specs=row(D_MODEL),
        out_shape=jax.ShapeDtypeStruct((t, D_MODEL), F32),
        compiler_params=pltpu.CompilerParams(dimension_semantics=("parallel",),
                                             vmem_limit_bytes=VMEM_LIMIT),
    )(x1, p2, gffn, wfi, wfo, gple, wpg, wpp)


def _block_diag_ones(width):
    idx = np.arange(width) // HEAD_DIM
    return jnp.asarray(idx[:, None] == idx[None, :], dtype=BF16)


def _inv_freq():
    return ROPE_THETA ** (-jnp.arange(ROPE_HALF, dtype=F32) / ROPE_HALF)


def _inv_freq_lanes(width):
    inv_freq = _inv_freq()
    per_head = jnp.concatenate([inv_freq, inv_freq, jnp.zeros((HEAD_DIM - ROPE_DIM,), F32)])
    return jnp.tile(per_head, width // HEAD_DIM)[None, :]


def _split_w_in(w):
    parts = jnp.split(w, IN_CUTS, axis=-1)
    (q_n, kc, vc, ks, vs, kw, vw, gate_n, q_m, k_m, v_m, gate_a, gate_b) = parts
    d = w.shape[0]
    q_n = q_n.reshape(d, NSA_GROUPS, NSA_REP, HEAD_DIM).transpose(0, 2, 1, 3).reshape(d, -1)
    gate_n = jnp.pad(gate_n, ((0, 0), (0, GATE_ROWS - gate_n.shape[1])))
    w_tok = jnp.concatenate([k_m, ks, kw, kc, vc], axis=1)
    w_feat = jnp.concatenate([q_n, q_m, v_m, vs, vw, gate_n], axis=1).T
    w_gate = jnp.concatenate([gate_a, gate_b], axis=1)
    return w_tok.astype(BF16), w_feat.astype(BF16), w_gate.astype(BF16)


def _compress_weights(w1, w2, pe):
    eye = jnp.eye(NSA_GROUPS, dtype=F32)
    halves = []
    for part in (w1[:CMP_STRIDE * HEAD_DIM], w1[CMP_STRIDE * HEAD_DIM:]):
        p3 = part.reshape(CMP_STRIDE, HEAD_DIM, CMP_HIDDEN)
        halves.append(jnp.einsum('idh,gk->igdkh', p3, eye)
                      .reshape(CMP_STRIDE * NSA_GROUPS * HEAD_DIM, NSA_GROUPS * CMP_HIDDEN))
    w_big = jnp.concatenate(halves, axis=1).astype(BF16)
    w2_bd = jnp.einsum('hd,kg->khgd', w2, eye).reshape(NSA_GROUPS * CMP_HIDDEN,
                                                        NSA_GROUPS * HEAD_DIM).astype(BF16)
    pe_rows = [jnp.broadcast_to(pe[a:a + CMP_STRIDE, None, :], (CMP_STRIDE, NSA_GROUPS, HEAD_DIM))
               .reshape(1, -1) for a in (0, CMP_STRIDE)]
    return w_big, w2_bd, pe_rows


def _overlap_matrix_t(n_cmp):
    c = np.arange(n_cmp)
    j = np.arange(SEL_BLOCKS_MAX)
    start, end = c * CMP_STRIDE, c * CMP_STRIDE + CMP_LEN - 1
    ov = (start[None, :] <= j[:, None] * SEL_BLOCK + SEL_BLOCK - 1) & (end[None, :] >= j[:, None] * SEL_BLOCK)
    return jnp.asarray(ov, dtype=BF16)


def kernel(x, p, positions, g_mix, w_in, nsa_q_gain, nsa_kc_gain, nsa_ks_gain, nsa_kw_gain, nsa_pe_k, nsa_pe_v, nsa_ck_w1, nsa_ck_w2, nsa_cv_w1, nsa_cv_w2, moba_q_gain, moba_k_gain, w_up_nsa, w_up_moba, w_out, g_ffn, w_ffn_in, w_ffn_out, g_ple, w_ple_gate, w_ple_proj):
    b, s, d = x.shape
    depth = w_in.shape[0]
    assert s % PAD_MULT == 0 and d == D_MODEL
    t = b * s
    n_sub = s // CMP_STRIDE

    posc = positions.reshape(t, 1)
    posr = positions.reshape(t // PROJ_TM, 1, PROJ_TM)
    pos_end = jnp.concatenate([positions[:, CMP_LEN - 1::CMP_STRIDE], positions[:, -1:]], axis=1)[:, :, None]
    invf256 = _inv_freq_lanes(256)
    invf8 = _inv_freq()[:, None]
    bd256, bd128 = _block_diag_ones(256), _block_diag_ones(LANES)
    ovl_t = _overlap_matrix_t(n_sub)
    tile = lambda g, n: jnp.tile(g, n)
    xi = x.reshape(t, d)

    for i in range(depth):
        w_tok, w_feat, w_gate = _split_w_in(w_in[i])
        g_tok = jnp.concatenate([tile(moba_k_gain[i], MOBA_HEADS), tile(nsa_ks_gain[i], NSA_GROUPS),
                                 tile(nsa_kw_gain[i], NSA_GROUPS)])[None, :]
        g_feat = (jnp.concatenate([tile(nsa_q_gain[i], NSA_HEADS), tile(moba_q_gain[i], MOBA_HEADS)])
                  * ATTN_SCALE)[:, None]
        gmix = g_mix[i][None, :]
        (qn, qm, gn, vm, vs, vw, km, ks, kw, kc_raw, vc_raw) = _proj_call(
            xi, posc, posr, gmix, w_tok, w_feat, g_tok, g_feat, invf256, invf8, bd256, b, s)

        wk_big, w2k_bd, pe_k = _compress_weights(nsa_ck_w1[i], nsa_ck_w2[i], nsa_pe_k[i])
        wv_big, w2v_bd, pe_v = _compress_weights(nsa_cv_w1[i], nsa_cv_w2[i], nsa_pe_v[i])
        pe4 = jnp.concatenate(pe_k + pe_v, axis=0)
        kc, vct = _compress_call(kc_raw.reshape(b, n_sub, CMP_STRIDE * KV_WIDTH),
                                 vc_raw.reshape(b, n_sub, CMP_STRIDE * KV_WIDTH),
                                 pe4, wk_big, wv_big, w2k_bd, w2v_bd,
                                 tile(nsa_kc_gain[i], NSA_GROUPS)[None, :], pos_end,
                                 invf256[:, :LANES], bd128)

        r3 = lambda a: a.reshape(b, s, a.shape[-1])
        y_nsa = _nsa_call(qn, kc, vct, r3(ks), vs.reshape(b, s // NSA_TK, KV_WIDTH, NSA_TK), r3(kw),
                          vw.reshape(b, s // WIN_TK, KV_WIDTH, WIN_TK), gn, ovl_t)
        y_moba = _moba_call(qm, r3(km), vm.reshape(b, s // MOBA_BLOCK, Q_WIDTH, MOBA_BLOCK))

        w_un = (w_up_nsa[i].reshape(NSA_GROUPS, NSA_REP, HEAD_DIM, d).transpose(1, 0, 2, 3)
                .reshape(NSA_HEADS * HEAD_DIM, d).astype(BF16))
        x1 = _merge_call(xi, y_nsa.reshape(t, -1), y_moba.reshape(t, -1), gmix, w_gate, w_un,
                         w_up_moba[i].astype(BF16), w_out[i].astype(BF16))
        xi = _ffn_call(x1, p[i].reshape(t, PLE_DIM), g_ffn[i][None, :], w_ffn_in[i].astype(BF16),
                       w_ffn_out[i].astype(BF16), g_ple[i][None, :], w_ple_gate[i].astype(BF16),
                       w_ple_proj[i].astype(BF16))
    return xi.reshape(b, s, d)
```

```python
import jax
import jax.numpy as jnp
import numpy as np
from jax import lax
from jax.experimental import pallas as pl
from jax.experimental.pallas import tpu as pltpu

F32 = jnp.float32
BF16 = jnp.bfloat16

D_MODEL = 1024
HEAD_DIM = 64
ROPE_DIM = HEAD_DIM // 4
ROPE_HALF = ROPE_DIM // 2
ROPE_THETA = 500000.0
NORM_EPS = 1e-6
ATTN_SCALE = HEAD_DIM ** -0.5

NSA_HEADS = 8
NSA_GROUPS = 2
NSA_REP = NSA_HEADS // NSA_GROUPS
CMP_LEN = 32
CMP_STRIDE = 16
CMP_HIDDEN = 256
SEL_BLOCK = 64
SEL_TOPN = 8
WINDOW = 512

MOBA_HEADS = 8
MOBA_BLOCK = 256
MOBA_TOPK = 3

PAD_MULT = 256
D_FF = ((-(-8 * D_MODEL // 3)) + 255) // 256 * 256
PLE_DIM = 256

IN_SPLITS = ((NSA_HEADS * HEAD_DIM,) + (NSA_GROUPS * HEAD_DIM,) * 6 + (3 * NSA_HEADS,)
             + (MOBA_HEADS * HEAD_DIM,) * 3 + (D_MODEL, D_MODEL))
IN_CUTS = tuple(int(c) for c in np.cumsum(IN_SPLITS)[:-1])

LANES = 128
SUBLANES = 8
NEG_BIG = -1e30
LOG2E = 1.4426950408889634
SUM_ROWS = 16
VMEM_LIMIT = 56 * 1024 * 1024

Q_WIDTH = NSA_HEADS * HEAD_DIM
KV_WIDTH = NSA_GROUPS * HEAD_DIM
GATE_ROWS = 32
TOK_NORMED = Q_WIDTH + 2 * KV_WIDTH
TOK_COLS = TOK_NORMED + 2 * KV_WIDTH
FEAT_ROWS = 3 * Q_WIDTH + 2 * KV_WIDTH + GATE_ROWS

PROJ_TM = MOBA_BLOCK
NSA_TQ = 128
NSA_TK = 256
WIN_TK = 128
MERGE_TM = 256
FFN_TM = 256
FFN_CHUNK = D_FF // 2
SEL_BLOCKS_MAX = 32
MOBA_BLOCKS_MAX = 8


def _dot(a, b):
    return jnp.dot(a, b, preferred_element_type=F32)


def _dot_nt(a, b):
    return lax.dot_general(a, b, (((1,), (1,)), ((), ())), preferred_element_type=F32)


def _split_bf16(a_f32):
    hi = a_f32.astype(BF16)
    return hi, (a_f32 - hi.astype(F32)).astype(BF16)


def _sigmoid(x):
    return 1.0 / (1.0 + jnp.exp(-x))


def _rms_rows(x, g):
    return x * lax.rsqrt(jnp.mean(x * x, axis=-1, keepdims=True) + NORM_EPS) * g


def _head_norm(t, bd, gain):
    hi, lo = _split_bf16(t * t)
    ss = _dot(hi, bd) + _dot(lo, bd)
    return t * lax.rsqrt(ss * (1.0 / HEAD_DIM) + NORM_EPS) * gain


def _rope_tables(pos_col, invf):
    ang = pos_col.astype(F32) * invf
    cos_a, sin_a = jnp.cos(ang), jnp.sin(ang)
    d = lax.broadcasted_iota(jnp.int32, ang.shape, 1) & (HEAD_DIM - 1)
    s_lo = jnp.where(d < ROPE_HALF, -sin_a, 0.0)
    s_hi = jnp.where((d >= ROPE_HALF) & (d < ROPE_DIM), sin_a, 0.0)
    return cos_a, s_lo, s_hi


def _rope(y, tables):
    cos_a, s_lo, s_hi = tables
    w = y.shape[1]
    return (y * cos_a + pltpu.roll(y, w - ROPE_HALF, 1) * s_lo
            + pltpu.roll(y, ROPE_HALF, 1) * s_hi)


def _ones_rows(tk):
    r = lax.broadcasted_iota(jnp.int32, (SUM_ROWS, tk), 0)
    return jnp.where(r == 0, 1.0, 0.0).astype(BF16)


def _with_sum_row(v_rows, ones_rows):
    return jnp.concatenate([v_rows, ones_rows], axis=0)


def _slot_update(load_s, bias, block_rows, m_ref, acc_ref, hs, v_aug):
    m = m_ref[hs, 0:1, :]
    n_sub = len(bias)
    if block_rows:
        dropped = [bias[a] < 0.0 for a in range(n_sub)]
        m_new = m
        for a in range(n_sub):
            top = jnp.max(load_s(a), axis=0, keepdims=True)
            m_new = jnp.maximum(m_new, jnp.where(dropped[a], NEG_BIG, top))
        e = [jnp.exp2(load_s(a) + jnp.where(dropped[a], NEG_BIG, -m_new)) for a in range(n_sub)]
    else:
        m_new = m
        for a in range(n_sub):
            m_new = jnp.maximum(m_new, jnp.max(load_s(a) + bias[a], axis=0, keepdims=True))
        e = [jnp.exp2(load_s(a) + bias[a] - m_new) for a in range(n_sub)]
    e = e[0] if n_sub == 1 else jnp.concatenate(e, axis=0)
    alpha = jnp.exp2(m - m_new)
    m_ref[hs, 0:1, :] = m_new
    acc_ref[hs] = alpha * acc_ref[hs] + _dot(v_aug, e.astype(BF16))


def _reset_state(m_ref, acc_ref):
    m_ref[...] = jnp.full(m_ref.shape, NEG_BIG, F32)
    acc_ref[...] = jnp.zeros(acc_ref.shape, F32)


def _normalised(acc, scale_row=1.0):
    return acc[:HEAD_DIM] * (scale_row / jnp.maximum(acc[HEAD_DIM:HEAD_DIM + 1], 1e-30))


def _rank_rows(v, n_rows):
    j_idx = lax.broadcasted_iota(jnp.int32, v.shape, 0)
    rank = jnp.zeros(v.shape, F32)
    for jp in range(n_rows):
        row = v[jp:jp + 1, :]
        beats = (row > v) | ((row == v) & (j_idx > jp))
        rank = rank + jnp.where(beats, 1.0, 0.0)
    return rank


def _proj_kernel(x_ref, posc_ref, posr_ref, gmix_ref, wtok_ref, wfeat_ref, gtok_ref, gfeat_ref,
                 invf_ref, invf8_ref, bd_ref,
                 qn_ref, qm_ref, gn_ref, vm_ref, vs_ref, vw_ref, km_ref, ks_ref, kw_ref,
                 kc_ref, vc_ref):
    h = _rms_rows(x_ref[...], gmix_ref[...]).astype(BF16)

    acc = _dot(h, wtok_ref[...])
    tables = _rope_tables(posc_ref[...], invf_ref[...])
    bd = bd_ref[...]
    for c in range(TOK_NORMED // 256):
        t = acc[:, c * 256:(c + 1) * 256]
        y = _rope(_head_norm(t, bd, gtok_ref[:, c * 256:(c + 1) * 256]), tables).astype(BF16)
        if c < 2:
            km_ref[:, c * 256:(c + 1) * 256] = y
        else:
            ks_ref[...] = y[:, :KV_WIDTH]
            kw_ref[...] = y[:, KV_WIDTH:]
    kc_ref[...] = acc[:, TOK_NORMED:TOK_NORMED + KV_WIDTH]
    vc_ref[...] = acc[:, TOK_NORMED + KV_WIDTH:]

    acc_t = _dot_nt(wfeat_ref[...], h)
    ang = invf8_ref[...] * posr_ref[0].astype(F32)
    cos_a, sin_a = jnp.cos(ang), jnp.sin(ang)
    for hh in range(2 * NSA_HEADS):
        t = acc_t[hh * HEAD_DIM:(hh + 1) * HEAD_DIM, :]
        ss = jnp.sum(t * t, axis=0, keepdims=True)
        y = t * lax.rsqrt(ss * (1.0 / HEAD_DIM) + NORM_EPS) * gfeat_ref[hh * HEAD_DIM:(hh + 1) * HEAD_DIM, :]
        a, b = y[:ROPE_HALF], y[ROPE_HALF:ROPE_DIM]
        y = jnp.concatenate([a * cos_a - b * sin_a, b * cos_a + a * sin_a, y[ROPE_DIM:]], axis=0)
        dst = qn_ref if hh < NSA_HEADS else qm_ref
        r0 = (hh % NSA_HEADS) * HEAD_DIM
        dst[0, r0:r0 + HEAD_DIM, :] = y.astype(BF16)
    o = 2 * Q_WIDTH
    vm_ref[0] = acc_t[o:o + Q_WIDTH].astype(BF16)
    vs_ref[0] = acc_t[o + Q_WIDTH:o + Q_WIDTH + KV_WIDTH].astype(BF16)
    vw = acc_t[o + Q_WIDTH + KV_WIDTH:o + Q_WIDTH + 2 * KV_WIDTH].astype(BF16)
    for j in range(PROJ_TM // WIN_TK):
        vw_ref[0, j] = vw[:, j * WIN_TK:(j + 1) * WIN_TK]
    gn_ref[0] = acc_t[o + Q_WIDTH + 2 * KV_WIDTH:]


def _proj_call(x2, posc, posr, gmix, wtok, wfeat, gtok, gfeat, invf, invf8, bd, b, s):
    t = x2.shape[0]
    tm = PROJ_TM
    nt = s // tm
    row = lambda w_: pl.BlockSpec((tm, w_), lambda i: (i, 0))
    full = lambda a: pl.BlockSpec(a.shape, lambda i: (0,) * a.ndim)
    feat = lambda r: pl.BlockSpec((1, r, tm), lambda i: (i // nt, 0, i % nt))
    tile = lambda r: pl.BlockSpec((1, r, tm), lambda i: (i, 0, 0))
    out_shape = [
        jax.ShapeDtypeStruct((b, Q_WIDTH, s), BF16),
        jax.ShapeDtypeStruct((b, Q_WIDTH, s), BF16),
        jax.ShapeDtypeStruct((b, GATE_ROWS, s), F32),
        jax.ShapeDtypeStruct((t // tm, Q_WIDTH, tm), BF16),
        jax.ShapeDtypeStruct((t // tm, KV_WIDTH, tm), BF16),
        jax.ShapeDtypeStruct((t // tm, tm // WIN_TK, KV_WIDTH, WIN_TK), BF16),
        jax.ShapeDtypeStruct((t, Q_WIDTH), BF16),
        jax.ShapeDtypeStruct((t, KV_WIDTH), BF16),
        jax.ShapeDtypeStruct((t, KV_WIDTH), BF16),
        jax.ShapeDtypeStruct((t, KV_WIDTH), F32),
        jax.ShapeDtypeStruct((t, KV_WIDTH), F32),
    ]
    out_specs = [feat(Q_WIDTH), feat(Q_WIDTH), feat(GATE_ROWS), tile(Q_WIDTH), tile(KV_WIDTH),
                 pl.BlockSpec((1, tm // WIN_TK, KV_WIDTH, WIN_TK), lambda i: (i, 0, 0, 0)),
                 row(Q_WIDTH), row(KV_WIDTH), row(KV_WIDTH), row(KV_WIDTH), row(KV_WIDTH)]
    return pl.pallas_call(
        _proj_kernel,
        grid=(t // tm,),
        in_specs=[row(D_MODEL), row(1), pl.BlockSpec((1, 1, tm), lambda i: (i, 0, 0)), full(gmix),
                  full(wtok), full(wfeat), full(gtok), full(gfeat), full(invf), full(invf8), full(bd)],
        out_specs=out_specs,
        out_shape=out_shape,
        compiler_params=pltpu.CompilerParams(dimension_semantics=("parallel",),
                                             vmem_limit_bytes=VMEM_LIMIT),
    )(x2, posc, posr, gmix, wtok, wfeat, gtok, gfeat, invf, invf8, bd)


def _compress_kernel(kc4_ref, vc4_ref, pe_ref, wk_ref, wv_ref, w2k_ref, w2v_ref, gain_ref,
                     pose_ref, invf_ref, bd_ref, kc_out, vct_out):
    half = NSA_GROUPS * CMP_HIDDEN

    def comp(x, pe_a, pe_b, w_ref, w2_ref):
        a = _dot((x + pe_a).astype(BF16), w_ref[:, :half])
        b = _dot((x + pe_b).astype(BF16), w_ref[:, half:])
        hid = a + pltpu.roll(b, b.shape[0] - 1, 0)
        act = hid * _sigmoid(hid)
        return _dot(act.astype(BF16), w2_ref[...])

    kc = comp(kc4_ref[0], pe_ref[0:1, :], pe_ref[1:2, :], wk_ref, w2k_ref)
    vc = comp(vc4_ref[0], pe_ref[2:3, :], pe_ref[3:4, :], wv_ref, w2v_ref)
    kc = _head_norm(kc, bd_ref[...], gain_ref[...])
    kc = _rope(kc, _rope_tables(pose_ref[0], invf_ref[...]))
    kc_out[0] = kc.astype(BF16)
    vct_out[0] = vc.T.astype(BF16)


def _compress_call(kc4, vc4, pe4, wk, wv, w2k, w2v, gain, pos_end, invf, bd):
    b, n_sub, width = kc4.shape
    blk = lambda shp: pl.BlockSpec((1,) + shp, lambda i: (i, 0, 0))
    full = lambda a: pl.BlockSpec(a.shape, lambda i: (0,) * a.ndim)
    return pl.pallas_call(
        _compress_kernel,
        grid=(b,),
        in_specs=[blk((n_sub, width)), blk((n_sub, width)), full(pe4), full(wk), full(wv),
                  full(w2k), full(w2v), full(gain), blk((n_sub, 1)), full(invf), full(bd)],
        out_specs=[blk((n_sub, KV_WIDTH)), blk((KV_WIDTH, n_sub))],
        out_shape=[jax.ShapeDtypeStruct((b, n_sub, KV_WIDTH), BF16),
                   jax.ShapeDtypeStruct((b, KV_WIDTH, n_sub), BF16)],
        compiler_params=pltpu.CompilerParams(dimension_semantics=("parallel",),
                                             vmem_limit_bytes=VMEM_LIMIT),
    )(kc4, vc4, pe4, wk, wv, w2k, w2v, gain, pos_end, invf, bd)


def _nsa_kernel(q_ref, kc_ref, vct_ref, ks_ref, vs_ref, kw_ref, vw_ref, gate_ref, ovl_ref, out_ref,
                sel_scr, lhs_scr, s0_scr, s1_scr, s2_scr, s3_scr, m_scr, acc_scr, o_scr):
    tq = q_ref.shape[2]
    n_cmp = kc_ref.shape[1]
    n_slots = NSA_GROUPS * NSA_REP
    n_pairs = n_slots // 2
    s_scr = (s0_scr, s1_scr, s2_scr, s3_scr)
    i = pl.program_id(1)
    t0 = i * tq
    t_row = t0 + lax.broadcasted_iota(jnp.int32, (1, tq), 1)

    zero = jnp.zeros((HEAD_DIM, tq), BF16)
    for g in range(NSA_GROUPS):
        for r in range(NSA_REP):
            hs = g * NSA_REP + r
            rows = q_ref[0, r * KV_WIDTH + g * HEAD_DIM:r * KV_WIDTH + (g + 1) * HEAD_DIM, :]
            lhs_scr[:, hs * tq:(hs + 1) * tq] = jnp.concatenate([rows, zero] if g == 0 else [zero, rows],
                                                                axis=0)
    gsig = _sigmoid(gate_ref[0])

    c_idx = lax.broadcasted_iota(jnp.int32, (n_cmp, tq), 0)
    valid_c = (c_idx * CMP_STRIDE + (CMP_LEN - 1) <= t_row) & (c_idx < n_cmp - 1)
    bias_c = jnp.where(valid_c, 0.0, NEG_BIG)
    p_sum = []
    for hs in range(n_slots):
        g = hs // NSA_REP
        x = _dot(kc_ref[0], lhs_scr[:, hs * tq:(hs + 1) * tq]) + bias_c
        m = jnp.max(x, axis=0, keepdims=True)
        e = jnp.where(valid_c, jnp.exp2(x - m), 0.0)
        p = e / jnp.maximum(jnp.sum(e, axis=0, keepdims=True), 1e-30)
        o_c = _dot(vct_ref[0, g * HEAD_DIM:(g + 1) * HEAD_DIM, :], p.astype(BF16))
        o_scr[hs] = o_c * gsig[3 * hs:3 * hs + 1, :]
        if hs % NSA_REP == 0:
            p_sum.append(p)
        else:
            p_sum[g] = p_sum[g] + p

    cur = t_row // SEL_BLOCK
    j_idx = lax.broadcasted_iota(jnp.int32, (SEL_BLOCKS_MAX, tq), 0)
    forced = (j_idx == 0) | (j_idx == cur) | (j_idx == cur - 1)
    for g in range(NSA_GROUPS):
        hi, lo = _split_bf16(p_sum[g])
        imp = _dot(ovl_ref[...], hi) + _dot(ovl_ref[...], lo)
        imp = jnp.where(forced, jnp.inf, jnp.where(j_idx > cur, -jnp.inf, imp))
        sel_scr[g] = jnp.where(_rank_rows(imp, SEL_BLOCKS_MAX) < SEL_TOPN, 0.0, NEG_BIG)

    def scores_into(pair, k_tile):
        tk = k_tile.shape[0]
        s_scr[pair][0:tk, :] = _dot(k_tile, lhs_scr[:, pair * 2 * tq:(pair + 1) * 2 * tq])

    def run_tile(v_tile, bias_of_group, block_rows, sub, next_keys):
        ones_rows = _ones_rows(v_tile.shape[1])
        bias = [bias_of_group(g) for g in range(NSA_GROUPS)]
        for pair in range(n_pairs):
            g = (2 * pair) // NSA_REP
            v_aug = _with_sum_row(v_tile[g * HEAD_DIM:(g + 1) * HEAD_DIM, :], ones_rows)
            for hh in range(2):
                load = lambda a, hh=hh, pair=pair: s_scr[pair][a * sub:(a + 1) * sub, hh * tq:(hh + 1) * tq]
                _slot_update(load, bias[g], block_rows, m_scr, acc_scr, 2 * pair + hh, v_aug)
            if next_keys is not None:
                scores_into(pair, next_keys())

    def fold_branch(branch):
        for hs in range(n_slots):
            o_scr[hs] = o_scr[hs] + _normalised(acc_scr[hs], gsig[3 * hs + branch:3 * hs + branch + 1, :])

    blocks_per_tile = NSA_TK // SEL_BLOCK

    def sel_keys(kt):
        return ks_ref[0, pl.ds(pl.multiple_of(kt * NSA_TK, NSA_TK), NSA_TK), :]

    def sel_rows(kt, g):
        return [sel_scr[g, pl.ds(kt * blocks_per_tile + a, 1), :] for a in range(blocks_per_tile)]

    def sel_diag_bias(kt, g):
        out = []
        for a, row in enumerate(sel_rows(kt, g)):
            kpos = kt * NSA_TK + a * SEL_BLOCK + lax.broadcasted_iota(jnp.int32, (SEL_BLOCK, tq), 0)
            out.append(jnp.where(kpos <= t_row, row, NEG_BIG))
        return out

    def sel_body(kt, carry):
        run_tile(vs_ref[0, kt], lambda g: sel_rows(kt, g), True, SEL_BLOCK, lambda: sel_keys(kt + 1))
        return carry

    last_sel = (t0 + tq - 1) // NSA_TK
    _reset_state(m_scr, acc_scr)
    for pair in range(n_pairs):
        scores_into(pair, sel_keys(0))
    lax.fori_loop(0, last_sel, sel_body, 0)
    run_tile(vs_ref[0, last_sel], lambda g: sel_diag_bias(last_sel, g), False, SEL_BLOCK, None)
    fold_branch(1)

    def win_keys(kt):
        return kw_ref[0, pl.ds(pl.multiple_of(kt * WIN_TK, WIN_TK), WIN_TK), :]

    def win_bias(kt):
        diff = t_row - (kt * WIN_TK + lax.broadcasted_iota(jnp.int32, (WIN_TK, tq), 0))
        return [jnp.where((diff >= 0) & (diff < WINDOW), 0.0, NEG_BIG)]

    def win_body(kt, carry):
        run_tile(vw_ref[0, kt], lambda g: win_bias(kt), False, WIN_TK, lambda: win_keys(kt + 1))
        return carry

    first_win = jnp.maximum((t0 - (WINDOW - 1)) // WIN_TK, 0)
    last_win = (t0 + tq - 1) // WIN_TK
    _reset_state(m_scr, acc_scr)
    for pair in range(n_pairs):
        scores_into(pair, win_keys(first_win))
    lax.fori_loop(first_win, last_win, win_body, 0)
    run_tile(vw_ref[0, last_win], lambda g: win_bias(last_win), False, WIN_TK, None)
    fold_branch(2)

    for r in range(NSA_REP):
        tile_t = jnp.concatenate([o_scr[r], o_scr[NSA_REP + r]], axis=0)
        out_ref[0, :, r * KV_WIDTH:(r + 1) * KV_WIDTH] = tile_t.T.astype(BF16)


def _nsa_call(qn, kc, vct, ks, vs, kw, vw, gn, ovl):
    b, _, s = qn.shape
    tq = NSA_TQ
    n_slots = NSA_GROUPS * NSA_REP
    assert s // SEL_BLOCK <= SEL_BLOCKS_MAX
    seq = lambda a: pl.BlockSpec((1,) + a.shape[1:], lambda bi, i: (bi,) + (0,) * (a.ndim - 1))
    full = lambda a: pl.BlockSpec(a.shape, lambda bi, i: (0,) * a.ndim)
    return pl.pallas_call(
        _nsa_kernel,
        grid=(b, s // tq),
        in_specs=[pl.BlockSpec((1, Q_WIDTH, tq), lambda bi, i: (bi, 0, i)),
                  seq(kc), seq(vct), seq(ks), seq(vs), seq(kw), seq(vw),
                  pl.BlockSpec((1, GATE_ROWS, tq), lambda bi, i: (bi, 0, i)), full(ovl)],
        out_specs=pl.BlockSpec((1, tq, Q_WIDTH), lambda bi, i: (bi, i, 0)),
        out_shape=jax.ShapeDtypeStruct((b, s, Q_WIDTH), BF16),
        scratch_shapes=[pltpu.VMEM((NSA_GROUPS, SEL_BLOCKS_MAX, tq), F32),
                        pltpu.VMEM((KV_WIDTH, n_slots * tq), BF16)]
        + [pltpu.VMEM((NSA_TK, 2 * tq), F32)] * (n_slots // 2)
        + [pltpu.VMEM((n_slots, SUBLANES, tq), F32),
           pltpu.VMEM((n_slots, HEAD_DIM + SUM_ROWS, tq), F32),
           pltpu.VMEM((n_slots, HEAD_DIM, tq), F32)],
        compiler_params=pltpu.CompilerParams(dimension_semantics=("parallel", "parallel"),
                                             vmem_limit_bytes=VMEM_LIMIT),
    )(qn, kc, vct, ks, vs, kw, vw, gn, ovl)


def _moba_kernel(q_ref, k_ref, v_ref, out_ref, km_scr, sel_scr, lhs_scr, s0_scr, s1_scr, m_scr,
                 acc_scr):
    bs = MOBA_BLOCK
    n_blocks = k_ref.shape[1] // bs
    s_scr = (s0_scr, s1_scr)
    i = pl.program_id(2)

    @pl.when(i == 0)
    def _():
        km_scr[...] = jnp.zeros(km_scr.shape, F32)
        for n in range(n_blocks):
            km_scr[n:n + 1, :] = jnp.mean(k_ref[0, n * bs:(n + 1) * bs, :].astype(F32), axis=0,
                                          keepdims=True)

    zero = jnp.zeros((HEAD_DIM, bs), BF16)
    lhs_scr[:, 0:bs] = jnp.concatenate([q_ref[0, :HEAD_DIM, :], zero], axis=0)
    lhs_scr[:, bs:2 * bs] = jnp.concatenate([zero, q_ref[0, HEAD_DIM:, :]], axis=0)

    km_hi, km_lo = _split_bf16(km_scr[...])
    sc = _dot(km_hi, lhs_scr[...]) + _dot(km_lo, lhs_scr[...])
    n_idx = lax.broadcasted_iota(jnp.int32, sc.shape, 0)
    sc = jnp.where(n_idx < i, sc, -jnp.inf)
    sel_scr[...] = jnp.where((_rank_rows(sc, MOBA_BLOCKS_MAX) < MOBA_TOPK) & (n_idx < i), 0.0, NEG_BIG)

    ones_rows = _ones_rows(bs)

    def keys(n):
        return k_ref[0, pl.ds(pl.multiple_of(n * bs, bs), bs), :]

    def scores_into(hs, k_tile):
        s_scr[hs][...] = _dot(k_tile, lhs_scr[:, hs * bs:(hs + 1) * bs])

    def run_block(v_tile, bias_of_slot, block_rows, next_keys):
        for hs in range(2):
            v_aug = _with_sum_row(v_tile[hs * HEAD_DIM:(hs + 1) * HEAD_DIM, :], ones_rows)
            load = lambda a, hs=hs: s_scr[hs][...]
            _slot_update(load, bias_of_slot(hs), block_rows, m_scr, acc_scr, hs, v_aug)
            if next_keys is not None:
                scores_into(hs, next_keys())

    def past_body(n, carry):
        run_block(v_ref[0, n], lambda hs: [sel_scr[pl.ds(n, 1), hs * bs:(hs + 1) * bs]], True,
                  lambda: keys(n + 1))
        return carry

    _reset_state(m_scr, acc_scr)
    for hs in range(2):
        scores_into(hs, keys(0))
    lax.fori_loop(0, i, past_body, 0)
    key_i = lax.broadcasted_iota(jnp.int32, (bs, bs), 0)
    qry_i = lax.broadcasted_iota(jnp.int32, (bs, bs), 1)
    bias_own = [jnp.where(key_i <= qry_i, 0.0, NEG_BIG)]
    run_block(v_ref[0, i], lambda hs: bias_own, False, None)
    o_t = jnp.concatenate([_normalised(acc_scr[hs]) for hs in range(2)], axis=0)
    out_ref[0] = o_t.T.astype(BF16)


def _moba_call(qm, km, vm):
    b, w, s = qm.shape
    bs = MOBA_BLOCK
    nb = s // bs
    assert nb <= MOBA_BLOCKS_MAX
    return pl.pallas_call(
        _moba_kernel,
        grid=(b, w // LANES, nb),
        in_specs=[pl.BlockSpec((1, LANES, bs), lambda bi, p, i: (bi, p, i)),
                  pl.BlockSpec((1, s, LANES), lambda bi, p, i: (bi, 0, p)),
                  pl.BlockSpec((1, nb, LANES, bs), lambda bi, p, i: (bi, 0, p, 0))],
        out_specs=pl.BlockSpec((1, bs, LANES), lambda bi, p, i: (bi, i, p)),
        out_shape=jax.ShapeDtypeStruct((b, s, w), BF16),
        scratch_shapes=[pltpu.VMEM((MOBA_BLOCKS_MAX, LANES), F32),
                        pltpu.VMEM((MOBA_BLOCKS_MAX, 2 * bs), F32),
                        pltpu.VMEM((LANES, 2 * bs), BF16),
                        pltpu.VMEM((bs, bs), F32), pltpu.VMEM((bs, bs), F32),
                        pltpu.VMEM((2, SUBLANES, bs), F32),
                        pltpu.VMEM((2, HEAD_DIM + SUM_ROWS, bs), F32)],
        compiler_params=pltpu.CompilerParams(
            dimension_semantics=("parallel", "parallel", "arbitrary"), vmem_limit_bytes=VMEM_LIMIT),
    )(qm, km, vm)


def _merge_kernel(x_ref, yn_ref, ym_ref, gmix_ref, wg_ref, wun_ref, wum_ref, wo_ref, out_ref):
    x = x_ref[...]
    h = _rms_rows(x, gmix_ref[...]).astype(BF16)
    ga = _sigmoid(_dot(h, wg_ref[:, :D_MODEL]))
    gb = _sigmoid(_dot(h, wg_ref[:, D_MODEL:]))
    merged = ga * _dot(yn_ref[...], wun_ref[...]) + gb * _dot(ym_ref[...], wum_ref[...])
    out_ref[...] = x + _dot(merged.astype(BF16), wo_ref[...])


def _merge_call(x2, yn, ym, gmix, wg, wun, wum, wo):
    t = x2.shape[0]
    tm = MERGE_TM
    row = lambda w_: pl.BlockSpec((tm, w_), lambda i: (i, 0))
    full = lambda a: pl.BlockSpec(a.shape, lambda i: (0, 0))
    return pl.pallas_call(
        _merge_kernel,
        grid=(t // tm,),
        in_specs=[row(D_MODEL), row(Q_WIDTH), row(Q_WIDTH), full(gmix), full(wg), full(wun),
                  full(wum), full(wo)],
        out_specs=row(D_MODEL),
        out_shape=jax.ShapeDtypeStruct((t, D_MODEL), F32),
        compiler_params=pltpu.CompilerParams(dimension_semantics=("parallel",),
                                             vmem_limit_bytes=VMEM_LIMIT),
    )(x2, yn, ym, gmix, wg, wun, wum, wo)


def _ffn_kernel(x_ref, p_ref, gffn_ref, wfi_ref, wfo_ref, gple_ref, wpg_ref, wpp_ref, out_ref):
    x = x_ref[...]
    h = _rms_rows(x, gffn_ref[...]).astype(BF16)
    y = x
    for c in range(D_FF // FFN_CHUNK):
        gate = _dot(h, wfi_ref[:, c * FFN_CHUNK:(c + 1) * FFN_CHUNK])
        up = _dot(h, wfi_ref[:, D_FF + c * FFN_CHUNK:D_FF + (c + 1) * FFN_CHUNK])
        act = (gate * _sigmoid(gate) * up).astype(BF16)
        y = y + _dot(act, wfo_ref[c * FFN_CHUNK:(c + 1) * FFN_CHUNK, :])
    h2 = _rms_rows(y, gple_ref[...]).astype(BF16)
    ple_gate = _sigmoid(_dot(h2, wpg_ref[...]))
    out_ref[...] = y + ple_gate * _dot(p_ref[...].astype(BF16), wpp_ref[...])


def _ffn_call(x1, p2, gffn, wfi, wfo, gple, wpg, wpp):
    t = x1.shape[0]
    tm = FFN_TM
    row = lambda w_: pl.BlockSpec((tm, w_), lambda i: (i, 0))
    full = lambda a: pl.BlockSpec(a.shape, lambda i: (0, 0))
    return pl.pallas_call(
        _ffn_kernel,
        grid=(t // tm,),
        in_specs=[row(D_MODEL), row(PLE_DIM), full(gffn), full(wfi), full(wfo), full(gple),
                  full(wpg), full(wpp)],
        out_specs=row(D_MODEL),
        out_shape=jax.ShapeDtypeStruct((t, D_MODEL), F32),
        compiler_params=pltpu.CompilerParams(dimension_semantics=("parallel",),
                                             vmem_limit_bytes=VMEM_LIMIT),
    )(x1, p2, gffn, wfi, wfo, gple, wpg, wpp)


def _block_diag_ones(width):
    idx = np.arange(width) // HEAD_DIM
    return jnp.asarray(idx[:, None] == idx[None, :], dtype=BF16)


def _inv_freq():
    return ROPE_THETA ** (-jnp.arange(ROPE_HALF, dtype=F32) / ROPE_HALF)


def _inv_freq_lanes(width):
    inv_freq = _inv_freq()
    per_head = jnp.concatenate([inv_freq, inv_freq, jnp.zeros((HEAD_DIM - ROPE_DIM,), F32)])
    return jnp.tile(per_head, width // HEAD_DIM)[None, :]


def _split_w_in(w):
    parts = jnp.split(w, IN_CUTS, axis=-1)
    (q_n, kc, vc, ks, vs, kw, vw, gate_n, q_m, k_m, v_m, gate_a, gate_b) = parts
    d = w.shape[0]
    q_n = q_n.reshape(d, NSA_GROUPS, NSA_REP, HEAD_DIM).transpose(0, 2, 1, 3).reshape(d, -1)
    gate_n = jnp.pad(gate_n, ((0, 0), (0, GATE_ROWS - gate_n.shape[1])))
    w_tok = jnp.concatenate([k_m, ks, kw, kc, vc], axis=1)
    w_feat = jnp.concatenate([q_n, q_m, v_m, vs, vw, gate_n], axis=1).T
    w_gate = jnp.concatenate([gate_a, gate_b], axis=1)
    return w_tok.astype(BF16), w_feat.astype(BF16), w_gate.astype(BF16)


def _compress_weights(w1, w2, pe):
    eye = jnp.eye(NSA_GROUPS, dtype=F32)
    halves = []
    for part in (w1[:CMP_STRIDE * HEAD_DIM], w1[CMP_STRIDE * HEAD_DIM:]):
        p3 = part.reshape(CMP_STRIDE, HEAD_DIM, CMP_HIDDEN)
        halves.append(jnp.einsum('idh,gk->igdkh', p3, eye)
                      .reshape(CMP_STRIDE * NSA_GROUPS * HEAD_DIM, NSA_GROUPS * CMP_HIDDEN))
    w_big = jnp.concatenate(halves, axis=1).astype(BF16)
    w2_bd = jnp.einsum('hd,kg->khgd', w2, eye).reshape(NSA_GROUPS * CMP_HIDDEN,
                                                        NSA_GROUPS * HEAD_DIM).astype(BF16)
    pe_rows = [jnp.broadcast_to(pe[a:a + CMP_STRIDE, None, :], (CMP_STRIDE, NSA_GROUPS, HEAD_DIM))
               .reshape(1, -1) for a in (0, CMP_STRIDE)]
    return w_big, w2_bd, pe_rows


def _overlap_matrix_t(n_cmp):
    c = np.arange(n_cmp)
    j = np.arange(SEL_BLOCKS_MAX)
    start, end = c * CMP_STRIDE, c * CMP_STRIDE + CMP_LEN - 1
    ov = (start[None, :] <= j[:, None] * SEL_BLOCK + SEL_BLOCK - 1) & (end[None, :] >= j[:, None] * SEL_BLOCK)
    return jnp.asarray(ov, dtype=BF16)


def kernel(x, p, positions, g_mix, w_in, nsa_q_gain, nsa_kc_gain, nsa_ks_gain, nsa_kw_gain, nsa_pe_k, nsa_pe_v, nsa_ck_w1, nsa_ck_w2, nsa_cv_w1, nsa_cv_w2, moba_q_gain, moba_k_gain, w_up_nsa, w_up_moba, w_out, g_ffn, w_ffn_in, w_ffn_out, g_ple, w_ple_gate, w_ple_proj):
    b, s, d = x.shape
    depth = w_in.shape[0]
    assert s % PAD_MULT == 0 and d == D_MODEL
    t = b * s
    n_sub = s // CMP_STRIDE

    posc = positions.reshape(t, 1)
    posr = positions.reshape(t // PROJ_TM, 1, PROJ_TM)
    pos_end = jnp.concatenate([positions[:, CMP_LEN - 1::CMP_STRIDE], positions[:, -1:]], axis=1)[:, :, None]
    invf256 = _inv_freq_lanes(256)
    invf8 = _inv_freq()[:, None]
    bd256, bd128 = _block_diag_ones(256), _block_diag_ones(LANES)
    ovl_t = _overlap_matrix_t(n_sub)
    tile = lambda g, n: jnp.tile(g, n)
    xi = x.reshape(t, d)

    for i in range(depth):
        w_tok, w_feat, w_gate = _split_w_in(w_in[i])
        g_tok = jnp.concatenate([tile(moba_k_gain[i], MOBA_HEADS), tile(nsa_ks_gain[i], NSA_GROUPS),
                                 tile(nsa_kw_gain[i], NSA_GROUPS)])[None, :]
        g_feat = (jnp.concatenate([tile(nsa_q_gain[i], NSA_HEADS), tile(moba_q_gain[i], MOBA_HEADS)])
                  * (ATTN_SCALE * LOG2E))[:, None]
        gmix = g_mix[i][None, :]
        (qn, qm, gn, vm, vs, vw, km, ks, kw, kc_raw, vc_raw) = _proj_call(
            xi, posc, posr, gmix, w_tok, w_feat, g_tok, g_feat, invf256, invf8, bd256, b, s)

        wk_big, w2k_bd, pe_k = _compress_weights(nsa_ck_w1[i], nsa_ck_w2[i], nsa_pe_k[i])
        wv_big, w2v_bd, pe_v = _compress_weights(nsa_cv_w1[i], nsa_cv_w2[i], nsa_pe_v[i])
        pe4 = jnp.concatenate(pe_k + pe_v, axis=0)
        kc, vct = _compress_call(kc_raw.reshape(b, n_sub, CMP_STRIDE * KV_WIDTH),
                                 vc_raw.reshape(b, n_sub, CMP_STRIDE * KV_WIDTH),
                                 pe4, wk_big, wv_big, w2k_bd, w2v_bd,
                                 tile(nsa_kc_gain[i], NSA_GROUPS)[None, :], pos_end,
                                 invf256[:, :LANES], bd128)

        r3 = lambda a: a.reshape(b, s, a.shape[-1])
        y_nsa = _nsa_call(qn, kc, vct, r3(ks), vs.reshape(b, s // NSA_TK, KV_WIDTH, NSA_TK), r3(kw),
                          vw.reshape(b, s // WIN_TK, KV_WIDTH, WIN_TK), gn, ovl_t)
        y_moba = _moba_call(qm, r3(km), vm.reshape(b, s // MOBA_BLOCK, Q_WIDTH, MOBA_BLOCK))

        w_un = (w_up_nsa[i].reshape(NSA_GROUPS, NSA_REP, HEAD_DIM, d).transpose(1, 0, 2, 3)
                .reshape(NSA_HEADS * HEAD_DIM, d).astype(BF16))
        x1 = _merge_call(xi, y_nsa.reshape(t, -1), y_moba.reshape(t, -1), gmix, w_gate, w_un,
                         w_up_moba[i].astype(BF16), w_out[i].astype(BF16))
        xi = _ffn_call(x1, p[i].reshape(t, PLE_DIM), g_ffn[i][None, :], w_ffn_in[i].astype(BF16),
                       w_ffn_out[i].astype(BF16), g_ple[i][None, :], w_ple_gate[i].astype(BF16),
                       w_ple_proj[i].astype(BF16))
    return xi.reshape(b, s, d)
```

```python
import jax
import jax.numpy as jnp
import numpy as np
from jax import lax
from jax.experimental import pallas as pl
from jax.experimental.pallas import tpu as pltpu

F32 = jnp.float32
BF16 = jnp.bfloat16

D_MODEL = 1024
HEAD_DIM = 64
ROPE_DIM = HEAD_DIM // 4
ROPE_HALF = ROPE_DIM // 2
ROPE_THETA = 500000.0
NORM_EPS = 1e-6
ATTN_SCALE = HEAD_DIM ** -0.5

NSA_HEADS = 8
NSA_GROUPS = 2
NSA_REP = NSA_HEADS // NSA_GROUPS
CMP_LEN = 32
CMP_STRIDE = 16
CMP_HIDDEN = 256
SEL_BLOCK = 64
SEL_TOPN = 8
WINDOW = 512

MOBA_HEADS = 8
MOBA_BLOCK = 256
MOBA_TOPK = 3

PAD_MULT = 256
D_FF = ((-(-8 * D_MODEL // 3)) + 255) // 256 * 256
PLE_DIM = 256

IN_SPLITS = ((NSA_HEADS * HEAD_DIM,) + (NSA_GROUPS * HEAD_DIM,) * 6 + (3 * NSA_HEADS,)
             + (MOBA_HEADS * HEAD_DIM,) * 3 + (D_MODEL, D_MODEL))
IN_CUTS = tuple(int(c) for c in np.cumsum(IN_SPLITS)[:-1])

LANES = 128
SUBLANES = 8
NEG_BIG = -1e30
LOG2E = 1.4426950408889634
SUM_ROWS = 16
VMEM_LIMIT = 56 * 1024 * 1024

Q_WIDTH = NSA_HEADS * HEAD_DIM
KV_WIDTH = NSA_GROUPS * HEAD_DIM
GATE_ROWS = 32
TOK_NORMED = Q_WIDTH + 2 * KV_WIDTH
TOK_COLS = TOK_NORMED + 2 * KV_WIDTH
FEAT_ROWS = 3 * Q_WIDTH + 2 * KV_WIDTH + GATE_ROWS

PROJ_TM = MOBA_BLOCK
NSA_TQ = 256
NSA_TK = 256
WIN_TK = 256
SCORE_LANES = 256
MERGE_TM = 256
FFN_TM = 256
FFN_CHUNK = D_FF // 2
SEL_BLOCKS_MAX = 32
MOBA_BLOCKS_MAX = 8
MOBA_STEP_HEADS = 8


def _dot(a, b):
    return jnp.dot(a, b, preferred_element_type=F32)


def _dot_nt(a, b):
    return lax.dot_general(a, b, (((1,), (1,)), ((), ())), preferred_element_type=F32)


def _split_bf16(a_f32):
    hi = a_f32.astype(BF16)
    return hi, (a_f32 - hi.astype(F32)).astype(BF16)


def _sigmoid(x):
    return 1.0 / (1.0 + jnp.exp(-x))


def _rms_rows(x, g):
    return x * lax.rsqrt(jnp.mean(x * x, axis=-1, keepdims=True) + NORM_EPS) * g


def _head_norm(t, bd, gain):
    hi, lo = _split_bf16(t * t)
    ss = _dot(hi, bd) + _dot(lo, bd)
    return t * lax.rsqrt(ss * (1.0 / HEAD_DIM) + NORM_EPS) * gain


def _rope_tables(pos_col, invf):
    ang = pos_col.astype(F32) * invf
    cos_a, sin_a = jnp.cos(ang), jnp.sin(ang)
    d = lax.broadcasted_iota(jnp.int32, ang.shape, 1) & (HEAD_DIM - 1)
    s_lo = jnp.where(d < ROPE_HALF, -sin_a, 0.0)
    s_hi = jnp.where((d >= ROPE_HALF) & (d < ROPE_DIM), sin_a, 0.0)
    return cos_a, s_lo, s_hi


def _rope(y, tables):
    cos_a, s_lo, s_hi = tables
    w = y.shape[1]
    return (y * cos_a + pltpu.roll(y, w - ROPE_HALF, 1) * s_lo
            + pltpu.roll(y, ROPE_HALF, 1) * s_hi)


def _ones_rows(tk):
    r = lax.broadcasted_iota(jnp.int32, (SUM_ROWS, tk), 0)
    return jnp.where(r == 0, 1.0, 0.0).astype(BF16)


def _with_sum_row(v_rows, ones_rows):
    return jnp.concatenate([v_rows, ones_rows], axis=0)


def _slot_update(load_s, bias, block_rows, m_ref, acc_ref, hs, v_aug):
    m = m_ref[hs, 0:1, :]
    n_sub = len(bias)
    if block_rows:
        dropped = [bias[a] < 0.0 for a in range(n_sub)]
        m_new = m
        for a in range(n_sub):
            top = jnp.max(load_s(a), axis=0, keepdims=True)
            m_new = jnp.maximum(m_new, jnp.where(dropped[a], NEG_BIG, top))
        e = [jnp.exp2(load_s(a) + jnp.where(dropped[a], NEG_BIG, -m_new)) for a in range(n_sub)]
    else:
        m_new = m
        for a in range(n_sub):
            m_new = jnp.maximum(m_new, jnp.max(load_s(a) + bias[a], axis=0, keepdims=True))
        e = [jnp.exp2(load_s(a) + bias[a] - m_new) for a in range(n_sub)]
    e = e[0] if n_sub == 1 else jnp.concatenate(e, axis=0)
    alpha = jnp.exp2(m - m_new)
    m_ref[hs, 0:1, :] = m_new
    acc_ref[hs] = alpha * acc_ref[hs] + _dot(v_aug, e.astype(BF16))


def _reset_state(m_ref, acc_ref):
    m_ref[...] = jnp.full(m_ref.shape, NEG_BIG, F32)
    acc_ref[...] = jnp.zeros(acc_ref.shape, F32)


def _normalised(acc, scale_row=1.0):
    return acc[:HEAD_DIM] * (scale_row / jnp.maximum(acc[HEAD_DIM:HEAD_DIM + 1], 1e-30))


def _rank_rows(v, n_rows):
    j_idx = lax.broadcasted_iota(jnp.int32, v.shape, 0)
    rank = jnp.zeros(v.shape, F32)
    for jp in range(n_rows):
        row = v[jp:jp + 1, :]
        beats = (row > v) | ((row == v) & (j_idx > jp))
        rank = rank + jnp.where(beats, 1.0, 0.0)
    return rank


def _proj_kernel(x_ref, posr_ref, gmix_ref, wtok_ref, wfeat_ref, gtok_ref, gfeat_ref,
                 invf8_ref, bd_ref, spread_ref,
                 qn_ref, qm_ref, gn_ref, vm_ref, vs_ref, vw_ref, km_ref, ks_ref, kw_ref,
                 kc_ref, vc_ref):
    tm = x_ref.shape[0]
    h = _rms_rows(x_ref[...], gmix_ref[...]).astype(BF16)

    ang = invf8_ref[...] * posr_ref[0].astype(F32)
    cos_a, sin_a = jnp.cos(ang), jnp.sin(ang)
    rows_t = jnp.concatenate([cos_a, sin_a, jnp.ones((SUBLANES, tm), F32),
                              jnp.zeros((LANES - 3 * SUBLANES, tm), F32)], axis=0)
    hi, lo = _split_bf16(rows_t.T)
    spread = _dot(hi, spread_ref[...]) + _dot(lo, spread_ref[...])
    tables = (spread[:, :256], spread[:, 256:512], spread[:, 512:])

    acc = _dot(h, wtok_ref[...])
    bd = bd_ref[...]
    for c in range(TOK_NORMED // 256):
        t = acc[:, c * 256:(c + 1) * 256]
        y = _rope(_head_norm(t, bd, gtok_ref[:, c * 256:(c + 1) * 256]), tables).astype(BF16)
        if c < 2:
            km_ref[:, c * 256:(c + 1) * 256] = y
        else:
            ks_ref[...] = y[:, :KV_WIDTH]
            kw_ref[...] = y[:, KV_WIDTH:]
    kc_ref[...] = acc[:, TOK_NORMED:TOK_NORMED + KV_WIDTH]
    vc_ref[...] = acc[:, TOK_NORMED + KV_WIDTH:]

    acc_t = _dot_nt(wfeat_ref[...], h)
    for hh in range(2 * NSA_HEADS):
        t = acc_t[hh * HEAD_DIM:(hh + 1) * HEAD_DIM, :]
        ss = jnp.sum(t * t, axis=0, keepdims=True)
        y = t * lax.rsqrt(ss * (1.0 / HEAD_DIM) + NORM_EPS) * gfeat_ref[hh * HEAD_DIM:(hh + 1) * HEAD_DIM, :]
        a, b = y[:ROPE_HALF], y[ROPE_HALF:ROPE_DIM]
        y = jnp.concatenate([a * cos_a - b * sin_a, b * cos_a + a * sin_a, y[ROPE_DIM:]], axis=0)
        dst = qn_ref if hh < NSA_HEADS else qm_ref
        r0 = (hh % NSA_HEADS) * HEAD_DIM
        dst[0, r0:r0 + HEAD_DIM, :] = y.astype(BF16)
    o = 2 * Q_WIDTH
    vm_ref[0] = acc_t[o:o + Q_WIDTH].astype(BF16)
    vs_ref[0] = acc_t[o + Q_WIDTH:o + Q_WIDTH + KV_WIDTH].astype(BF16)
    vw = acc_t[o + Q_WIDTH + KV_WIDTH:o + Q_WIDTH + 2 * KV_WIDTH].astype(BF16)
    for j in range(PROJ_TM // WIN_TK):
        vw_ref[0, j] = vw[:, j * WIN_TK:(j + 1) * WIN_TK]
    gn_ref[0] = acc_t[o + Q_WIDTH + 2 * KV_WIDTH:]


def _proj_call(x2, posr, gmix, wtok, wfeat, gtok, gfeat, invf8, bd, spread, b, s):
    t = x2.shape[0]
    tm = PROJ_TM
    nt = s // tm
    row = lambda w_: pl.BlockSpec((tm, w_), lambda i: (i, 0))
    full = lambda a: pl.BlockSpec(a.shape, lambda i: (0,) * a.ndim)
    feat = lambda r: pl.BlockSpec((1, r, tm), lambda i: (i // nt, 0, i % nt))
    tile = lambda r: pl.BlockSpec((1, r, tm), lambda i: (i, 0, 0))
    out_shape = [
        jax.ShapeDtypeStruct((b, Q_WIDTH, s), BF16),
        jax.ShapeDtypeStruct((b, Q_WIDTH, s), BF16),
        jax.ShapeDtypeStruct((b, GATE_ROWS, s), F32),
        jax.ShapeDtypeStruct((t // tm, Q_WIDTH, tm), BF16),
        jax.ShapeDtypeStruct((t // tm, KV_WIDTH, tm), BF16),
        jax.ShapeDtypeStruct((t // tm, tm // WIN_TK, KV_WIDTH, WIN_TK), BF16),
        jax.ShapeDtypeStruct((t, Q_WIDTH), BF16),
        jax.ShapeDtypeStruct((t, KV_WIDTH), BF16),
        jax.ShapeDtypeStruct((t, KV_WIDTH), BF16),
        jax.ShapeDtypeStruct((t, KV_WIDTH), F32),
        jax.ShapeDtypeStruct((t, KV_WIDTH), F32),
    ]
    out_specs = [feat(Q_WIDTH), feat(Q_WIDTH), feat(GATE_ROWS), tile(Q_WIDTH), tile(KV_WIDTH),
                 pl.BlockSpec((1, tm // WIN_TK, KV_WIDTH, WIN_TK), lambda i: (i, 0, 0, 0)),
                 row(Q_WIDTH), row(KV_WIDTH), row(KV_WIDTH), row(KV_WIDTH), row(KV_WIDTH)]
    return pl.pallas_call(
        _proj_kernel,
        grid=(t // tm,),
        in_specs=[row(D_MODEL), pl.BlockSpec((1, 1, tm), lambda i: (i, 0, 0)), full(gmix),
                  full(wtok), full(wfeat), full(gtok), full(gfeat), full(invf8), full(bd), full(spread)],
        out_specs=out_specs,
        out_shape=out_shape,
        compiler_params=pltpu.CompilerParams(dimension_semantics=("parallel",),
                                             vmem_limit_bytes=VMEM_LIMIT),
    )(x2, posr, gmix, wtok, wfeat, gtok, gfeat, invf8, bd, spread)


def _compress_kernel(kc4_ref, vc4_ref, pe_ref, wk_ref, wv_ref, w2k_ref, w2v_ref, gain_ref,
                     pose_ref, invf_ref, bd_ref, kc_out, vct_out):
    half = NSA_GROUPS * CMP_HIDDEN

    def comp(x, pe_a, pe_b, w_ref, w2_ref):
        a = _dot((x + pe_a).astype(BF16), w_ref[:, :half])
        b = _dot((x + pe_b).astype(BF16), w_ref[:, half:])
        hid = a + pltpu.roll(b, b.shape[0] - 1, 0)
        act = hid * _sigmoid(hid)
        return _dot(act.astype(BF16), w2_ref[...])

    kc = comp(kc4_ref[0], pe_ref[0:1, :], pe_ref[1:2, :], wk_ref, w2k_ref)
    vc = comp(vc4_ref[0], pe_ref[2:3, :], pe_ref[3:4, :], wv_ref, w2v_ref)
    kc = _head_norm(kc, bd_ref[...], gain_ref[...])
    kc = _rope(kc, _rope_tables(pose_ref[0], invf_ref[...]))
    kc_out[0] = kc.astype(BF16)
    vct_out[0] = vc.T.astype(BF16)


def _compress_call(kc4, vc4, pe4, wk, wv, w2k, w2v, gain, pos_end, invf, bd):
    b, n_sub, width = kc4.shape
    blk = lambda shp: pl.BlockSpec((1,) + shp, lambda i: (i, 0, 0))
    full = lambda a: pl.BlockSpec(a.shape, lambda i: (0,) * a.ndim)
    return pl.pallas_call(
        _compress_kernel,
        grid=(b,),
        in_specs=[blk((n_sub, width)), blk((n_sub, width)), full(pe4), full(wk), full(wv),
                  full(w2k), full(w2v), full(gain), blk((n_sub, 1)), full(invf), full(bd)],
        out_specs=[blk((n_sub, KV_WIDTH)), blk((KV_WIDTH, n_sub))],
        out_shape=[jax.ShapeDtypeStruct((b, n_sub, KV_WIDTH), BF16),
                   jax.ShapeDtypeStruct((b, KV_WIDTH, n_sub), BF16)],
        compiler_params=pltpu.CompilerParams(dimension_semantics=("parallel",),
                                             vmem_limit_bytes=VMEM_LIMIT),
    )(kc4, vc4, pe4, wk, wv, w2k, w2v, gain, pos_end, invf, bd)


def _nsa_kernel(q_ref, kc_ref, vct_ref, ks_ref, vs_ref, kw_ref, vw_ref, gate_ref, ovl_ref, out_ref,
                sel_scr, lhs_scr, m_scr, acc_scr, o_scr, *s_scr):
    tq = q_ref.shape[2]
    n_cmp = kc_ref.shape[1]
    n_slots = NSA_GROUPS * NSA_REP
    n_units = len(s_scr)
    per_unit = n_slots // n_units
    i = pl.program_id(1)
    t0 = i * tq
    t_row = t0 + lax.broadcasted_iota(jnp.int32, (1, tq), 1)

    zero = jnp.zeros((HEAD_DIM, tq), BF16)
    for g in range(NSA_GROUPS):
        for r in range(NSA_REP):
            hs = g * NSA_REP + r
            rows = q_ref[0, r * KV_WIDTH + g * HEAD_DIM:r * KV_WIDTH + (g + 1) * HEAD_DIM, :]
            lhs_scr[:, hs * tq:(hs + 1) * tq] = jnp.concatenate([rows, zero] if g == 0 else [zero, rows],
                                                                axis=0)
    gsig = _sigmoid(gate_ref[0])

    c_idx = lax.broadcasted_iota(jnp.int32, (n_cmp, tq), 0)
    valid_c = (c_idx * CMP_STRIDE + (CMP_LEN - 1) <= t_row) & (c_idx < n_cmp - 1)
    bias_c = jnp.where(valid_c, 0.0, NEG_BIG)
    p_sum = []
    for hs in range(n_slots):
        g = hs // NSA_REP
        x = _dot(kc_ref[0], lhs_scr[:, hs * tq:(hs + 1) * tq]) + bias_c
        m = jnp.max(x, axis=0, keepdims=True)
        e = jnp.where(valid_c, jnp.exp2(x - m), 0.0)
        p = e / jnp.maximum(jnp.sum(e, axis=0, keepdims=True), 1e-30)
        o_c = _dot(vct_ref[0, g * HEAD_DIM:(g + 1) * HEAD_DIM, :], p.astype(BF16))
        o_scr[hs] = o_c * gsig[3 * hs:3 * hs + 1, :]
        if hs % NSA_REP == 0:
            p_sum.append(p)
        else:
            p_sum[g] = p_sum[g] + p

    cur = t_row // SEL_BLOCK
    j_idx = lax.broadcasted_iota(jnp.int32, (SEL_BLOCKS_MAX, tq), 0)
    forced = (j_idx == 0) | (j_idx == cur) | (j_idx == cur - 1)
    for g in range(NSA_GROUPS):
        hi, lo = _split_bf16(p_sum[g])
        imp = _dot(ovl_ref[...], hi) + _dot(ovl_ref[...], lo)
        imp = jnp.where(forced, jnp.inf, jnp.where(j_idx > cur, -jnp.inf, imp))
        sel_scr[g] = jnp.where(_rank_rows(imp, SEL_BLOCKS_MAX) < SEL_TOPN, 0.0, NEG_BIG)

    def scores_into(unit, k_tile):
        tk = k_tile.shape[0]
        s_scr[unit][0:tk, :] = _dot(k_tile, lhs_scr[:, unit * SCORE_LANES:(unit + 1) * SCORE_LANES])

    def run_tile(v_tile, bias_of_group, block_rows, sub, next_keys):
        ones_rows = _ones_rows(v_tile.shape[1])
        bias = [bias_of_group(g) for g in range(NSA_GROUPS)]
        for unit in range(n_units):
            g = (unit * per_unit) // NSA_REP
            v_aug = _with_sum_row(v_tile[g * HEAD_DIM:(g + 1) * HEAD_DIM, :], ones_rows)
            for hh in range(per_unit):
                load = lambda a, hh=hh, unit=unit: s_scr[unit][a * sub:(a + 1) * sub, hh * tq:(hh + 1) * tq]
                _slot_update(load, bias[g], block_rows, m_scr, acc_scr, unit * per_unit + hh, v_aug)
            if next_keys is not None:
                scores_into(unit, next_keys())

    def fold_branch(branch):
        for hs in range(n_slots):
            o_scr[hs] = o_scr[hs] + _normalised(acc_scr[hs], gsig[3 * hs + branch:3 * hs + branch + 1, :])

    blocks_per_tile = NSA_TK // SEL_BLOCK

    def sel_keys(kt):
        return ks_ref[0, pl.ds(pl.multiple_of(kt * NSA_TK, NSA_TK), NSA_TK), :]

    def sel_rows(kt, g):
        return [sel_scr[g, pl.ds(kt * blocks_per_tile + a, 1), :] for a in range(blocks_per_tile)]

    def sel_diag_bias(kt, g):
        out = []
        for a, row in enumerate(sel_rows(kt, g)):
            kpos = kt * NSA_TK + a * SEL_BLOCK + lax.broadcasted_iota(jnp.int32, (SEL_BLOCK, tq), 0)
            out.append(jnp.where(kpos <= t_row, row, NEG_BIG))
        return out

    def sel_body(kt, carry):
        run_tile(vs_ref[0, kt], lambda g: sel_rows(kt, g), True, SEL_BLOCK, lambda: sel_keys(kt + 1))
        return carry

    last_sel = (t0 + tq - 1) // NSA_TK
    _reset_state(m_scr, acc_scr)
    for unit in range(n_units):
        scores_into(unit, sel_keys(0))
    lax.fori_loop(0, last_sel, sel_body, 0)
    run_tile(vs_ref[0, last_sel], lambda g: sel_diag_bias(last_sel, g), False, SEL_BLOCK, None)
    fold_branch(1)

    def win_keys(kt):
        return kw_ref[0, pl.ds(pl.multiple_of(kt * WIN_TK, WIN_TK), WIN_TK), :]

    def win_bias(kt):
        diff = t_row - (kt * WIN_TK + lax.broadcasted_iota(jnp.int32, (WIN_TK, tq), 0))
        return [jnp.where((diff >= 0) & (diff < WINDOW), 0.0, NEG_BIG)]

    def win_body(kt, carry):
        run_tile(vw_ref[0, kt], lambda g: win_bias(kt), False, WIN_TK, lambda: win_keys(kt + 1))
        return carry

    first_win = jnp.maximum((t0 - (WINDOW - 1)) // WIN_TK, 0)
    last_win = (t0 + tq - 1) // WIN_TK
    _reset_state(m_scr, acc_scr)
    for unit in range(n_units):
        scores_into(unit, win_keys(first_win))
    lax.fori_loop(first_win, last_win, win_body, 0)
    run_tile(vw_ref[0, last_win], lambda g: win_bias(last_win), False, WIN_TK, None)
    fold_branch(2)

    for r in range(NSA_REP):
        tile_t = jnp.concatenate([o_scr[r], o_scr[NSA_REP + r]], axis=0)
        out_ref[0, :, r * KV_WIDTH:(r + 1) * KV_WIDTH] = tile_t.T.astype(BF16)


def _nsa_call(qn, kc, vct, ks, vs, kw, vw, gn, ovl):
    b, _, s = qn.shape
    tq = NSA_TQ
    n_slots = NSA_GROUPS * NSA_REP
    assert s // SEL_BLOCK <= SEL_BLOCKS_MAX
    seq = lambda a: pl.BlockSpec((1,) + a.shape[1:], lambda bi, i: (bi,) + (0,) * (a.ndim - 1))
    full = lambda a: pl.BlockSpec(a.shape, lambda bi, i: (0,) * a.ndim)
    return pl.pallas_call(
        _nsa_kernel,
        grid=(b, s // tq),
        in_specs=[pl.BlockSpec((1, Q_WIDTH, tq), lambda bi, i: (bi, 0, i)),
                  seq(kc), seq(vct), seq(ks), seq(vs), seq(kw), seq(vw),
                  pl.BlockSpec((1, GATE_ROWS, tq), lambda bi, i: (bi, 0, i)), full(ovl)],
        out_specs=pl.BlockSpec((1, tq, Q_WIDTH), lambda bi, i: (bi, i, 0)),
        out_shape=jax.ShapeDtypeStruct((b, s, Q_WIDTH), BF16),
        scratch_shapes=[pltpu.VMEM((NSA_GROUPS, SEL_BLOCKS_MAX, tq), F32),
                        pltpu.VMEM((KV_WIDTH, n_slots * tq), BF16),
                        pltpu.VMEM((n_slots, SUBLANES, tq), F32),
                        pltpu.VMEM((n_slots, HEAD_DIM + SUM_ROWS, tq), F32),
                        pltpu.VMEM((n_slots, HEAD_DIM, tq), F32)]
        + [pltpu.VMEM((max(NSA_TK, WIN_TK), SCORE_LANES), F32)] * (n_slots * tq // SCORE_LANES),
        compiler_params=pltpu.CompilerParams(dimension_semantics=("parallel", "parallel"),
                                             vmem_limit_bytes=VMEM_LIMIT),
    )(qn, kc, vct, ks, vs, kw, vw, gn, ovl)


def _moba_kernel(q_ref, k_ref, v_ref, out_ref, km_scr, sel_scr, lhs_scr, m_scr, acc_scr, *s_scr):
    bs = MOBA_BLOCK
    n_blocks = k_ref.shape[1] // bs
    n_slots = len(s_scr)
    i = pl.program_id(2)

    @pl.when(i == 0)
    def _():
        km_scr[...] = jnp.zeros(km_scr.shape, F32)
        for n in range(n_blocks):
            km_scr[n:n + 1, :] = jnp.mean(k_ref[0, n * bs:(n + 1) * bs, :].astype(F32), axis=0,
                                          keepdims=True)

    def pair_lanes(hs):
        return slice((hs // 2) * LANES, (hs // 2 + 1) * LANES)

    zero = jnp.zeros((HEAD_DIM, bs), BF16)
    n_idx = lax.broadcasted_iota(jnp.int32, (MOBA_BLOCKS_MAX, bs), 0)
    for hs in range(n_slots):
        rows = q_ref[0, hs * HEAD_DIM:(hs + 1) * HEAD_DIM, :]
        lhs_scr[hs] = jnp.concatenate([rows, zero] if hs % 2 == 0 else [zero, rows], axis=0)
        km_hi, km_lo = _split_bf16(km_scr[:, pair_lanes(hs)])
        sc = _dot(km_hi, lhs_scr[hs]) + _dot(km_lo, lhs_scr[hs])
        sc = jnp.where(n_idx < i, sc, -jnp.inf)
        sel_scr[hs] = jnp.where((_rank_rows(sc, MOBA_BLOCKS_MAX) < MOBA_TOPK) & (n_idx < i), 0.0, NEG_BIG)

    ones_rows = _ones_rows(bs)

    def scores_into(hs, n):
        k_tile = k_ref[0, pl.ds(pl.multiple_of(n * bs, bs), bs), pair_lanes(hs)]
        s_scr[hs][...] = _dot(k_tile, lhs_scr[hs])

    def run_block(n, bias_of_slot, block_rows, prefetch):
        for hs in range(n_slots):
            v_aug = _with_sum_row(v_ref[0, n, hs * HEAD_DIM:(hs + 1) * HEAD_DIM, :], ones_rows)
            load = lambda a, hs=hs: s_scr[hs][...]
            _slot_update(load, bias_of_slot(hs), block_rows, m_scr, acc_scr, hs, v_aug)
            if prefetch:
                scores_into(hs, n + 1)

    def past_body(n, carry):
        run_block(n, lambda hs: [sel_scr[hs, pl.ds(n, 1), :]], True, True)
        return carry

    _reset_state(m_scr, acc_scr)
    for hs in range(n_slots):
        scores_into(hs, 0)
    lax.fori_loop(0, i, past_body, 0)
    key_i = lax.broadcasted_iota(jnp.int32, (bs, bs), 0)
    qry_i = lax.broadcasted_iota(jnp.int32, (bs, bs), 1)
    bias_own = [jnp.where(key_i <= qry_i, 0.0, NEG_BIG)]
    run_block(i, lambda hs: bias_own, False, False)
    for p in range(n_slots // 2):
        o_t = jnp.concatenate([_normalised(acc_scr[2 * p]), _normalised(acc_scr[2 * p + 1])], axis=0)
        out_ref[0, :, p * LANES:(p + 1) * LANES] = o_t.T.astype(BF16)


def _moba_call(qm, km, vm):
    b, w, s = qm.shape
    bs = MOBA_BLOCK
    nb = s // bs
    wb = MOBA_STEP_HEADS * HEAD_DIM
    assert nb <= MOBA_BLOCKS_MAX and w % wb == 0
    return pl.pallas_call(
        _moba_kernel,
        grid=(b, w // wb, nb),
        in_specs=[pl.BlockSpec((1, wb, bs), lambda bi, p, i: (bi, p, i)),
                  pl.BlockSpec((1, s, wb), lambda bi, p, i: (bi, 0, p)),
                  pl.BlockSpec((1, nb, wb, bs), lambda bi, p, i: (bi, 0, p, 0))],
        out_specs=pl.BlockSpec((1, bs, wb), lambda bi, p, i: (bi, i, p)),
        out_shape=jax.ShapeDtypeStruct((b, s, w), BF16),
        scratch_shapes=[pltpu.VMEM((MOBA_BLOCKS_MAX, wb), F32),
                        pltpu.VMEM((MOBA_STEP_HEADS, MOBA_BLOCKS_MAX, bs), F32),
                        pltpu.VMEM((MOBA_STEP_HEADS, LANES, bs), BF16),
                        pltpu.VMEM((MOBA_STEP_HEADS, SUBLANES, bs), F32),
                        pltpu.VMEM((MOBA_STEP_HEADS, HEAD_DIM + SUM_ROWS, bs), F32)]
        + [pltpu.VMEM((bs, bs), F32)] * MOBA_STEP_HEADS,
        compiler_params=pltpu.CompilerParams(
            dimension_semantics=("parallel", "parallel", "arbitrary"), vmem_limit_bytes=VMEM_LIMIT),
    )(qm, km, vm)


def _merge_kernel(x_ref, yn_ref, ym_ref, gmix_ref, wg_ref, wun_ref, wum_ref, wo_ref, out_ref):
    x = x_ref[...]
    h = _rms_rows(x, gmix_ref[...]).astype(BF16)
    ga = _sigmoid(_dot(h, wg_ref[:, :D_MODEL]))
    gb = _sigmoid(_dot(h, wg_ref[:, D_MODEL:]))
    merged = ga * _dot(yn_ref[...], wun_ref[...]) + gb * _dot(ym_ref[...], wum_ref[...])
    out_ref[...] = x + _dot(merged.astype(BF16), wo_ref[...])


def _merge_call(x2, yn, ym, gmix, wg, wun, wum, wo):
    t = x2.shape[0]
    tm = MERGE_TM
    row = lambda w_: pl.BlockSpec((tm, w_), lambda i: (i, 0))
    full = lambda a: pl.BlockSpec(a.shape, lambda i: (0, 0))
    return pl.pallas_call(
        _merge_kernel,
        grid=(t // tm,),
        in_specs=[row(D_MODEL), row(Q_WIDTH), row(Q_WIDTH), full(gmix), full(wg), full(wun),
                  full(wum), full(wo)],
        out_specs=row(D_MODEL),
        out_shape=jax.ShapeDtypeStruct((t, D_MODEL), F32),
        compiler_params=pltpu.CompilerParams(dimension_semantics=("parallel",),
                                             vmem_limit_bytes=VMEM_LIMIT),
    )(x2, yn, ym, gmix, wg, wun, wum, wo)


def _ffn_kernel(x_ref, p_ref, gffn_ref, wfi_ref, wfo_ref, gple_ref, wpg_ref, wpp_ref, out_ref):
    x = x_ref[...]
    h = _rms_rows(x, gffn_ref[...]).astype(BF16)
    y = x
    for c in range(D_FF // FFN_CHUNK):
        gate = _dot(h, wfi_ref[:, c * FFN_CHUNK:(c + 1) * FFN_CHUNK])
        up = _dot(h, wfi_ref[:, D_FF + c * FFN_CHUNK:D_FF + (c + 1) * FFN_CHUNK])
        act = (gate * _sigmoid(gate) * up).astype(BF16)
        y = y + _dot(act, wfo_ref[c * FFN_CHUNK:(c + 1) * FFN_CHUNK, :])
    h2 = _rms_rows(y, gple_ref[...]).astype(BF16)
    ple_gate = _sigmoid(_dot(h2, wpg_ref[...]))
    out_ref[...] = y + ple_gate * _dot(p_ref[...].astype(BF16), wpp_ref[...])


def _ffn_call(x1, p2, gffn, wfi, wfo, gple, wpg, wpp):
    t = x1.shape[0]
    tm = FFN_TM
    row = lambda w_: pl.BlockSpec((tm, w_), lambda i: (i, 0))
    full = lambda a: pl.BlockSpec(a.shape, lambda i: (0, 0))
    return pl.pallas_call(
        _ffn_kernel,
        grid=(t // tm,),
        in_specs=[row(D_MODEL), row(PLE_DIM), full(gffn), full(wfi), full(wfo), full(gple),
                  full(wpg), full(wpp)],
        out_specs=row(D_MODEL),
        out_shape=jax.ShapeDtypeStruct((t, D_MODEL), F32),
        compiler_params=pltpu.CompilerParams(dimension_semantics=("parallel",),
                                             vmem_limit_bytes=VMEM_LIMIT),
    )(x1, p2, gffn, wfi, wfo, gple, wpg, wpp)


def _block_diag_ones(width):
    idx = np.arange(width) // HEAD_DIM
    return jnp.asarray(idx[:, None] == idx[None, :], dtype=BF16)


def _inv_freq():
    return ROPE_THETA ** (-jnp.arange(ROPE_HALF, dtype=F32) / ROPE_HALF)


def _inv_freq_lanes(width):
    inv_freq = _inv_freq()
    per_head = jnp.concatenate([inv_freq, inv_freq, jnp.zeros((HEAD_DIM - ROPE_DIM,), F32)])
    return jnp.tile(per_head, width // HEAD_DIM)[None, :]


def _rope_spread_matrix(width):
    m = np.zeros((LANES, 3 * width), np.float32)
    for lane in range(width):
        d = lane % HEAD_DIM
        if d < ROPE_DIM:
            m[d % ROPE_HALF, lane] = 1.0
        else:
            m[2 * ROPE_HALF, lane] = 1.0
        if d < ROPE_HALF:
            m[ROPE_HALF + d, width + lane] = -1.0
        elif d < ROPE_DIM:
            m[ROPE_HALF + d - ROPE_HALF, 2 * width + lane] = 1.0
    return jnp.asarray(m, dtype=BF16)


def _split_w_in(w):
    parts = jnp.split(w, IN_CUTS, axis=-1)
    (q_n, kc, vc, ks, vs, kw, vw, gate_n, q_m, k_m, v_m, gate_a, gate_b) = parts
    d = w.shape[0]
    q_n = q_n.reshape(d, NSA_GROUPS, NSA_REP, HEAD_DIM).transpose(0, 2, 1, 3).reshape(d, -1)
    gate_n = jnp.pad(gate_n, ((0, 0), (0, GATE_ROWS - gate_n.shape[1])))
    w_tok = jnp.concatenate([k_m, ks, kw, kc, vc], axis=1)
    w_feat = jnp.concatenate([q_n, q_m, v_m, vs, vw, gate_n], axis=1).T
    w_gate = jnp.concatenate([gate_a, gate_b], axis=1)
    return w_tok.astype(BF16), w_feat.astype(BF16), w_gate.astype(BF16)


def _compress_weights(w1, w2, pe):
    eye = jnp.eye(NSA_GROUPS, dtype=F32)
    halves = []
    for part in (w1[:CMP_STRIDE * HEAD_DIM], w1[CMP_STRIDE * HEAD_DIM:]):
        p3 = part.reshape(CMP_STRIDE, HEAD_DIM, CMP_HIDDEN)
        halves.append(jnp.einsum('idh,gk->igdkh', p3, eye)
                      .reshape(CMP_STRIDE * NSA_GROUPS * HEAD_DIM, NSA_GROUPS * CMP_HIDDEN))
    w_big = jnp.concatenate(halves, axis=1).astype(BF16)
    w2_bd = jnp.einsum('hd,kg->khgd', w2, eye).reshape(NSA_GROUPS * CMP_HIDDEN,
                                                        NSA_GROUPS * HEAD_DIM).astype(BF16)
    pe_rows = [jnp.broadcast_to(pe[a:a + CMP_STRIDE, None, :], (CMP_STRIDE, NSA_GROUPS, HEAD_DIM))
               .reshape(1, -1) for a in (0, CMP_STRIDE)]
    return w_big, w2_bd, pe_rows


def _overlap_matrix_t(n_cmp):
    c = np.arange(n_cmp)
    j = np.arange(SEL_BLOCKS_MAX)
    start, end = c * CMP_STRIDE, c * CMP_STRIDE + CMP_LEN - 1
    ov = (start[None, :] <= j[:, None] * SEL_BLOCK + SEL_BLOCK - 1) & (end[None, :] >= j[:, None] * SEL_BLOCK)
    return jnp.asarray(ov, dtype=BF16)


def kernel(x, p, positions, g_mix, w_in, nsa_q_gain, nsa_kc_gain, nsa_ks_gain, nsa_kw_gain, nsa_pe_k, nsa_pe_v, nsa_ck_w1, nsa_ck_w2, nsa_cv_w1, nsa_cv_w2, moba_q_gain, moba_k_gain, w_up_nsa, w_up_moba, w_out, g_ffn, w_ffn_in, w_ffn_out, g_ple, w_ple_gate, w_ple_proj):
    b, s, d = x.shape
    depth = w_in.shape[0]
    assert s % PAD_MULT == 0 and d == D_MODEL
    t = b * s
    n_sub = s // CMP_STRIDE

    assert ROPE_HALF == SUBLANES
    posr = positions.reshape(t // PROJ_TM, 1, PROJ_TM)
    pos_end = jnp.concatenate([positions[:, CMP_LEN - 1::CMP_STRIDE], positions[:, -1:]], axis=1)[:, :, None]
    invf128 = _inv_freq_lanes(LANES)
    invf8 = _inv_freq()[:, None]
    spread = _rope_spread_matrix(256)
    bd256, bd128 = _block_diag_ones(256), _block_diag_ones(LANES)
    ovl_t = _overlap_matrix_t(n_sub)
    tile = lambda g, n: jnp.tile(g, n)
    xi = x.reshape(t, d)

    for i in range(depth):
        w_tok, w_feat, w_gate = _split_w_in(w_in[i])
        g_tok = jnp.concatenate([tile(moba_k_gain[i], MOBA_HEADS), tile(nsa_ks_gain[i], NSA_GROUPS),
                                 tile(nsa_kw_gain[i], NSA_GROUPS)])[None, :]
        g_feat = (jnp.concatenate([tile(nsa_q_gain[i], NSA_HEADS), tile(moba_q_gain[i], MOBA_HEADS)])
                  * (ATTN_SCALE * LOG2E))[:, None]
        gmix = g_mix[i][None, :]
        (qn, qm, gn, vm, vs, vw, km, ks, kw, kc_raw, vc_raw) = _proj_call(
            xi, posr, gmix, w_tok, w_feat, g_tok, g_feat, invf8, bd256, spread, b, s)

        wk_big, w2k_bd, pe_k = _compress_weights(nsa_ck_w1[i], nsa_ck_w2[i], nsa_pe_k[i])
        wv_big, w2v_bd, pe_v = _compress_weights(nsa_cv_w1[i], nsa_cv_w2[i], nsa_pe_v[i])
        pe4 = jnp.concatenate(pe_k + pe_v, axis=0)
        kc, vct = _compress_call(kc_raw.reshape(b, n_sub, CMP_STRIDE * KV_WIDTH),
                                 vc_raw.reshape(b, n_sub, CMP_STRIDE * KV_WIDTH),
                                 pe4, wk_big, wv_big, w2k_bd, w2v_bd,
                                 tile(nsa_kc_gain[i], NSA_GROUPS)[None, :], pos_end,
                                 invf128, bd128)

        r3 = lambda a: a.reshape(b, s, a.shape[-1])
        y_nsa = _nsa_call(qn, kc, vct, r3(ks), vs.reshape(b, s // NSA_TK, KV_WIDTH, NSA_TK), r3(kw),
                          vw.reshape(b, s // WIN_TK, KV_WIDTH, WIN_TK), gn, ovl_t)
        y_moba = _moba_call(qm, r3(km), vm.reshape(b, s // MOBA_BLOCK, Q_WIDTH, MOBA_BLOCK))

        w_un = (w_up_nsa[i].reshape(NSA_GROUPS, NSA_REP, HEAD_DIM, d).transpose(1, 0, 2, 3)
                .reshape(NSA_HEADS * HEAD_DIM, d).astype(BF16))
        x1 = _merge_call(xi, y_nsa.reshape(t, -1), y_moba.reshape(t, -1), gmix, w_gate, w_un,
                         w_up_moba[i].astype(BF16), w_out[i].astype(BF16))
        xi = _ffn_call(x1, p[i].reshape(t, PLE_DIM), g_ffn[i][None, :], w_ffn_in[i].astype(BF16),
                       w_ffn_out[i].astype(BF16), g_ple[i][None, :], w_ple_gate[i].astype(BF16),
                       w_ple_proj[i].astype(BF16))
    return xi.reshape(b, s, d)
```

```python
import jax
import jax.numpy as jnp
import numpy as np
from jax import lax
from jax.experimental import pallas as pl
from jax.experimental.pallas import tpu as pltpu

F32 = jnp.float32
BF16 = jnp.bfloat16

D_MODEL = 1024
HEAD_DIM = 64
ROPE_DIM = HEAD_DIM // 4
ROPE_HALF = ROPE_DIM // 2
ROPE_THETA = 500000.0
NORM_EPS = 1e-6
ATTN_SCALE = HEAD_DIM ** -0.5

NSA_HEADS = 8
NSA_GROUPS = 2
NSA_REP = NSA_HEADS // NSA_GROUPS
CMP_LEN = 32
CMP_STRIDE = 16
CMP_HIDDEN = 256
SEL_BLOCK = 64
SEL_TOPN = 8
WINDOW = 512

MOBA_HEADS = 8
MOBA_BLOCK = 256
MOBA_TOPK = 3

PAD_MULT = 256
D_FF = ((-(-8 * D_MODEL // 3)) + 255) // 256 * 256
PLE_DIM = 256

IN_SPLITS = ((NSA_HEADS * HEAD_DIM,) + (NSA_GROUPS * HEAD_DIM,) * 6 + (3 * NSA_HEADS,)
             + (MOBA_HEADS * HEAD_DIM,) * 3 + (D_MODEL, D_MODEL))
IN_CUTS = tuple(int(c) for c in np.cumsum(IN_SPLITS)[:-1])

LANES = 128
SUBLANES = 8
NEG_BIG = -1e30
LOG2E = 1.4426950408889634
SUM_ROWS = 16
VMEM_LIMIT = 56 * 1024 * 1024

Q_WIDTH = NSA_HEADS * HEAD_DIM
KV_WIDTH = NSA_GROUPS * HEAD_DIM
GATE_ROWS = 32
TOK_NORMED = Q_WIDTH + 2 * KV_WIDTH
TOK_COLS = TOK_NORMED + 2 * KV_WIDTH
FEAT_ROWS = 3 * Q_WIDTH + 2 * KV_WIDTH + GATE_ROWS

PROJ_TM = MOBA_BLOCK
NSA_TQ = 256
NSA_TK = 256
WIN_TK = 256
SCORE_LANES = 256
MERGE_TM = 256
FFN_TM = 256
FFN_CHUNK = D_FF // 2
SEL_BLOCKS_MAX = 32
MOBA_BLOCKS_MAX = 8
MOBA_STEP_HEADS = 8


def _dot(a, b):
    return jnp.dot(a, b, preferred_element_type=F32)


def _dot_nt(a, b):
    return lax.dot_general(a, b, (((1,), (1,)), ((), ())), preferred_element_type=F32)


def _split_bf16(a_f32):
    hi = a_f32.astype(BF16)
    return hi, (a_f32 - hi.astype(F32)).astype(BF16)


def _sigmoid(x):
    return 1.0 / (1.0 + jnp.exp(-x))


def _rms_rows(x, g):
    return x * lax.rsqrt(jnp.mean(x * x, axis=-1, keepdims=True) + NORM_EPS) * g


def _head_norm(t, bd, gain):
    hi, lo = _split_bf16(t * t)
    ss = _dot(hi, bd) + _dot(lo, bd)
    return t * lax.rsqrt(ss * (1.0 / HEAD_DIM) + NORM_EPS) * gain


def _rope_tables(pos_col, invf):
    ang = pos_col.astype(F32) * invf
    cos_a, sin_a = jnp.cos(ang), jnp.sin(ang)
    d = lax.broadcasted_iota(jnp.int32, ang.shape, 1) & (HEAD_DIM - 1)
    s_lo = jnp.where(d < ROPE_HALF, -sin_a, 0.0)
    s_hi = jnp.where((d >= ROPE_HALF) & (d < ROPE_DIM), sin_a, 0.0)
    return cos_a, s_lo, s_hi


def _rope(y, tables):
    cos_a, s_lo, s_hi = tables
    w = y.shape[1]
    return (y * cos_a + pltpu.roll(y, w - ROPE_HALF, 1) * s_lo
            + pltpu.roll(y, ROPE_HALF, 1) * s_hi)


def _ones_rows(tk):
    r = lax.broadcasted_iota(jnp.int32, (SUM_ROWS, tk), 0)
    return jnp.where(r == 0, 1.0, 0.0).astype(BF16)


def _with_sum_row(v_rows, ones_rows):
    return jnp.concatenate([v_rows, ones_rows], axis=0)


def _slot_update(load_s, bias, block_rows, m_ref, acc_ref, hs, v_aug):
    m = m_ref[hs, 0:1, :]
    n_sub = len(bias)
    if block_rows:
        dropped = [bias[a] < 0.0 for a in range(n_sub)]
        m_new = m
        for a in range(n_sub):
            top = jnp.max(load_s(a), axis=0, keepdims=True)
            m_new = jnp.maximum(m_new, jnp.where(dropped[a], NEG_BIG, top))
        e = [jnp.exp2(load_s(a) + jnp.where(dropped[a], NEG_BIG, -m_new)) for a in range(n_sub)]
    else:
        m_new = m
        for a in range(n_sub):
            m_new = jnp.maximum(m_new, jnp.max(load_s(a) + bias[a], axis=0, keepdims=True))
        e = [jnp.exp2(load_s(a) + bias[a] - m_new) for a in range(n_sub)]
    e = e[0] if n_sub == 1 else jnp.concatenate(e, axis=0)
    alpha = jnp.exp2(m - m_new)
    m_ref[hs, 0:1, :] = m_new
    acc_ref[hs] = alpha * acc_ref[hs] + _dot(v_aug, e.astype(BF16))


def _reset_state(m_ref, acc_ref):
    m_ref[...] = jnp.full(m_ref.shape, NEG_BIG, F32)
    acc_ref[...] = jnp.zeros(acc_ref.shape, F32)


def _normalised(acc, scale_row=1.0):
    return acc[:HEAD_DIM] * (scale_row / jnp.maximum(acc[HEAD_DIM:HEAD_DIM + 1], 1e-30))


def _rank_rows(v, n_rows):
    n, q = v.shape
    slabs = [v[a:a + SUBLANES] for a in range(0, n, SUBLANES)]
    ranks = [jnp.zeros((SUBLANES, q), F32) for _ in slabs]
    sub = lax.broadcasted_iota(jnp.int32, (SUBLANES, q), 0)
    for jp in range(n_rows):
        row = v[jp:jp + 1, :]
        for si, slab in enumerate(slabs):
            first = si * SUBLANES
            if first > jp:
                beats = jnp.where(row >= slab, 1.0, 0.0)
            elif first + SUBLANES - 1 < jp:
                beats = jnp.where(row > slab, 1.0, 0.0)
            else:
                ge = jnp.where(row >= slab, 1.0, 0.0)
                gt = jnp.where(row > slab, 1.0, 0.0)
                beats = gt + (ge - gt) * jnp.where(sub > jp - first, 1.0, 0.0)
            ranks[si] = ranks[si] + beats
    return ranks[0] if len(ranks) == 1 else jnp.concatenate(ranks, axis=0)


def _proj_kernel(x_ref, posr_ref, gmix_ref, wtok_ref, wfeat_ref, gtok_ref, gfeat_ref,
                 invf8_ref, bd_ref, spread_ref,
                 qn_ref, qm_ref, gn_ref, vm_ref, vs_ref, vw_ref, km_ref, ks_ref, kw_ref,
                 kc_ref, vc_ref):
    tm = x_ref.shape[0]
    h = _rms_rows(x_ref[...], gmix_ref[...]).astype(BF16)

    ang = invf8_ref[...] * posr_ref[0].astype(F32)
    cos_a, sin_a = jnp.cos(ang), jnp.sin(ang)
    rows_t = jnp.concatenate([cos_a, sin_a, jnp.ones((SUBLANES, tm), F32),
                              jnp.zeros((LANES - 3 * SUBLANES, tm), F32)], axis=0)
    hi, lo = _split_bf16(rows_t.T)
    spread = _dot(hi, spread_ref[...]) + _dot(lo, spread_ref[...])
    tables = (spread[:, :256], spread[:, 256:512], spread[:, 512:])

    acc = _dot(h, wtok_ref[...])
    bd = bd_ref[...]
    for c in range(TOK_NORMED // 256):
        t = acc[:, c * 256:(c + 1) * 256]
        y = _rope(_head_norm(t, bd, gtok_ref[:, c * 256:(c + 1) * 256]), tables).astype(BF16)
        if c < 2:
            km_ref[:, c * 256:(c + 1) * 256] = y
        else:
            ks_ref[...] = y[:, :KV_WIDTH]
            kw_ref[...] = y[:, KV_WIDTH:]
    kc_ref[...] = acc[:, TOK_NORMED:TOK_NORMED + KV_WIDTH]
    vc_ref[...] = acc[:, TOK_NORMED + KV_WIDTH:]

    acc_t = _dot_nt(wfeat_ref[...], h)
    for hh in range(2 * NSA_HEADS):
        t = acc_t[hh * HEAD_DIM:(hh + 1) * HEAD_DIM, :]
        ss = jnp.sum(t * t, axis=0, keepdims=True)
        y = t * lax.rsqrt(ss * (1.0 / HEAD_DIM) + NORM_EPS) * gfeat_ref[hh * HEAD_DIM:(hh + 1) * HEAD_DIM, :]
        a, b = y[:ROPE_HALF], y[ROPE_HALF:ROPE_DIM]
        y = jnp.concatenate([a * cos_a - b * sin_a, b * cos_a + a * sin_a, y[ROPE_DIM:]], axis=0)
        dst = qn_ref if hh < NSA_HEADS else qm_ref
        r0 = (hh % NSA_HEADS) * HEAD_DIM
        dst[0, r0:r0 + HEAD_DIM, :] = y.astype(BF16)
    o = 2 * Q_WIDTH
    vm_ref[0] = acc_t[o:o + Q_WIDTH].astype(BF16)
    vs_ref[0] = acc_t[o + Q_WIDTH:o + Q_WIDTH + KV_WIDTH].astype(BF16)
    vw = acc_t[o + Q_WIDTH + KV_WIDTH:o + Q_WIDTH + 2 * KV_WIDTH].astype(BF16)
    for j in range(PROJ_TM // WIN_TK):
        vw_ref[0, j] = vw[:, j * WIN_TK:(j + 1) * WIN_TK]
    gn_ref[0] = acc_t[o + Q_WIDTH + 2 * KV_WIDTH:]


def _proj_call(x2, posr, gmix, wtok, wfeat, gtok, gfeat, invf8, bd, spread, b, s):
    t = x2.shape[0]
    tm = PROJ_TM
    nt = s // tm
    row = lambda w_: pl.BlockSpec((tm, w_), lambda i: (i, 0))
    full = lambda a: pl.BlockSpec(a.shape, lambda i: (0,) * a.ndim)
    feat = lambda r: pl.BlockSpec((1, r, tm), lambda i: (i // nt, 0, i % nt))
    tile = lambda r: pl.BlockSpec((1, r, tm), lambda i: (i, 0, 0))
    out_shape = [
        jax.ShapeDtypeStruct((b, Q_WIDTH, s), BF16),
        jax.ShapeDtypeStruct((b, Q_WIDTH, s), BF16),
        jax.ShapeDtypeStruct((b, GATE_ROWS, s), F32),
        jax.ShapeDtypeStruct((t // tm, Q_WIDTH, tm), BF16),
        jax.ShapeDtypeStruct((t // tm, KV_WIDTH, tm), BF16),
        jax.ShapeDtypeStruct((t // tm, tm // WIN_TK, KV_WIDTH, WIN_TK), BF16),
        jax.ShapeDtypeStruct((t, Q_WIDTH), BF16),
        jax.ShapeDtypeStruct((t, KV_WIDTH), BF16),
        jax.ShapeDtypeStruct((t, KV_WIDTH), BF16),
        jax.ShapeDtypeStruct((t, KV_WIDTH), F32),
        jax.ShapeDtypeStruct((t, KV_WIDTH), F32),
    ]
    out_specs = [feat(Q_WIDTH), feat(Q_WIDTH), feat(GATE_ROWS), tile(Q_WIDTH), tile(KV_WIDTH),
                 pl.BlockSpec((1, tm // WIN_TK, KV_WIDTH, WIN_TK), lambda i: (i, 0, 0, 0)),
                 row(Q_WIDTH), row(KV_WIDTH), row(KV_WIDTH), row(KV_WIDTH), row(KV_WIDTH)]
    return pl.pallas_call(
        _proj_kernel,
        grid=(t // tm,),
        in_specs=[row(D_MODEL), pl.BlockSpec((1, 1, tm), lambda i: (i, 0, 0)), full(gmix),
                  full(wtok), full(wfeat), full(gtok), full(gfeat), full(invf8), full(bd), full(spread)],
        out_specs=out_specs,
        out_shape=out_shape,
        compiler_params=pltpu.CompilerParams(dimension_semantics=("parallel",),
                                             vmem_limit_bytes=VMEM_LIMIT),
    )(x2, posr, gmix, wtok, wfeat, gtok, gfeat, invf8, bd, spread)


def _compress_kernel(kc4_ref, vc4_ref, pe_ref, wk_ref, wv_ref, w2k_ref, w2v_ref, gain_ref,
                     pose_ref, invf_ref, bd_ref, kc_out, vct_out):
    half = NSA_GROUPS * CMP_HIDDEN

    def comp(x, pe_a, pe_b, w_ref, w2_ref):
        a = _dot((x + pe_a).astype(BF16), w_ref[:, :half])
        b = _dot((x + pe_b).astype(BF16), w_ref[:, half:])
        hid = a + pltpu.roll(b, b.shape[0] - 1, 0)
        act = hid * _sigmoid(hid)
        return _dot(act.astype(BF16), w2_ref[...])

    kc = comp(kc4_ref[0], pe_ref[0:1, :], pe_ref[1:2, :], wk_ref, w2k_ref)
    vc = comp(vc4_ref[0], pe_ref[2:3, :], pe_ref[3:4, :], wv_ref, w2v_ref)
    kc = _head_norm(kc, bd_ref[...], gain_ref[...])
    kc = _rope(kc, _rope_tables(pose_ref[0], invf_ref[...]))
    kc_out[0] = kc.astype(BF16)
    vct_out[0] = vc.T.astype(BF16)


def _compress_call(kc4, vc4, pe4, wk, wv, w2k, w2v, gain, pos_end, invf, bd):
    b, n_sub, width = kc4.shape
    blk = lambda shp: pl.BlockSpec((1,) + shp, lambda i: (i, 0, 0))
    full = lambda a: pl.BlockSpec(a.shape, lambda i: (0,) * a.ndim)
    return pl.pallas_call(
        _compress_kernel,
        grid=(b,),
        in_specs=[blk((n_sub, width)), blk((n_sub, width)), full(pe4), full(wk), full(wv),
                  full(w2k), full(w2v), full(gain), blk((n_sub, 1)), full(invf), full(bd)],
        out_specs=[blk((n_sub, KV_WIDTH)), blk((KV_WIDTH, n_sub))],
        out_shape=[jax.ShapeDtypeStruct((b, n_sub, KV_WIDTH), BF16),
                   jax.ShapeDtypeStruct((b, KV_WIDTH, n_sub), BF16)],
        compiler_params=pltpu.CompilerParams(dimension_semantics=("parallel",),
                                             vmem_limit_bytes=VMEM_LIMIT),
    )(kc4, vc4, pe4, wk, wv, w2k, w2v, gain, pos_end, invf, bd)


def _nsa_kernel(q_ref, kc_ref, vct_ref, ks_ref, vs_ref, kw_ref, vw_ref, gate_ref, ovl_ref, out_ref,
                sel_scr, lhs_scr, m_scr, acc_scr, o_scr, *bufs):
    tq = q_ref.shape[2]
    n_cmp = kc_ref.shape[1]
    n_slots = NSA_GROUPS * NSA_REP
    n_units = len(bufs) // 2
    s_scr, c_scr = bufs[:n_units], bufs[n_units:]
    per_unit = n_slots // n_units
    i = pl.program_id(1)
    t0 = i * tq
    t_row = t0 + lax.broadcasted_iota(jnp.int32, (1, tq), 1)

    zero = jnp.zeros((HEAD_DIM, tq), BF16)
    for g in range(NSA_GROUPS):
        for r in range(NSA_REP):
            hs = g * NSA_REP + r
            rows = q_ref[0, r * KV_WIDTH + g * HEAD_DIM:r * KV_WIDTH + (g + 1) * HEAD_DIM, :]
            lhs_scr[:, hs * tq:(hs + 1) * tq] = jnp.concatenate([rows, zero] if g == 0 else [zero, rows],
                                                                axis=0)
    gsig = _sigmoid(gate_ref[0])

    def unit_queries(unit):
        return lhs_scr[:, unit * SCORE_LANES:(unit + 1) * SCORE_LANES]

    def scores_into(unit, k_tile):
        tk = k_tile.shape[0]
        s_scr[unit][0:tk, :] = _dot(k_tile, unit_queries(unit))

    def sel_keys(kt):
        return ks_ref[0, pl.ds(pl.multiple_of(kt * NSA_TK, NSA_TK), NSA_TK), :]

    def win_keys(kt):
        return kw_ref[0, pl.ds(pl.multiple_of(kt * WIN_TK, WIN_TK), WIN_TK), :]

    for unit in range(n_units):
        c_scr[unit][...] = _dot(kc_ref[0], unit_queries(unit))
    for unit in range(n_units):
        scores_into(unit, sel_keys(0))

    c_idx = lax.broadcasted_iota(jnp.int32, (n_cmp, tq), 0)
    valid_c = (c_idx * CMP_STRIDE + (CMP_LEN - 1) <= t_row) & (c_idx < n_cmp - 1)
    bias_c = jnp.where(valid_c, 0.0, NEG_BIG)
    p_sum = []
    for hs in range(n_slots):
        g = hs // NSA_REP
        unit, hh = divmod(hs, per_unit)
        x = c_scr[unit][:, hh * tq:(hh + 1) * tq] + bias_c
        m = jnp.max(x, axis=0, keepdims=True)
        e = jnp.where(valid_c, jnp.exp2(x - m), 0.0)
        p = e / jnp.maximum(jnp.sum(e, axis=0, keepdims=True), 1e-30)
        o_c = _dot(vct_ref[0, g * HEAD_DIM:(g + 1) * HEAD_DIM, :], p.astype(BF16))
        o_scr[hs] = o_c * gsig[3 * hs:3 * hs + 1, :]
        if hs % NSA_REP == 0:
            p_sum.append(p)
        else:
            p_sum[g] = p_sum[g] + p

    cur = t_row // SEL_BLOCK
    j_idx = lax.broadcasted_iota(jnp.int32, (SEL_BLOCKS_MAX, tq), 0)
    forced = (j_idx == 0) | (j_idx == cur) | (j_idx == cur - 1)
    for g in range(NSA_GROUPS):
        hi, lo = _split_bf16(p_sum[g])
        imp = _dot(ovl_ref[...], hi) + _dot(ovl_ref[...], lo)
        imp = jnp.where(forced, jnp.inf, jnp.where(j_idx > cur, -jnp.inf, imp))
        sel_scr[g] = jnp.where(_rank_rows(imp, SEL_BLOCKS_MAX) < SEL_TOPN, 0.0, NEG_BIG)

    def run_tile(v_tile, bias_of_group, block_rows, sub, next_keys):
        ones_rows = _ones_rows(v_tile.shape[1])
        bias = [bias_of_group(g) for g in range(NSA_GROUPS)]
        for unit in range(n_units):
            g = (unit * per_unit) // NSA_REP
            v_aug = _with_sum_row(v_tile[g * HEAD_DIM:(g + 1) * HEAD_DIM, :], ones_rows)
            for hh in range(per_unit):
                load = lambda a, hh=hh, unit=unit: s_scr[unit][a * sub:(a + 1) * sub, hh * tq:(hh + 1) * tq]
                _slot_update(load, bias[g], block_rows, m_scr, acc_scr, unit * per_unit + hh, v_aug)
            if next_keys is not None:
                scores_into(unit, next_keys())

    def fold_branch(branch):
        for hs in range(n_slots):
            o_scr[hs] = o_scr[hs] + _normalised(acc_scr[hs], gsig[3 * hs + branch:3 * hs + branch + 1, :])

    blocks_per_tile = NSA_TK // SEL_BLOCK
    first_win = jnp.maximum((t0 - (WINDOW - 1)) // WIN_TK, 0)
    last_win = (t0 + tq - 1) // WIN_TK

    def sel_rows(kt, g):
        return [sel_scr[g, pl.ds(kt * blocks_per_tile + a, 1), :] for a in range(blocks_per_tile)]

    def sel_diag_bias(kt, g):
        out = []
        for a, row in enumerate(sel_rows(kt, g)):
            kpos = kt * NSA_TK + a * SEL_BLOCK + lax.broadcasted_iota(jnp.int32, (SEL_BLOCK, tq), 0)
            out.append(jnp.where(kpos <= t_row, row, NEG_BIG))
        return out

    def sel_body(kt, carry):
        run_tile(vs_ref[0, kt], lambda g: sel_rows(kt, g), True, SEL_BLOCK, lambda: sel_keys(kt + 1))
        return carry

    last_sel = (t0 + tq - 1) // NSA_TK
    _reset_state(m_scr, acc_scr)
    lax.fori_loop(0, last_sel, sel_body, 0)
    run_tile(vs_ref[0, last_sel], lambda g: sel_diag_bias(last_sel, g), False, SEL_BLOCK,
             lambda: win_keys(first_win))
    fold_branch(1)

    def win_bias(kt):
        diff = t_row - (kt * WIN_TK + lax.broadcasted_iota(jnp.int32, (WIN_TK, tq), 0))
        return [jnp.where((diff >= 0) & (diff < WINDOW), 0.0, NEG_BIG)]

    def win_body(kt, carry):
        run_tile(vw_ref[0, kt], lambda g: win_bias(kt), False, WIN_TK, lambda: win_keys(kt + 1))
        return carry

    _reset_state(m_scr, acc_scr)
    lax.fori_loop(first_win, last_win, win_body, 0)
    run_tile(vw_ref[0, last_win], lambda g: win_bias(last_win), False, WIN_TK, None)
    fold_branch(2)

    for r in range(NSA_REP):
        tile_t = jnp.concatenate([o_scr[r], o_scr[NSA_REP + r]], axis=0)
        out_ref[0, :, r * KV_WIDTH:(r + 1) * KV_WIDTH] = tile_t.T.astype(BF16)


def _nsa_call(qn, kc, vct, ks, vs, kw, vw, gn, ovl):
    b, _, s = qn.shape
    tq = NSA_TQ
    n_slots = NSA_GROUPS * NSA_REP
    n_units = n_slots * tq // SCORE_LANES
    assert s // SEL_BLOCK <= SEL_BLOCKS_MAX
    seq = lambda a: pl.BlockSpec((1,) + a.shape[1:], lambda bi, i: (bi,) + (0,) * (a.ndim - 1))
    full = lambda a: pl.BlockSpec(a.shape, lambda bi, i: (0,) * a.ndim)
    return pl.pallas_call(
        _nsa_kernel,
        grid=(b, s // tq),
        in_specs=[pl.BlockSpec((1, Q_WIDTH, tq), lambda bi, i: (bi, 0, i)),
                  seq(kc), seq(vct), seq(ks), seq(vs), seq(kw), seq(vw),
                  pl.BlockSpec((1, GATE_ROWS, tq), lambda bi, i: (bi, 0, i)), full(ovl)],
        out_specs=pl.BlockSpec((1, tq, Q_WIDTH), lambda bi, i: (bi, i, 0)),
        out_shape=jax.ShapeDtypeStruct((b, s, Q_WIDTH), BF16),
        scratch_shapes=[pltpu.VMEM((NSA_GROUPS, SEL_BLOCKS_MAX, tq), F32),
                        pltpu.VMEM((KV_WIDTH, n_slots * tq), BF16),
                        pltpu.VMEM((n_slots, SUBLANES, tq), F32),
                        pltpu.VMEM((n_slots, HEAD_DIM + SUM_ROWS, tq), F32),
                        pltpu.VMEM((n_slots, HEAD_DIM, tq), F32)]
        + [pltpu.VMEM((max(NSA_TK, WIN_TK), SCORE_LANES), F32)] * n_units
        + [pltpu.VMEM((kc.shape[1], SCORE_LANES), F32)] * n_units,
        compiler_params=pltpu.CompilerParams(dimension_semantics=("parallel", "parallel"),
                                             vmem_limit_bytes=VMEM_LIMIT),
    )(qn, kc, vct, ks, vs, kw, vw, gn, ovl)


def _moba_kernel(q_ref, k_ref, v_ref, out_ref, km_scr, sel_scr, lhs_scr, m_scr, acc_scr, *s_scr):
    bs = MOBA_BLOCK
    n_blocks = k_ref.shape[1] // bs
    n_slots = len(s_scr)
    i = pl.program_id(2)

    @pl.when(i == 0)
    def _():
        km_scr[...] = jnp.zeros(km_scr.shape, F32)
        for n in range(n_blocks):
            km_scr[n:n + 1, :] = jnp.mean(k_ref[0, n * bs:(n + 1) * bs, :].astype(F32), axis=0,
                                          keepdims=True)

    def pair_lanes(hs):
        return slice((hs // 2) * LANES, (hs // 2 + 1) * LANES)

    zero = jnp.zeros((HEAD_DIM, bs), BF16)
    n_idx = lax.broadcasted_iota(jnp.int32, (MOBA_BLOCKS_MAX, bs), 0)
    for hs in range(n_slots):
        rows = q_ref[0, hs * HEAD_DIM:(hs + 1) * HEAD_DIM, :]
        lhs_scr[hs] = jnp.concatenate([rows, zero] if hs % 2 == 0 else [zero, rows], axis=0)

    def scores_into(hs, n):
        k_tile = k_ref[0, pl.ds(pl.multiple_of(n * bs, bs), bs), pair_lanes(hs)]
        s_scr[hs][...] = _dot(k_tile, lhs_scr[hs])

    block_scores = []
    for hs in range(n_slots):
        km_hi, km_lo = _split_bf16(km_scr[:, pair_lanes(hs)])
        block_scores.append(_dot(km_hi, lhs_scr[hs]) + _dot(km_lo, lhs_scr[hs]))
    for hs in range(n_slots):
        scores_into(hs, 0)

    for hs in range(n_slots):
        sc = jnp.where(n_idx < i, block_scores[hs], -jnp.inf)
        sel_scr[hs] = jnp.where((_rank_rows(sc, MOBA_BLOCKS_MAX) < MOBA_TOPK) & (n_idx < i), 0.0, NEG_BIG)

    ones_rows = _ones_rows(bs)

    def run_block(n, bias_of_slot, block_rows, prefetch):
        for hs in range(n_slots):
            v_aug = _with_sum_row(v_ref[0, n, hs * HEAD_DIM:(hs + 1) * HEAD_DIM, :], ones_rows)
            load = lambda a, hs=hs: s_scr[hs][...]
            _slot_update(load, bias_of_slot(hs), block_rows, m_scr, acc_scr, hs, v_aug)
            if prefetch:
                scores_into(hs, n + 1)

    def past_body(n, carry):
        run_block(n, lambda hs: [sel_scr[hs, pl.ds(n, 1), :]], True, True)
        return carry

    _reset_state(m_scr, acc_scr)
    lax.fori_loop(0, i, past_body, 0)
    key_i = lax.broadcasted_iota(jnp.int32, (bs, bs), 0)
    qry_i = lax.broadcasted_iota(jnp.int32, (bs, bs), 1)
    bias_own = [jnp.where(key_i <= qry_i, 0.0, NEG_BIG)]
    run_block(i, lambda hs: bias_own, False, False)
    for p in range(n_slots // 2):
        o_t = jnp.concatenate([_normalised(acc_scr[2 * p]), _normalised(acc_scr[2 * p + 1])], axis=0)
        out_ref[0, :, p * LANES:(p + 1) * LANES] = o_t.T.astype(BF16)


def _moba_call(qm, km, vm):
    b, w, s = qm.shape
    bs = MOBA_BLOCK
    nb = s // bs
    wb = MOBA_STEP_HEADS * HEAD_DIM
    assert nb <= MOBA_BLOCKS_MAX and w % wb == 0
    return pl.pallas_call(
        _moba_kernel,
        grid=(b, w // wb, nb),
        in_specs=[pl.BlockSpec((1, wb, bs), lambda bi, p, i: (bi, p, i)),
                  pl.BlockSpec((1, s, wb), lambda bi, p, i: (bi, 0, p)),
                  pl.BlockSpec((1, nb, wb, bs), lambda bi, p, i: (bi, 0, p, 0))],
        out_specs=pl.BlockSpec((1, bs, wb), lambda bi, p, i: (bi, i, p)),
        out_shape=jax.ShapeDtypeStruct((b, s, w), BF16),
        scratch_shapes=[pltpu.VMEM((MOBA_BLOCKS_MAX, wb), F32),
                        pltpu.VMEM((MOBA_STEP_HEADS, MOBA_BLOCKS_MAX, bs), F32),
                        pltpu.VMEM((MOBA_STEP_HEADS, LANES, bs), BF16),
                        pltpu.VMEM((MOBA_STEP_HEADS, SUBLANES, bs), F32),
                        pltpu.VMEM((MOBA_STEP_HEADS, HEAD_DIM + SUM_ROWS, bs), F32)]
        + [pltpu.VMEM((bs, bs), F32)] * MOBA_STEP_HEADS,
        compiler_params=pltpu.CompilerParams(
            dimension_semantics=("parallel", "parallel", "arbitrary"), vmem_limit_bytes=VMEM_LIMIT),
    )(qm, km, vm)


def _merge_kernel(x_ref, yn_ref, ym_ref, gmix_ref, wg_ref, wun_ref, wum_ref, wo_ref, out_ref):
    x = x_ref[...]
    h = _rms_rows(x, gmix_ref[...]).astype(BF16)
    ga = _sigmoid(_dot(h, wg_ref[:, :D_MODEL]))
    gb = _sigmoid(_dot(h, wg_ref[:, D_MODEL:]))
    merged = ga * _dot(yn_ref[...], wun_ref[...]) + gb * _dot(ym_ref[...], wum_ref[...])
    out_ref[...] = x + _dot(merged.astype(BF16), wo_ref[...])


def _merge_call(x2, yn, ym, gmix, wg, wun, wum, wo):
    t = x2.shape[0]
    tm = MERGE_TM
    row = lambda w_: pl.BlockSpec((tm, w_), lambda i: (i, 0))
    full = lambda a: pl.BlockSpec(a.shape, lambda i: (0, 0))
    return pl.pallas_call(
        _merge_kernel,
        grid=(t // tm,),
        in_specs=[row(D_MODEL), row(Q_WIDTH), row(Q_WIDTH), full(gmix), full(wg), full(wun),
                  full(wum), full(wo)],
        out_specs=row(D_MODEL),
        out_shape=jax.ShapeDtypeStruct((t, D_MODEL), F32),
        compiler_params=pltpu.CompilerParams(dimension_semantics=("parallel",),
                                             vmem_limit_bytes=VMEM_LIMIT),
    )(x2, yn, ym, gmix, wg, wun, wum, wo)


def _ffn_kernel(x_ref, p_ref, gffn_ref, wfi_ref, wfo_ref, gple_ref, wpg_ref, wpp_ref, out_ref):
    x = x_ref[...]
    h = _rms_rows(x, gffn_ref[...]).astype(BF16)
    y = x
    for c in range(D_FF // FFN_CHUNK):
        gate = _dot(h, wfi_ref[:, c * FFN_CHUNK:(c + 1) * FFN_CHUNK])
        up = _dot(h, wfi_ref[:, D_FF + c * FFN_CHUNK:D_FF + (c + 1) * FFN_CHUNK])
        act = (gate * _sigmoid(gate) * up).astype(BF16)
        y = y + _dot(act, wfo_ref[c * FFN_CHUNK:(c + 1) * FFN_CHUNK, :])
    h2 = _rms_rows(y, gple_ref[...]).astype(BF16)
    ple_gate = _sigmoid(_dot(h2, wpg_ref[...]))
    out_ref[...] = y + ple_gate * _dot(p_ref[...].astype(BF16), wpp_ref[...])


def _ffn_call(x1, p2, gffn, wfi, wfo, gple, wpg, wpp):
    t = x1.shape[0]
    tm = FFN_TM
    row = lambda w_: pl.BlockSpec((tm, w_), lambda i: (i, 0))
    full = lambda a: pl.BlockSpec(a.shape, lambda i: (0, 0))
    return pl.pallas_call(
        _ffn_kernel,
        grid=(t // tm,),
        in_specs=[row(D_MODEL), row(PLE_DIM), full(gffn), full(wfi), full(wfo), full(gple),
                  full(wpg), full(wpp)],
        out_specs=row(D_MODEL),
        out_shape=jax.ShapeDtypeStruct((t, D_MODEL), F32),
        compiler_params=pltpu.CompilerParams(dimension_semantics=("parallel",),
                                             vmem_limit_bytes=VMEM_LIMIT),
    )(x1, p2, gffn, wfi, wfo, gple, wpg, wpp)


def _block_diag_ones(width):
    idx = np.arange(width) // HEAD_DIM
    return jnp.asarray(idx[:, None] == idx[None, :], dtype=BF16)


def _inv_freq():
    return ROPE_THETA ** (-jnp.arange(ROPE_HALF, dtype=F32) / ROPE_HALF)


def _inv_freq_lanes(width):
    inv_freq = _inv_freq()
    per_head = jnp.concatenate([inv_freq, inv_freq, jnp.zeros((HEAD_DIM - ROPE_DIM,), F32)])
    return jnp.tile(per_head, width // HEAD_DIM)[None, :]


def _rope_spread_matrix(width):
    m = np.zeros((LANES, 3 * width), np.float32)
    for lane in range(width):
        d = lane % HEAD_DIM
        if d < ROPE_DIM:
            m[d % ROPE_HALF, lane] = 1.0
        else:
            m[2 * ROPE_HALF, lane] = 1.0
        if d < ROPE_HALF:
            m[ROPE_HALF + d, width + lane] = -1.0
        elif d < ROPE_DIM:
            m[ROPE_HALF + d - ROPE_HALF, 2 * width + lane] = 1.0
    return jnp.asarray(m, dtype=BF16)


def _split_w_in(w):
    parts = jnp.split(w, IN_CUTS, axis=-1)
    (q_n, kc, vc, ks, vs, kw, vw, gate_n, q_m, k_m, v_m, gate_a, gate_b) = parts
    d = w.shape[0]
    q_n = q_n.reshape(d, NSA_GROUPS, NSA_REP, HEAD_DIM).transpose(0, 2, 1, 3).reshape(d, -1)
    gate_n = jnp.pad(gate_n, ((0, 0), (0, GATE_ROWS - gate_n.shape[1])))
    w_tok = jnp.concatenate([k_m, ks, kw, kc, vc], axis=1)
    w_feat = jnp.concatenate([q_n, q_m, v_m, vs, vw, gate_n], axis=1).T
    w_gate = jnp.concatenate([gate_a, gate_b], axis=1)
    return w_tok.astype(BF16), w_feat.astype(BF16), w_gate.astype(BF16)


def _compress_weights(w1, w2, pe):
    eye = jnp.eye(NSA_GROUPS, dtype=F32)
    halves = []
    for part in (w1[:CMP_STRIDE * HEAD_DIM], w1[CMP_STRIDE * HEAD_DIM:]):
        p3 = part.reshape(CMP_STRIDE, HEAD_DIM, CMP_HIDDEN)
        halves.append(jnp.einsum('idh,gk->igdkh', p3, eye)
                      .reshape(CMP_STRIDE * NSA_GROUPS * HEAD_DIM, NSA_GROUPS * CMP_HIDDEN))
    w_big = jnp.concatenate(halves, axis=1).astype(BF16)
    w2_bd = jnp.einsum('hd,kg->khgd', w2, eye).reshape(NSA_GROUPS * CMP_HIDDEN,
                                                        NSA_GROUPS * HEAD_DIM).astype(BF16)
    pe_rows = [jnp.broadcast_to(pe[a:a + CMP_STRIDE, None, :], (CMP_STRIDE, NSA_GROUPS, HEAD_DIM))
               .reshape(1, -1) for a in (0, CMP_STRIDE)]
    return w_big, w2_bd, pe_rows


def _overlap_matrix_t(n_cmp):
    c = np.arange(n_cmp)
    j = np.arange(SEL_BLOCKS_MAX)
    start, end = c * CMP_STRIDE, c * CMP_STRIDE + CMP_LEN - 1
    ov = (start[None, :] <= j[:, None] * SEL_BLOCK + SEL_BLOCK - 1) & (end[None, :] >= j[:, None] * SEL_BLOCK)
    return jnp.asarray(ov, dtype=BF16)


def kernel(x, p, positions, g_mix, w_in, nsa_q_gain, nsa_kc_gain, nsa_ks_gain, nsa_kw_gain, nsa_pe_k, nsa_pe_v, nsa_ck_w1, nsa_ck_w2, nsa_cv_w1, nsa_cv_w2, moba_q_gain, moba_k_gain, w_up_nsa, w_up_moba, w_out, g_ffn, w_ffn_in, w_ffn_out, g_ple, w_ple_gate, w_ple_proj):
    b, s, d = x.shape
    depth = w_in.shape[0]
    assert s % PAD_MULT == 0 and d == D_MODEL
    t = b * s
    n_sub = s // CMP_STRIDE

    assert ROPE_HALF == SUBLANES
    posr = positions.reshape(t // PROJ_TM, 1, PROJ_TM)
    pos_end = jnp.concatenate([positions[:, CMP_LEN - 1::CMP_STRIDE], positions[:, -1:]], axis=1)[:, :, None]
    invf128 = _inv_freq_lanes(LANES)
    invf8 = _inv_freq()[:, None]
    spread = _rope_spread_matrix(256)
    bd256, bd128 = _block_diag_ones(256), _block_diag_ones(LANES)
    ovl_t = _overlap_matrix_t(n_sub)
    tile = lambda g, n: jnp.tile(g, n)
    xi = x.reshape(t, d)

    for i in range(depth):
        w_tok, w_feat, w_gate = _split_w_in(w_in[i])
        g_tok = jnp.concatenate([tile(moba_k_gain[i], MOBA_HEADS), tile(nsa_ks_gain[i], NSA_GROUPS),
                                 tile(nsa_kw_gain[i], NSA_GROUPS)])[None, :]
        g_feat = (jnp.concatenate([tile(nsa_q_gain[i], NSA_HEADS), tile(moba_q_gain[i], MOBA_HEADS)])
                  * (ATTN_SCALE * LOG2E))[:, None]
        gmix = g_mix[i][None, :]
        (qn, qm, gn, vm, vs, vw, km, ks, kw, kc_raw, vc_raw) = _proj_call(
            xi, posr, gmix, w_tok, w_feat, g_tok, g_feat, invf8, bd256, spread, b, s)

        wk_big, w2k_bd, pe_k = _compress_weights(nsa_ck_w1[i], nsa_ck_w2[i], nsa_pe_k[i])
        wv_big, w2v_bd, pe_v = _compress_weights(nsa_cv_w1[i], nsa_cv_w2[i], nsa_pe_v[i])
        pe4 = jnp.concatenate(pe_k + pe_v, axis=0)
        kc, vct = _compress_call(kc_raw.reshape(b, n_sub, CMP_STRIDE * KV_WIDTH),
                                 vc_raw.reshape(b, n_sub, CMP_STRIDE * KV_WIDTH),
                                 pe4, wk_big, wv_big, w2k_bd, w2v_bd,
                                 tile(nsa_kc_gain[i], NSA_GROUPS)[None, :], pos_end,
                                 invf128, bd128)

        r3 = lambda a: a.reshape(b, s, a.shape[-1])
        y_nsa = _nsa_call(qn, kc, vct, r3(ks), vs.reshape(b, s // NSA_TK, KV_WIDTH, NSA_TK), r3(kw),
                          vw.reshape(b, s // WIN_TK, KV_WIDTH, WIN_TK), gn, ovl_t)
        y_moba = _moba_call(qm, r3(km), vm.reshape(b, s // MOBA_BLOCK, Q_WIDTH, MOBA_BLOCK))

        w_un = (w_up_nsa[i].reshape(NSA_GROUPS, NSA_REP, HEAD_DIM, d).transpose(1, 0, 2, 3)
                .reshape(NSA_HEADS * HEAD_DIM, d).astype(BF16))
        x1 = _merge_call(xi, y_nsa.reshape(t, -1), y_moba.reshape(t, -1), gmix, w_gate, w_un,
                         w_up_moba[i].astype(BF16), w_out[i].astype(BF16))
        xi = _ffn_call(x1, p[i].reshape(t, PLE_DIM), g_ffn[i][None, :], w_ffn_in[i].astype(BF16),
                       w_ffn_out[i].astype(BF16), g_ple[i][None, :], w_ple_gate[i].astype(BF16),
                       w_ple_proj[i].astype(BF16))
    return xi.reshape(b, s, d)
```

```python
import jax
import jax.numpy as jnp
import numpy as np
from jax import lax
from jax.experimental import pallas as pl
from jax.experimental.pallas import tpu as pltpu

F32 = jnp.float32
BF16 = jnp.bfloat16

D_MODEL = 1024
HEAD_DIM = 64
ROPE_DIM = HEAD_DIM // 4
ROPE_HALF = ROPE_DIM // 2
ROPE_THETA = 500000.0
NORM_EPS = 1e-6
ATTN_SCALE = HEAD_DIM ** -0.5

NSA_HEADS = 8
NSA_GROUPS = 2
NSA_REP = NSA_HEADS // NSA_GROUPS
CMP_LEN = 32
CMP_STRIDE = 16
CMP_HIDDEN = 256
SEL_BLOCK = 64
SEL_TOPN = 8
WINDOW = 512

MOBA_HEADS = 8
MOBA_BLOCK = 256
MOBA_TOPK = 3

PAD_MULT = 256
D_FF = ((-(-8 * D_MODEL // 3)) + 255) // 256 * 256
PLE_DIM = 256

IN_SPLITS = ((NSA_HEADS * HEAD_DIM,) + (NSA_GROUPS * HEAD_DIM,) * 6 + (3 * NSA_HEADS,)
             + (MOBA_HEADS * HEAD_DIM,) * 3 + (D_MODEL, D_MODEL))
IN_CUTS = tuple(int(c) for c in np.cumsum(IN_SPLITS)[:-1])

LANES = 128
SUBLANES = 8
NEG_BIG = -1e30
LOG2E = 1.4426950408889634
SUM_ROWS = 16
VMEM_LIMIT = 56 * 1024 * 1024

Q_WIDTH = NSA_HEADS * HEAD_DIM
KV_WIDTH = NSA_GROUPS * HEAD_DIM
GATE_ROWS = 32
TOK_NORMED = Q_WIDTH + 2 * KV_WIDTH
TOK_COLS = TOK_NORMED + 2 * KV_WIDTH
FEAT_ROWS = 3 * Q_WIDTH + 2 * KV_WIDTH + GATE_ROWS

PROJ_TM = 512
V_TILE = 256
NSA_TQ = 256
NSA_TK = V_TILE
WIN_TK = V_TILE
SCORE_LANES = 256
MERGE_TM = 512
FFN_TM = 512
FFN_CHUNK = D_FF // 2
SEL_BLOCKS_MAX = 32
MOBA_BLOCKS_MAX = 8
MOBA_STEP_HEADS = 8


def _dot(a, b):
    return jnp.dot(a, b, preferred_element_type=F32)


def _dot_nt(a, b):
    return lax.dot_general(a, b, (((1,), (1,)), ((), ())), preferred_element_type=F32)


def _split_bf16(a_f32):
    hi = a_f32.astype(BF16)
    return hi, (a_f32 - hi.astype(F32)).astype(BF16)


def _sigmoid(x):
    return 1.0 / (1.0 + jnp.exp(-x))


def _rms_rows(x, g):
    return x * lax.rsqrt(jnp.mean(x * x, axis=-1, keepdims=True) + NORM_EPS) * g


def _head_norm(t, bd, gain):
    hi, lo = _split_bf16(t * t)
    ss = _dot(hi, bd) + _dot(lo, bd)
    return t * lax.rsqrt(ss * (1.0 / HEAD_DIM) + NORM_EPS) * gain


def _rope_tables(pos_col, invf):
    ang = pos_col.astype(F32) * invf
    cos_a, sin_a = jnp.cos(ang), jnp.sin(ang)
    d = lax.broadcasted_iota(jnp.int32, ang.shape, 1) & (HEAD_DIM - 1)
    s_lo = jnp.where(d < ROPE_HALF, -sin_a, 0.0)
    s_hi = jnp.where((d >= ROPE_HALF) & (d < ROPE_DIM), sin_a, 0.0)
    return cos_a, s_lo, s_hi


def _rope(y, tables):
    cos_a, s_lo, s_hi = tables
    w = y.shape[1]
    return (y * cos_a + pltpu.roll(y, w - ROPE_HALF, 1) * s_lo
            + pltpu.roll(y, ROPE_HALF, 1) * s_hi)


def _ones_rows(tk):
    r = lax.broadcasted_iota(jnp.int32, (SUM_ROWS, tk), 0)
    return jnp.where(r == 0, 1.0, 0.0).astype(BF16)


def _with_sum_row(v_rows, ones_rows):
    return jnp.concatenate([v_rows, ones_rows], axis=0)


def _slot_update(load_s, bias, block_rows, m_ref, acc_ref, hs, v_aug):
    m = m_ref[hs, 0:1, :]
    n_sub = len(bias)
    if block_rows:
        dropped = [bias[a] < 0.0 for a in range(n_sub)]
        m_new = m
        for a in range(n_sub):
            top = jnp.max(load_s(a), axis=0, keepdims=True)
            m_new = jnp.maximum(m_new, jnp.where(dropped[a], NEG_BIG, top))
        e = [jnp.exp2(load_s(a) + jnp.where(dropped[a], NEG_BIG, -m_new)) for a in range(n_sub)]
    else:
        m_new = m
        for a in range(n_sub):
            m_new = jnp.maximum(m_new, jnp.max(load_s(a) + bias[a], axis=0, keepdims=True))
        e = [jnp.exp2(load_s(a) + bias[a] - m_new) for a in range(n_sub)]
    e = e[0] if n_sub == 1 else jnp.concatenate(e, axis=0)
    alpha = jnp.exp2(m - m_new)
    m_ref[hs, 0:1, :] = m_new
    acc_ref[hs] = alpha * acc_ref[hs] + _dot(v_aug, e.astype(BF16))


def _reset_state(m_ref, acc_ref):
    m_ref[...] = jnp.full(m_ref.shape, NEG_BIG, F32)
    acc_ref[...] = jnp.zeros(acc_ref.shape, F32)


def _normalised(acc, scale_row=1.0):
    return acc[:HEAD_DIM] * (scale_row / jnp.maximum(acc[HEAD_DIM:HEAD_DIM + 1], 1e-30))


def _rank_rows(v, n_rows):
    n, q = v.shape
    slabs = [v[a:a + SUBLANES] for a in range(0, n, SUBLANES)]
    ranks = [jnp.zeros((SUBLANES, q), F32) for _ in slabs]
    sub = lax.broadcasted_iota(jnp.int32, (SUBLANES, q), 0)
    for jp in range(n_rows):
        row = v[jp:jp + 1, :]
        for si, slab in enumerate(slabs):
            first = si * SUBLANES
            if first > jp:
                beats = jnp.where(row >= slab, 1.0, 0.0)
            elif first + SUBLANES - 1 < jp:
                beats = jnp.where(row > slab, 1.0, 0.0)
            else:
                ge = jnp.where(row >= slab, 1.0, 0.0)
                gt = jnp.where(row > slab, 1.0, 0.0)
                beats = gt + (ge - gt) * jnp.where(sub > jp - first, 1.0, 0.0)
            ranks[si] = ranks[si] + beats
    return ranks[0] if len(ranks) == 1 else jnp.concatenate(ranks, axis=0)


def _proj_kernel(x_ref, posr_ref, gmix_ref, wtok_ref, wfeat_ref, gtok_ref, gfeat_ref,
                 invf8_ref, bd_ref, spread_ref,
                 qn_ref, qm_ref, gn_ref, vm_ref, vs_ref, vw_ref, km_ref, ks_ref, kw_ref,
                 kc_ref, vc_ref):
    tm = x_ref.shape[0]
    h = _rms_rows(x_ref[...], gmix_ref[...]).astype(BF16)

    ang = invf8_ref[...] * posr_ref[0].astype(F32)
    cos_a, sin_a = jnp.cos(ang), jnp.sin(ang)
    rows_t = jnp.concatenate([cos_a, sin_a, jnp.ones((SUBLANES, tm), F32),
                              jnp.zeros((LANES - 3 * SUBLANES, tm), F32)], axis=0)
    hi, lo = _split_bf16(rows_t.T)
    spread = _dot(hi, spread_ref[...]) + _dot(lo, spread_ref[...])
    tables = (spread[:, :256], spread[:, 256:512], spread[:, 512:])

    acc = _dot(h, wtok_ref[...])
    bd = bd_ref[...]
    for c in range(TOK_NORMED // 256):
        t = acc[:, c * 256:(c + 1) * 256]
        y = _rope(_head_norm(t, bd, gtok_ref[:, c * 256:(c + 1) * 256]), tables).astype(BF16)
        if c < 2:
            km_ref[:, c * 256:(c + 1) * 256] = y
        else:
            ks_ref[...] = y[:, :KV_WIDTH]
            kw_ref[...] = y[:, KV_WIDTH:]
    kc_ref[...] = acc[:, TOK_NORMED:TOK_NORMED + KV_WIDTH]
    vc_ref[...] = acc[:, TOK_NORMED + KV_WIDTH:]

    acc_t = _dot_nt(wfeat_ref[...], h)
    for hh in range(2 * NSA_HEADS):
        t = acc_t[hh * HEAD_DIM:(hh + 1) * HEAD_DIM, :]
        ss = jnp.sum(t * t, axis=0, keepdims=True)
        y = t * lax.rsqrt(ss * (1.0 / HEAD_DIM) + NORM_EPS) * gfeat_ref[hh * HEAD_DIM:(hh + 1) * HEAD_DIM, :]
        a, b = y[:ROPE_HALF], y[ROPE_HALF:ROPE_DIM]
        y = jnp.concatenate([a * cos_a - b * sin_a, b * cos_a + a * sin_a, y[ROPE_DIM:]], axis=0)
        dst = qn_ref if hh < NSA_HEADS else qm_ref
        r0 = (hh % NSA_HEADS) * HEAD_DIM
        dst[0, r0:r0 + HEAD_DIM, :] = y.astype(BF16)
    o = 2 * Q_WIDTH
    for j in range(tm // V_TILE):
        cols = slice(j * V_TILE, (j + 1) * V_TILE)
        vm_ref[j] = acc_t[o:o + Q_WIDTH, cols].astype(BF16)
        vs_ref[j] = acc_t[o + Q_WIDTH:o + Q_WIDTH + KV_WIDTH, cols].astype(BF16)
        vw_ref[j] = acc_t[o + Q_WIDTH + KV_WIDTH:o + Q_WIDTH + 2 * KV_WIDTH, cols].astype(BF16)
    gn_ref[0] = acc_t[o + Q_WIDTH + 2 * KV_WIDTH:]


def _proj_call(x2, posr, gmix, wtok, wfeat, gtok, gfeat, invf8, bd, spread, b, s):
    t = x2.shape[0]
    tm = PROJ_TM
    nt = s // tm
    row = lambda w_: pl.BlockSpec((tm, w_), lambda i: (i, 0))
    full = lambda a: pl.BlockSpec(a.shape, lambda i: (0,) * a.ndim, pipeline_mode=pl.Buffered(1))
    feat = lambda r: pl.BlockSpec((1, r, tm), lambda i: (i // nt, 0, i % nt))
    tile = lambda r: pl.BlockSpec((tm // V_TILE, r, V_TILE), lambda i: (i, 0, 0))
    out_shape = [
        jax.ShapeDtypeStruct((b, Q_WIDTH, s), BF16),
        jax.ShapeDtypeStruct((b, Q_WIDTH, s), BF16),
        jax.ShapeDtypeStruct((b, GATE_ROWS, s), F32),
        jax.ShapeDtypeStruct((t // V_TILE, Q_WIDTH, V_TILE), BF16),
        jax.ShapeDtypeStruct((t // V_TILE, KV_WIDTH, V_TILE), BF16),
        jax.ShapeDtypeStruct((t // V_TILE, KV_WIDTH, V_TILE), BF16),
        jax.ShapeDtypeStruct((t, Q_WIDTH), BF16),
        jax.ShapeDtypeStruct((t, KV_WIDTH), BF16),
        jax.ShapeDtypeStruct((t, KV_WIDTH), BF16),
        jax.ShapeDtypeStruct((t, KV_WIDTH), F32),
        jax.ShapeDtypeStruct((t, KV_WIDTH), F32),
    ]
    out_specs = [feat(Q_WIDTH), feat(Q_WIDTH), feat(GATE_ROWS), tile(Q_WIDTH), tile(KV_WIDTH),
                 tile(KV_WIDTH), row(Q_WIDTH), row(KV_WIDTH), row(KV_WIDTH), row(KV_WIDTH), row(KV_WIDTH)]
    return pl.pallas_call(
        _proj_kernel,
        grid=(t // tm,),
        in_specs=[row(D_MODEL), pl.BlockSpec((1, 1, tm), lambda i: (i, 0, 0)), full(gmix),
                  full(wtok), full(wfeat), full(gtok), full(gfeat), full(invf8), full(bd), full(spread)],
        out_specs=out_specs,
        out_shape=out_shape,
        compiler_params=pltpu.CompilerParams(dimension_semantics=("parallel",),
                                             vmem_limit_bytes=VMEM_LIMIT),
    )(x2, posr, gmix, wtok, wfeat, gtok, gfeat, invf8, bd, spread)


def _compress_kernel(kcr_ref, vcr_ref, pe_ref, wk_ref, wv_ref, w2k_ref, w2v_ref, gain_ref,
                     pose_ref, invf_ref, bd_ref, kc_out, vct_out):
    half = NSA_GROUPS * CMP_HIDDEN
    n_sub = kcr_ref.shape[1] // CMP_STRIDE

    def sub_blocks(ref):
        return jnp.concatenate([ref[0, pl.ds(i, n_sub, stride=CMP_STRIDE), :] for i in range(CMP_STRIDE)],
                               axis=1)

    def comp(x, pe_a, pe_b, w_ref, w2_ref):
        a = _dot((x + pe_a).astype(BF16), w_ref[:, :half])
        b = _dot((x + pe_b).astype(BF16), w_ref[:, half:])
        hid = a + pltpu.roll(b, b.shape[0] - 1, 0)
        act = hid * _sigmoid(hid)
        return _dot(act.astype(BF16), w2_ref[...])

    kc = comp(sub_blocks(kcr_ref), pe_ref[0:1, :], pe_ref[1:2, :], wk_ref, w2k_ref)
    vc = comp(sub_blocks(vcr_ref), pe_ref[2:3, :], pe_ref[3:4, :], wv_ref, w2v_ref)
    kc = _head_norm(kc, bd_ref[...], gain_ref[...])
    kc = _rope(kc, _rope_tables(pose_ref[0], invf_ref[...]))
    kc_out[0] = kc.astype(BF16)
    vct_out[0] = vc.T.astype(BF16)


def _compress_call(kc_raw, vc_raw, pe4, wk, wv, w2k, w2v, gain, pos_end, invf, bd):
    b, s, width = kc_raw.shape
    n_sub = s // CMP_STRIDE
    blk = lambda shp: pl.BlockSpec((1,) + shp, lambda i: (i, 0, 0))
    full = lambda a: pl.BlockSpec(a.shape, lambda i: (0,) * a.ndim)
    return pl.pallas_call(
        _compress_kernel,
        grid=(b,),
        in_specs=[blk((s, width)), blk((s, width)), full(pe4), full(wk), full(wv),
                  full(w2k), full(w2v), full(gain), blk((n_sub, 1)), full(invf), full(bd)],
        out_specs=[blk((n_sub, KV_WIDTH)), blk((KV_WIDTH, n_sub))],
        out_shape=[jax.ShapeDtypeStruct((b, n_sub, KV_WIDTH), BF16),
                   jax.ShapeDtypeStruct((b, KV_WIDTH, n_sub), BF16)],
        compiler_params=pltpu.CompilerParams(dimension_semantics=("parallel",),
                                             vmem_limit_bytes=VMEM_LIMIT),
    )(kc_raw, vc_raw, pe4, wk, wv, w2k, w2v, gain, pos_end, invf, bd)


def _nsa_kernel(q_ref, kc_ref, vct_ref, ks_ref, vs_ref, kw_ref, vw_ref, gate_ref, ovl_ref, out_ref,
                sel_scr, lhs_scr, m_scr, acc_scr, o_scr, *bufs):
    tq = q_ref.shape[2]
    n_cmp = kc_ref.shape[1]
    n_slots = NSA_GROUPS * NSA_REP
    n_units = len(bufs) // 2
    s_scr, c_scr = bufs[:n_units], bufs[n_units:]
    per_unit = n_slots // n_units
    i = pl.program_id(1)
    t0 = i * tq
    t_row = t0 + lax.broadcasted_iota(jnp.int32, (1, tq), 1)

    zero = jnp.zeros((HEAD_DIM, tq), BF16)
    for g in range(NSA_GROUPS):
        for r in range(NSA_REP):
            hs = g * NSA_REP + r
            rows = q_ref[0, r * KV_WIDTH + g * HEAD_DIM:r * KV_WIDTH + (g + 1) * HEAD_DIM, :]
            lhs_scr[:, hs * tq:(hs + 1) * tq] = jnp.concatenate([rows, zero] if g == 0 else [zero, rows],
                                                                axis=0)
    gsig = _sigmoid(gate_ref[0])

    def unit_queries(unit):
        return lhs_scr[:, unit * SCORE_LANES:(unit + 1) * SCORE_LANES]

    def scores_into(unit, k_tile):
        tk = k_tile.shape[0]
        s_scr[unit][0:tk, :] = _dot(k_tile, unit_queries(unit))

    def sel_keys(kt):
        return ks_ref[0, pl.ds(pl.multiple_of(kt * NSA_TK, NSA_TK), NSA_TK), :]

    def win_keys(kt):
        return kw_ref[0, pl.ds(pl.multiple_of(kt * WIN_TK, WIN_TK), WIN_TK), :]

    for unit in range(n_units):
        c_scr[unit][...] = _dot(kc_ref[0], unit_queries(unit))
    for unit in range(n_units):
        scores_into(unit, sel_keys(0))

    c_idx = lax.broadcasted_iota(jnp.int32, (n_cmp, tq), 0)
    valid_c = (c_idx * CMP_STRIDE + (CMP_LEN - 1) <= t_row) & (c_idx < n_cmp - 1)
    bias_c = jnp.where(valid_c, 0.0, NEG_BIG)
    p_sum = []
    for hs in range(n_slots):
        g = hs // NSA_REP
        unit, hh = divmod(hs, per_unit)
        x = c_scr[unit][:, hh * tq:(hh + 1) * tq] + bias_c
        m = jnp.max(x, axis=0, keepdims=True)
        e = jnp.where(valid_c, jnp.exp2(x - m), 0.0)
        p = e / jnp.maximum(jnp.sum(e, axis=0, keepdims=True), 1e-30)
        o_c = _dot(vct_ref[0, g * HEAD_DIM:(g + 1) * HEAD_DIM, :], p.astype(BF16))
        o_scr[hs] = o_c * gsig[3 * hs:3 * hs + 1, :]
        if hs % NSA_REP == 0:
            p_sum.append(p)
        else:
            p_sum[g] = p_sum[g] + p

    cur = t_row // SEL_BLOCK
    j_idx = lax.broadcasted_iota(jnp.int32, (SEL_BLOCKS_MAX, tq), 0)
    forced = (j_idx == 0) | (j_idx == cur) | (j_idx == cur - 1)
    for g in range(NSA_GROUPS):
        hi, lo = _split_bf16(p_sum[g])
        imp = _dot(ovl_ref[...], hi) + _dot(ovl_ref[...], lo)
        imp = jnp.where(forced, jnp.inf, jnp.where(j_idx > cur, -jnp.inf, imp))
        sel_scr[g] = jnp.where(_rank_rows(imp, SEL_BLOCKS_MAX) < SEL_TOPN, 0.0, NEG_BIG)

    def run_tile(v_tile, bias_of_group, block_rows, sub, next_keys):
        ones_rows = _ones_rows(v_tile.shape[1])
        bias = [bias_of_group(g) for g in range(NSA_GROUPS)]
        for unit in range(n_units):
            g = (unit * per_unit) // NSA_REP
            v_aug = _with_sum_row(v_tile[g * HEAD_DIM:(g + 1) * HEAD_DIM, :], ones_rows)
            for hh in range(per_unit):
                def load(a, hh=hh, unit=unit):
                    return s_scr[unit][a * sub:(a + 1) * sub, hh * tq:(hh + 1) * tq]

                _slot_update(load, bias[g], block_rows, m_scr, acc_scr, unit * per_unit + hh, v_aug)
            if next_keys is not None:
                scores_into(unit, next_keys())

    def fold_branch(branch):
        for hs in range(n_slots):
            o_scr[hs] = o_scr[hs] + _normalised(acc_scr[hs], gsig[3 * hs + branch:3 * hs + branch + 1, :])

    blocks_per_tile = NSA_TK // SEL_BLOCK
    first_win = jnp.maximum((t0 - (WINDOW - 1)) // WIN_TK, 0)
    last_win = (t0 + tq - 1) // WIN_TK

    def sel_rows(kt, g):
        return [sel_scr[g, pl.ds(kt * blocks_per_tile + a, 1), :] for a in range(blocks_per_tile)]

    def sel_diag_bias(kt, g):
        out = []
        for a, row in enumerate(sel_rows(kt, g)):
            kpos = kt * NSA_TK + a * SEL_BLOCK + lax.broadcasted_iota(jnp.int32, (SEL_BLOCK, tq), 0)
            out.append(jnp.where(kpos <= t_row, row, NEG_BIG))
        return out

    def sel_body(kt, carry):
        run_tile(vs_ref[0, kt], lambda g: sel_rows(kt, g), True, SEL_BLOCK, lambda: sel_keys(kt + 1))
        return carry

    last_sel = (t0 + tq - 1) // NSA_TK
    _reset_state(m_scr, acc_scr)
    lax.fori_loop(0, last_sel, sel_body, 0)
    run_tile(vs_ref[0, last_sel], lambda g: sel_diag_bias(last_sel, g), False, SEL_BLOCK,
             lambda: win_keys(first_win))
    fold_branch(1)

    def win_bias(kt):
        diff = t_row - (kt * WIN_TK + lax.broadcasted_iota(jnp.int32, (WIN_TK, tq), 0))
        return [jnp.where((diff >= 0) & (diff < WINDOW), 0.0, NEG_BIG)]

    def win_body(kt, carry):
        run_tile(vw_ref[0, kt], lambda g: win_bias(kt), False, WIN_TK, lambda: win_keys(kt + 1))
        return carry

    _reset_state(m_scr, acc_scr)
    lax.fori_loop(first_win, last_win, win_body, 0)
    run_tile(vw_ref[0, last_win], lambda g: win_bias(last_win), False, WIN_TK, None)
    fold_branch(2)

    for r in range(NSA_REP):
        tile_t = jnp.concatenate([o_scr[r], o_scr[NSA_REP + r]], axis=0)
        out_ref[0, :, r * KV_WIDTH:(r + 1) * KV_WIDTH] = tile_t.T.astype(BF16)


def _nsa_call(qn, kc, vct, ks, vs, kw, vw, gn, ovl):
    b, _, s = qn.shape
    tq = NSA_TQ
    n_slots = NSA_GROUPS * NSA_REP
    n_units = n_slots * tq // SCORE_LANES
    assert s // SEL_BLOCK <= SEL_BLOCKS_MAX
    seq = lambda a: pl.BlockSpec((1,) + a.shape[1:], lambda bi, i: (bi,) + (0,) * (a.ndim - 1))
    full = lambda a: pl.BlockSpec(a.shape, lambda bi, i: (0,) * a.ndim)
    return pl.pallas_call(
        _nsa_kernel,
        grid=(b, s // tq),
        in_specs=[pl.BlockSpec((1, Q_WIDTH, tq), lambda bi, i: (bi, 0, i)),
                  seq(kc), seq(vct), seq(ks), seq(vs), seq(kw), seq(vw),
                  pl.BlockSpec((1, GATE_ROWS, tq), lambda bi, i: (bi, 0, i)), full(ovl)],
        out_specs=pl.BlockSpec((1, tq, Q_WIDTH), lambda bi, i: (bi, i, 0)),
        out_shape=jax.ShapeDtypeStruct((b, s, Q_WIDTH), BF16),
        scratch_shapes=[pltpu.VMEM((NSA_GROUPS, SEL_BLOCKS_MAX, tq), F32),
                        pltpu.VMEM((KV_WIDTH, n_slots * tq), BF16),
                        pltpu.VMEM((n_slots, SUBLANES, tq), F32),
                        pltpu.VMEM((n_slots, HEAD_DIM + SUM_ROWS, tq), F32),
                        pltpu.VMEM((n_slots, HEAD_DIM, tq), F32)]
        + [pltpu.VMEM((max(NSA_TK, WIN_TK), SCORE_LANES), F32)] * n_units
        + [pltpu.VMEM((kc.shape[1], SCORE_LANES), F32)] * n_units,
        compiler_params=pltpu.CompilerParams(dimension_semantics=("parallel", "parallel"),
                                             vmem_limit_bytes=VMEM_LIMIT),
    )(qn, kc, vct, ks, vs, kw, vw, gn, ovl)


def _moba_kernel(q_ref, k_ref, v_ref, out_ref, km_scr, sel_scr, lhs_scr, m_scr, acc_scr, *s_scr):
    bs = MOBA_BLOCK
    n_blocks = k_ref.shape[1] // bs
    n_slots = len(s_scr)
    i = pl.program_id(2)

    @pl.when(i == 0)
    def _():
        km_scr[...] = jnp.zeros(km_scr.shape, F32)
        for n in range(n_blocks):
            km_scr[n:n + 1, :] = jnp.mean(k_ref[0, n * bs:(n + 1) * bs, :].astype(F32), axis=0,
                                          keepdims=True)

    def pair_lanes(hs):
        return slice((hs // 2) * LANES, (hs // 2 + 1) * LANES)

    zero = jnp.zeros((HEAD_DIM, bs), BF16)
    n_idx = lax.broadcasted_iota(jnp.int32, (MOBA_BLOCKS_MAX, bs), 0)
    for hs in range(n_slots):
        rows = q_ref[0, hs * HEAD_DIM:(hs + 1) * HEAD_DIM, :]
        lhs_scr[hs] = jnp.concatenate([rows, zero] if hs % 2 == 0 else [zero, rows], axis=0)

    def scores_into(hs, n):
        k_tile = k_ref[0, pl.ds(pl.multiple_of(n * bs, bs), bs), pair_lanes(hs)]
        s_scr[hs][...] = _dot(k_tile, lhs_scr[hs])

    block_scores = []
    for hs in range(n_slots):
        km_hi, km_lo = _split_bf16(km_scr[:, pair_lanes(hs)])
        block_scores.append(_dot(km_hi, lhs_scr[hs]) + _dot(km_lo, lhs_scr[hs]))
    for hs in range(n_slots):
        scores_into(hs, 0)

    for hs in range(n_slots):
        sc = jnp.where(n_idx < i, block_scores[hs], -jnp.inf)
        sel_scr[hs] = jnp.where((_rank_rows(sc, MOBA_BLOCKS_MAX) < MOBA_TOPK) & (n_idx < i), 0.0, NEG_BIG)

    ones_rows = _ones_rows(bs)

    def run_block(n, bias_of_slot, block_rows, prefetch):
        for hs in range(n_slots):
            v_aug = _with_sum_row(v_ref[0, n, hs * HEAD_DIM:(hs + 1) * HEAD_DIM, :], ones_rows)
            def load(a, hs=hs):
                return s_scr[hs][...]

            _slot_update(load, bias_of_slot(hs), block_rows, m_scr, acc_scr, hs, v_aug)
            if prefetch:
                scores_into(hs, n + 1)

    def past_body(n, carry):
        run_block(n, lambda hs: [sel_scr[hs, pl.ds(n, 1), :]], True, True)
        return carry

    _reset_state(m_scr, acc_scr)
    lax.fori_loop(0, i, past_body, 0)
    key_i = lax.broadcasted_iota(jnp.int32, (bs, bs), 0)
    qry_i = lax.broadcasted_iota(jnp.int32, (bs, bs), 1)
    bias_own = [jnp.where(key_i <= qry_i, 0.0, NEG_BIG)]
    run_block(i, lambda hs: bias_own, False, False)
    for p in range(n_slots // 2):
        o_t = jnp.concatenate([_normalised(acc_scr[2 * p]), _normalised(acc_scr[2 * p + 1])], axis=0)
        out_ref[0, :, p * LANES:(p + 1) * LANES] = o_t.T.astype(BF16)


def _moba_call(qm, km, vm):
    b, w, s = qm.shape
    bs = MOBA_BLOCK
    nb = s // bs
    wb = MOBA_STEP_HEADS * HEAD_DIM
    assert nb <= MOBA_BLOCKS_MAX and w % wb == 0
    return pl.pallas_call(
        _moba_kernel,
        grid=(b, w // wb, nb),
        in_specs=[pl.BlockSpec((1, wb, bs), lambda bi, p, i: (bi, p, i)),
                  pl.BlockSpec((1, s, wb), lambda bi, p, i: (bi, 0, p)),
                  pl.BlockSpec((1, nb, wb, bs), lambda bi, p, i: (bi, 0, p, 0))],
        out_specs=pl.BlockSpec((1, bs, wb), lambda bi, p, i: (bi, i, p)),
        out_shape=jax.ShapeDtypeStruct((b, s, w), BF16),
        scratch_shapes=[pltpu.VMEM((MOBA_BLOCKS_MAX, wb), F32),
                        pltpu.VMEM((MOBA_STEP_HEADS, MOBA_BLOCKS_MAX, bs), F32),
                        pltpu.VMEM((MOBA_STEP_HEADS, LANES, bs), BF16),
                        pltpu.VMEM((MOBA_STEP_HEADS, SUBLANES, bs), F32),
                        pltpu.VMEM((MOBA_STEP_HEADS, HEAD_DIM + SUM_ROWS, bs), F32)]
        + [pltpu.VMEM((bs, bs), F32)] * MOBA_STEP_HEADS,
        compiler_params=pltpu.CompilerParams(
            dimension_semantics=("parallel", "parallel", "arbitrary"), vmem_limit_bytes=VMEM_LIMIT),
    )(qm, km, vm)


def _merge_kernel(x_ref, yn_ref, ym_ref, gmix_ref, wg_ref, wun_ref, wum_ref, wo_ref, out_ref):
    x = x_ref[...]
    h = _rms_rows(x, gmix_ref[...]).astype(BF16)
    ga = _sigmoid(_dot(h, wg_ref[:, :D_MODEL]))
    gb = _sigmoid(_dot(h, wg_ref[:, D_MODEL:]))
    merged = ga * _dot(yn_ref[...], wun_ref[...]) + gb * _dot(ym_ref[...], wum_ref[...])
    out_ref[...] = x + _dot(merged.astype(BF16), wo_ref[...])


def _merge_call(x2, yn, ym, gmix, wg, wun, wum, wo):
    t = x2.shape[0]
    tm = MERGE_TM
    row = lambda w_: pl.BlockSpec((tm, w_), lambda i: (i, 0))
    full = lambda a: pl.BlockSpec(a.shape, lambda i: (0, 0), pipeline_mode=pl.Buffered(1))
    return pl.pallas_call(
        _merge_kernel,
        grid=(t // tm,),
        in_specs=[row(D_MODEL), row(Q_WIDTH), row(Q_WIDTH), full(gmix), full(wg), full(wun),
                  full(wum), full(wo)],
        out_specs=row(D_MODEL),
        out_shape=jax.ShapeDtypeStruct((t, D_MODEL), F32),
        compiler_params=pltpu.CompilerParams(dimension_semantics=("parallel",),
                                             vmem_limit_bytes=VMEM_LIMIT),
    )(x2, yn, ym, gmix, wg, wun, wum, wo)


def _ffn_kernel(x_ref, p_ref, gffn_ref, wfi_ref, wfo_ref, gple_ref, wpg_ref, wpp_ref, out_ref):
    x = x_ref[...]
    h = _rms_rows(x, gffn_ref[...]).astype(BF16)
    y = x
    for c in range(D_FF // FFN_CHUNK):
        gate = _dot(h, wfi_ref[:, c * FFN_CHUNK:(c + 1) * FFN_CHUNK])
        up = _dot(h, wfi_ref[:, D_FF + c * FFN_CHUNK:D_FF + (c + 1) * FFN_CHUNK])
        act = (gate * _sigmoid(gate) * up).astype(BF16)
        y = y + _dot(act, wfo_ref[c * FFN_CHUNK:(c + 1) * FFN_CHUNK, :])
    h2 = _rms_rows(y, gple_ref[...]).astype(BF16)
    ple_gate = _sigmoid(_dot(h2, wpg_ref[...]))
    out_ref[...] = y + ple_gate * _dot(p_ref[...].astype(BF16), wpp_ref[...])


def _ffn_call(x1, p2, gffn, wfi, wfo, gple, wpg, wpp):
    t = x1.shape[0]
    tm = FFN_TM
    row = lambda w_: pl.BlockSpec((tm, w_), lambda i: (i, 0))
    full = lambda a: pl.BlockSpec(a.shape, lambda i: (0, 0), pipeline_mode=pl.Buffered(1))
    return pl.pallas_call(
        _ffn_kernel,
        grid=(t // tm,),
        in_specs=[row(D_MODEL), row(PLE_DIM), full(gffn), full(wfi), full(wfo), full(gple),
                  full(wpg), full(wpp)],
        out_specs=row(D_MODEL),
        out_shape=jax.ShapeDtypeStruct((t, D_MODEL), F32),
        compiler_params=pltpu.CompilerParams(dimension_semantics=("parallel",),
                                             vmem_limit_bytes=VMEM_LIMIT),
    )(x1, p2, gffn, wfi, wfo, gple, wpg, wpp)


def _block_diag_ones(width):
    idx = np.arange(width) // HEAD_DIM
    return jnp.asarray(idx[:, None] == idx[None, :], dtype=BF16)


def _inv_freq():
    return ROPE_THETA ** (-jnp.arange(ROPE_HALF, dtype=F32) / ROPE_HALF)


def _inv_freq_lanes(width):
    inv_freq = _inv_freq()
    per_head = jnp.concatenate([inv_freq, inv_freq, jnp.zeros((HEAD_DIM - ROPE_DIM,), F32)])
    return jnp.tile(per_head, width // HEAD_DIM)[None, :]


def _rope_spread_matrix(width):
    m = np.zeros((LANES, 3 * width), np.float32)
    for lane in range(width):
        d = lane % HEAD_DIM
        if d < ROPE_DIM:
            m[d % ROPE_HALF, lane] = 1.0
        else:
            m[2 * ROPE_HALF, lane] = 1.0
        if d < ROPE_HALF:
            m[ROPE_HALF + d, width + lane] = -1.0
        elif d < ROPE_DIM:
            m[ROPE_HALF + d - ROPE_HALF, 2 * width + lane] = 1.0
    return jnp.asarray(m, dtype=BF16)


def _split_w_in(w):
    parts = jnp.split(w, IN_CUTS, axis=-1)
    (q_n, kc, vc, ks, vs, kw, vw, gate_n, q_m, k_m, v_m, gate_a, gate_b) = parts
    d = w.shape[0]
    q_n = q_n.reshape(d, NSA_GROUPS, NSA_REP, HEAD_DIM).transpose(0, 2, 1, 3).reshape(d, -1)
    gate_n = jnp.pad(gate_n, ((0, 0), (0, GATE_ROWS - gate_n.shape[1])))
    w_tok = jnp.concatenate([k_m, ks, kw, kc, vc], axis=1)
    w_feat = jnp.concatenate([q_n, q_m, v_m, vs, vw, gate_n], axis=1).T
    w_gate = jnp.concatenate([gate_a, gate_b], axis=1)
    return w_tok.astype(BF16), w_feat.astype(BF16), w_gate.astype(BF16)


def _compress_weights(w1, w2, pe):
    eye = jnp.eye(NSA_GROUPS, dtype=F32)
    halves = []
    for part in (w1[:CMP_STRIDE * HEAD_DIM], w1[CMP_STRIDE * HEAD_DIM:]):
        p3 = part.reshape(CMP_STRIDE, HEAD_DIM, CMP_HIDDEN)
        halves.append(jnp.einsum('idh,gk->igdkh', p3, eye)
                      .reshape(CMP_STRIDE * NSA_GROUPS * HEAD_DIM, NSA_GROUPS * CMP_HIDDEN))
    w_big = jnp.concatenate(halves, axis=1).astype(BF16)
    w2_bd = jnp.einsum('hd,kg->khgd', w2, eye).reshape(NSA_GROUPS * CMP_HIDDEN,
                                                        NSA_GROUPS * HEAD_DIM).astype(BF16)
    pe_rows = [jnp.broadcast_to(pe[a:a + CMP_STRIDE, None, :], (CMP_STRIDE, NSA_GROUPS, HEAD_DIM))
               .reshape(1, -1) for a in (0, CMP_STRIDE)]
    return w_big, w2_bd, pe_rows


def _overlap_matrix_t(n_cmp):
    c = np.arange(n_cmp)
    j = np.arange(SEL_BLOCKS_MAX)
    start, end = c * CMP_STRIDE, c * CMP_STRIDE + CMP_LEN - 1
    ov = (start[None, :] <= j[:, None] * SEL_BLOCK + SEL_BLOCK - 1) & (end[None, :] >= j[:, None] * SEL_BLOCK)
    return jnp.asarray(ov, dtype=BF16)


def kernel(x, p, positions, g_mix, w_in, nsa_q_gain, nsa_kc_gain, nsa_ks_gain, nsa_kw_gain, nsa_pe_k, nsa_pe_v, nsa_ck_w1, nsa_ck_w2, nsa_cv_w1, nsa_cv_w2, moba_q_gain, moba_k_gain, w_up_nsa, w_up_moba, w_out, g_ffn, w_ffn_in, w_ffn_out, g_ple, w_ple_gate, w_ple_proj):
    b, s, d = x.shape
    depth = w_in.shape[0]
    assert s % PAD_MULT == 0 and s % PROJ_TM == 0 and d == D_MODEL and V_TILE == MOBA_BLOCK
    t = b * s
    n_sub = s // CMP_STRIDE

    assert ROPE_HALF == SUBLANES
    posr = positions.reshape(t // PROJ_TM, 1, PROJ_TM)
    pos_end = jnp.concatenate([positions[:, CMP_LEN - 1::CMP_STRIDE], positions[:, -1:]], axis=1)[:, :, None]
    invf128 = _inv_freq_lanes(LANES)
    invf8 = _inv_freq()[:, None]
    spread = _rope_spread_matrix(256)
    bd256, bd128 = _block_diag_ones(256), _block_diag_ones(LANES)
    ovl_t = _overlap_matrix_t(n_sub)
    tile = lambda g, n: jnp.tile(g, n)
    xi = x.reshape(t, d)

    for i in range(depth):
        w_tok, w_feat, w_gate = _split_w_in(w_in[i])
        g_tok = jnp.concatenate([tile(moba_k_gain[i], MOBA_HEADS), tile(nsa_ks_gain[i], NSA_GROUPS),
                                 tile(nsa_kw_gain[i], NSA_GROUPS)])[None, :]
        g_feat = (jnp.concatenate([tile(nsa_q_gain[i], NSA_HEADS), tile(moba_q_gain[i], MOBA_HEADS)])
                  * (ATTN_SCALE * LOG2E))[:, None]
        gmix = g_mix[i][None, :]
        (qn, qm, gn, vm, vs, vw, km, ks, kw, kc_raw, vc_raw) = _proj_call(
            xi, posr, gmix, w_tok, w_feat, g_tok, g_feat, invf8, bd256, spread, b, s)

        wk_big, w2k_bd, pe_k = _compress_weights(nsa_ck_w1[i], nsa_ck_w2[i], nsa_pe_k[i])
        wv_big, w2v_bd, pe_v = _compress_weights(nsa_cv_w1[i], nsa_cv_w2[i], nsa_pe_v[i])
        pe4 = jnp.concatenate(pe_k + pe_v, axis=0)
        kc, vct = _compress_call(kc_raw.reshape(b, s, KV_WIDTH), vc_raw.reshape(b, s, KV_WIDTH), pe4, wk_big, wv_big, w2k_bd, w2v_bd,
                                 tile(nsa_kc_gain[i], NSA_GROUPS)[None, :], pos_end,
                                 invf128, bd128)

        r3 = lambda a: a.reshape(b, s, a.shape[-1])
        y_nsa = _nsa_call(qn, kc, vct, r3(ks), vs.reshape(b, s // NSA_TK, KV_WIDTH, NSA_TK), r3(kw),
                          vw.reshape(b, s // WIN_TK, KV_WIDTH, WIN_TK), gn, ovl_t)
        y_moba = _moba_call(qm, r3(km), vm.reshape(b, s // MOBA_BLOCK, Q_WIDTH, MOBA_BLOCK))

        w_un = (w_up_nsa[i].reshape(NSA_GROUPS, NSA_REP, HEAD_DIM, d).transpose(1, 0, 2, 3)
                .reshape(NSA_HEADS * HEAD_DIM, d).astype(BF16))
        x1 = _merge_call(xi, y_nsa.reshape(t, -1), y_moba.reshape(t, -1), gmix, w_gate, w_un,
                         w_up_moba[i].astype(BF16), w_out[i].astype(BF16))
        xi = _ffn_call(x1, p[i].reshape(t, PLE_DIM), g_ffn[i][None, :], w_ffn_in[i].astype(BF16),
                       w_ffn_out[i].astype(BF16), g_ple[i][None, :], w_ple_gate[i].astype(BF16),
                       w_ple_proj[i].astype(BF16))
    return xi.reshape(b, s, d)
```

```python
import jax
import jax.numpy as jnp
import numpy as np
from jax import lax
from jax.experimental import pallas as pl
from jax.experimental.pallas import tpu as pltpu

F32 = jnp.float32
BF16 = jnp.bfloat16

D_MODEL = 1024
HEAD_DIM = 64
ROPE_DIM = HEAD_DIM // 4
ROPE_HALF = ROPE_DIM // 2
ROPE_THETA = 500000.0
NORM_EPS = 1e-6
ATTN_SCALE = HEAD_DIM ** -0.5

NSA_HEADS = 8
NSA_GROUPS = 2
NSA_REP = NSA_HEADS // NSA_GROUPS
CMP_LEN = 32
CMP_STRIDE = 16
CMP_HIDDEN = 256
SEL_BLOCK = 64
SEL_TOPN = 8
WINDOW = 512

MOBA_HEADS = 8
MOBA_BLOCK = 256
MOBA_TOPK = 3

PAD_MULT = 256
D_FF = ((-(-8 * D_MODEL // 3)) + 255) // 256 * 256
PLE_DIM = 256

IN_SPLITS = ((NSA_HEADS * HEAD_DIM,) + (NSA_GROUPS * HEAD_DIM,) * 6 + (3 * NSA_HEADS,)
             + (MOBA_HEADS * HEAD_DIM,) * 3 + (D_MODEL, D_MODEL))
IN_CUTS = tuple(int(c) for c in np.cumsum(IN_SPLITS)[:-1])

LANES = 128
SUBLANES = 8
NEG_BIG = -1e30
LOG2E = 1.4426950408889634
SUM_ROWS = 16
VMEM_LIMIT = 56 * 1024 * 1024

Q_WIDTH = NSA_HEADS * HEAD_DIM
KV_WIDTH = NSA_GROUPS * HEAD_DIM
GATE_ROWS = 32
TOK_NORMED = Q_WIDTH + 2 * KV_WIDTH
TOK_COLS = TOK_NORMED + 2 * KV_WIDTH
FEAT_ROWS = 3 * Q_WIDTH + 2 * KV_WIDTH + GATE_ROWS

PROJ_TM = 512
V_TILE = 256
NSA_TQ = 256
NSA_TK = V_TILE
WIN_TK = V_TILE
SCORE_LANES = 256
MERGE_TM = 512
FFN_TM = 512
FFN_CHUNK = D_FF // 2
SEL_BLOCKS_MAX = 32
MOBA_BLOCKS_MAX = 8
MOBA_STEP_HEADS = 8


def _dot(a, b):
    return jnp.dot(a, b, preferred_element_type=F32)


def _dot_nt(a, b):
    return lax.dot_general(a, b, (((1,), (1,)), ((), ())), preferred_element_type=F32)


def _dot_tn(a, b):
    return lax.dot_general(a, b, (((0,), (0,)), ((), ())), preferred_element_type=F32)


def _split_bf16(a_f32):
    hi = a_f32.astype(BF16)
    return hi, (a_f32 - hi.astype(F32)).astype(BF16)


def _sigmoid(x):
    return 1.0 / (1.0 + jnp.exp(-x))


def _rms_rows(x, g):
    return x * lax.rsqrt(jnp.mean(x * x, axis=-1, keepdims=True) + NORM_EPS) * g


def _head_norm(t, bd, gain):
    hi, lo = _split_bf16(t * t)
    ss = _dot(hi, bd) + _dot(lo, bd)
    return t * lax.rsqrt(ss * (1.0 / HEAD_DIM) + NORM_EPS) * gain


def _rope_tables(pos_col, invf):
    ang = pos_col.astype(F32) * invf
    cos_a, sin_a = jnp.cos(ang), jnp.sin(ang)
    d = lax.broadcasted_iota(jnp.int32, ang.shape, 1) & (HEAD_DIM - 1)
    s_lo = jnp.where(d < ROPE_HALF, -sin_a, 0.0)
    s_hi = jnp.where((d >= ROPE_HALF) & (d < ROPE_DIM), sin_a, 0.0)
    return cos_a, s_lo, s_hi


def _rope(y, tables):
    cos_a, s_lo, s_hi = tables
    w = y.shape[1]
    return (y * cos_a + pltpu.roll(y, w - ROPE_HALF, 1) * s_lo
            + pltpu.roll(y, ROPE_HALF, 1) * s_hi)


def _ones_rows(tk):
    r = lax.broadcasted_iota(jnp.int32, (SUM_ROWS, tk), 0)
    return jnp.where(r == 0, 1.0, 0.0).astype(BF16)


def _with_sum_row(v_rows, ones_rows):
    return jnp.concatenate([v_rows, ones_rows], axis=0)


def _slot_update(load_s, bias, block_rows, m_ref, acc_ref, hs, v_aug):
    m = m_ref[hs, 0:1, :]
    n_sub = len(bias)
    if block_rows:
        dropped = [bias[a] < 0.0 for a in range(n_sub)]
        m_new = m
        for a in range(n_sub):
            top = jnp.max(load_s(a), axis=0, keepdims=True)
            m_new = jnp.maximum(m_new, jnp.where(dropped[a], NEG_BIG, top))
        e = [jnp.exp2(load_s(a) + jnp.where(dropped[a], NEG_BIG, -m_new)) for a in range(n_sub)]
    else:
        m_new = m
        for a in range(n_sub):
            m_new = jnp.maximum(m_new, jnp.max(load_s(a) + bias[a], axis=0, keepdims=True))
        e = [jnp.exp2(load_s(a) + bias[a] - m_new) for a in range(n_sub)]
    e = e[0] if n_sub == 1 else jnp.concatenate(e, axis=0)
    alpha = jnp.exp2(m - m_new)
    m_ref[hs, 0:1, :] = m_new
    acc_ref[hs] = alpha * acc_ref[hs] + _dot(v_aug, e.astype(BF16))


def _reset_state(m_ref, acc_ref):
    m_ref[...] = jnp.full(m_ref.shape, NEG_BIG, F32)
    acc_ref[...] = jnp.zeros(acc_ref.shape, F32)


def _normalised(acc, scale_row=1.0):
    return acc[:HEAD_DIM] * (scale_row / jnp.maximum(acc[HEAD_DIM:HEAD_DIM + 1], 1e-30))


def _rank_rows(v, n_rows):
    n, q = v.shape
    slabs = [v[a:a + SUBLANES] for a in range(0, n, SUBLANES)]
    ranks = [jnp.zeros((SUBLANES, q), F32) for _ in slabs]
    sub = lax.broadcasted_iota(jnp.int32, (SUBLANES, q), 0)
    for jp in range(n_rows):
        row = v[jp:jp + 1, :]
        for si, slab in enumerate(slabs):
            first = si * SUBLANES
            if first > jp:
                beats = jnp.where(row >= slab, 1.0, 0.0)
            elif first + SUBLANES - 1 < jp:
                beats = jnp.where(row > slab, 1.0, 0.0)
            else:
                ge = jnp.where(row >= slab, 1.0, 0.0)
                gt = jnp.where(row > slab, 1.0, 0.0)
                beats = gt + (ge - gt) * jnp.where(sub > jp - first, 1.0, 0.0)
            ranks[si] = ranks[si] + beats
    return ranks[0] if len(ranks) == 1 else jnp.concatenate(ranks, axis=0)


def _proj_kernel(x_ref, posr_ref, gmix_ref, wtok_ref, wfeat_ref, gtok_ref, gfeat_ref,
                 invf8_ref, bd_ref, spread_ref,
                 qn_ref, qm_ref, gn_ref, vm_ref, vs_ref, vw_ref, km_ref, ks_ref, kw_ref,
                 kc_ref, vc_ref):
    tm = x_ref.shape[0]
    h = _rms_rows(x_ref[...], gmix_ref[...]).astype(BF16)

    ang = invf8_ref[...] * posr_ref[0].astype(F32)
    cos_a, sin_a = jnp.cos(ang), jnp.sin(ang)
    rows_t = jnp.concatenate([cos_a, sin_a, jnp.ones((SUBLANES, tm), F32),
                              jnp.zeros((LANES - 3 * SUBLANES, tm), F32)], axis=0)
    hi, lo = _split_bf16(rows_t.T)
    spread = _dot(hi, spread_ref[...]) + _dot(lo, spread_ref[...])
    tables = (spread[:, :256], spread[:, 256:512], spread[:, 512:])

    acc = _dot(h, wtok_ref[...])
    bd = bd_ref[...]
    for c in range(TOK_NORMED // 256):
        t = acc[:, c * 256:(c + 1) * 256]
        y = _rope(_head_norm(t, bd, gtok_ref[:, c * 256:(c + 1) * 256]), tables).astype(BF16)
        if c < 2:
            km_ref[:, c * 256:(c + 1) * 256] = y
        else:
            ks_ref[...] = y[:, :KV_WIDTH]
            kw_ref[...] = y[:, KV_WIDTH:]
    kc_ref[...] = acc[:, TOK_NORMED:TOK_NORMED + KV_WIDTH]
    vc_ref[...] = acc[:, TOK_NORMED + KV_WIDTH:]

    acc_t = _dot_nt(wfeat_ref[...], h)
    for hh in range(2 * NSA_HEADS):
        t = acc_t[hh * HEAD_DIM:(hh + 1) * HEAD_DIM, :]
        ss = jnp.sum(t * t, axis=0, keepdims=True)
        y = t * lax.rsqrt(ss * (1.0 / HEAD_DIM) + NORM_EPS) * gfeat_ref[hh * HEAD_DIM:(hh + 1) * HEAD_DIM, :]
        a, b = y[:ROPE_HALF], y[ROPE_HALF:ROPE_DIM]
        y = jnp.concatenate([a * cos_a - b * sin_a, b * cos_a + a * sin_a, y[ROPE_DIM:]], axis=0)
        dst = qn_ref if hh < NSA_HEADS else qm_ref
        r0 = (hh % NSA_HEADS) * HEAD_DIM
        dst[0, r0:r0 + HEAD_DIM, :] = y.astype(BF16)
    o = 2 * Q_WIDTH
    for j in range(tm // V_TILE):
        cols = slice(j * V_TILE, (j + 1) * V_TILE)
        vm_ref[j] = acc_t[o:o + Q_WIDTH, cols].astype(BF16)
        vs_ref[j] = acc_t[o + Q_WIDTH:o + Q_WIDTH + KV_WIDTH, cols].astype(BF16)
        vw_ref[j] = acc_t[o + Q_WIDTH + KV_WIDTH:o + Q_WIDTH + 2 * KV_WIDTH, cols].astype(BF16)
    gn_ref[0] = acc_t[o + Q_WIDTH + 2 * KV_WIDTH:]


def _proj_call(x2, posr, gmix, wtok, wfeat, gtok, gfeat, invf8, bd, spread, b, s):
    t = x2.shape[0]
    tm = PROJ_TM
    nt = s // tm
    row = lambda w_: pl.BlockSpec((tm, w_), lambda i: (i, 0))
    full = lambda a: pl.BlockSpec(a.shape, lambda i: (0,) * a.ndim, pipeline_mode=pl.Buffered(1))
    feat = lambda r: pl.BlockSpec((1, r, tm), lambda i: (i // nt, 0, i % nt))
    tile = lambda r: pl.BlockSpec((tm // V_TILE, r, V_TILE), lambda i: (i, 0, 0))
    out_shape = [
        jax.ShapeDtypeStruct((b, Q_WIDTH, s), BF16),
        jax.ShapeDtypeStruct((b, Q_WIDTH, s), BF16),
        jax.ShapeDtypeStruct((b, GATE_ROWS, s), F32),
        jax.ShapeDtypeStruct((t // V_TILE, Q_WIDTH, V_TILE), BF16),
        jax.ShapeDtypeStruct((t // V_TILE, KV_WIDTH, V_TILE), BF16),
        jax.ShapeDtypeStruct((t // V_TILE, KV_WIDTH, V_TILE), BF16),
        jax.ShapeDtypeStruct((t, Q_WIDTH), BF16),
        jax.ShapeDtypeStruct((t, KV_WIDTH), BF16),
        jax.ShapeDtypeStruct((t, KV_WIDTH), BF16),
        jax.ShapeDtypeStruct((t, KV_WIDTH), F32),
        jax.ShapeDtypeStruct((t, KV_WIDTH), F32),
    ]
    out_specs = [feat(Q_WIDTH), feat(Q_WIDTH), feat(GATE_ROWS), tile(Q_WIDTH), tile(KV_WIDTH),
                 tile(KV_WIDTH), row(Q_WIDTH), row(KV_WIDTH), row(KV_WIDTH), row(KV_WIDTH), row(KV_WIDTH)]
    return pl.pallas_call(
        _proj_kernel,
        grid=(t // tm,),
        in_specs=[row(D_MODEL), pl.BlockSpec((1, 1, tm), lambda i: (i, 0, 0)), full(gmix),
                  full(wtok), full(wfeat), full(gtok), full(gfeat), full(invf8), full(bd), full(spread)],
        out_specs=out_specs,
        out_shape=out_shape,
        compiler_params=pltpu.CompilerParams(dimension_semantics=("parallel",),
                                             vmem_limit_bytes=VMEM_LIMIT),
    )(x2, posr, gmix, wtok, wfeat, gtok, gfeat, invf8, bd, spread)


def _compress_kernel(kcr_ref, vcr_ref, pe_ref, wk_ref, wv_ref, w2k_ref, w2v_ref, gain_ref,
                     pose_ref, invf_ref, bd_ref, kc_out, vct_out):
    half = NSA_GROUPS * CMP_HIDDEN
    n_sub = kcr_ref.shape[1] // CMP_STRIDE

    def sub_blocks(ref):
        return jnp.concatenate([ref[0, pl.ds(i, n_sub, stride=CMP_STRIDE), :] for i in range(CMP_STRIDE)],
                               axis=1)

    def comp(x, pe_a, pe_b, w_ref, w2_ref):
        a = _dot((x + pe_a).astype(BF16), w_ref[:, :half])
        b = _dot((x + pe_b).astype(BF16), w_ref[:, half:])
        hid = a + pltpu.roll(b, b.shape[0] - 1, 0)
        act = hid * _sigmoid(hid)
        return _dot(act.astype(BF16), w2_ref[...])

    kc = comp(sub_blocks(kcr_ref), pe_ref[0:1, :], pe_ref[1:2, :], wk_ref, w2k_ref)
    vc = comp(sub_blocks(vcr_ref), pe_ref[2:3, :], pe_ref[3:4, :], wv_ref, w2v_ref)
    kc = _head_norm(kc, bd_ref[...], gain_ref[...])
    kc = _rope(kc, _rope_tables(pose_ref[0], invf_ref[...]))
    kc_out[0] = kc.astype(BF16)
    vct_out[0] = vc.T.astype(BF16)


def _compress_call(kc_raw, vc_raw, pe4, wk, wv, w2k, w2v, gain, pos_end, invf, bd):
    b, s, width = kc_raw.shape
    n_sub = s // CMP_STRIDE
    blk = lambda shp: pl.BlockSpec((1,) + shp, lambda i: (i, 0, 0))
    full = lambda a: pl.BlockSpec(a.shape, lambda i: (0,) * a.ndim)
    return pl.pallas_call(
        _compress_kernel,
        grid=(b,),
        in_specs=[blk((s, width)), blk((s, width)), full(pe4), full(wk), full(wv),
                  full(w2k), full(w2v), full(gain), blk((n_sub, 1)), full(invf), full(bd)],
        out_specs=[blk((n_sub, KV_WIDTH)), blk((KV_WIDTH, n_sub))],
        out_shape=[jax.ShapeDtypeStruct((b, n_sub, KV_WIDTH), BF16),
                   jax.ShapeDtypeStruct((b, KV_WIDTH, n_sub), BF16)],
        compiler_params=pltpu.CompilerParams(dimension_semantics=("parallel",),
                                             vmem_limit_bytes=VMEM_LIMIT),
    )(kc_raw, vc_raw, pe4, wk, wv, w2k, w2v, gain, pos_end, invf, bd)


def _nsa_kernel(q_ref, kc_ref, vct_ref, ks_ref, vs_ref, kw_ref, vw_ref, gate_ref, ovl_ref, out_ref,
                sel_scr, lhs_scr, m_scr, acc_scr, o_scr, *bufs):
    tq = q_ref.shape[2]
    n_cmp = kc_ref.shape[1]
    n_slots = NSA_GROUPS * NSA_REP
    n_units = len(bufs) // 2
    s_scr, c_scr = bufs[:n_units], bufs[n_units:]
    per_unit = n_slots // n_units
    i = pl.program_id(1)
    t0 = i * tq
    t_row = t0 + lax.broadcasted_iota(jnp.int32, (1, tq), 1)

    zero = jnp.zeros((HEAD_DIM, tq), BF16)
    for g in range(NSA_GROUPS):
        for r in range(NSA_REP):
            hs = g * NSA_REP + r
            rows = q_ref[0, r * KV_WIDTH + g * HEAD_DIM:r * KV_WIDTH + (g + 1) * HEAD_DIM, :]
            lhs_scr[:, hs * tq:(hs + 1) * tq] = jnp.concatenate([rows, zero] if g == 0 else [zero, rows],
                                                                axis=0)
    gsig = _sigmoid(gate_ref[0])

    def unit_queries(unit):
        return lhs_scr[:, unit * SCORE_LANES:(unit + 1) * SCORE_LANES]

    def scores_into(unit, k_tile):
        tk = k_tile.shape[0]
        s_scr[unit][0:tk, :] = _dot(k_tile, unit_queries(unit))

    def sel_keys(kt):
        return ks_ref[0, pl.ds(pl.multiple_of(kt * NSA_TK, NSA_TK), NSA_TK), :]

    def win_keys(kt):
        return kw_ref[0, pl.ds(pl.multiple_of(kt * WIN_TK, WIN_TK), WIN_TK), :]

    for unit in range(n_units):
        c_scr[unit][...] = _dot(kc_ref[0], unit_queries(unit))
    for unit in range(n_units):
        scores_into(unit, sel_keys(0))

    c_idx = lax.broadcasted_iota(jnp.int32, (n_cmp, tq), 0)
    valid_c = (c_idx * CMP_STRIDE + (CMP_LEN - 1) <= t_row) & (c_idx < n_cmp - 1)
    bias_c = jnp.where(valid_c, 0.0, NEG_BIG)
    p_sum = []
    for hs in range(n_slots):
        g = hs // NSA_REP
        unit, hh = divmod(hs, per_unit)
        x = c_scr[unit][:, hh * tq:(hh + 1) * tq] + bias_c
        m = jnp.max(x, axis=0, keepdims=True)
        e = jnp.where(valid_c, jnp.exp2(x - m), 0.0)
        p = e / jnp.maximum(jnp.sum(e, axis=0, keepdims=True), 1e-30)
        o_c = _dot(vct_ref[0, g * HEAD_DIM:(g + 1) * HEAD_DIM, :], p.astype(BF16))
        o_scr[hs] = o_c * gsig[3 * hs:3 * hs + 1, :]
        if hs % NSA_REP == 0:
            p_sum.append(p)
        else:
            p_sum[g] = p_sum[g] + p

    cur = t_row // SEL_BLOCK
    j_idx = lax.broadcasted_iota(jnp.int32, (SEL_BLOCKS_MAX, tq), 0)
    forced = (j_idx == 0) | (j_idx == cur) | (j_idx == cur - 1)
    for g in range(NSA_GROUPS):
        hi, lo = _split_bf16(p_sum[g])
        imp = _dot(ovl_ref[...], hi) + _dot(ovl_ref[...], lo)
        imp = jnp.where(forced, jnp.inf, jnp.where(j_idx > cur, -jnp.inf, imp))
        sel_scr[g] = jnp.where(_rank_rows(imp, SEL_BLOCKS_MAX) < SEL_TOPN, 0.0, NEG_BIG)

    def run_tile(v_tile, bias_of_group, block_rows, sub, next_keys):
        ones_rows = _ones_rows(v_tile.shape[1])
        bias = [bias_of_group(g) for g in range(NSA_GROUPS)]
        for unit in range(n_units):
            g = (unit * per_unit) // NSA_REP
            v_aug = _with_sum_row(v_tile[g * HEAD_DIM:(g + 1) * HEAD_DIM, :], ones_rows)
            for hh in range(per_unit):
                def load(a, hh=hh, unit=unit):
                    return s_scr[unit][a * sub:(a + 1) * sub, hh * tq:(hh + 1) * tq]

                _slot_update(load, bias[g], block_rows, m_scr, acc_scr, unit * per_unit + hh, v_aug)
            if next_keys is not None:
                scores_into(unit, next_keys())

    def fold_branch(branch):
        for hs in range(n_slots):
            o_scr[hs] = o_scr[hs] + _normalised(acc_scr[hs], gsig[3 * hs + branch:3 * hs + branch + 1, :])

    blocks_per_tile = NSA_TK // SEL_BLOCK
    first_win = jnp.maximum((t0 - (WINDOW - 1)) // WIN_TK, 0)
    last_win = (t0 + tq - 1) // WIN_TK

    def sel_rows(kt, g):
        return [sel_scr[g, pl.ds(kt * blocks_per_tile + a, 1), :] for a in range(blocks_per_tile)]

    def sel_diag_bias(kt, g):
        out = []
        for a, row in enumerate(sel_rows(kt, g)):
            kpos = kt * NSA_TK + a * SEL_BLOCK + lax.broadcasted_iota(jnp.int32, (SEL_BLOCK, tq), 0)
            out.append(jnp.where(kpos <= t_row, row, NEG_BIG))
        return out

    def sel_body(kt, carry):
        run_tile(vs_ref[0, kt], lambda g: sel_rows(kt, g), True, SEL_BLOCK, lambda: sel_keys(kt + 1))
        return carry

    last_sel = (t0 + tq - 1) // NSA_TK
    _reset_state(m_scr, acc_scr)
    lax.fori_loop(0, last_sel, sel_body, 0)
    run_tile(vs_ref[0, last_sel], lambda g: sel_diag_bias(last_sel, g), False, SEL_BLOCK,
             lambda: win_keys(first_win))
    fold_branch(1)

    def win_bias(kt):
        diff = t_row - (kt * WIN_TK + lax.broadcasted_iota(jnp.int32, (WIN_TK, tq), 0))
        return [jnp.where((diff >= 0) & (diff < WINDOW), 0.0, NEG_BIG)]

    def win_body(kt, carry):
        run_tile(vw_ref[0, kt], lambda g: win_bias(kt), False, WIN_TK, lambda: win_keys(kt + 1))
        return carry

    _reset_state(m_scr, acc_scr)
    lax.fori_loop(first_win, last_win, win_body, 0)
    run_tile(vw_ref[0, last_win], lambda g: win_bias(last_win), False, WIN_TK, None)
    fold_branch(2)

    for r in range(NSA_REP):
        out_ref[0, r * KV_WIDTH:r * KV_WIDTH + HEAD_DIM, :] = o_scr[r].astype(BF16)
        out_ref[0, r * KV_WIDTH + HEAD_DIM:(r + 1) * KV_WIDTH, :] = o_scr[NSA_REP + r].astype(BF16)


def _nsa_call(qn, kc, vct, ks, vs, kw, vw, gn, ovl):
    b, _, s = qn.shape
    tq = NSA_TQ
    n_slots = NSA_GROUPS * NSA_REP
    n_units = n_slots * tq // SCORE_LANES
    assert s // SEL_BLOCK <= SEL_BLOCKS_MAX
    seq = lambda a: pl.BlockSpec((1,) + a.shape[1:], lambda bi, i: (bi,) + (0,) * (a.ndim - 1))
    full = lambda a: pl.BlockSpec(a.shape, lambda bi, i: (0,) * a.ndim)
    return pl.pallas_call(
        _nsa_kernel,
        grid=(b, s // tq),
        in_specs=[pl.BlockSpec((1, Q_WIDTH, tq), lambda bi, i: (bi, 0, i)),
                  seq(kc), seq(vct), seq(ks), seq(vs), seq(kw), seq(vw),
                  pl.BlockSpec((1, GATE_ROWS, tq), lambda bi, i: (bi, 0, i)), full(ovl)],
        out_specs=pl.BlockSpec((1, Q_WIDTH, tq), lambda bi, i: (bi, 0, i)),
        out_shape=jax.ShapeDtypeStruct((b, Q_WIDTH, s), BF16),
        scratch_shapes=[pltpu.VMEM((NSA_GROUPS, SEL_BLOCKS_MAX, tq), F32),
                        pltpu.VMEM((KV_WIDTH, n_slots * tq), BF16),
                        pltpu.VMEM((n_slots, SUBLANES, tq), F32),
                        pltpu.VMEM((n_slots, HEAD_DIM + SUM_ROWS, tq), F32),
                        pltpu.VMEM((n_slots, HEAD_DIM, tq), F32)]
        + [pltpu.VMEM((max(NSA_TK, WIN_TK), SCORE_LANES), F32)] * n_units
        + [pltpu.VMEM((kc.shape[1], SCORE_LANES), F32)] * n_units,
        compiler_params=pltpu.CompilerParams(dimension_semantics=("parallel", "parallel"),
                                             vmem_limit_bytes=VMEM_LIMIT),
    )(qn, kc, vct, ks, vs, kw, vw, gn, ovl)


def _moba_kernel(q_ref, k_ref, v_ref, out_ref, km_scr, sel_scr, lhs_scr, m_scr, acc_scr, *s_scr):
    bs = MOBA_BLOCK
    n_blocks = k_ref.shape[1] // bs
    n_slots = len(s_scr)
    i = pl.program_id(2)

    @pl.when(i == 0)
    def _():
        km_scr[...] = jnp.zeros(km_scr.shape, F32)
        for n in range(n_blocks):
            km_scr[n:n + 1, :] = jnp.mean(k_ref[0, n * bs:(n + 1) * bs, :].astype(F32), axis=0,
                                          keepdims=True)

    def pair_lanes(hs):
        return slice((hs // 2) * LANES, (hs // 2 + 1) * LANES)

    zero = jnp.zeros((HEAD_DIM, bs), BF16)
    n_idx = lax.broadcasted_iota(jnp.int32, (MOBA_BLOCKS_MAX, bs), 0)
    for hs in range(n_slots):
        rows = q_ref[0, hs * HEAD_DIM:(hs + 1) * HEAD_DIM, :]
        lhs_scr[hs] = jnp.concatenate([rows, zero] if hs % 2 == 0 else [zero, rows], axis=0)

    def scores_into(hs, n):
        k_tile = k_ref[0, pl.ds(pl.multiple_of(n * bs, bs), bs), pair_lanes(hs)]
        s_scr[hs][...] = _dot(k_tile, lhs_scr[hs])

    block_scores = []
    for hs in range(n_slots):
        km_hi, km_lo = _split_bf16(km_scr[:, pair_lanes(hs)])
        block_scores.append(_dot(km_hi, lhs_scr[hs]) + _dot(km_lo, lhs_scr[hs]))
    for hs in range(n_slots):
        scores_into(hs, 0)

    for hs in range(n_slots):
        sc = jnp.where(n_idx < i, block_scores[hs], -jnp.inf)
        sel_scr[hs] = jnp.where((_rank_rows(sc, MOBA_BLOCKS_MAX) < MOBA_TOPK) & (n_idx < i), 0.0, NEG_BIG)

    ones_rows = _ones_rows(bs)

    def run_block(n, bias_of_slot, block_rows, prefetch):
        for hs in range(n_slots):
            v_aug = _with_sum_row(v_ref[0, n, hs * HEAD_DIM:(hs + 1) * HEAD_DIM, :], ones_rows)
            def load(a, hs=hs):
                return s_scr[hs][...]

            _slot_update(load, bias_of_slot(hs), block_rows, m_scr, acc_scr, hs, v_aug)
            if prefetch:
                scores_into(hs, n + 1)

    def past_body(n, carry):
        run_block(n, lambda hs: [sel_scr[hs, pl.ds(n, 1), :]], True, True)
        return carry

    _reset_state(m_scr, acc_scr)
    lax.fori_loop(0, i, past_body, 0)
    key_i = lax.broadcasted_iota(jnp.int32, (bs, bs), 0)
    qry_i = lax.broadcasted_iota(jnp.int32, (bs, bs), 1)
    bias_own = [jnp.where(key_i <= qry_i, 0.0, NEG_BIG)]
    run_block(i, lambda hs: bias_own, False, False)
    for hs in range(n_slots):
        out_ref[0, hs * HEAD_DIM:(hs + 1) * HEAD_DIM, :] = _normalised(acc_scr[hs]).astype(BF16)


def _moba_call(qm, km, vm):
    b, w, s = qm.shape
    bs = MOBA_BLOCK
    nb = s // bs
    wb = MOBA_STEP_HEADS * HEAD_DIM
    assert nb <= MOBA_BLOCKS_MAX and w % wb == 0
    return pl.pallas_call(
        _moba_kernel,
        grid=(b, w // wb, nb),
        in_specs=[pl.BlockSpec((1, wb, bs), lambda bi, p, i: (bi, p, i)),
                  pl.BlockSpec((1, s, wb), lambda bi, p, i: (bi, 0, p)),
                  pl.BlockSpec((1, nb, wb, bs), lambda bi, p, i: (bi, 0, p, 0))],
        out_specs=pl.BlockSpec((1, wb, bs), lambda bi, p, i: (bi, p, i)),
        out_shape=jax.ShapeDtypeStruct((b, w, s), BF16),
        scratch_shapes=[pltpu.VMEM((MOBA_BLOCKS_MAX, wb), F32),
                        pltpu.VMEM((MOBA_STEP_HEADS, MOBA_BLOCKS_MAX, bs), F32),
                        pltpu.VMEM((MOBA_STEP_HEADS, LANES, bs), BF16),
                        pltpu.VMEM((MOBA_STEP_HEADS, SUBLANES, bs), F32),
                        pltpu.VMEM((MOBA_STEP_HEADS, HEAD_DIM + SUM_ROWS, bs), F32)]
        + [pltpu.VMEM((bs, bs), F32)] * MOBA_STEP_HEADS,
        compiler_params=pltpu.CompilerParams(
            dimension_semantics=("parallel", "parallel", "arbitrary"), vmem_limit_bytes=VMEM_LIMIT),
    )(qm, km, vm)


def _merge_kernel(x_ref, yn_ref, ym_ref, gmix_ref, wg_ref, wun_ref, wum_ref, wo_ref, out_ref):
    x = x_ref[...]
    h = _rms_rows(x, gmix_ref[...]).astype(BF16)
    ga = _sigmoid(_dot(h, wg_ref[:, :D_MODEL]))
    gb = _sigmoid(_dot(h, wg_ref[:, D_MODEL:]))
    merged = ga * _dot_tn(yn_ref[0], wun_ref[...]) + gb * _dot_tn(ym_ref[0], wum_ref[...])
    out_ref[...] = x + _dot(merged.astype(BF16), wo_ref[...])


def _merge_call(x2, yn, ym, gmix, wg, wun, wum, wo):
    t = x2.shape[0]
    tm = MERGE_TM
    nt = yn.shape[2] // tm
    row = lambda w_: pl.BlockSpec((tm, w_), lambda i: (i, 0))
    feat = pl.BlockSpec((1, Q_WIDTH, tm), lambda i: (i // nt, 0, i % nt))
    full = lambda a: pl.BlockSpec(a.shape, lambda i: (0, 0), pipeline_mode=pl.Buffered(1))
    return pl.pallas_call(
        _merge_kernel,
        grid=(t // tm,),
        in_specs=[row(D_MODEL), feat, feat, full(gmix), full(wg), full(wun),
                  full(wum), full(wo)],
        out_specs=row(D_MODEL),
        out_shape=jax.ShapeDtypeStruct((t, D_MODEL), F32),
        compiler_params=pltpu.CompilerParams(dimension_semantics=("parallel",),
                                             vmem_limit_bytes=VMEM_LIMIT),
    )(x2, yn, ym, gmix, wg, wun, wum, wo)


def _ffn_kernel(x_ref, p_ref, gffn_ref, wfi_ref, wfo_ref, gple_ref, wpg_ref, wpp_ref, out_ref):
    x = x_ref[...]
    h = _rms_rows(x, gffn_ref[...]).astype(BF16)
    y = x
    for c in range(D_FF // FFN_CHUNK):
        gate = _dot(h, wfi_ref[:, c * FFN_CHUNK:(c + 1) * FFN_CHUNK])
        up = _dot(h, wfi_ref[:, D_FF + c * FFN_CHUNK:D_FF + (c + 1) * FFN_CHUNK])
        act = (gate * _sigmoid(gate) * up).astype(BF16)
        y = y + _dot(act, wfo_ref[c * FFN_CHUNK:(c + 1) * FFN_CHUNK, :])
    h2 = _rms_rows(y, gple_ref[...]).astype(BF16)
    ple_gate = _sigmoid(_dot(h2, wpg_ref[...]))
    out_ref[...] = y + ple_gate * _dot(p_ref[...].astype(BF16), wpp_ref[...])


def _ffn_call(x1, p2, gffn, wfi, wfo, gple, wpg, wpp):
    t = x1.shape[0]
    tm = FFN_TM
    row = lambda w_: pl.BlockSpec((tm, w_), lambda i: (i, 0))
    full = lambda a: pl.BlockSpec(a.shape, lambda i: (0, 0), pipeline_mode=pl.Buffered(1))
    return pl.pallas_call(
        _ffn_kernel,
        grid=(t // tm,),
        in_specs=[row(D_MODEL), row(PLE_DIM), full(gffn), full(wfi), full(wfo), full(gple),
                  full(wpg), full(wpp)],
        out_specs=row(D_MODEL),
        out_shape=jax.ShapeDtypeStruct((t, D_MODEL), F32),
        compiler_params=pltpu.CompilerParams(dimension_semantics=("parallel",),
                                             vmem_limit_bytes=VMEM_LIMIT),
    )(x1, p2, gffn, wfi, wfo, gple, wpg, wpp)


def _block_diag_ones(width):
    idx = np.arange(width) // HEAD_DIM
    return jnp.asarray(idx[:, None] == idx[None, :], dtype=BF16)


def _inv_freq():
    return ROPE_THETA ** (-jnp.arange(ROPE_HALF, dtype=F32) / ROPE_HALF)


def _inv_freq_lanes(width):
    inv_freq = _inv_freq()
    per_head = jnp.concatenate([inv_freq, inv_freq, jnp.zeros((HEAD_DIM - ROPE_DIM,), F32)])
    return jnp.tile(per_head, width // HEAD_DIM)[None, :]


def _rope_spread_matrix(width):
    m = np.zeros((LANES, 3 * width), np.float32)
    for lane in range(width):
        d = lane % HEAD_DIM
        if d < ROPE_DIM:
            m[d % ROPE_HALF, lane] = 1.0
        else:
            m[2 * ROPE_HALF, lane] = 1.0
        if d < ROPE_HALF:
            m[ROPE_HALF + d, width + lane] = -1.0
        elif d < ROPE_DIM:
            m[ROPE_HALF + d - ROPE_HALF, 2 * width + lane] = 1.0
    return jnp.asarray(m, dtype=BF16)


def _split_w_in(w):
    parts = jnp.split(w, IN_CUTS, axis=-1)
    (q_n, kc, vc, ks, vs, kw, vw, gate_n, q_m, k_m, v_m, gate_a, gate_b) = parts
    d = w.shape[0]
    q_n = q_n.reshape(d, NSA_GROUPS, NSA_REP, HEAD_DIM).transpose(0, 2, 1, 3).reshape(d, -1)
    gate_n = jnp.pad(gate_n, ((0, 0), (0, GATE_ROWS - gate_n.shape[1])))
    w_tok = jnp.concatenate([k_m, ks, kw, kc, vc], axis=1)
    w_feat = jnp.concatenate([q_n, q_m, v_m, vs, vw, gate_n], axis=1).T
    w_gate = jnp.concatenate([gate_a, gate_b], axis=1)
    return w_tok.astype(BF16), w_feat.astype(BF16), w_gate.astype(BF16)


def _compress_weights(w1, w2, pe):
    eye = jnp.eye(NSA_GROUPS, dtype=F32)
    halves = []
    for part in (w1[:CMP_STRIDE * HEAD_DIM], w1[CMP_STRIDE * HEAD_DIM:]):
        p3 = part.reshape(CMP_STRIDE, HEAD_DIM, CMP_HIDDEN)
        halves.append(jnp.einsum('idh,gk->igdkh', p3, eye)
                      .reshape(CMP_STRIDE * NSA_GROUPS * HEAD_DIM, NSA_GROUPS * CMP_HIDDEN))
    w_big = jnp.concatenate(halves, axis=1).astype(BF16)
    w2_bd = jnp.einsum('hd,kg->khgd', w2, eye).reshape(NSA_GROUPS * CMP_HIDDEN,
                                                        NSA_GROUPS * HEAD_DIM).astype(BF16)
    pe_rows = [jnp.broadcast_to(pe[a:a + CMP_STRIDE, None, :], (CMP_STRIDE, NSA_GROUPS, HEAD_DIM))
               .reshape(1, -1) for a in (0, CMP_STRIDE)]
    return w_big, w2_bd, pe_rows


def _overlap_matrix_t(n_cmp):
    c = np.arange(n_cmp)
    j = np.arange(SEL_BLOCKS_MAX)
    start, end = c * CMP_STRIDE, c * CMP_STRIDE + CMP_LEN - 1
    ov = (start[None, :] <= j[:, None] * SEL_BLOCK + SEL_BLOCK - 1) & (end[None, :] >= j[:, None] * SEL_BLOCK)
    return jnp.asarray(ov, dtype=BF16)


def kernel(x, p, positions, g_mix, w_in, nsa_q_gain, nsa_kc_gain, nsa_ks_gain, nsa_kw_gain, nsa_pe_k, nsa_pe_v, nsa_ck_w1, nsa_ck_w2, nsa_cv_w1, nsa_cv_w2, moba_q_gain, moba_k_gain, w_up_nsa, w_up_moba, w_out, g_ffn, w_ffn_in, w_ffn_out, g_ple, w_ple_gate, w_ple_proj):
    b, s, d = x.shape
    depth = w_in.shape[0]
    assert s % PAD_MULT == 0 and s % PROJ_TM == 0 and d == D_MODEL and V_TILE == MOBA_BLOCK
    t = b * s
    n_sub = s // CMP_STRIDE

    assert ROPE_HALF == SUBLANES
    posr = positions.reshape(t // PROJ_TM, 1, PROJ_TM)
    pos_end = jnp.concatenate([positions[:, CMP_LEN - 1::CMP_STRIDE], positions[:, -1:]], axis=1)[:, :, None]
    invf128 = _inv_freq_lanes(LANES)
    invf8 = _inv_freq()[:, None]
    spread = _rope_spread_matrix(256)
    bd256, bd128 = _block_diag_ones(256), _block_diag_ones(LANES)
    ovl_t = _overlap_matrix_t(n_sub)
    tile = lambda g, n: jnp.tile(g, n)
    xi = x.reshape(t, d)

    for i in range(depth):
        w_tok, w_feat, w_gate = _split_w_in(w_in[i])
        g_tok = jnp.concatenate([tile(moba_k_gain[i], MOBA_HEADS), tile(nsa_ks_gain[i], NSA_GROUPS),
                                 tile(nsa_kw_gain[i], NSA_GROUPS)])[None, :]
        g_feat = (jnp.concatenate([tile(nsa_q_gain[i], NSA_HEADS), tile(moba_q_gain[i], MOBA_HEADS)])
                  * (ATTN_SCALE * LOG2E))[:, None]
        gmix = g_mix[i][None, :]
        (qn, qm, gn, vm, vs, vw, km, ks, kw, kc_raw, vc_raw) = _proj_call(
            xi, posr, gmix, w_tok, w_feat, g_tok, g_feat, invf8, bd256, spread, b, s)

        wk_big, w2k_bd, pe_k = _compress_weights(nsa_ck_w1[i], nsa_ck_w2[i], nsa_pe_k[i])
        wv_big, w2v_bd, pe_v = _compress_weights(nsa_cv_w1[i], nsa_cv_w2[i], nsa_pe_v[i])
        pe4 = jnp.concatenate(pe_k + pe_v, axis=0)
        kc, vct = _compress_call(kc_raw.reshape(b, s, KV_WIDTH), vc_raw.reshape(b, s, KV_WIDTH), pe4, wk_big, wv_big, w2k_bd, w2v_bd,
                                 tile(nsa_kc_gain[i], NSA_GROUPS)[None, :], pos_end,
                                 invf128, bd128)

        r3 = lambda a: a.reshape(b, s, a.shape[-1])
        y_nsa = _nsa_call(qn, kc, vct, r3(ks), vs.reshape(b, s // NSA_TK, KV_WIDTH, NSA_TK), r3(kw),
                          vw.reshape(b, s // WIN_TK, KV_WIDTH, WIN_TK), gn, ovl_t)
        y_moba = _moba_call(qm, r3(km), vm.reshape(b, s // MOBA_BLOCK, Q_WIDTH, MOBA_BLOCK))

        w_un = (w_up_nsa[i].reshape(NSA_GROUPS, NSA_REP, HEAD_DIM, d).transpose(1, 0, 2, 3)
                .reshape(NSA_HEADS * HEAD_DIM, d).astype(BF16))
        x1 = _merge_call(xi, y_nsa, y_moba, gmix, w_gate, w_un,
                         w_up_moba[i].astype(BF16), w_out[i].astype(BF16))
        xi = _ffn_call(x1, p[i].reshape(t, PLE_DIM), g_ffn[i][None, :], w_ffn_in[i].astype(BF16),
                       w_ffn_out[i].astype(BF16), g_ple[i][None, :], w_ple_gate[i].astype(BF16),
                       w_ple_proj[i].astype(BF16))
    return xi.reshape(b, s, d)
```

```python
import jax
import jax.numpy as jnp
import numpy as np
from jax import lax
from jax.experimental import pallas as pl
from jax.experimental.pallas import tpu as pltpu

F32 = jnp.float32
BF16 = jnp.bfloat16

D_MODEL = 1024
HEAD_DIM = 64
ROPE_DIM = HEAD_DIM // 4
ROPE_HALF = ROPE_DIM // 2
ROPE_THETA = 500000.0
NORM_EPS = 1e-6
ATTN_SCALE = HEAD_DIM ** -0.5

NSA_HEADS = 8
NSA_GROUPS = 2
NSA_REP = NSA_HEADS // NSA_GROUPS
CMP_LEN = 32
CMP_STRIDE = 16
CMP_HIDDEN = 256
SEL_BLOCK = 64
SEL_TOPN = 8
WINDOW = 512

MOBA_HEADS = 8
MOBA_BLOCK = 256
MOBA_TOPK = 3

PAD_MULT = 256
D_FF = ((-(-8 * D_MODEL // 3)) + 255) // 256 * 256
PLE_DIM = 256

IN_SPLITS = ((NSA_HEADS * HEAD_DIM,) + (NSA_GROUPS * HEAD_DIM,) * 6 + (3 * NSA_HEADS,)
             + (MOBA_HEADS * HEAD_DIM,) * 3 + (D_MODEL, D_MODEL))
IN_CUTS = tuple(int(c) for c in np.cumsum(IN_SPLITS)[:-1])

LANES = 128
SUBLANES = 8
NEG_BIG = -1e30
LOG2E = 1.4426950408889634
SUM_ROWS = 16
VMEM_LIMIT = 56 * 1024 * 1024

Q_WIDTH = NSA_HEADS * HEAD_DIM
KV_WIDTH = NSA_GROUPS * HEAD_DIM
GATE_ROWS = 32
TOK_NORMED = Q_WIDTH + 2 * KV_WIDTH
TOK_COLS = TOK_NORMED + 2 * KV_WIDTH
FEAT_ROWS = 3 * Q_WIDTH + 2 * KV_WIDTH + GATE_ROWS

PROJ_TM = 512
V_TILE = 256
NSA_TQ = 256
NSA_TK = V_TILE
WIN_TK = V_TILE
SCORE_LANES = 256
MERGE_TM = 512
FFN_TM = 512
FFN_CHUNK = D_FF // 2
SEL_BLOCKS_MAX = 32
MOBA_BLOCKS_MAX = 8
MOBA_STEP_HEADS = 8
OWN_SUB = 64


def _dot(a, b):
    return jnp.dot(a, b, preferred_element_type=F32)


def _dot_nt(a, b):
    return lax.dot_general(a, b, (((1,), (1,)), ((), ())), preferred_element_type=F32)


def _dot_tn(a, b):
    return lax.dot_general(a, b, (((0,), (0,)), ((), ())), preferred_element_type=F32)


def _split_bf16(a_f32):
    hi = a_f32.astype(BF16)
    return hi, (a_f32 - hi.astype(F32)).astype(BF16)


def _sigmoid(x):
    return 1.0 / (1.0 + jnp.exp(-x))


def _rms_rows(x, g):
    return x * lax.rsqrt(jnp.mean(x * x, axis=-1, keepdims=True) + NORM_EPS) * g


def _head_norm(t, bd, gain):
    hi, lo = _split_bf16(t * t)
    ss = _dot(hi, bd) + _dot(lo, bd)
    return t * lax.rsqrt(ss * (1.0 / HEAD_DIM) + NORM_EPS) * gain


def _rope_tables(pos_col, invf):
    ang = pos_col.astype(F32) * invf
    cos_a, sin_a = jnp.cos(ang), jnp.sin(ang)
    d = lax.broadcasted_iota(jnp.int32, ang.shape, 1) & (HEAD_DIM - 1)
    s_lo = jnp.where(d < ROPE_HALF, -sin_a, 0.0)
    s_hi = jnp.where((d >= ROPE_HALF) & (d < ROPE_DIM), sin_a, 0.0)
    return cos_a, s_lo, s_hi


def _rope(y, tables):
    cos_a, s_lo, s_hi = tables
    w = y.shape[1]
    return (y * cos_a + pltpu.roll(y, w - ROPE_HALF, 1) * s_lo
            + pltpu.roll(y, ROPE_HALF, 1) * s_hi)


def _ones_rows(tk):
    r = lax.broadcasted_iota(jnp.int32, (SUM_ROWS, tk), 0)
    return jnp.where(r == 0, 1.0, 0.0).astype(BF16)


def _with_sum_row(v_rows, ones_rows):
    return jnp.concatenate([v_rows, ones_rows], axis=0)


def _slot_update(load_s, plan, m_ref, acc_ref, hs, v_aug):
    n_sub, n_lane = len(plan), len(plan[0])
    sub = v_aug.shape[1] // n_sub
    e_cols, alphas = [], []
    for h in range(n_lane):
        lanes = slice(h * LANES, (h + 1) * LANES)
        m = m_ref[hs, 0:1, lanes]
        m_new = m
        for a in range(n_sub):
            if plan[a][h] is None:
                continue
            kind, arg = plan[a][h]
            if kind == 'rows':
                top = jnp.max(load_s(a, h), axis=0, keepdims=True)
                m_new = jnp.maximum(m_new, top if arg is None else jnp.where(arg < 0.0, NEG_BIG, top))
            else:
                m_new = jnp.maximum(m_new, jnp.max(load_s(a, h) + arg, axis=0, keepdims=True))
        e = []
        for a in range(n_sub):
            if plan[a][h] is None:
                e.append(jnp.zeros((sub, LANES), BF16))
                continue
            kind, arg = plan[a][h]
            if kind == 'rows':
                shift = -m_new if arg is None else jnp.where(arg < 0.0, NEG_BIG, -m_new)
                e.append(jnp.exp2(load_s(a, h) + shift).astype(BF16))
            else:
                e.append(jnp.exp2(load_s(a, h) + arg - m_new).astype(BF16))
        e_cols.append(e[0] if n_sub == 1 else jnp.concatenate(e, axis=0))
        alphas.append(jnp.exp2(m - m_new))
        m_ref[hs, 0:1, lanes] = m_new
    e_all = e_cols[0] if n_lane == 1 else jnp.concatenate(e_cols, axis=1)
    alpha = alphas[0] if n_lane == 1 else jnp.concatenate(alphas, axis=1)
    acc_ref[hs] = alpha * acc_ref[hs] + _dot(v_aug, e_all)


def _visibility_plan(n_sub, sub, n_lane, visible, kept_row=None):
    plan = []
    for a in range(n_sub):
        row_plan = []
        for h in range(n_lane):
            k0, k1, q0, q1 = a * sub, (a + 1) * sub - 1, h * LANES, (h + 1) * LANES - 1
            corners = [visible(k, q) for k in (k0, k1) for q in (q0, q1)]
            row = None if kept_row is None else kept_row(a, h)
            if not any(corners):
                row_plan.append(None)
            elif all(corners):
                row_plan.append(('rows', row))
            else:
                k_off = k0 + lax.broadcasted_iota(jnp.int32, (sub, LANES), 0)
                q_off = q0 + lax.broadcasted_iota(jnp.int32, (sub, LANES), 1)
                keep = 0.0 if row is None else row
                row_plan.append(('tile', jnp.where(visible(k_off, q_off), keep, NEG_BIG)))
        plan.append(row_plan)
    return plan


def _reset_state(m_ref, acc_ref):
    m_ref[...] = jnp.full(m_ref.shape, NEG_BIG, F32)
    acc_ref[...] = jnp.zeros(acc_ref.shape, F32)


def _normalised(acc, scale_row=1.0):
    return acc[:HEAD_DIM] * (scale_row / jnp.maximum(acc[HEAD_DIM:HEAD_DIM + 1], 1e-30))


def _rank_rows(v, n_rows):
    n, q = v.shape
    slabs = [v[a:a + SUBLANES] for a in range(0, n, SUBLANES)]
    ranks = [jnp.zeros((SUBLANES, q), F32) for _ in slabs]
    sub = lax.broadcasted_iota(jnp.int32, (SUBLANES, q), 0)
    for jp in range(n_rows):
        row = v[jp:jp + 1, :]
        for si, slab in enumerate(slabs):
            first = si * SUBLANES
            if first > jp:
                beats = jnp.where(row >= slab, 1.0, 0.0)
            elif first + SUBLANES - 1 < jp:
                beats = jnp.where(row > slab, 1.0, 0.0)
            else:
                ge = jnp.where(row >= slab, 1.0, 0.0)
                gt = jnp.where(row > slab, 1.0, 0.0)
                beats = gt + (ge - gt) * jnp.where(sub > jp - first, 1.0, 0.0)
            ranks[si] = ranks[si] + beats
    return ranks[0] if len(ranks) == 1 else jnp.concatenate(ranks, axis=0)


def _proj_kernel(x_ref, posr_ref, gmix_ref, wtok_ref, wfeat_ref, gtok_ref, gfeat_ref,
                 invf8_ref, bd_ref, spread_ref,
                 qn_ref, qm_ref, gn_ref, vm_ref, vs_ref, vw_ref, km_ref, ks_ref, kw_ref,
                 kc_ref, vc_ref):
    tm = x_ref.shape[0]
    h = _rms_rows(x_ref[...], gmix_ref[...]).astype(BF16)

    ang = invf8_ref[...] * posr_ref[0].astype(F32)
    cos_a, sin_a = jnp.cos(ang), jnp.sin(ang)
    rows_t = jnp.concatenate([cos_a, sin_a, jnp.ones((SUBLANES, tm), F32),
                              jnp.zeros((LANES - 3 * SUBLANES, tm), F32)], axis=0)
    hi, lo = _split_bf16(rows_t.T)
    spread = _dot(hi, spread_ref[...]) + _dot(lo, spread_ref[...])
    tables = (spread[:, :256], spread[:, 256:512], spread[:, 512:])

    acc = _dot(h, wtok_ref[...])
    bd = bd_ref[...]
    for c in range(TOK_NORMED // 256):
        t = acc[:, c * 256:(c + 1) * 256]
        y = _rope(_head_norm(t, bd, gtok_ref[:, c * 256:(c + 1) * 256]), tables).astype(BF16)
        if c < 2:
            km_ref[:, c * 256:(c + 1) * 256] = y
        else:
            ks_ref[...] = y[:, :KV_WIDTH]
            kw_ref[...] = y[:, KV_WIDTH:]
    kc_ref[...] = acc[:, TOK_NORMED:TOK_NORMED + KV_WIDTH]
    vc_ref[...] = acc[:, TOK_NORMED + KV_WIDTH:]

    acc_t = _dot_nt(wfeat_ref[...], h)
    for hh in range(2 * NSA_HEADS):
        t = acc_t[hh * HEAD_DIM:(hh + 1) * HEAD_DIM, :]
        ss = jnp.sum(t * t, axis=0, keepdims=True)
        y = t * lax.rsqrt(ss * (1.0 / HEAD_DIM) + NORM_EPS) * gfeat_ref[hh * HEAD_DIM:(hh + 1) * HEAD_DIM, :]
        a, b = y[:ROPE_HALF], y[ROPE_HALF:ROPE_DIM]
        y = jnp.concatenate([a * cos_a - b * sin_a, b * cos_a + a * sin_a, y[ROPE_DIM:]], axis=0)
        dst = qn_ref if hh < NSA_HEADS else qm_ref
        r0 = (hh % NSA_HEADS) * HEAD_DIM
        dst[0, r0:r0 + HEAD_DIM, :] = y.astype(BF16)
    o = 2 * Q_WIDTH
    for j in range(tm // V_TILE):
        cols = slice(j * V_TILE, (j + 1) * V_TILE)
        vm_ref[j] = acc_t[o:o + Q_WIDTH, cols].astype(BF16)
        vs_ref[j] = acc_t[o + Q_WIDTH:o + Q_WIDTH + KV_WIDTH, cols].astype(BF16)
        vw_ref[j] = acc_t[o + Q_WIDTH + KV_WIDTH:o + Q_WIDTH + 2 * KV_WIDTH, cols].astype(BF16)
    gn_ref[0] = acc_t[o + Q_WIDTH + 2 * KV_WIDTH:]


def _proj_call(x2, posr, gmix, wtok, wfeat, gtok, gfeat, invf8, bd, spread, b, s):
    t = x2.shape[0]
    tm = PROJ_TM
    nt = s // tm
    row = lambda w_: pl.BlockSpec((tm, w_), lambda i: (i, 0))
    full = lambda a: pl.BlockSpec(a.shape, lambda i: (0,) * a.ndim, pipeline_mode=pl.Buffered(1))
    feat = lambda r: pl.BlockSpec((1, r, tm), lambda i: (i // nt, 0, i % nt))
    tile = lambda r: pl.BlockSpec((tm // V_TILE, r, V_TILE), lambda i: (i, 0, 0))
    out_shape = [
        jax.ShapeDtypeStruct((b, Q_WIDTH, s), BF16),
        jax.ShapeDtypeStruct((b, Q_WIDTH, s), BF16),
        jax.ShapeDtypeStruct((b, GATE_ROWS, s), F32),
        jax.ShapeDtypeStruct((t // V_TILE, Q_WIDTH, V_TILE), BF16),
        jax.ShapeDtypeStruct((t // V_TILE, KV_WIDTH, V_TILE), BF16),
        jax.ShapeDtypeStruct((t // V_TILE, KV_WIDTH, V_TILE), BF16),
        jax.ShapeDtypeStruct((t, Q_WIDTH), BF16),
        jax.ShapeDtypeStruct((t, KV_WIDTH), BF16),
        jax.ShapeDtypeStruct((t, KV_WIDTH), BF16),
        jax.ShapeDtypeStruct((t, KV_WIDTH), F32),
        jax.ShapeDtypeStruct((t, KV_WIDTH), F32),
    ]
    out_specs = [feat(Q_WIDTH), feat(Q_WIDTH), feat(GATE_ROWS), tile(Q_WIDTH), tile(KV_WIDTH),
                 tile(KV_WIDTH), row(Q_WIDTH), row(KV_WIDTH), row(KV_WIDTH), row(KV_WIDTH), row(KV_WIDTH)]
    return pl.pallas_call(
        _proj_kernel,
        grid=(t // tm,),
        in_specs=[row(D_MODEL), pl.BlockSpec((1, 1, tm), lambda i: (i, 0, 0)), full(gmix),
                  full(wtok), full(wfeat), full(gtok), full(gfeat), full(invf8), full(bd), full(spread)],
        out_specs=out_specs,
        out_shape=out_shape,
        compiler_params=pltpu.CompilerParams(dimension_semantics=("parallel",),
                                             vmem_limit_bytes=VMEM_LIMIT),
    )(x2, posr, gmix, wtok, wfeat, gtok, gfeat, invf8, bd, spread)


def _compress_kernel(kcr_ref, vcr_ref, pe_ref, wk_ref, wv_ref, w2k_ref, w2v_ref, gain_ref,
                     pose_ref, invf_ref, bd_ref, kc_out, vct_out):
    half = NSA_GROUPS * CMP_HIDDEN
    n_sub = kcr_ref.shape[1] // CMP_STRIDE

    def sub_blocks(ref):
        return jnp.concatenate([ref[0, pl.ds(i, n_sub, stride=CMP_STRIDE), :] for i in range(CMP_STRIDE)],
                               axis=1)

    def comp(x, pe_a, pe_b, w_ref, w2_ref):
        a = _dot((x + pe_a).astype(BF16), w_ref[:, :half])
        b = _dot((x + pe_b).astype(BF16), w_ref[:, half:])
        hid = a + pltpu.roll(b, b.shape[0] - 1, 0)
        act = hid * _sigmoid(hid)
        return _dot(act.astype(BF16), w2_ref[...])

    kc = comp(sub_blocks(kcr_ref), pe_ref[0:1, :], pe_ref[1:2, :], wk_ref, w2k_ref)
    vc = comp(sub_blocks(vcr_ref), pe_ref[2:3, :], pe_ref[3:4, :], wv_ref, w2v_ref)
    kc = _head_norm(kc, bd_ref[...], gain_ref[...])
    kc = _rope(kc, _rope_tables(pose_ref[0], invf_ref[...]))
    kc_out[0] = kc.astype(BF16)
    vct_out[0] = vc.T.astype(BF16)


def _compress_call(kc_raw, vc_raw, pe4, wk, wv, w2k, w2v, gain, pos_end, invf, bd):
    b, s, width = kc_raw.shape
    n_sub = s // CMP_STRIDE
    blk = lambda shp: pl.BlockSpec((1,) + shp, lambda i: (i, 0, 0))
    full = lambda a: pl.BlockSpec(a.shape, lambda i: (0,) * a.ndim)
    return pl.pallas_call(
        _compress_kernel,
        grid=(b,),
        in_specs=[blk((s, width)), blk((s, width)), full(pe4), full(wk), full(wv),
                  full(w2k), full(w2v), full(gain), blk((n_sub, 1)), full(invf), full(bd)],
        out_specs=[blk((n_sub, KV_WIDTH)), blk((KV_WIDTH, n_sub))],
        out_shape=[jax.ShapeDtypeStruct((b, n_sub, KV_WIDTH), BF16),
                   jax.ShapeDtypeStruct((b, KV_WIDTH, n_sub), BF16)],
        compiler_params=pltpu.CompilerParams(dimension_semantics=("parallel",),
                                             vmem_limit_bytes=VMEM_LIMIT),
    )(kc_raw, vc_raw, pe4, wk, wv, w2k, w2v, gain, pos_end, invf, bd)


def _nsa_kernel(q_ref, kc_ref, vct_ref, ks_ref, vs_ref, kw_ref, vw_ref, gate_ref, ovl_ref, out_ref,
                sel_scr, lhs_scr, m_scr, acc_scr, o_scr, *bufs):
    tq = q_ref.shape[2]
    n_cmp = kc_ref.shape[1]
    n_slots = NSA_GROUPS * NSA_REP
    n_units = len(bufs) // 2
    s_scr, c_scr = bufs[:n_units], bufs[n_units:]
    per_unit = n_slots // n_units
    i = pl.program_id(1)
    t0 = i * tq
    t_row = t0 + lax.broadcasted_iota(jnp.int32, (1, tq), 1)

    zero = jnp.zeros((HEAD_DIM, tq), BF16)
    for g in range(NSA_GROUPS):
        for r in range(NSA_REP):
            hs = g * NSA_REP + r
            rows = q_ref[0, r * KV_WIDTH + g * HEAD_DIM:r * KV_WIDTH + (g + 1) * HEAD_DIM, :]
            lhs_scr[:, hs * tq:(hs + 1) * tq] = jnp.concatenate([rows, zero] if g == 0 else [zero, rows],
                                                                axis=0)
    gsig = _sigmoid(gate_ref[0])

    def unit_queries(unit):
        return lhs_scr[:, unit * SCORE_LANES:(unit + 1) * SCORE_LANES]

    def scores_into(unit, k_tile):
        tk = k_tile.shape[0]
        s_scr[unit][0:tk, :] = _dot(k_tile, unit_queries(unit))

    def sel_keys(kt):
        return ks_ref[0, pl.ds(pl.multiple_of(kt * NSA_TK, NSA_TK), NSA_TK), :]

    def win_keys(kt):
        return kw_ref[0, pl.ds(pl.multiple_of(kt * WIN_TK, WIN_TK), WIN_TK), :]

    for unit in range(n_units):
        c_scr[unit][...] = _dot(kc_ref[0], unit_queries(unit))
    for unit in range(n_units):
        scores_into(unit, sel_keys(0))

    c_idx = lax.broadcasted_iota(jnp.int32, (n_cmp, tq), 0)
    valid_c = (c_idx * CMP_STRIDE + (CMP_LEN - 1) <= t_row) & (c_idx < n_cmp - 1)
    bias_c = jnp.where(valid_c, 0.0, NEG_BIG)
    p_sum = []
    for hs in range(n_slots):
        g = hs // NSA_REP
        unit, hh = divmod(hs, per_unit)
        x = c_scr[unit][:, hh * tq:(hh + 1) * tq] + bias_c
        m = jnp.max(x, axis=0, keepdims=True)
        e = jnp.where(valid_c, jnp.exp2(x - m), 0.0)
        p = e / jnp.maximum(jnp.sum(e, axis=0, keepdims=True), 1e-30)
        o_c = _dot(vct_ref[0, g * HEAD_DIM:(g + 1) * HEAD_DIM, :], p.astype(BF16))
        o_scr[hs] = o_c * gsig[3 * hs:3 * hs + 1, :]
        if hs % NSA_REP == 0:
            p_sum.append(p)
        else:
            p_sum[g] = p_sum[g] + p

    cur = t_row // SEL_BLOCK
    j_idx = lax.broadcasted_iota(jnp.int32, (SEL_BLOCKS_MAX, tq), 0)
    forced = (j_idx == 0) | (j_idx == cur) | (j_idx == cur - 1)
    for g in range(NSA_GROUPS):
        hi, lo = _split_bf16(p_sum[g])
        imp = _dot(ovl_ref[...], hi) + _dot(ovl_ref[...], lo)
        imp = jnp.where(forced, jnp.inf, jnp.where(j_idx > cur, -jnp.inf, imp))
        sel_scr[g] = jnp.where(_rank_rows(imp, SEL_BLOCKS_MAX) < SEL_TOPN, 0.0, NEG_BIG)

    n_lane = tq // LANES

    def run_tile(v_tile, plan_of_group, next_keys):
        ones_rows = _ones_rows(v_tile.shape[1])
        plans = [plan_of_group(g) for g in range(NSA_GROUPS)]
        for unit in range(n_units):
            g = (unit * per_unit) // NSA_REP
            sub = v_tile.shape[1] // len(plans[g])
            v_aug = _with_sum_row(v_tile[g * HEAD_DIM:(g + 1) * HEAD_DIM, :], ones_rows)
            for hh in range(per_unit):
                def load(a, h, hh=hh, unit=unit, sub=sub):
                    return s_scr[unit][a * sub:(a + 1) * sub, hh * tq + h * LANES:hh * tq + (h + 1) * LANES]

                _slot_update(load, plans[g], m_scr, acc_scr, unit * per_unit + hh, v_aug)
            if next_keys is not None:
                scores_into(unit, next_keys())

    def fold_branch(branch):
        for hs in range(n_slots):
            o_scr[hs] = o_scr[hs] + _normalised(acc_scr[hs], gsig[3 * hs + branch:3 * hs + branch + 1, :])

    blocks_per_tile = NSA_TK // SEL_BLOCK
    first_win = jnp.maximum((t0 - (WINDOW - 1)) // WIN_TK, 0)

    def sel_row(kt, g, a, h):
        return sel_scr[g, pl.ds(kt * blocks_per_tile + a, 1), :][:, h * LANES:(h + 1) * LANES]

    def sel_plan(kt, g):
        return [[('rows', sel_row(kt, g, a, h)) for h in range(n_lane)] for a in range(blocks_per_tile)]

    def causal(k_off, q_off):
        return k_off <= q_off

    def sel_body(kt, carry):
        run_tile(vs_ref[0, kt], lambda g: sel_plan(kt, g), lambda: sel_keys(kt + 1))
        return carry

    last_sel = (t0 + tq - 1) // NSA_TK
    _reset_state(m_scr, acc_scr)
    lax.fori_loop(0, last_sel, sel_body, 0)
    run_tile(vs_ref[0, last_sel],
             lambda g: _visibility_plan(blocks_per_tile, SEL_BLOCK, n_lane, causal,
                                        lambda a, h: sel_row(last_sel, g, a, h)),
             lambda: win_keys(first_win))
    fold_branch(1)

    def win_tile(kt, visible, next_keys):
        run_tile(vw_ref[0, kt], lambda g: _visibility_plan(blocks_per_tile, SEL_BLOCK, n_lane, visible),
                 next_keys)

    _reset_state(m_scr, acc_scr)

    @pl.when(i >= 2)
    def _():
        win_tile(i - 2, lambda k_off, q_off: q_off < k_off, lambda: win_keys(i - 1))

    @pl.when(i >= 1)
    def _():
        win_tile(i - 1, lambda k_off, q_off: True, lambda: win_keys(i))

    win_tile(i, causal, None)
    fold_branch(2)

    for r in range(NSA_REP):
        out_ref[0, r * KV_WIDTH:r * KV_WIDTH + HEAD_DIM, :] = o_scr[r].astype(BF16)
        out_ref[0, r * KV_WIDTH + HEAD_DIM:(r + 1) * KV_WIDTH, :] = o_scr[NSA_REP + r].astype(BF16)


def _nsa_call(qn, kc, vct, ks, vs, kw, vw, gn, ovl):
    b, _, s = qn.shape
    tq = NSA_TQ
    n_slots = NSA_GROUPS * NSA_REP
    n_units = n_slots * tq // SCORE_LANES
    assert s // SEL_BLOCK <= SEL_BLOCKS_MAX and tq == WIN_TK == NSA_TK and WINDOW == 2 * WIN_TK
    seq = lambda a: pl.BlockSpec((1,) + a.shape[1:], lambda bi, i: (bi,) + (0,) * (a.ndim - 1))
    full = lambda a: pl.BlockSpec(a.shape, lambda bi, i: (0,) * a.ndim)
    return pl.pallas_call(
        _nsa_kernel,
        grid=(b, s // tq),
        in_specs=[pl.BlockSpec((1, Q_WIDTH, tq), lambda bi, i: (bi, 0, i)),
                  seq(kc), seq(vct), seq(ks), seq(vs), seq(kw), seq(vw),
                  pl.BlockSpec((1, GATE_ROWS, tq), lambda bi, i: (bi, 0, i)), full(ovl)],
        out_specs=pl.BlockSpec((1, Q_WIDTH, tq), lambda bi, i: (bi, 0, i)),
        out_shape=jax.ShapeDtypeStruct((b, Q_WIDTH, s), BF16),
        scratch_shapes=[pltpu.VMEM((NSA_GROUPS, SEL_BLOCKS_MAX, tq), F32),
                        pltpu.VMEM((KV_WIDTH, n_slots * tq), BF16),
                        pltpu.VMEM((n_slots, SUBLANES, tq), F32),
                        pltpu.VMEM((n_slots, HEAD_DIM + SUM_ROWS, tq), F32),
                        pltpu.VMEM((n_slots, HEAD_DIM, tq), F32)]
        + [pltpu.VMEM((max(NSA_TK, WIN_TK), SCORE_LANES), F32)] * n_units
        + [pltpu.VMEM((kc.shape[1], SCORE_LANES), F32)] * n_units,
        compiler_params=pltpu.CompilerParams(dimension_semantics=("parallel", "parallel"),
                                             vmem_limit_bytes=VMEM_LIMIT),
    )(qn, kc, vct, ks, vs, kw, vw, gn, ovl)


def _moba_kernel(q_ref, k_ref, v_ref, out_ref, km_scr, sel_scr, lhs_scr, m_scr, acc_scr, *s_scr):
    bs = MOBA_BLOCK
    n_blocks = k_ref.shape[1] // bs
    n_slots = len(s_scr)
    i = pl.program_id(2)

    @pl.when(i == 0)
    def _():
        km_scr[...] = jnp.zeros(km_scr.shape, F32)
        for n in range(n_blocks):
            km_scr[n:n + 1, :] = jnp.mean(k_ref[0, n * bs:(n + 1) * bs, :].astype(F32), axis=0,
                                          keepdims=True)

    def pair_lanes(hs):
        return slice((hs // 2) * LANES, (hs // 2 + 1) * LANES)

    zero = jnp.zeros((HEAD_DIM, bs), BF16)
    n_idx = lax.broadcasted_iota(jnp.int32, (MOBA_BLOCKS_MAX, bs), 0)
    for hs in range(n_slots):
        rows = q_ref[0, hs * HEAD_DIM:(hs + 1) * HEAD_DIM, :]
        lhs_scr[hs] = jnp.concatenate([rows, zero] if hs % 2 == 0 else [zero, rows], axis=0)

    def scores_into(hs, n):
        k_tile = k_ref[0, pl.ds(pl.multiple_of(n * bs, bs), bs), pair_lanes(hs)]
        s_scr[hs][...] = _dot(k_tile, lhs_scr[hs])

    block_scores = []
    for hs in range(n_slots):
        km_hi, km_lo = _split_bf16(km_scr[:, pair_lanes(hs)])
        block_scores.append(_dot(km_hi, lhs_scr[hs]) + _dot(km_lo, lhs_scr[hs]))
    for hs in range(n_slots):
        scores_into(hs, 0)

    for hs in range(n_slots):
        sc = jnp.where(n_idx < i, block_scores[hs], -jnp.inf)
        sel_scr[hs] = jnp.where((_rank_rows(sc, MOBA_BLOCKS_MAX) < MOBA_TOPK) & (n_idx < i), 0.0, NEG_BIG)

    ones_rows = _ones_rows(bs)

    n_lane = bs // LANES

    def run_block(n, plan_of_slot, prefetch):
        for hs in range(n_slots):
            plan = plan_of_slot(hs)
            sub = bs // len(plan)
            v_aug = _with_sum_row(v_ref[0, n, hs * HEAD_DIM:(hs + 1) * HEAD_DIM, :], ones_rows)

            def load(a, h, hs=hs, sub=sub):
                return s_scr[hs][a * sub:(a + 1) * sub, h * LANES:(h + 1) * LANES]

            _slot_update(load, plan, m_scr, acc_scr, hs, v_aug)
            if prefetch:
                scores_into(hs, n + 1)

    def past_body(n, carry):
        run_block(n, lambda hs: [[('rows', sel_scr[hs, pl.ds(n, 1), :][:, h * LANES:(h + 1) * LANES])
                                  for h in range(n_lane)]], True)
        return carry

    _reset_state(m_scr, acc_scr)
    lax.fori_loop(0, i, past_body, 0)
    own_plan = _visibility_plan(bs // OWN_SUB, OWN_SUB, n_lane, lambda k_off, q_off: k_off <= q_off)
    run_block(i, lambda hs: own_plan, False)
    for hs in range(n_slots):
        out_ref[0, hs * HEAD_DIM:(hs + 1) * HEAD_DIM, :] = _normalised(acc_scr[hs]).astype(BF16)


def _moba_call(qm, km, vm):
    b, w, s = qm.shape
    bs = MOBA_BLOCK
    nb = s // bs
    wb = MOBA_STEP_HEADS * HEAD_DIM
    assert nb <= MOBA_BLOCKS_MAX and w % wb == 0
    return pl.pallas_call(
        _moba_kernel,
        grid=(b, w // wb, nb),
        in_specs=[pl.BlockSpec((1, wb, bs), lambda bi, p, i: (bi, p, i)),
                  pl.BlockSpec((1, s, wb), lambda bi, p, i: (bi, 0, p)),
                  pl.BlockSpec((1, nb, wb, bs), lambda bi, p, i: (bi, 0, p, 0))],
        out_specs=pl.BlockSpec((1, wb, bs), lambda bi, p, i: (bi, p, i)),
        out_shape=jax.ShapeDtypeStruct((b, w, s), BF16),
        scratch_shapes=[pltpu.VMEM((MOBA_BLOCKS_MAX, wb), F32),
                        pltpu.VMEM((MOBA_STEP_HEADS, MOBA_BLOCKS_MAX, bs), F32),
                        pltpu.VMEM((MOBA_STEP_HEADS, LANES, bs), BF16),
                        pltpu.VMEM((MOBA_STEP_HEADS, SUBLANES, bs), F32),
                        pltpu.VMEM((MOBA_STEP_HEADS, HEAD_DIM + SUM_ROWS, bs), F32)]
        + [pltpu.VMEM((bs, bs), F32)] * MOBA_STEP_HEADS,
        compiler_params=pltpu.CompilerParams(
            dimension_semantics=("parallel", "parallel", "arbitrary"), vmem_limit_bytes=VMEM_LIMIT),
    )(qm, km, vm)


def _merge_kernel(x_ref, yn_ref, ym_ref, gmix_ref, wg_ref, wun_ref, wum_ref, wo_ref, out_ref):
    x = x_ref[...]
    h = _rms_rows(x, gmix_ref[...]).astype(BF16)
    ga = _sigmoid(_dot(h, wg_ref[:, :D_MODEL]))
    gb = _sigmoid(_dot(h, wg_ref[:, D_MODEL:]))
    merged = ga * _dot_tn(yn_ref[0], wun_ref[...]) + gb * _dot_tn(ym_ref[0], wum_ref[...])
    out_ref[...] = x + _dot(merged.astype(BF16), wo_ref[...])


def _merge_call(x2, yn, ym, gmix, wg, wun, wum, wo):
    t = x2.shape[0]
    tm = MERGE_TM
    nt = yn.shape[2] // tm
    row = lambda w_: pl.BlockSpec((tm, w_), lambda i: (i, 0))
    feat = pl.BlockSpec((1, Q_WIDTH, tm), lambda i: (i // nt, 0, i % nt))
    full = lambda a: pl.BlockSpec(a.shape, lambda i: (0, 0), pipeline_mode=pl.Buffered(1))
    return pl.pallas_call(
        _merge_kernel,
        grid=(t // tm,),
        in_specs=[row(D_MODEL), feat, feat, full(gmix), full(wg), full(wun),
                  full(wum), full(wo)],
        out_specs=row(D_MODEL),
        out_shape=jax.ShapeDtypeStruct((t, D_MODEL), F32),
        compiler_params=pltpu.CompilerParams(dimension_semantics=("parallel",),
                                             vmem_limit_bytes=VMEM_LIMIT),
    )(x2, yn, ym, gmix, wg, wun, wum, wo)


def _ffn_kernel(x_ref, p_ref, gffn_ref, wfi_ref, wfo_ref, gple_ref, wpg_ref, wpp_ref, out_ref):
    x = x_ref[...]
    h = _rms_rows(x, gffn_ref[...]).astype(BF16)
    y = x
    for c in range(D_FF // FFN_CHUNK):
        gate = _dot(h, wfi_ref[:, c * FFN_CHUNK:(c + 1) * FFN_CHUNK])
        up = _dot(h, wfi_ref[:, D_FF + c * FFN_CHUNK:D_FF + (c + 1) * FFN_CHUNK])
        act = (gate * _sigmoid(gate) * up).astype(BF16)
        y = y + _dot(act, wfo_ref[c * FFN_CHUNK:(c + 1) * FFN_CHUNK, :])
    h2 = _rms_rows(y, gple_ref[...]).astype(BF16)
    ple_gate = _sigmoid(_dot(h2, wpg_ref[...]))
    out_ref[...] = y + ple_gate * _dot(p_ref[...].astype(BF16), wpp_ref[...])


def _ffn_call(x1, p2, gffn, wfi, wfo, gple, wpg, wpp):
    t = x1.shape[0]
    tm = FFN_TM
    row = lambda w_: pl.BlockSpec((tm, w_), lambda i: (i, 0))
    full = lambda a: pl.BlockSpec(a.shape, lambda i: (0, 0), pipeline_mode=pl.Buffered(1))
    return pl.pallas_call(
        _ffn_kernel,
        grid=(t // tm,),
        in_specs=[row(D_MODEL), row(PLE_DIM), full(gffn), full(wfi), full(wfo), full(gple),
                  full(wpg), full(wpp)],
        out_specs=row(D_MODEL),
        out_shape=jax.ShapeDtypeStruct((t, D_MODEL), F32),
        compiler_params=pltpu.CompilerParams(dimension_semantics=("parallel",),
                                             vmem_limit_bytes=VMEM_LIMIT),
    )(x1, p2, gffn, wfi, wfo, gple, wpg, wpp)


def _block_diag_ones(width):
    idx = np.arange(width) // HEAD_DIM
    return jnp.asarray(idx[:, None] == idx[None, :], dtype=BF16)


def _inv_freq():
    return ROPE_THETA ** (-jnp.arange(ROPE_HALF, dtype=F32) / ROPE_HALF)


def _inv_freq_lanes(width):
    inv_freq = _inv_freq()
    per_head = jnp.concatenate([inv_freq, inv_freq, jnp.zeros((HEAD_DIM - ROPE_DIM,), F32)])
    return jnp.tile(per_head, width // HEAD_DIM)[None, :]


def _rope_spread_matrix(width):
    m = np.zeros((LANES, 3 * width), np.float32)
    for lane in range(width):
        d = lane % HEAD_DIM
        if d < ROPE_DIM:
            m[d % ROPE_HALF, lane] = 1.0
        else:
            m[2 * ROPE_HALF, lane] = 1.0
        if d < ROPE_HALF:
            m[ROPE_HALF + d, width + lane] = -1.0
        elif d < ROPE_DIM:
            m[ROPE_HALF + d - ROPE_HALF, 2 * width + lane] = 1.0
    return jnp.asarray(m, dtype=BF16)


def _split_w_in(w):
    parts = jnp.split(w, IN_CUTS, axis=-1)
    (q_n, kc, vc, ks, vs, kw, vw, gate_n, q_m, k_m, v_m, gate_a, gate_b) = parts
    d = w.shape[0]
    q_n = q_n.reshape(d, NSA_GROUPS, NSA_REP, HEAD_DIM).transpose(0, 2, 1, 3).reshape(d, -1)
    gate_n = jnp.pad(gate_n, ((0, 0), (0, GATE_ROWS - gate_n.shape[1])))
    w_tok = jnp.concatenate([k_m, ks, kw, kc, vc], axis=1)
    w_feat = jnp.concatenate([q_n, q_m, v_m, vs, vw, gate_n], axis=1).T
    w_gate = jnp.concatenate([gate_a, gate_b], axis=1)
    return w_tok.astype(BF16), w_feat.astype(BF16), w_gate.astype(BF16)


def _compress_weights(w1, w2, pe):
    eye = jnp.eye(NSA_GROUPS, dtype=F32)
    halves = []
    for part in (w1[:CMP_STRIDE * HEAD_DIM], w1[CMP_STRIDE * HEAD_DIM:]):
        p3 = part.reshape(CMP_STRIDE, HEAD_DIM, CMP_HIDDEN)
        halves.append(jnp.einsum('idh,gk->igdkh', p3, eye)
                      .reshape(CMP_STRIDE * NSA_GROUPS * HEAD_DIM, NSA_GROUPS * CMP_HIDDEN))
    w_big = jnp.concatenate(halves, axis=1).astype(BF16)
    w2_bd = jnp.einsum('hd,kg->khgd', w2, eye).reshape(NSA_GROUPS * CMP_HIDDEN,
                                                        NSA_GROUPS * HEAD_DIM).astype(BF16)
    pe_rows = [jnp.broadcast_to(pe[a:a + CMP_STRIDE, None, :], (CMP_STRIDE, NSA_GROUPS, HEAD_DIM))
               .reshape(1, -1) for a in (0, CMP_STRIDE)]
    return w_big, w2_bd, pe_rows


def _overlap_matrix_t(n_cmp):
    c = np.arange(n_cmp)
    j = np.arange(SEL_BLOCKS_MAX)
    start, end = c * CMP_STRIDE, c * CMP_STRIDE + CMP_LEN - 1
    ov = (start[None, :] <= j[:, None] * SEL_BLOCK + SEL_BLOCK - 1) & (end[None, :] >= j[:, None] * SEL_BLOCK)
    return jnp.asarray(ov, dtype=BF16)


def kernel(x, p, positions, g_mix, w_in, nsa_q_gain, nsa_kc_gain, nsa_ks_gain, nsa_kw_gain, nsa_pe_k, nsa_pe_v, nsa_ck_w1, nsa_ck_w2, nsa_cv_w1, nsa_cv_w2, moba_q_gain, moba_k_gain, w_up_nsa, w_up_moba, w_out, g_ffn, w_ffn_in, w_ffn_out, g_ple, w_ple_gate, w_ple_proj):
    b, s, d = x.shape
    depth = w_in.shape[0]
    assert s % PAD_MULT == 0 and s % PROJ_TM == 0 and d == D_MODEL and V_TILE == MOBA_BLOCK
    t = b * s
    n_sub = s // CMP_STRIDE

    assert ROPE_HALF == SUBLANES
    posr = positions.reshape(t // PROJ_TM, 1, PROJ_TM)
    pos_end = jnp.concatenate([positions[:, CMP_LEN - 1::CMP_STRIDE], positions[:, -1:]], axis=1)[:, :, None]
    invf128 = _inv_freq_lanes(LANES)
    invf8 = _inv_freq()[:, None]
    spread = _rope_spread_matrix(256)
    bd256, bd128 = _block_diag_ones(256), _block_diag_ones(LANES)
    ovl_t = _overlap_matrix_t(n_sub)
    tile = lambda g, n: jnp.tile(g, n)
    xi = x.reshape(t, d)

    for i in range(depth):
        w_tok, w_feat, w_gate = _split_w_in(w_in[i])
        g_tok = jnp.concatenate([tile(moba_k_gain[i], MOBA_HEADS), tile(nsa_ks_gain[i], NSA_GROUPS),
                                 tile(nsa_kw_gain[i], NSA_GROUPS)])[None, :]
        g_feat = (jnp.concatenate([tile(nsa_q_gain[i], NSA_HEADS), tile(moba_q_gain[i], MOBA_HEADS)])
                  * (ATTN_SCALE * LOG2E))[:, None]
        gmix = g_mix[i][None, :]
        (qn, qm, gn, vm, vs, vw, km, ks, kw, kc_raw, vc_raw) = _proj_call(
            xi, posr, gmix, w_tok, w_feat, g_tok, g_feat, invf8, bd256, spread, b, s)

        wk_big, w2k_bd, pe_k = _compress_weights(nsa_ck_w1[i], nsa_ck_w2[i], nsa_pe_k[i])
        wv_big, w2v_bd, pe_v = _compress_weights(nsa_cv_w1[i], nsa_cv_w2[i], nsa_pe_v[i])
        pe4 = jnp.concatenate(pe_k + pe_v, axis=0)
        kc, vct = _compress_call(kc_raw.reshape(b, s, KV_WIDTH), vc_raw.reshape(b, s, KV_WIDTH), pe4, wk_big, wv_big, w2k_bd, w2v_bd,
                                 tile(nsa_kc_gain[i], NSA_GROUPS)[None, :], pos_end,
                                 invf128, bd128)

        r3 = lambda a: a.reshape(b, s, a.shape[-1])
        y_nsa = _nsa_call(qn, kc, vct, r3(ks), vs.reshape(b, s // NSA_TK, KV_WIDTH, NSA_TK), r3(kw),
                          vw.reshape(b, s // WIN_TK, KV_WIDTH, WIN_TK), gn, ovl_t)
        y_moba = _moba_call(qm, r3(km), vm.reshape(b, s // MOBA_BLOCK, Q_WIDTH, MOBA_BLOCK))

        w_un = (w_up_nsa[i].reshape(NSA_GROUPS, NSA_REP, HEAD_DIM, d).transpose(1, 0, 2, 3)
                .reshape(NSA_HEADS * HEAD_DIM, d).astype(BF16))
        x1 = _merge_call(xi, y_nsa, y_moba, gmix, w_gate, w_un,
                         w_up_moba[i].astype(BF16), w_out[i].astype(BF16))
        xi = _ffn_call(x1, p[i].reshape(t, PLE_DIM), g_ffn[i][None, :], w_ffn_in[i].astype(BF16),
                       w_ffn_out[i].astype(BF16), g_ple[i][None, :], w_ple_gate[i].astype(BF16),
                       w_ple_proj[i].astype(BF16))
    return xi.reshape(b, s, d)
```

```python
import jax
import jax.numpy as jnp
import numpy as np
from jax import lax
from jax.experimental import pallas as pl
from jax.experimental.pallas import tpu as pltpu

F32 = jnp.float32
BF16 = jnp.bfloat16

D_MODEL = 1024
HEAD_DIM = 64
ROPE_DIM = HEAD_DIM // 4
ROPE_HALF = ROPE_DIM // 2
ROPE_THETA = 500000.0
NORM_EPS = 1e-6
ATTN_SCALE = HEAD_DIM ** -0.5

NSA_HEADS = 8
NSA_GROUPS = 2
NSA_REP = NSA_HEADS // NSA_GROUPS
CMP_LEN = 32
CMP_STRIDE = 16
CMP_HIDDEN = 256
SEL_BLOCK = 64
SEL_TOPN = 8
WINDOW = 512

MOBA_HEADS = 8
MOBA_BLOCK = 256
MOBA_TOPK = 3

PAD_MULT = 256
D_FF = ((-(-8 * D_MODEL // 3)) + 255) // 256 * 256
PLE_DIM = 256

IN_SPLITS = ((NSA_HEADS * HEAD_DIM,) + (NSA_GROUPS * HEAD_DIM,) * 6 + (3 * NSA_HEADS,)
             + (MOBA_HEADS * HEAD_DIM,) * 3 + (D_MODEL, D_MODEL))
IN_CUTS = tuple(int(c) for c in np.cumsum(IN_SPLITS)[:-1])

LANES = 128
SUBLANES = 8
NEG_BIG = -1e30
LOG2E = 1.4426950408889634
SUM_ROWS = 16
VMEM_LIMIT = 56 * 1024 * 1024

Q_WIDTH = NSA_HEADS * HEAD_DIM
KV_WIDTH = NSA_GROUPS * HEAD_DIM
GATE_ROWS = 32
KEY_ROWS = Q_WIDTH + 2 * KV_WIDTH
FEAT_ROWS = 2 * Q_WIDTH + KEY_ROWS + Q_WIDTH + 2 * KV_WIDTH + GATE_ROWS

PROJ_TM = 512
V_TILE = 256
NSA_TQ = 256
NSA_TK = V_TILE
WIN_TK = V_TILE
SCORE_LANES = 256
MERGE_TM = 512
FFN_TM = 512
MXU_TILE = 256
FFN_CHUNKS = ((0, 6 * MXU_TILE), (6 * MXU_TILE, D_FF))
SEL_BLOCKS_MAX = 32
MOBA_BLOCKS_MAX = 8
MOBA_STEP_HEADS = 8
OWN_SUB = 64


def _dot(a, b):
    return jnp.dot(a, b, preferred_element_type=F32)


def _dot_nt(a, b):
    return lax.dot_general(a, b, (((1,), (1,)), ((), ())), preferred_element_type=F32)


def _dot_tn(a, b):
    return lax.dot_general(a, b, (((0,), (0,)), ((), ())), preferred_element_type=F32)


def _split_bf16(a_f32):
    hi = a_f32.astype(BF16)
    return hi, (a_f32 - hi.astype(F32)).astype(BF16)


def _sigmoid(x):
    return 1.0 / (1.0 + jnp.exp(-x))


def _rms_rows(x, g):
    return x * lax.rsqrt(jnp.mean(x * x, axis=-1, keepdims=True) + NORM_EPS) * g


def _head_norm(t, bd, gain):
    hi, lo = _split_bf16(t * t)
    ss = _dot(hi, bd) + _dot(lo, bd)
    return t * lax.rsqrt(ss * (1.0 / HEAD_DIM) + NORM_EPS) * gain


def _rope_tables(pos_col, invf):
    ang = pos_col.astype(F32) * invf
    cos_a, sin_a = jnp.cos(ang), jnp.sin(ang)
    d = lax.broadcasted_iota(jnp.int32, ang.shape, 1) & (HEAD_DIM - 1)
    s_lo = jnp.where(d < ROPE_HALF, -sin_a, 0.0)
    s_hi = jnp.where((d >= ROPE_HALF) & (d < ROPE_DIM), sin_a, 0.0)
    return cos_a, s_lo, s_hi


def _rope(y, tables):
    cos_a, s_lo, s_hi = tables
    w = y.shape[1]
    return (y * cos_a + pltpu.roll(y, w - ROPE_HALF, 1) * s_lo
            + pltpu.roll(y, ROPE_HALF, 1) * s_hi)


def _ones_rows(tk):
    r = lax.broadcasted_iota(jnp.int32, (SUM_ROWS, tk), 0)
    return jnp.where(r == 0, 1.0, 0.0).astype(BF16)


def _with_sum_row(v_rows, ones_rows):
    return jnp.concatenate([v_rows, ones_rows], axis=0)


def _slot_update(load_s, plan, m_ref, acc_ref, hs, v_aug):
    n_sub, n_lane = len(plan), len(plan[0])
    sub = v_aug.shape[1] // n_sub
    e_cols, alphas = [], []
    for h in range(n_lane):
        lanes = slice(h * LANES, (h + 1) * LANES)
        m = m_ref[hs, 0:1, lanes]
        m_new = m
        for a in range(n_sub):
            if plan[a][h] is None:
                continue
            kind, arg = plan[a][h]
            if kind == 'rows':
                top = jnp.max(load_s(a, h), axis=0, keepdims=True)
                m_new = jnp.maximum(m_new, top if arg is None else jnp.where(arg < 0.0, NEG_BIG, top))
            else:
                m_new = jnp.maximum(m_new, jnp.max(load_s(a, h) + arg, axis=0, keepdims=True))
        e = []
        for a in range(n_sub):
            if plan[a][h] is None:
                e.append(jnp.zeros((sub, LANES), BF16))
                continue
            kind, arg = plan[a][h]
            if kind == 'rows':
                shift = -m_new if arg is None else jnp.where(arg < 0.0, NEG_BIG, -m_new)
                e.append(jnp.exp2(load_s(a, h) + shift).astype(BF16))
            else:
                e.append(jnp.exp2(load_s(a, h) + arg - m_new).astype(BF16))
        e_cols.append(e[0] if n_sub == 1 else jnp.concatenate(e, axis=0))
        alphas.append(jnp.exp2(m - m_new))
        m_ref[hs, 0:1, lanes] = m_new
    e_all = e_cols[0] if n_lane == 1 else jnp.concatenate(e_cols, axis=1)
    alpha = alphas[0] if n_lane == 1 else jnp.concatenate(alphas, axis=1)
    acc_ref[hs] = alpha * acc_ref[hs] + _dot(v_aug, e_all)


def _visibility_plan(n_sub, sub, n_lane, visible, kept_row=None):
    plan = []
    for a in range(n_sub):
        row_plan = []
        for h in range(n_lane):
            k0, k1, q0, q1 = a * sub, (a + 1) * sub - 1, h * LANES, (h + 1) * LANES - 1
            corners = [visible(k, q) for k in (k0, k1) for q in (q0, q1)]
            row = None if kept_row is None else kept_row(a, h)
            if not any(corners):
                row_plan.append(None)
            elif all(corners):
                row_plan.append(('rows', row))
            else:
                k_off = k0 + lax.broadcasted_iota(jnp.int32, (sub, LANES), 0)
                q_off = q0 + lax.broadcasted_iota(jnp.int32, (sub, LANES), 1)
                keep = 0.0 if row is None else row
                row_plan.append(('tile', jnp.where(visible(k_off, q_off), keep, NEG_BIG)))
        plan.append(row_plan)
    return plan


def _reset_state(m_ref, acc_ref):
    m_ref[...] = jnp.full(m_ref.shape, NEG_BIG, F32)
    acc_ref[...] = jnp.zeros(acc_ref.shape, F32)


def _normalised(acc, scale_row=1.0):
    return acc[:HEAD_DIM] * (scale_row / jnp.maximum(acc[HEAD_DIM:HEAD_DIM + 1], 1e-30))


def _rank_rows(v, n_rows):
    n, q = v.shape
    slabs = [v[a:a + SUBLANES] for a in range(0, n, SUBLANES)]
    ranks = [jnp.zeros((SUBLANES, q), F32) for _ in slabs]
    sub = lax.broadcasted_iota(jnp.int32, (SUBLANES, q), 0)
    for jp in range(n_rows):
        row = v[jp:jp + 1, :]
        for si, slab in enumerate(slabs):
            first = si * SUBLANES
            if first > jp:
                beats = jnp.where(row >= slab, 1.0, 0.0)
            elif first + SUBLANES - 1 < jp:
                beats = jnp.where(row > slab, 1.0, 0.0)
            else:
                ge = jnp.where(row >= slab, 1.0, 0.0)
                gt = jnp.where(row > slab, 1.0, 0.0)
                beats = gt + (ge - gt) * jnp.where(sub > jp - first, 1.0, 0.0)
            ranks[si] = ranks[si] + beats
    return ranks[0] if len(ranks) == 1 else jnp.concatenate(ranks, axis=0)


def _proj_kernel(x_ref, posr_ref, gmix_ref, wtok_ref, wfeat_ref, gfeat_ref, invf8_ref,
                 qn_ref, qm_ref, gn_ref, vm_ref, vs_ref, vw_ref, km_ref, ks_ref, kw_ref,
                 kc_ref, vc_ref):
    tm = x_ref.shape[0]
    h = _rms_rows(x_ref[...], gmix_ref[...]).astype(BF16)

    acc = _dot(h, wtok_ref[...])
    kc_ref[...] = acc[:, :KV_WIDTH]
    vc_ref[...] = acc[:, KV_WIDTH:]

    acc_t = _dot_nt(wfeat_ref[...], h)
    ang = invf8_ref[...] * posr_ref[0].astype(F32)
    cos_a, sin_a = jnp.cos(ang), jnp.sin(ang)

    def normed_head(hh):
        t = acc_t[hh * HEAD_DIM:(hh + 1) * HEAD_DIM, :]
        ss = jnp.sum(t * t, axis=0, keepdims=True)
        y = t * lax.rsqrt(ss * (1.0 / HEAD_DIM) + NORM_EPS) * gfeat_ref[hh * HEAD_DIM:(hh + 1) * HEAD_DIM, :]
        a, b = y[:ROPE_HALF], y[ROPE_HALF:ROPE_DIM]
        return jnp.concatenate([a * cos_a - b * sin_a, b * cos_a + a * sin_a, y[ROPE_DIM:]], axis=0)

    for hh in range(2 * NSA_HEADS):
        dst = qn_ref if hh < NSA_HEADS else qm_ref
        r0 = (hh % NSA_HEADS) * HEAD_DIM
        dst[0, r0:r0 + HEAD_DIM, :] = normed_head(hh).astype(BF16)
    k0 = 2 * NSA_HEADS
    for pair in range(KEY_ROWS // LANES):
        y = jnp.concatenate([normed_head(k0 + 2 * pair), normed_head(k0 + 2 * pair + 1)], axis=0)
        y = y.T.astype(BF16)
        if pair < Q_WIDTH // LANES:
            km_ref[:, pair * LANES:(pair + 1) * LANES] = y
        elif pair == Q_WIDTH // LANES:
            ks_ref[...] = y
        else:
            kw_ref[...] = y
    o = 2 * Q_WIDTH + KEY_ROWS
    for j in range(tm // V_TILE):
        cols = slice(j * V_TILE, (j + 1) * V_TILE)
        vm_ref[j] = acc_t[o:o + Q_WIDTH, cols].astype(BF16)
        vs_ref[j] = acc_t[o + Q_WIDTH:o + Q_WIDTH + KV_WIDTH, cols].astype(BF16)
        vw_ref[j] = acc_t[o + Q_WIDTH + KV_WIDTH:o + Q_WIDTH + 2 * KV_WIDTH, cols].astype(BF16)
    gn_ref[0] = acc_t[o + Q_WIDTH + 2 * KV_WIDTH:]


def _proj_call(x2, posr, gmix, wtok, wfeat, gfeat, invf8, b, s):
    t = x2.shape[0]
    tm = PROJ_TM
    nt = s // tm
    row = lambda w_: pl.BlockSpec((tm, w_), lambda i: (i, 0))
    full = lambda a: pl.BlockSpec(a.shape, lambda i: (0,) * a.ndim, pipeline_mode=pl.Buffered(1))
    feat = lambda r: pl.BlockSpec((1, r, tm), lambda i: (i // nt, 0, i % nt))
    tile = lambda r: pl.BlockSpec((tm // V_TILE, r, V_TILE), lambda i: (i, 0, 0))
    out_shape = [
        jax.ShapeDtypeStruct((b, Q_WIDTH, s), BF16),
        jax.ShapeDtypeStruct((b, Q_WIDTH, s), BF16),
        jax.ShapeDtypeStruct((b, GATE_ROWS, s), F32),
        jax.ShapeDtypeStruct((t // V_TILE, Q_WIDTH, V_TILE), BF16),
        jax.ShapeDtypeStruct((t // V_TILE, KV_WIDTH, V_TILE), BF16),
        jax.ShapeDtypeStruct((t // V_TILE, KV_WIDTH, V_TILE), BF16),
        jax.ShapeDtypeStruct((t, Q_WIDTH), BF16),
        jax.ShapeDtypeStruct((t, KV_WIDTH), BF16),
        jax.ShapeDtypeStruct((t, KV_WIDTH), BF16),
        jax.ShapeDtypeStruct((t, KV_WIDTH), F32),
        jax.ShapeDtypeStruct((t, KV_WIDTH), F32),
    ]
    out_specs = [feat(Q_WIDTH), feat(Q_WIDTH), feat(GATE_ROWS), tile(Q_WIDTH), tile(KV_WIDTH),
                 tile(KV_WIDTH), row(Q_WIDTH), row(KV_WIDTH), row(KV_WIDTH), row(KV_WIDTH), row(KV_WIDTH)]
    return pl.pallas_call(
        _proj_kernel,
        grid=(t // tm,),
        in_specs=[row(D_MODEL), pl.BlockSpec((1, 1, tm), lambda i: (i, 0, 0)), full(gmix),
                  full(wtok), full(wfeat), full(gfeat), full(invf8)],
        out_specs=out_specs,
        out_shape=out_shape,
        compiler_params=pltpu.CompilerParams(dimension_semantics=("parallel",),
                                             vmem_limit_bytes=VMEM_LIMIT),
    )(x2, posr, gmix, wtok, wfeat, gfeat, invf8)


def _compress_kernel(kcr_ref, vcr_ref, pe_ref, wk_ref, wv_ref, w2k_ref, w2v_ref, gain_ref,
                     pose_ref, invf_ref, bd_ref, kc_out, vct_out):
    half = NSA_GROUPS * CMP_HIDDEN
    n_sub = kcr_ref.shape[1] // CMP_STRIDE

    def sub_blocks(ref):
        return jnp.concatenate([ref[0, pl.ds(i, n_sub, stride=CMP_STRIDE), :] for i in range(CMP_STRIDE)],
                               axis=1)

    def comp(x, pe_a, pe_b, w_ref, w2_ref):
        a = _dot((x + pe_a).astype(BF16), w_ref[:, :half])
        b = _dot((x + pe_b).astype(BF16), w_ref[:, half:])
        hid = a + pltpu.roll(b, b.shape[0] - 1, 0)
        act = hid * _sigmoid(hid)
        return _dot(act.astype(BF16), w2_ref[...])

    kc = comp(sub_blocks(kcr_ref), pe_ref[0:1, :], pe_ref[1:2, :], wk_ref, w2k_ref)
    vc = comp(sub_blocks(vcr_ref), pe_ref[2:3, :], pe_ref[3:4, :], wv_ref, w2v_ref)
    kc = _head_norm(kc, bd_ref[...], gain_ref[...])
    kc = _rope(kc, _rope_tables(pose_ref[0], invf_ref[...]))
    kc_out[0] = kc.astype(BF16)
    vct_out[0] = vc.T.astype(BF16)


def _compress_call(kc_raw, vc_raw, pe4, wk, wv, w2k, w2v, gain, pos_end, invf, bd):
    b, s, width = kc_raw.shape
    n_sub = s // CMP_STRIDE
    blk = lambda shp: pl.BlockSpec((1,) + shp, lambda i: (i, 0, 0))
    full = lambda a: pl.BlockSpec(a.shape, lambda i: (0,) * a.ndim)
    return pl.pallas_call(
        _compress_kernel,
        grid=(b,),
        in_specs=[blk((s, width)), blk((s, width)), full(pe4), full(wk), full(wv),
                  full(w2k), full(w2v), full(gain), blk((n_sub, 1)), full(invf), full(bd)],
        out_specs=[blk((n_sub, KV_WIDTH)), blk((KV_WIDTH, n_sub))],
        out_shape=[jax.ShapeDtypeStruct((b, n_sub, KV_WIDTH), BF16),
                   jax.ShapeDtypeStruct((b, KV_WIDTH, n_sub), BF16)],
        compiler_params=pltpu.CompilerParams(dimension_semantics=("parallel",),
                                             vmem_limit_bytes=VMEM_LIMIT),
    )(kc_raw, vc_raw, pe4, wk, wv, w2k, w2v, gain, pos_end, invf, bd)


def _nsa_kernel(q_ref, kc_ref, vct_ref, ks_ref, vs_ref, kw_ref, vw_ref, gate_ref, ovl_ref, out_ref,
                sel_scr, lhs_scr, m_scr, acc_scr, o_scr, *bufs):
    tq = q_ref.shape[2]
    n_cmp = kc_ref.shape[1]
    n_slots = NSA_GROUPS * NSA_REP
    n_units = len(bufs) // 2
    s_scr, c_scr = bufs[:n_units], bufs[n_units:]
    per_unit = n_slots // n_units
    i = pl.program_id(1)
    t0 = i * tq
    t_row = t0 + lax.broadcasted_iota(jnp.int32, (1, tq), 1)

    zero = jnp.zeros((HEAD_DIM, tq), BF16)
    for g in range(NSA_GROUPS):
        for r in range(NSA_REP):
            hs = g * NSA_REP + r
            rows = q_ref[0, r * KV_WIDTH + g * HEAD_DIM:r * KV_WIDTH + (g + 1) * HEAD_DIM, :]
            lhs_scr[:, hs * tq:(hs + 1) * tq] = jnp.concatenate([rows, zero] if g == 0 else [zero, rows],
                                                                axis=0)
    gsig = _sigmoid(gate_ref[0])

    def unit_queries(unit):
        return lhs_scr[:, unit * SCORE_LANES:(unit + 1) * SCORE_LANES]

    def scores_into(unit, k_tile):
        tk = k_tile.shape[0]
        s_scr[unit][0:tk, :] = _dot(k_tile, unit_queries(unit))

    def sel_keys(kt):
        return ks_ref[0, pl.ds(pl.multiple_of(kt * NSA_TK, NSA_TK), NSA_TK), :]

    def win_keys(kt):
        return kw_ref[0, pl.ds(pl.multiple_of(kt * WIN_TK, WIN_TK), WIN_TK), :]

    for unit in range(n_units):
        c_scr[unit][...] = _dot(kc_ref[0], unit_queries(unit))
    for unit in range(n_units):
        scores_into(unit, sel_keys(0))

    c_idx = lax.broadcasted_iota(jnp.int32, (n_cmp, tq), 0)
    valid_c = (c_idx * CMP_STRIDE + (CMP_LEN - 1) <= t_row) & (c_idx < n_cmp - 1)
    bias_c = jnp.where(valid_c, 0.0, NEG_BIG)
    p_sum = []
    for hs in range(n_slots):
        g = hs // NSA_REP
        unit, hh = divmod(hs, per_unit)
        x = c_scr[unit][:, hh * tq:(hh + 1) * tq] + bias_c
        m = jnp.max(x, axis=0, keepdims=True)
        e = jnp.where(valid_c, jnp.exp2(x - m), 0.0)
        p = e / jnp.maximum(jnp.sum(e, axis=0, keepdims=True), 1e-30)
        o_c = _dot(vct_ref[0, g * HEAD_DIM:(g + 1) * HEAD_DIM, :], p.astype(BF16))
        o_scr[hs] = o_c * gsig[3 * hs:3 * hs + 1, :]
        if hs % NSA_REP == 0:
            p_sum.append(p)
        else:
            p_sum[g] = p_sum[g] + p

    cur = t_row // SEL_BLOCK
    j_idx = lax.broadcasted_iota(jnp.int32, (SEL_BLOCKS_MAX, tq), 0)
    forced = (j_idx == 0) | (j_idx == cur) | (j_idx == cur - 1)
    for g in range(NSA_GROUPS):
        hi, lo = _split_bf16(p_sum[g])
        imp = _dot(ovl_ref[...], hi) + _dot(ovl_ref[...], lo)
        imp = jnp.where(forced, jnp.inf, jnp.where(j_idx > cur, -jnp.inf, imp))
        sel_scr[g] = jnp.where(_rank_rows(imp, SEL_BLOCKS_MAX) < SEL_TOPN, 0.0, NEG_BIG)

    n_lane = tq // LANES

    def run_tile(v_tile, plan_of_group, next_keys):
        ones_rows = _ones_rows(v_tile.shape[1])
        plans = [plan_of_group(g) for g in range(NSA_GROUPS)]
        for unit in range(n_units):
            g = (unit * per_unit) // NSA_REP
            sub = v_tile.shape[1] // len(plans[g])
            v_aug = _with_sum_row(v_tile[g * HEAD_DIM:(g + 1) * HEAD_DIM, :], ones_rows)
            for hh in range(per_unit):
                def load(a, h, hh=hh, unit=unit, sub=sub):
                    return s_scr[unit][a * sub:(a + 1) * sub, hh * tq + h * LANES:hh * tq + (h + 1) * LANES]

                _slot_update(load, plans[g], m_scr, acc_scr, unit * per_unit + hh, v_aug)
            if next_keys is not None:
                scores_into(unit, next_keys())

    def fold_branch(branch):
        for hs in range(n_slots):
            o_scr[hs] = o_scr[hs] + _normalised(acc_scr[hs], gsig[3 * hs + branch:3 * hs + branch + 1, :])

    blocks_per_tile = NSA_TK // SEL_BLOCK
    first_win = jnp.maximum((t0 - (WINDOW - 1)) // WIN_TK, 0)

    def sel_row(kt, g, a, h):
        return sel_scr[g, pl.ds(kt * blocks_per_tile + a, 1), :][:, h * LANES:(h + 1) * LANES]

    def sel_plan(kt, g):
        return [[('rows', sel_row(kt, g, a, h)) for h in range(n_lane)] for a in range(blocks_per_tile)]

    def causal(k_off, q_off):
        return k_off <= q_off

    def sel_body(kt, carry):
        run_tile(vs_ref[0, kt], lambda g: sel_plan(kt, g), lambda: sel_keys(kt + 1))
        return carry

    last_sel = (t0 + tq - 1) // NSA_TK
    _reset_state(m_scr, acc_scr)
    lax.fori_loop(0, last_sel, sel_body, 0)
    run_tile(vs_ref[0, last_sel],
             lambda g: _visibility_plan(blocks_per_tile, SEL_BLOCK, n_lane, causal,
                                        lambda a, h: sel_row(last_sel, g, a, h)),
             lambda: win_keys(first_win))
    fold_branch(1)

    def win_tile(kt, visible, next_keys):
        run_tile(vw_ref[0, kt], lambda g: _visibility_plan(blocks_per_tile, SEL_BLOCK, n_lane, visible),
                 next_keys)

    _reset_state(m_scr, acc_scr)

    @pl.when(i >= 2)
    def _():
        win_tile(i - 2, lambda k_off, q_off: q_off < k_off, lambda: win_keys(i - 1))

    @pl.when(i >= 1)
    def _():
        win_tile(i - 1, lambda k_off, q_off: True, lambda: win_keys(i))

    win_tile(i, causal, None)
    fold_branch(2)

    for r in range(NSA_REP):
        out_ref[0, r * KV_WIDTH:r * KV_WIDTH + HEAD_DIM, :] = o_scr[r].astype(BF16)
        out_ref[0, r * KV_WIDTH + HEAD_DIM:(r + 1) * KV_WIDTH, :] = o_scr[NSA_REP + r].astype(BF16)


def _nsa_call(qn, kc, vct, ks, vs, kw, vw, gn, ovl):
    b, _, s = qn.shape
    tq = NSA_TQ
    n_slots = NSA_GROUPS * NSA_REP
    n_units = n_slots * tq // SCORE_LANES
    assert s // SEL_BLOCK <= SEL_BLOCKS_MAX and tq == WIN_TK == NSA_TK and WINDOW == 2 * WIN_TK
    seq = lambda a: pl.BlockSpec((1,) + a.shape[1:], lambda bi, i: (bi,) + (0,) * (a.ndim - 1))
    full = lambda a: pl.BlockSpec(a.shape, lambda bi, i: (0,) * a.ndim)
    return pl.pallas_call(
        _nsa_kernel,
        grid=(b, s // tq),
        in_specs=[pl.BlockSpec((1, Q_WIDTH, tq), lambda bi, i: (bi, 0, i)),
                  seq(kc), seq(vct), seq(ks), seq(vs), seq(kw), seq(vw),
                  pl.BlockSpec((1, GATE_ROWS, tq), lambda bi, i: (bi, 0, i)), full(ovl)],
        out_specs=pl.BlockSpec((1, Q_WIDTH, tq), lambda bi, i: (bi, 0, i)),
        out_shape=jax.ShapeDtypeStruct((b, Q_WIDTH, s), BF16),
        scratch_shapes=[pltpu.VMEM((NSA_GROUPS, SEL_BLOCKS_MAX, tq), F32),
                        pltpu.VMEM((KV_WIDTH, n_slots * tq), BF16),
                        pltpu.VMEM((n_slots, SUBLANES, tq), F32),
                        pltpu.VMEM((n_slots, HEAD_DIM + SUM_ROWS, tq), F32),
                        pltpu.VMEM((n_slots, HEAD_DIM, tq), F32)]
        + [pltpu.VMEM((max(NSA_TK, WIN_TK), SCORE_LANES), F32)] * n_units
        + [pltpu.VMEM((kc.shape[1], SCORE_LANES), F32)] * n_units,
        compiler_params=pltpu.CompilerParams(dimension_semantics=("parallel", "parallel"),
                                             vmem_limit_bytes=VMEM_LIMIT),
    )(qn, kc, vct, ks, vs, kw, vw, gn, ovl)


def _moba_kernel(q_ref, k_ref, v_ref, out_ref, km_scr, sel_scr, lhs_scr, m_scr, acc_scr, *s_scr):
    bs = MOBA_BLOCK
    n_blocks = k_ref.shape[1] // bs
    n_slots = len(s_scr)
    i = pl.program_id(2)

    @pl.when(i == 0)
    def _():
        km_scr[...] = jnp.zeros(km_scr.shape, F32)
        for n in range(n_blocks):
            km_scr[n:n + 1, :] = jnp.mean(k_ref[0, n * bs:(n + 1) * bs, :].astype(F32), axis=0,
                                          keepdims=True)

    def pair_lanes(hs):
        return slice((hs // 2) * LANES, (hs // 2 + 1) * LANES)

    zero = jnp.zeros((HEAD_DIM, bs), BF16)
    n_idx = lax.broadcasted_iota(jnp.int32, (MOBA_BLOCKS_MAX, bs), 0)
    for hs in range(n_slots):
        rows = q_ref[0, hs * HEAD_DIM:(hs + 1) * HEAD_DIM, :]
        lhs_scr[hs] = jnp.concatenate([rows, zero] if hs % 2 == 0 else [zero, rows], axis=0)

    def scores_into(hs, n):
        k_tile = k_ref[0, pl.ds(pl.multiple_of(n * bs, bs), bs), pair_lanes(hs)]
        s_scr[hs][...] = _dot(k_tile, lhs_scr[hs])

    block_scores = []
    for hs in range(n_slots):
        km_hi, km_lo = _split_bf16(km_scr[:, pair_lanes(hs)])
        block_scores.append(_dot(km_hi, lhs_scr[hs]) + _dot(km_lo, lhs_scr[hs]))
    for hs in range(n_slots):
        scores_into(hs, 0)

    for hs in range(n_slots):
        sc = jnp.where(n_idx < i, block_scores[hs], -jnp.inf)
        sel_scr[hs] = jnp.where((_rank_rows(sc, MOBA_BLOCKS_MAX) < MOBA_TOPK) & (n_idx < i), 0.0, NEG_BIG)

    ones_rows = _ones_rows(bs)

    n_lane = bs // LANES

    def run_block(n, plan_of_slot, prefetch):
        for hs in range(n_slots):
            plan = plan_of_slot(hs)
            sub = bs // len(plan)
            v_aug = _with_sum_row(v_ref[0, n, hs * HEAD_DIM:(hs + 1) * HEAD_DIM, :], ones_rows)

            def load(a, h, hs=hs, sub=sub):
                return s_scr[hs][a * sub:(a + 1) * sub, h * LANES:(h + 1) * LANES]

            _slot_update(load, plan, m_scr, acc_scr, hs, v_aug)
            if prefetch:
                scores_into(hs, n + 1)

    def past_body(n, carry):
        run_block(n, lambda hs: [[('rows', sel_scr[hs, pl.ds(n, 1), :][:, h * LANES:(h + 1) * LANES])
                                  for h in range(n_lane)]], True)
        return carry

    _reset_state(m_scr, acc_scr)
    lax.fori_loop(0, i, past_body, 0)
    own_plan = _visibility_plan(bs // OWN_SUB, OWN_SUB, n_lane, lambda k_off, q_off: k_off <= q_off)
    run_block(i, lambda hs: own_plan, False)
    for hs in range(n_slots):
        out_ref[0, hs * HEAD_DIM:(hs + 1) * HEAD_DIM, :] = _normalised(acc_scr[hs]).astype(BF16)


def _moba_call(qm, km, vm):
    b, w, s = qm.shape
    bs = MOBA_BLOCK
    nb = s // bs
    wb = MOBA_STEP_HEADS * HEAD_DIM
    assert nb <= MOBA_BLOCKS_MAX and w % wb == 0
    return pl.pallas_call(
        _moba_kernel,
        grid=(b, w // wb, nb),
        in_specs=[pl.BlockSpec((1, wb, bs), lambda bi, p, i: (bi, p, i)),
                  pl.BlockSpec((1, s, wb), lambda bi, p, i: (bi, 0, p)),
                  pl.BlockSpec((1, nb, wb, bs), lambda bi, p, i: (bi, 0, p, 0))],
        out_specs=pl.BlockSpec((1, wb, bs), lambda bi, p, i: (bi, p, i)),
        out_shape=jax.ShapeDtypeStruct((b, w, s), BF16),
        scratch_shapes=[pltpu.VMEM((MOBA_BLOCKS_MAX, wb), F32),
                        pltpu.VMEM((MOBA_STEP_HEADS, MOBA_BLOCKS_MAX, bs), F32),
                        pltpu.VMEM((MOBA_STEP_HEADS, LANES, bs), BF16),
                        pltpu.VMEM((MOBA_STEP_HEADS, SUBLANES, bs), F32),
                        pltpu.VMEM((MOBA_STEP_HEADS, HEAD_DIM + SUM_ROWS, bs), F32)]
        + [pltpu.VMEM((bs, bs), F32)] * MOBA_STEP_HEADS,
        compiler_params=pltpu.CompilerParams(
            dimension_semantics=("parallel", "parallel", "arbitrary"), vmem_limit_bytes=VMEM_LIMIT),
    )(qm, km, vm)


def _merge_kernel(x_ref, yn_ref, ym_ref, gmix_ref, wg_ref, wun_ref, wum_ref, wo_ref, out_ref):
    x = x_ref[...]
    h = _rms_rows(x, gmix_ref[...]).astype(BF16)
    ga = _sigmoid(_dot(h, wg_ref[:, :D_MODEL]))
    gb = _sigmoid(_dot(h, wg_ref[:, D_MODEL:]))
    merged = ga * _dot_tn(yn_ref[0], wun_ref[...]) + gb * _dot_tn(ym_ref[0], wum_ref[...])
    out_ref[...] = x + _dot(merged.astype(BF16), wo_ref[...])


def _merge_call(x2, yn, ym, gmix, wg, wun, wum, wo):
    t = x2.shape[0]
    tm = MERGE_TM
    nt = yn.shape[2] // tm
    row = lambda w_: pl.BlockSpec((tm, w_), lambda i: (i, 0))
    feat = pl.BlockSpec((1, Q_WIDTH, tm), lambda i: (i // nt, 0, i % nt))
    full = lambda a: pl.BlockSpec(a.shape, lambda i: (0, 0), pipeline_mode=pl.Buffered(1))
    return pl.pallas_call(
        _merge_kernel,
        grid=(t // tm,),
        in_specs=[row(D_MODEL), feat, feat, full(gmix), full(wg), full(wun),
                  full(wum), full(wo)],
        out_specs=row(D_MODEL),
        out_shape=jax.ShapeDtypeStruct((t, D_MODEL), F32),
        compiler_params=pltpu.CompilerParams(dimension_semantics=("parallel",),
                                             vmem_limit_bytes=VMEM_LIMIT),
    )(x2, yn, ym, gmix, wg, wun, wum, wo)


def _ffn_kernel(x_ref, p_ref, gffn_ref, wfi_ref, wfo_ref, gple_ref, wpg_ref, wpp_ref, out_ref):
    x = x_ref[...]
    h = _rms_rows(x, gffn_ref[...]).astype(BF16)
    y = x
    for lo, hi in FFN_CHUNKS:
        gate = _dot(h, wfi_ref[:, lo:hi])
        up = _dot(h, wfi_ref[:, D_FF + lo:D_FF + hi])
        act = (gate * _sigmoid(gate) * up).astype(BF16)
        y = y + _dot(act, wfo_ref[lo:hi, :])
    h2 = _rms_rows(y, gple_ref[...]).astype(BF16)
    ple_gate = _sigmoid(_dot(h2, wpg_ref[...]))
    out_ref[...] = y + ple_gate * _dot(p_ref[...].astype(BF16), wpp_ref[...])


def _ffn_call(x1, p2, gffn, wfi, wfo, gple, wpg, wpp):
    t = x1.shape[0]
    tm = FFN_TM
    row = lambda w_: pl.BlockSpec((tm, w_), lambda i: (i, 0))
    full = lambda a: pl.BlockSpec(a.shape, lambda i: (0, 0), pipeline_mode=pl.Buffered(1))
    return pl.pallas_call(
        _ffn_kernel,
        grid=(t // tm,),
        in_specs=[row(D_MODEL), row(PLE_DIM), full(gffn), full(wfi), full(wfo), full(gple),
                  full(wpg), full(wpp)],
        out_specs=row(D_MODEL),
        out_shape=jax.ShapeDtypeStruct((t, D_MODEL), F32),
        compiler_params=pltpu.CompilerParams(dimension_semantics=("parallel",),
                                             vmem_limit_bytes=VMEM_LIMIT),
    )(x1, p2, gffn, wfi, wfo, gple, wpg, wpp)


def _block_diag_ones(width):
    idx = np.arange(width) // HEAD_DIM
    return jnp.asarray(idx[:, None] == idx[None, :], dtype=BF16)


def _inv_freq():
    return ROPE_THETA ** (-jnp.arange(ROPE_HALF, dtype=F32) / ROPE_HALF)


def _inv_freq_lanes(width):
    inv_freq = _inv_freq()
    per_head = jnp.concatenate([inv_freq, inv_freq, jnp.zeros((HEAD_DIM - ROPE_DIM,), F32)])
    return jnp.tile(per_head, width // HEAD_DIM)[None, :]


def _split_w_in(w):
    parts = jnp.split(w, IN_CUTS, axis=-1)
    (q_n, kc, vc, ks, vs, kw, vw, gate_n, q_m, k_m, v_m, gate_a, gate_b) = parts
    d = w.shape[0]
    q_n = q_n.reshape(d, NSA_GROUPS, NSA_REP, HEAD_DIM).transpose(0, 2, 1, 3).reshape(d, -1)
    gate_n = jnp.pad(gate_n, ((0, 0), (0, GATE_ROWS - gate_n.shape[1])))
    w_tok = jnp.concatenate([kc, vc], axis=1)
    w_feat = jnp.concatenate([q_n, q_m, k_m, ks, kw, v_m, vs, vw, gate_n], axis=1).T
    w_gate = jnp.concatenate([gate_a, gate_b], axis=1)
    return w_tok.astype(BF16), w_feat.astype(BF16), w_gate.astype(BF16)


def _compress_weights(w1, w2, pe):
    eye = jnp.eye(NSA_GROUPS, dtype=F32)
    halves = []
    for part in (w1[:CMP_STRIDE * HEAD_DIM], w1[CMP_STRIDE * HEAD_DIM:]):
        p3 = part.reshape(CMP_STRIDE, HEAD_DIM, CMP_HIDDEN)
        halves.append(jnp.einsum('idh,gk->igdkh', p3, eye)
                      .reshape(CMP_STRIDE * NSA_GROUPS * HEAD_DIM, NSA_GROUPS * CMP_HIDDEN))
    w_big = jnp.concatenate(halves, axis=1).astype(BF16)
    w2_bd = jnp.einsum('hd,kg->khgd', w2, eye).reshape(NSA_GROUPS * CMP_HIDDEN,
                                                        NSA_GROUPS * HEAD_DIM).astype(BF16)
    pe_rows = [jnp.broadcast_to(pe[a:a + CMP_STRIDE, None, :], (CMP_STRIDE, NSA_GROUPS, HEAD_DIM))
               .reshape(1, -1) for a in (0, CMP_STRIDE)]
    return w_big, w2_bd, pe_rows


def _overlap_matrix_t(n_cmp):
    c = np.arange(n_cmp)
    j = np.arange(SEL_BLOCKS_MAX)
    start, end = c * CMP_STRIDE, c * CMP_STRIDE + CMP_LEN - 1
    ov = (start[None, :] <= j[:, None] * SEL_BLOCK + SEL_BLOCK - 1) & (end[None, :] >= j[:, None] * SEL_BLOCK)
    return jnp.asarray(ov, dtype=BF16)


def kernel(x, p, positions, g_mix, w_in, nsa_q_gain, nsa_kc_gain, nsa_ks_gain, nsa_kw_gain, nsa_pe_k, nsa_pe_v, nsa_ck_w1, nsa_ck_w2, nsa_cv_w1, nsa_cv_w2, moba_q_gain, moba_k_gain, w_up_nsa, w_up_moba, w_out, g_ffn, w_ffn_in, w_ffn_out, g_ple, w_ple_gate, w_ple_proj):
    b, s, d = x.shape
    depth = w_in.shape[0]
    assert s % PAD_MULT == 0 and s % PROJ_TM == 0 and d == D_MODEL and V_TILE == MOBA_BLOCK
    t = b * s
    n_sub = s // CMP_STRIDE

    assert ROPE_HALF == SUBLANES
    posr = positions.reshape(t // PROJ_TM, 1, PROJ_TM)
    pos_end = jnp.concatenate([positions[:, CMP_LEN - 1::CMP_STRIDE], positions[:, -1:]], axis=1)[:, :, None]
    invf128 = _inv_freq_lanes(LANES)
    invf8 = _inv_freq()[:, None]
    bd128 = _block_diag_ones(LANES)
    ovl_t = _overlap_matrix_t(n_sub)
    tile = lambda g, n: jnp.tile(g, n)
    xi = x.reshape(t, d)

    for i in range(depth):
        w_tok, w_feat, w_gate = _split_w_in(w_in[i])
        g_feat = jnp.concatenate([
            jnp.concatenate([tile(nsa_q_gain[i], NSA_HEADS), tile(moba_q_gain[i], MOBA_HEADS)])
            * (ATTN_SCALE * LOG2E),
            tile(moba_k_gain[i], MOBA_HEADS), tile(nsa_ks_gain[i], NSA_GROUPS),
            tile(nsa_kw_gain[i], NSA_GROUPS)])[:, None]
        gmix = g_mix[i][None, :]
        (qn, qm, gn, vm, vs, vw, km, ks, kw, kc_raw, vc_raw) = _proj_call(
            xi, posr, gmix, w_tok, w_feat, g_feat, invf8, b, s)

        wk_big, w2k_bd, pe_k = _compress_weights(nsa_ck_w1[i], nsa_ck_w2[i], nsa_pe_k[i])
        wv_big, w2v_bd, pe_v = _compress_weights(nsa_cv_w1[i], nsa_cv_w2[i], nsa_pe_v[i])
        pe4 = jnp.concatenate(pe_k + pe_v, axis=0)
        kc, vct = _compress_call(kc_raw.reshape(b, s, KV_WIDTH), vc_raw.reshape(b, s, KV_WIDTH), pe4, wk_big, wv_big, w2k_bd, w2v_bd,
                                 tile(nsa_kc_gain[i], NSA_GROUPS)[None, :], pos_end,
                                 invf128, bd128)

        r3 = lambda a: a.reshape(b, s, a.shape[-1])
        y_nsa = _nsa_call(qn, kc, vct, r3(ks), vs.reshape(b, s // NSA_TK, KV_WIDTH, NSA_TK), r3(kw),
                          vw.reshape(b, s // WIN_TK, KV_WIDTH, WIN_TK), gn, ovl_t)
        y_moba = _moba_call(qm, r3(km), vm.reshape(b, s // MOBA_BLOCK, Q_WIDTH, MOBA_BLOCK))

        w_un = (w_up_nsa[i].reshape(NSA_GROUPS, NSA_REP, HEAD_DIM, d).transpose(1, 0, 2, 3)
                .reshape(NSA_HEADS * HEAD_DIM, d).astype(BF16))
        x1 = _merge_call(xi, y_nsa, y_moba, gmix, w_gate, w_un,
                         w_up_moba[i].astype(BF16), w_out[i].astype(BF16))
        xi = _ffn_call(x1, p[i].reshape(t, PLE_DIM), g_ffn[i][None, :], w_ffn_in[i].astype(BF16),
                       w_ffn_out[i].astype(BF16), g_ple[i][None, :], w_ple_gate[i].astype(BF16),
                       w_ple_proj[i].astype(BF16))
    return xi.reshape(b, s, d)
```

```python
import jax
import jax.numpy as jnp
import numpy as np
from jax import lax
from jax.experimental import pallas as pl
from jax.experimental.pallas import tpu as pltpu

F32 = jnp.float32
BF16 = jnp.bfloat16

D_MODEL = 1024
HEAD_DIM = 64
ROPE_DIM = HEAD_DIM // 4
ROPE_HALF = ROPE_DIM // 2
ROPE_THETA = 500000.0
NORM_EPS = 1e-6
ATTN_SCALE = HEAD_DIM ** -0.5

NSA_HEADS = 8
NSA_GROUPS = 2
NSA_REP = NSA_HEADS // NSA_GROUPS
CMP_LEN = 32
CMP_STRIDE = 16
CMP_HIDDEN = 256
SEL_BLOCK = 64
SEL_TOPN = 8
WINDOW = 512

MOBA_HEADS = 8
MOBA_BLOCK = 256
MOBA_TOPK = 3

PAD_MULT = 256
D_FF = ((-(-8 * D_MODEL // 3)) + 255) // 256 * 256
PLE_DIM = 256

IN_SPLITS = ((NSA_HEADS * HEAD_DIM,) + (NSA_GROUPS * HEAD_DIM,) * 6 + (3 * NSA_HEADS,)
             + (MOBA_HEADS * HEAD_DIM,) * 3 + (D_MODEL, D_MODEL))
IN_CUTS = tuple(int(c) for c in np.cumsum(IN_SPLITS)[:-1])

LANES = 128
SUBLANES = 8
NEG_BIG = -1e30
LOG2E = 1.4426950408889634
SUM_ROWS = 16
VMEM_LIMIT = 56 * 1024 * 1024

Q_WIDTH = NSA_HEADS * HEAD_DIM
KV_WIDTH = NSA_GROUPS * HEAD_DIM
GATE_ROWS = 32
KEY_ROWS = Q_WIDTH + 2 * KV_WIDTH
FEAT_ROWS = 2 * Q_WIDTH + KEY_ROWS + Q_WIDTH + 2 * KV_WIDTH + GATE_ROWS

PROJ_TM = 512
V_TILE = 256
NSA_TQ = 256
NSA_TK = V_TILE
WIN_TK = V_TILE
SCORE_LANES = 256
POST_TM = 512
MXU_TILE = 256
FFN_CHUNKS = ((0, 6 * MXU_TILE), (6 * MXU_TILE, D_FF))
SEL_BLOCKS_MAX = 32
MOBA_BLOCKS_MAX = 8
MOBA_STEP_HEADS = 8
OWN_SUB = 64


def _dot(a, b):
    return jnp.dot(a, b, preferred_element_type=F32)


def _dot_nt(a, b):
    return lax.dot_general(a, b, (((1,), (1,)), ((), ())), preferred_element_type=F32)


def _dot_tn(a, b):
    return lax.dot_general(a, b, (((0,), (0,)), ((), ())), preferred_element_type=F32)


def _split_bf16(a_f32):
    hi = a_f32.astype(BF16)
    return hi, (a_f32 - hi.astype(F32)).astype(BF16)


def _sigmoid(x):
    return 1.0 / (1.0 + jnp.exp(-x))


def _rms_rows(x, g):
    return x * lax.rsqrt(jnp.mean(x * x, axis=-1, keepdims=True) + NORM_EPS) * g


def _head_norm(t, bd, gain):
    hi, lo = _split_bf16(t * t)
    ss = _dot(hi, bd) + _dot(lo, bd)
    return t * lax.rsqrt(ss * (1.0 / HEAD_DIM) + NORM_EPS) * gain


def _rope_tables(pos_col, invf):
    ang = pos_col.astype(F32) * invf
    cos_a, sin_a = jnp.cos(ang), jnp.sin(ang)
    d = lax.broadcasted_iota(jnp.int32, ang.shape, 1) & (HEAD_DIM - 1)
    s_lo = jnp.where(d < ROPE_HALF, -sin_a, 0.0)
    s_hi = jnp.where((d >= ROPE_HALF) & (d < ROPE_DIM), sin_a, 0.0)
    return cos_a, s_lo, s_hi


def _rope(y, tables):
    cos_a, s_lo, s_hi = tables
    w = y.shape[1]
    return (y * cos_a + pltpu.roll(y, w - ROPE_HALF, 1) * s_lo
            + pltpu.roll(y, ROPE_HALF, 1) * s_hi)


def _ones_rows(tk):
    r = lax.broadcasted_iota(jnp.int32, (SUM_ROWS, tk), 0)
    return jnp.where(r == 0, 1.0, 0.0).astype(BF16)


def _with_sum_row(v_rows, ones_rows):
    return jnp.concatenate([v_rows, ones_rows], axis=0)


def _slot_update(load_s, plan, m_ref, acc_ref, hs, v_aug):
    n_sub, n_lane = len(plan), len(plan[0])
    sub = v_aug.shape[1] // n_sub
    e_cols, alphas = [], []
    for h in range(n_lane):
        lanes = slice(h * LANES, (h + 1) * LANES)
        m = m_ref[hs, 0:1, lanes]
        m_new = m
        for a in range(n_sub):
            if plan[a][h] is None:
                continue
            kind, arg = plan[a][h]
            if kind == 'rows':
                top = jnp.max(load_s(a, h), axis=0, keepdims=True)
                m_new = jnp.maximum(m_new, top if arg is None else jnp.where(arg < 0.0, NEG_BIG, top))
            else:
                m_new = jnp.maximum(m_new, jnp.max(load_s(a, h) + arg, axis=0, keepdims=True))
        e = []
        for a in range(n_sub):
            if plan[a][h] is None:
                e.append(jnp.zeros((sub, LANES), BF16))
                continue
            kind, arg = plan[a][h]
            if kind == 'rows':
                shift = -m_new if arg is None else jnp.where(arg < 0.0, NEG_BIG, -m_new)
                e.append(jnp.exp2(load_s(a, h) + shift).astype(BF16))
            else:
                e.append(jnp.exp2(load_s(a, h) + arg - m_new).astype(BF16))
        e_cols.append(e[0] if n_sub == 1 else jnp.concatenate(e, axis=0))
        alphas.append(jnp.exp2(m - m_new))
        m_ref[hs, 0:1, lanes] = m_new
    e_all = e_cols[0] if n_lane == 1 else jnp.concatenate(e_cols, axis=1)
    alpha = alphas[0] if n_lane == 1 else jnp.concatenate(alphas, axis=1)
    acc_ref[hs] = alpha * acc_ref[hs] + _dot(v_aug, e_all)


def _visibility_plan(n_sub, sub, n_lane, visible, kept_row=None):
    plan = []
    for a in range(n_sub):
        row_plan = []
        for h in range(n_lane):
            k0, k1, q0, q1 = a * sub, (a + 1) * sub - 1, h * LANES, (h + 1) * LANES - 1
            corners = [visible(k, q) for k in (k0, k1) for q in (q0, q1)]
            row = None if kept_row is None else kept_row(a, h)
            if not any(corners):
                row_plan.append(None)
            elif all(corners):
                row_plan.append(('rows', row))
            else:
                k_off = k0 + lax.broadcasted_iota(jnp.int32, (sub, LANES), 0)
                q_off = q0 + lax.broadcasted_iota(jnp.int32, (sub, LANES), 1)
                keep = 0.0 if row is None else row
                row_plan.append(('tile', jnp.where(visible(k_off, q_off), keep, NEG_BIG)))
        plan.append(row_plan)
    return plan


def _reset_state(m_ref, acc_ref):
    m_ref[...] = jnp.full(m_ref.shape, NEG_BIG, F32)
    acc_ref[...] = jnp.zeros(acc_ref.shape, F32)


def _normalised(acc, scale_row=1.0):
    return acc[:HEAD_DIM] * (scale_row / jnp.maximum(acc[HEAD_DIM:HEAD_DIM + 1], 1e-30))


def _rank_rows(v, n_rows):
    n, q = v.shape
    slabs = [v[a:a + SUBLANES] for a in range(0, n, SUBLANES)]
    ranks = [jnp.zeros((SUBLANES, q), F32) for _ in slabs]
    sub = lax.broadcasted_iota(jnp.int32, (SUBLANES, q), 0)
    for jp in range(n_rows):
        row = v[jp:jp + 1, :]
        for si, slab in enumerate(slabs):
            first = si * SUBLANES
            if first > jp:
                beats = jnp.where(row >= slab, 1.0, 0.0)
            elif first + SUBLANES - 1 < jp:
                beats = jnp.where(row > slab, 1.0, 0.0)
            else:
                ge = jnp.where(row >= slab, 1.0, 0.0)
                gt = jnp.where(row > slab, 1.0, 0.0)
                beats = gt + (ge - gt) * jnp.where(sub > jp - first, 1.0, 0.0)
            ranks[si] = ranks[si] + beats
    return ranks[0] if len(ranks) == 1 else jnp.concatenate(ranks, axis=0)


def _proj_kernel(x_ref, posr_ref, gmix_ref, wtok_ref, wfeat_ref, gfeat_ref, invf8_ref,
                 qn_ref, qm_ref, gn_ref, vm_ref, vs_ref, vw_ref, km_ref, ks_ref, kw_ref,
                 kc_ref, vc_ref):
    tm = x_ref.shape[0]
    h = _rms_rows(x_ref[...], gmix_ref[...]).astype(BF16)

    acc = _dot(h, wtok_ref[...])
    kc_ref[...] = acc[:, :KV_WIDTH]
    vc_ref[...] = acc[:, KV_WIDTH:]

    acc_t = _dot_nt(wfeat_ref[...], h)
    ang = invf8_ref[...] * posr_ref[0].astype(F32)
    cos_a, sin_a = jnp.cos(ang), jnp.sin(ang)

    def normed_head(hh):
        t = acc_t[hh * HEAD_DIM:(hh + 1) * HEAD_DIM, :]
        ss = jnp.sum(t * t, axis=0, keepdims=True)
        y = t * lax.rsqrt(ss * (1.0 / HEAD_DIM) + NORM_EPS) * gfeat_ref[hh * HEAD_DIM:(hh + 1) * HEAD_DIM, :]
        a, b = y[:ROPE_HALF], y[ROPE_HALF:ROPE_DIM]
        return jnp.concatenate([a * cos_a - b * sin_a, b * cos_a + a * sin_a, y[ROPE_DIM:]], axis=0)

    for hh in range(2 * NSA_HEADS):
        dst = qn_ref if hh < NSA_HEADS else qm_ref
        r0 = (hh % NSA_HEADS) * HEAD_DIM
        dst[0, r0:r0 + HEAD_DIM, :] = normed_head(hh).astype(BF16)
    k0 = 2 * NSA_HEADS
    for pair in range(KEY_ROWS // LANES):
        y = jnp.concatenate([normed_head(k0 + 2 * pair), normed_head(k0 + 2 * pair + 1)], axis=0)
        y = y.T.astype(BF16)
        if pair < Q_WIDTH // LANES:
            km_ref[:, pair * LANES:(pair + 1) * LANES] = y
        elif pair == Q_WIDTH // LANES:
            ks_ref[...] = y
        else:
            kw_ref[...] = y
    o = 2 * Q_WIDTH + KEY_ROWS
    for j in range(tm // V_TILE):
        cols = slice(j * V_TILE, (j + 1) * V_TILE)
        vm_ref[j] = acc_t[o:o + Q_WIDTH, cols].astype(BF16)
        vs_ref[j] = acc_t[o + Q_WIDTH:o + Q_WIDTH + KV_WIDTH, cols].astype(BF16)
        vw_ref[j] = acc_t[o + Q_WIDTH + KV_WIDTH:o + Q_WIDTH + 2 * KV_WIDTH, cols].astype(BF16)
    gn_ref[0] = acc_t[o + Q_WIDTH + 2 * KV_WIDTH:]


def _proj_call(x2, posr, gmix, wtok, wfeat, gfeat, invf8, b, s):
    t = x2.shape[0]
    tm = PROJ_TM
    nt = s // tm
    row = lambda w_: pl.BlockSpec((tm, w_), lambda i: (i, 0))
    full = lambda a: pl.BlockSpec(a.shape, lambda i: (0,) * a.ndim, pipeline_mode=pl.Buffered(1))
    feat = lambda r: pl.BlockSpec((1, r, tm), lambda i: (i // nt, 0, i % nt))
    tile = lambda r: pl.BlockSpec((tm // V_TILE, r, V_TILE), lambda i: (i, 0, 0))
    out_shape = [
        jax.ShapeDtypeStruct((b, Q_WIDTH, s), BF16),
        jax.ShapeDtypeStruct((b, Q_WIDTH, s), BF16),
        jax.ShapeDtypeStruct((b, GATE_ROWS, s), F32),
        jax.ShapeDtypeStruct((t // V_TILE, Q_WIDTH, V_TILE), BF16),
        jax.ShapeDtypeStruct((t // V_TILE, KV_WIDTH, V_TILE), BF16),
        jax.ShapeDtypeStruct((t // V_TILE, KV_WIDTH, V_TILE), BF16),
        jax.ShapeDtypeStruct((t, Q_WIDTH), BF16),
        jax.ShapeDtypeStruct((t, KV_WIDTH), BF16),
        jax.ShapeDtypeStruct((t, KV_WIDTH), BF16),
        jax.ShapeDtypeStruct((t, KV_WIDTH), F32),
        jax.ShapeDtypeStruct((t, KV_WIDTH), F32),
    ]
    out_specs = [feat(Q_WIDTH), feat(Q_WIDTH), feat(GATE_ROWS), tile(Q_WIDTH), tile(KV_WIDTH),
                 tile(KV_WIDTH), row(Q_WIDTH), row(KV_WIDTH), row(KV_WIDTH), row(KV_WIDTH), row(KV_WIDTH)]
    return pl.pallas_call(
        _proj_kernel,
        grid=(t // tm,),
        in_specs=[row(D_MODEL), pl.BlockSpec((1, 1, tm), lambda i: (i, 0, 0)), full(gmix),
                  full(wtok), full(wfeat), full(gfeat), full(invf8)],
        out_specs=out_specs,
        out_shape=out_shape,
        compiler_params=pltpu.CompilerParams(dimension_semantics=("parallel",),
                                             vmem_limit_bytes=VMEM_LIMIT),
    )(x2, posr, gmix, wtok, wfeat, gfeat, invf8)


def _compress_kernel(kcr_ref, vcr_ref, pe_ref, wk_ref, wv_ref, w2k_ref, w2v_ref, gain_ref,
                     pose_ref, invf_ref, bd_ref, kc_out, vct_out):
    half = NSA_GROUPS * CMP_HIDDEN
    n_sub = kcr_ref.shape[1] // CMP_STRIDE

    def sub_blocks(ref):
        return jnp.concatenate([ref[0, pl.ds(i, n_sub, stride=CMP_STRIDE), :] for i in range(CMP_STRIDE)],
                               axis=1)

    def comp(x, pe_a, pe_b, w_ref, w2_ref):
        a = _dot((x + pe_a).astype(BF16), w_ref[:, :half])
        b = _dot((x + pe_b).astype(BF16), w_ref[:, half:])
        hid = a + pltpu.roll(b, b.shape[0] - 1, 0)
        act = hid * _sigmoid(hid)
        return _dot(act.astype(BF16), w2_ref[...])

    kc = comp(sub_blocks(kcr_ref), pe_ref[0:1, :], pe_ref[1:2, :], wk_ref, w2k_ref)
    vc = comp(sub_blocks(vcr_ref), pe_ref[2:3, :], pe_ref[3:4, :], wv_ref, w2v_ref)
    kc = _head_norm(kc, bd_ref[...], gain_ref[...])
    kc = _rope(kc, _rope_tables(pose_ref[0], invf_ref[...]))
    kc_out[0] = kc.astype(BF16)
    vct_out[0] = vc.T.astype(BF16)


def _compress_call(kc_raw, vc_raw, pe4, wk, wv, w2k, w2v, gain, pos_end, invf, bd):
    b, s, width = kc_raw.shape
    n_sub = s // CMP_STRIDE
    blk = lambda shp: pl.BlockSpec((1,) + shp, lambda i: (i, 0, 0))
    full = lambda a: pl.BlockSpec(a.shape, lambda i: (0,) * a.ndim)
    return pl.pallas_call(
        _compress_kernel,
        grid=(b,),
        in_specs=[blk((s, width)), blk((s, width)), full(pe4), full(wk), full(wv),
                  full(w2k), full(w2v), full(gain), blk((n_sub, 1)), full(invf), full(bd)],
        out_specs=[blk((n_sub, KV_WIDTH)), blk((KV_WIDTH, n_sub))],
        out_shape=[jax.ShapeDtypeStruct((b, n_sub, KV_WIDTH), BF16),
                   jax.ShapeDtypeStruct((b, KV_WIDTH, n_sub), BF16)],
        compiler_params=pltpu.CompilerParams(dimension_semantics=("parallel",),
                                             vmem_limit_bytes=VMEM_LIMIT),
    )(kc_raw, vc_raw, pe4, wk, wv, w2k, w2v, gain, pos_end, invf, bd)


def _nsa_kernel(q_ref, kc_ref, vct_ref, ks_ref, vs_ref, kw_ref, vw_ref, gate_ref, ovl_ref, out_ref,
                sel_scr, lhs_scr, m_scr, acc_scr, o_scr, *bufs):
    tq = q_ref.shape[2]
    n_cmp = kc_ref.shape[1]
    n_slots = NSA_GROUPS * NSA_REP
    n_units = len(bufs) // 2
    s_scr, c_scr = bufs[:n_units], bufs[n_units:]
    per_unit = n_slots // n_units
    i = pl.program_id(1)
    t0 = i * tq
    t_row = t0 + lax.broadcasted_iota(jnp.int32, (1, tq), 1)

    zero = jnp.zeros((HEAD_DIM, tq), BF16)
    for g in range(NSA_GROUPS):
        for r in range(NSA_REP):
            hs = g * NSA_REP + r
            rows = q_ref[0, r * KV_WIDTH + g * HEAD_DIM:r * KV_WIDTH + (g + 1) * HEAD_DIM, :]
            lhs_scr[:, hs * tq:(hs + 1) * tq] = jnp.concatenate([rows, zero] if g == 0 else [zero, rows],
                                                                axis=0)
    gsig = _sigmoid(gate_ref[0])

    def unit_queries(unit):
        return lhs_scr[:, unit * SCORE_LANES:(unit + 1) * SCORE_LANES]

    def scores_into(unit, k_tile):
        tk = k_tile.shape[0]
        s_scr[unit][0:tk, :] = _dot(k_tile, unit_queries(unit))

    def sel_keys(kt):
        return ks_ref[0, pl.ds(pl.multiple_of(kt * NSA_TK, NSA_TK), NSA_TK), :]

    def win_keys(kt):
        return kw_ref[0, pl.ds(pl.multiple_of(kt * WIN_TK, WIN_TK), WIN_TK), :]

    for unit in range(n_units):
        c_scr[unit][...] = _dot(kc_ref[0], unit_queries(unit))
    for unit in range(n_units):
        scores_into(unit, sel_keys(0))

    c_idx = lax.broadcasted_iota(jnp.int32, (n_cmp, tq), 0)
    valid_c = (c_idx * CMP_STRIDE + (CMP_LEN - 1) <= t_row) & (c_idx < n_cmp - 1)
    bias_c = jnp.where(valid_c, 0.0, NEG_BIG)
    p_sum = []
    for hs in range(n_slots):
        g = hs // NSA_REP
        unit, hh = divmod(hs, per_unit)
        x = c_scr[unit][:, hh * tq:(hh + 1) * tq] + bias_c
        m = jnp.max(x, axis=0, keepdims=True)
        e = jnp.where(valid_c, jnp.exp2(x - m), 0.0)
        p = e / jnp.maximum(jnp.sum(e, axis=0, keepdims=True), 1e-30)
        o_c = _dot(vct_ref[0, g * HEAD_DIM:(g + 1) * HEAD_DIM, :], p.astype(BF16))
        o_scr[hs] = o_c * gsig[3 * hs:3 * hs + 1, :]
        if hs % NSA_REP == 0:
            p_sum.append(p)
        else:
            p_sum[g] = p_sum[g] + p

    cur = t_row // SEL_BLOCK
    j_idx = lax.broadcasted_iota(jnp.int32, (SEL_BLOCKS_MAX, tq), 0)
    forced = (j_idx == 0) | (j_idx == cur) | (j_idx == cur - 1)
    for g in range(NSA_GROUPS):
        hi, lo = _split_bf16(p_sum[g])
        imp = _dot(ovl_ref[...], hi) + _dot(ovl_ref[...], lo)
        imp = jnp.where(forced, jnp.inf, jnp.where(j_idx > cur, -jnp.inf, imp))
        sel_scr[g] = jnp.where(_rank_rows(imp, SEL_BLOCKS_MAX) < SEL_TOPN, 0.0, NEG_BIG)

    n_lane = tq // LANES

    def run_tile(v_tile, plan_of_group, next_keys):
        ones_rows = _ones_rows(v_tile.shape[1])
        plans = [plan_of_group(g) for g in range(NSA_GROUPS)]
        for unit in range(n_units):
            g = (unit * per_unit) // NSA_REP
            sub = v_tile.shape[1] // len(plans[g])
            v_aug = _with_sum_row(v_tile[g * HEAD_DIM:(g + 1) * HEAD_DIM, :], ones_rows)
            for hh in range(per_unit):
                def load(a, h, hh=hh, unit=unit, sub=sub):
                    return s_scr[unit][a * sub:(a + 1) * sub, hh * tq + h * LANES:hh * tq + (h + 1) * LANES]

                _slot_update(load, plans[g], m_scr, acc_scr, unit * per_unit + hh, v_aug)
            if next_keys is not None:
                scores_into(unit, next_keys())

    def fold_branch(branch):
        for hs in range(n_slots):
            o_scr[hs] = o_scr[hs] + _normalised(acc_scr[hs], gsig[3 * hs + branch:3 * hs + branch + 1, :])

    blocks_per_tile = NSA_TK // SEL_BLOCK
    first_win = jnp.maximum((t0 - (WINDOW - 1)) // WIN_TK, 0)

    def sel_row(kt, g, a, h):
        return sel_scr[g, pl.ds(kt * blocks_per_tile + a, 1), :][:, h * LANES:(h + 1) * LANES]

    def sel_plan(kt, g):
        return [[('rows', sel_row(kt, g, a, h)) for h in range(n_lane)] for a in range(blocks_per_tile)]

    def causal(k_off, q_off):
        return k_off <= q_off

    def sel_body(kt, carry):
        run_tile(vs_ref[0, kt], lambda g: sel_plan(kt, g), lambda: sel_keys(kt + 1))
        return carry

    last_sel = (t0 + tq - 1) // NSA_TK
    _reset_state(m_scr, acc_scr)
    lax.fori_loop(0, last_sel, sel_body, 0)
    run_tile(vs_ref[0, last_sel],
             lambda g: _visibility_plan(blocks_per_tile, SEL_BLOCK, n_lane, causal,
                                        lambda a, h: sel_row(last_sel, g, a, h)),
             lambda: win_keys(first_win))
    fold_branch(1)

    def win_tile(kt, visible, next_keys):
        run_tile(vw_ref[0, kt], lambda g: _visibility_plan(blocks_per_tile, SEL_BLOCK, n_lane, visible),
                 next_keys)

    _reset_state(m_scr, acc_scr)

    @pl.when(i >= 2)
    def _():
        win_tile(i - 2, lambda k_off, q_off: q_off < k_off, lambda: win_keys(i - 1))

    @pl.when(i >= 1)
    def _():
        win_tile(i - 1, lambda k_off, q_off: True, lambda: win_keys(i))

    win_tile(i, causal, None)
    fold_branch(2)

    for r in range(NSA_REP):
        out_ref[0, r * KV_WIDTH:r * KV_WIDTH + HEAD_DIM, :] = o_scr[r].astype(BF16)
        out_ref[0, r * KV_WIDTH + HEAD_DIM:(r + 1) * KV_WIDTH, :] = o_scr[NSA_REP + r].astype(BF16)


def _nsa_call(qn, kc, vct, ks, vs, kw, vw, gn, ovl):
    b, _, s = qn.shape
    tq = NSA_TQ
    n_slots = NSA_GROUPS * NSA_REP
    n_units = n_slots * tq // SCORE_LANES
    assert s // SEL_BLOCK <= SEL_BLOCKS_MAX and tq == WIN_TK == NSA_TK and WINDOW == 2 * WIN_TK
    seq = lambda a: pl.BlockSpec((1,) + a.shape[1:], lambda bi, i: (bi,) + (0,) * (a.ndim - 1))
    full = lambda a: pl.BlockSpec(a.shape, lambda bi, i: (0,) * a.ndim)
    return pl.pallas_call(
        _nsa_kernel,
        grid=(b, s // tq),
        in_specs=[pl.BlockSpec((1, Q_WIDTH, tq), lambda bi, i: (bi, 0, i)),
                  seq(kc), seq(vct), seq(ks), seq(vs), seq(kw), seq(vw),
                  pl.BlockSpec((1, GATE_ROWS, tq), lambda bi, i: (bi, 0, i)), full(ovl)],
        out_specs=pl.BlockSpec((1, Q_WIDTH, tq), lambda bi, i: (bi, 0, i)),
        out_shape=jax.ShapeDtypeStruct((b, Q_WIDTH, s), BF16),
        scratch_shapes=[pltpu.VMEM((NSA_GROUPS, SEL_BLOCKS_MAX, tq), F32),
                        pltpu.VMEM((KV_WIDTH, n_slots * tq), BF16),
                        pltpu.VMEM((n_slots, SUBLANES, tq), F32),
                        pltpu.VMEM((n_slots, HEAD_DIM + SUM_ROWS, tq), F32),
                        pltpu.VMEM((n_slots, HEAD_DIM, tq), F32)]
        + [pltpu.VMEM((max(NSA_TK, WIN_TK), SCORE_LANES), F32)] * n_units
        + [pltpu.VMEM((kc.shape[1], SCORE_LANES), F32)] * n_units,
        compiler_params=pltpu.CompilerParams(dimension_semantics=("parallel", "parallel"),
                                             vmem_limit_bytes=VMEM_LIMIT),
    )(qn, kc, vct, ks, vs, kw, vw, gn, ovl)


def _moba_kernel(q_ref, k_ref, v_ref, out_ref, km_scr, sel_scr, lhs_scr, m_scr, acc_scr, *s_scr):
    bs = MOBA_BLOCK
    n_blocks = k_ref.shape[1] // bs
    n_slots = len(s_scr)
    i = pl.program_id(2)

    @pl.when(i == 0)
    def _():
        km_scr[...] = jnp.zeros(km_scr.shape, F32)
        for n in range(n_blocks):
            km_scr[n:n + 1, :] = jnp.mean(k_ref[0, n * bs:(n + 1) * bs, :].astype(F32), axis=0,
                                          keepdims=True)

    def pair_lanes(hs):
        return slice((hs // 2) * LANES, (hs // 2 + 1) * LANES)

    zero = jnp.zeros((HEAD_DIM, bs), BF16)
    n_idx = lax.broadcasted_iota(jnp.int32, (MOBA_BLOCKS_MAX, bs), 0)
    for hs in range(n_slots):
        rows = q_ref[0, hs * HEAD_DIM:(hs + 1) * HEAD_DIM, :]
        lhs_scr[hs] = jnp.concatenate([rows, zero] if hs % 2 == 0 else [zero, rows], axis=0)

    def scores_into(hs, n):
        k_tile = k_ref[0, pl.ds(pl.multiple_of(n * bs, bs), bs), pair_lanes(hs)]
        s_scr[hs][...] = _dot(k_tile, lhs_scr[hs])

    block_scores = []
    for hs in range(n_slots):
        km_hi, km_lo = _split_bf16(km_scr[:, pair_lanes(hs)])
        block_scores.append(_dot(km_hi, lhs_scr[hs]) + _dot(km_lo, lhs_scr[hs]))
    for hs in range(n_slots):
        scores_into(hs, 0)

    for hs in range(n_slots):
        sc = jnp.where(n_idx < i, block_scores[hs], -jnp.inf)
        sel_scr[hs] = jnp.where((_rank_rows(sc, MOBA_BLOCKS_MAX) < MOBA_TOPK) & (n_idx < i), 0.0, NEG_BIG)

    ones_rows = _ones_rows(bs)

    n_lane = bs // LANES

    def run_block(n, plan_of_slot, prefetch):
        for hs in range(n_slots):
            plan = plan_of_slot(hs)
            sub = bs // len(plan)
            v_aug = _with_sum_row(v_ref[0, n, hs * HEAD_DIM:(hs + 1) * HEAD_DIM, :], ones_rows)

            def load(a, h, hs=hs, sub=sub):
                return s_scr[hs][a * sub:(a + 1) * sub, h * LANES:(h + 1) * LANES]

            _slot_update(load, plan, m_scr, acc_scr, hs, v_aug)
            if prefetch:
                scores_into(hs, n + 1)

    def past_body(n, carry):
        run_block(n, lambda hs: [[('rows', sel_scr[hs, pl.ds(n, 1), :][:, h * LANES:(h + 1) * LANES])
                                  for h in range(n_lane)]], True)
        return carry

    _reset_state(m_scr, acc_scr)
    lax.fori_loop(0, i, past_body, 0)
    own_plan = _visibility_plan(bs // OWN_SUB, OWN_SUB, n_lane, lambda k_off, q_off: k_off <= q_off)
    run_block(i, lambda hs: own_plan, False)
    for hs in range(n_slots):
        out_ref[0, hs * HEAD_DIM:(hs + 1) * HEAD_DIM, :] = _normalised(acc_scr[hs]).astype(BF16)


def _moba_call(qm, km, vm):
    b, w, s = qm.shape
    bs = MOBA_BLOCK
    nb = s // bs
    wb = MOBA_STEP_HEADS * HEAD_DIM
    assert nb <= MOBA_BLOCKS_MAX and w % wb == 0
    return pl.pallas_call(
        _moba_kernel,
        grid=(b, w // wb, nb),
        in_specs=[pl.BlockSpec((1, wb, bs), lambda bi, p, i: (bi, p, i)),
                  pl.BlockSpec((1, s, wb), lambda bi, p, i: (bi, 0, p)),
                  pl.BlockSpec((1, nb, wb, bs), lambda bi, p, i: (bi, 0, p, 0))],
        out_specs=pl.BlockSpec((1, wb, bs), lambda bi, p, i: (bi, p, i)),
        out_shape=jax.ShapeDtypeStruct((b, w, s), BF16),
        scratch_shapes=[pltpu.VMEM((MOBA_BLOCKS_MAX, wb), F32),
                        pltpu.VMEM((MOBA_STEP_HEADS, MOBA_BLOCKS_MAX, bs), F32),
                        pltpu.VMEM((MOBA_STEP_HEADS, LANES, bs), BF16),
                        pltpu.VMEM((MOBA_STEP_HEADS, SUBLANES, bs), F32),
                        pltpu.VMEM((MOBA_STEP_HEADS, HEAD_DIM + SUM_ROWS, bs), F32)]
        + [pltpu.VMEM((bs, bs), F32)] * MOBA_STEP_HEADS,
        compiler_params=pltpu.CompilerParams(
            dimension_semantics=("parallel", "parallel", "arbitrary"), vmem_limit_bytes=VMEM_LIMIT),
    )(qm, km, vm)


def _post_kernel(x_ref, yn_ref, ym_ref, p_ref, gmix_ref, wg_ref, wun_ref, wum_ref, wo_ref,
                 gffn_ref, wfi_ref, wfo_ref, gple_ref, wpg_ref, wpp_ref, out_ref):
    x = x_ref[...]
    h = _rms_rows(x, gmix_ref[...]).astype(BF16)
    ga = _sigmoid(_dot(h, wg_ref[:, :D_MODEL]))
    gb = _sigmoid(_dot(h, wg_ref[:, D_MODEL:]))
    merged = ga * _dot_tn(yn_ref[0], wun_ref[...]) + gb * _dot_tn(ym_ref[0], wum_ref[...])
    x = x + _dot(merged.astype(BF16), wo_ref[...])

    h = _rms_rows(x, gffn_ref[...]).astype(BF16)
    y = x
    for lo, hi in FFN_CHUNKS:
        gate = _dot(h, wfi_ref[:, lo:hi])
        up = _dot(h, wfi_ref[:, D_FF + lo:D_FF + hi])
        act = (gate * _sigmoid(gate) * up).astype(BF16)
        y = y + _dot(act, wfo_ref[lo:hi, :])

    h2 = _rms_rows(y, gple_ref[...]).astype(BF16)
    ple_gate = _sigmoid(_dot(h2, wpg_ref[...]))
    out_ref[...] = y + ple_gate * _dot(p_ref[...].astype(BF16), wpp_ref[...])


def _post_call(x2, yn, ym, p2, gmix, wg, wun, wum, wo, gffn, wfi, wfo, gple, wpg, wpp):
    t = x2.shape[0]
    tm = POST_TM
    nt = yn.shape[2] // tm
    row = lambda w_: pl.BlockSpec((tm, w_), lambda i: (i, 0))
    feat = pl.BlockSpec((1, Q_WIDTH, tm), lambda i: (i // nt, 0, i % nt))
    full = lambda a: pl.BlockSpec(a.shape, lambda i: (0, 0), pipeline_mode=pl.Buffered(1))
    weights = (gmix, wg, wun, wum, wo, gffn, wfi, wfo, gple, wpg, wpp)
    return pl.pallas_call(
        _post_kernel,
        grid=(t // tm,),
        in_specs=[row(D_MODEL), feat, feat, row(PLE_DIM)] + [full(w) for w in weights],
        out_specs=row(D_MODEL),
        out_shape=jax.ShapeDtypeStruct((t, D_MODEL), F32),
        compiler_params=pltpu.CompilerParams(dimension_semantics=("parallel",),
                                             vmem_limit_bytes=VMEM_LIMIT),
    )(x2, yn, ym, p2, *weights)


def _block_diag_ones(width):
    idx = np.arange(width) // HEAD_DIM
    return jnp.asarray(idx[:, None] == idx[None, :], dtype=BF16)


def _inv_freq():
    return ROPE_THETA ** (-jnp.arange(ROPE_HALF, dtype=F32) / ROPE_HALF)


def _inv_freq_lanes(width):
    inv_freq = _inv_freq()
    per_head = jnp.concatenate([inv_freq, inv_freq, jnp.zeros((HEAD_DIM - ROPE_DIM,), F32)])
    return jnp.tile(per_head, width // HEAD_DIM)[None, :]


def _split_w_in(w):
    parts = jnp.split(w, IN_CUTS, axis=-1)
    (q_n, kc, vc, ks, vs, kw, vw, gate_n, q_m, k_m, v_m, gate_a, gate_b) = parts
    d = w.shape[0]
    q_n = q_n.reshape(d, NSA_GROUPS, NSA_REP, HEAD_DIM).transpose(0, 2, 1, 3).reshape(d, -1)
    gate_n = jnp.pad(gate_n, ((0, 0), (0, GATE_ROWS - gate_n.shape[1])))
    w_tok = jnp.concatenate([kc, vc], axis=1)
    w_feat = jnp.concatenate([q_n, q_m, k_m, ks, kw, v_m, vs, vw, gate_n], axis=1).T
    w_gate = jnp.concatenate([gate_a, gate_b], axis=1)
    return w_tok.astype(BF16), w_feat.astype(BF16), w_gate.astype(BF16)


def _compress_weights(w1, w2, pe):
    eye = jnp.eye(NSA_GROUPS, dtype=F32)
    halves = []
    for part in (w1[:CMP_STRIDE * HEAD_DIM], w1[CMP_STRIDE * HEAD_DIM:]):
        p3 = part.reshape(CMP_STRIDE, HEAD_DIM, CMP_HIDDEN)
        halves.append(jnp.einsum('idh,gk->igdkh', p3, eye)
                      .reshape(CMP_STRIDE * NSA_GROUPS * HEAD_DIM, NSA_GROUPS * CMP_HIDDEN))
    w_big = jnp.concatenate(halves, axis=1).astype(BF16)
    w2_bd = jnp.einsum('hd,kg->khgd', w2, eye).reshape(NSA_GROUPS * CMP_HIDDEN,
                                                        NSA_GROUPS * HEAD_DIM).astype(BF16)
    pe_rows = [jnp.broadcast_to(pe[a:a + CMP_STRIDE, None, :], (CMP_STRIDE, NSA_GROUPS, HEAD_DIM))
               .reshape(1, -1) for a in (0, CMP_STRIDE)]
    return w_big, w2_bd, pe_rows


def _overlap_matrix_t(n_cmp):
    c = np.arange(n_cmp)
    j = np.arange(SEL_BLOCKS_MAX)
    start, end = c * CMP_STRIDE, c * CMP_STRIDE + CMP_LEN - 1
    ov = (start[None, :] <= j[:, None] * SEL_BLOCK + SEL_BLOCK - 1) & (end[None, :] >= j[:, None] * SEL_BLOCK)
    return jnp.asarray(ov, dtype=BF16)


def kernel(x, p, positions, g_mix, w_in, nsa_q_gain, nsa_kc_gain, nsa_ks_gain, nsa_kw_gain, nsa_pe_k, nsa_pe_v, nsa_ck_w1, nsa_ck_w2, nsa_cv_w1, nsa_cv_w2, moba_q_gain, moba_k_gain, w_up_nsa, w_up_moba, w_out, g_ffn, w_ffn_in, w_ffn_out, g_ple, w_ple_gate, w_ple_proj):
    b, s, d = x.shape
    depth = w_in.shape[0]
    assert s % PAD_MULT == 0 and s % PROJ_TM == 0 and d == D_MODEL and V_TILE == MOBA_BLOCK
    t = b * s
    n_sub = s // CMP_STRIDE

    assert ROPE_HALF == SUBLANES
    posr = positions.reshape(t // PROJ_TM, 1, PROJ_TM)
    pos_end = jnp.concatenate([positions[:, CMP_LEN - 1::CMP_STRIDE], positions[:, -1:]], axis=1)[:, :, None]
    invf128 = _inv_freq_lanes(LANES)
    invf8 = _inv_freq()[:, None]
    bd128 = _block_diag_ones(LANES)
    ovl_t = _overlap_matrix_t(n_sub)
    tile = lambda g, n: jnp.tile(g, n)
    xi = x.reshape(t, d)

    for i in range(depth):
        w_tok, w_feat, w_gate = _split_w_in(w_in[i])
        g_feat = jnp.concatenate([
            jnp.concatenate([tile(nsa_q_gain[i], NSA_HEADS), tile(moba_q_gain[i], MOBA_HEADS)])
            * (ATTN_SCALE * LOG2E),
            tile(moba_k_gain[i], MOBA_HEADS), tile(nsa_ks_gain[i], NSA_GROUPS),
            tile(nsa_kw_gain[i], NSA_GROUPS)])[:, None]
        gmix = g_mix[i][None, :]
        (qn, qm, gn, vm, vs, vw, km, ks, kw, kc_raw, vc_raw) = _proj_call(
            xi, posr, gmix, w_tok, w_feat, g_feat, invf8, b, s)

        wk_big, w2k_bd, pe_k = _compress_weights(nsa_ck_w1[i], nsa_ck_w2[i], nsa_pe_k[i])
        wv_big, w2v_bd, pe_v = _compress_weights(nsa_cv_w1[i], nsa_cv_w2[i], nsa_pe_v[i])
        pe4 = jnp.concatenate(pe_k + pe_v, axis=0)
        kc, vct = _compress_call(kc_raw.reshape(b, s, KV_WIDTH), vc_raw.reshape(b, s, KV_WIDTH), pe4, wk_big, wv_big, w2k_bd, w2v_bd,
                                 tile(nsa_kc_gain[i], NSA_GROUPS)[None, :], pos_end,
                                 invf128, bd128)

        r3 = lambda a: a.reshape(b, s, a.shape[-1])
        y_nsa = _nsa_call(qn, kc, vct, r3(ks), vs.reshape(b, s // NSA_TK, KV_WIDTH, NSA_TK), r3(kw),
                          vw.reshape(b, s // WIN_TK, KV_WIDTH, WIN_TK), gn, ovl_t)
        y_moba = _moba_call(qm, r3(km), vm.reshape(b, s // MOBA_BLOCK, Q_WIDTH, MOBA_BLOCK))

        w_un = (w_up_nsa[i].reshape(NSA_GROUPS, NSA_REP, HEAD_DIM, d).transpose(1, 0, 2, 3)
                .reshape(NSA_HEADS * HEAD_DIM, d).astype(BF16))
        xi = _post_call(xi, y_nsa, y_moba, p[i].reshape(t, PLE_DIM), gmix, w_gate, w_un,
                        w_up_moba[i].astype(BF16), w_out[i].astype(BF16), g_ffn[i][None, :],
                        w_ffn_in[i].astype(BF16), w_ffn_out[i].astype(BF16), g_ple[i][None, :],
                        w_ple_gate[i].astype(BF16), w_ple_proj[i].astype(BF16))
    return xi.reshape(b, s, d)
```

```python
import jax
import jax.numpy as jnp
import numpy as np
from jax import lax
from jax.experimental import pallas as pl
from jax.experimental.pallas import tpu as pltpu

F32 = jnp.float32
BF16 = jnp.bfloat16

D_MODEL = 1024
HEAD_DIM = 64
ROPE_DIM = HEAD_DIM // 4
ROPE_HALF = ROPE_DIM // 2
ROPE_THETA = 500000.0
NORM_EPS = 1e-6
ATTN_SCALE = HEAD_DIM ** -0.5

NSA_HEADS = 8
NSA_GROUPS = 2
NSA_REP = NSA_HEADS // NSA_GROUPS
CMP_LEN = 32
CMP_STRIDE = 16
CMP_HIDDEN = 256
SEL_BLOCK = 64
SEL_TOPN = 8
WINDOW = 512

MOBA_HEADS = 8
MOBA_BLOCK = 256
MOBA_TOPK = 3

PAD_MULT = 256
D_FF = ((-(-8 * D_MODEL // 3)) + 255) // 256 * 256
PLE_DIM = 256

IN_SPLITS = ((NSA_HEADS * HEAD_DIM,) + (NSA_GROUPS * HEAD_DIM,) * 6 + (3 * NSA_HEADS,)
             + (MOBA_HEADS * HEAD_DIM,) * 3 + (D_MODEL, D_MODEL))
IN_CUTS = tuple(int(c) for c in np.cumsum(IN_SPLITS)[:-1])

LANES = 128
SUBLANES = 8
NEG_BIG = -1e30
LOG2E = 1.4426950408889634
SUM_ROWS = 16
VMEM_LIMIT = 56 * 1024 * 1024

Q_WIDTH = NSA_HEADS * HEAD_DIM
KV_WIDTH = NSA_GROUPS * HEAD_DIM
GATE_ROWS = 32
KEY_ROWS = Q_WIDTH + 2 * KV_WIDTH
FEAT_ROWS = 2 * Q_WIDTH + KEY_ROWS + Q_WIDTH + 2 * KV_WIDTH + GATE_ROWS

PROJ_TM = 512
V_TILE = 256
NSA_TQ = 256
NSA_TK = V_TILE
WIN_TK = V_TILE
SCORE_LANES = 256
POST_TM = 512
MXU_TILE = 256
FFN_CHUNKS = ((0, 6 * MXU_TILE), (6 * MXU_TILE, D_FF))
SEL_BLOCKS_MAX = 32
MOBA_BLOCKS_MAX = 8
MOBA_STEP_HEADS = 8
OWN_SUB = 64


def _dot(a, b):
    return jnp.dot(a, b, preferred_element_type=F32)


def _dot_nt(a, b):
    return lax.dot_general(a, b, (((1,), (1,)), ((), ())), preferred_element_type=F32)


def _dot_tn(a, b):
    return lax.dot_general(a, b, (((0,), (0,)), ((), ())), preferred_element_type=F32)


def _split_bf16(a_f32):
    hi = a_f32.astype(BF16)
    return hi, (a_f32 - hi.astype(F32)).astype(BF16)


def _sigmoid(x):
    return 1.0 / (1.0 + jnp.exp(-x))


def _rms_rows(x, g):
    return x * lax.rsqrt(jnp.mean(x * x, axis=-1, keepdims=True) + NORM_EPS) * g


def _head_norm(t, bd, gain):
    hi, lo = _split_bf16(t * t)
    ss = _dot(hi, bd) + _dot(lo, bd)
    return t * lax.rsqrt(ss * (1.0 / HEAD_DIM) + NORM_EPS) * gain


def _rope_tables(pos_col, invf):
    ang = pos_col.astype(F32) * invf
    cos_a, sin_a = jnp.cos(ang), jnp.sin(ang)
    d = lax.broadcasted_iota(jnp.int32, ang.shape, 1) & (HEAD_DIM - 1)
    s_lo = jnp.where(d < ROPE_HALF, -sin_a, 0.0)
    s_hi = jnp.where((d >= ROPE_HALF) & (d < ROPE_DIM), sin_a, 0.0)
    return cos_a, s_lo, s_hi


def _rope(y, tables):
    cos_a, s_lo, s_hi = tables
    w = y.shape[1]
    return (y * cos_a + pltpu.roll(y, w - ROPE_HALF, 1) * s_lo
            + pltpu.roll(y, ROPE_HALF, 1) * s_hi)


def _ones_rows(tk):
    r = lax.broadcasted_iota(jnp.int32, (SUM_ROWS, tk), 0)
    return jnp.where(r == 0, 1.0, 0.0).astype(BF16)


def _with_sum_row(v_rows, ones_rows):
    return jnp.concatenate([v_rows, ones_rows], axis=0)


def _slot_update(load_s, plan, m_ref, acc_ref, hs, v_aug):
    n_sub, n_lane = len(plan), len(plan[0])
    sub = v_aug.shape[1] // n_sub
    e_cols, alphas = [], []
    for h in range(n_lane):
        lanes = slice(h * LANES, (h + 1) * LANES)
        m = m_ref[hs, 0:1, lanes]
        m_new = m
        for a in range(n_sub):
            if plan[a][h] is None:
                continue
            kind, arg = plan[a][h]
            if kind == 'rows':
                top = jnp.max(load_s(a, h), axis=0, keepdims=True)
                m_new = jnp.maximum(m_new, top if arg is None else jnp.where(arg < 0.0, NEG_BIG, top))
            else:
                m_new = jnp.maximum(m_new, jnp.max(load_s(a, h) + arg, axis=0, keepdims=True))
        e = []
        for a in range(n_sub):
            if plan[a][h] is None:
                e.append(jnp.zeros((sub, LANES), BF16))
                continue
            kind, arg = plan[a][h]
            if kind == 'rows':
                shift = -m_new if arg is None else jnp.where(arg < 0.0, NEG_BIG, -m_new)
                e.append(jnp.exp2(load_s(a, h) + shift).astype(BF16))
            else:
                e.append(jnp.exp2(load_s(a, h) + arg - m_new).astype(BF16))
        e_cols.append(e[0] if n_sub == 1 else jnp.concatenate(e, axis=0))
        alphas.append(jnp.exp2(m - m_new))
        m_ref[hs, 0:1, lanes] = m_new
    e_all = e_cols[0] if n_lane == 1 else jnp.concatenate(e_cols, axis=1)
    alpha = alphas[0] if n_lane == 1 else jnp.concatenate(alphas, axis=1)
    acc_ref[hs] = alpha * acc_ref[hs] + _dot(v_aug, e_all)


def _visibility_plan(n_sub, sub, n_lane, visible, kept_row=None):
    plan = []
    for a in range(n_sub):
        row_plan = []
        for h in range(n_lane):
            k0, k1, q0, q1 = a * sub, (a + 1) * sub - 1, h * LANES, (h + 1) * LANES - 1
            corners = [visible(k, q) for k in (k0, k1) for q in (q0, q1)]
            row = None if kept_row is None else kept_row(a, h)
            if not any(corners):
                row_plan.append(None)
            elif all(corners):
                row_plan.append(('rows', row))
            else:
                k_off = k0 + lax.broadcasted_iota(jnp.int32, (sub, LANES), 0)
                q_off = q0 + lax.broadcasted_iota(jnp.int32, (sub, LANES), 1)
                keep = 0.0 if row is None else row
                row_plan.append(('tile', jnp.where(visible(k_off, q_off), keep, NEG_BIG)))
        plan.append(row_plan)
    return plan


def _reset_state(m_ref, acc_ref):
    m_ref[...] = jnp.full(m_ref.shape, NEG_BIG, F32)
    acc_ref[...] = jnp.zeros(acc_ref.shape, F32)


def _normalised(acc, scale_row=1.0):
    return acc[:HEAD_DIM] * (scale_row / jnp.maximum(acc[HEAD_DIM:HEAD_DIM + 1], 1e-30))


def _rank_rows(v, n_rows):
    n, q = v.shape
    slabs = [v[a:a + SUBLANES] for a in range(0, n, SUBLANES)]
    ranks = [jnp.zeros((SUBLANES, q), F32) for _ in slabs]
    sub = lax.broadcasted_iota(jnp.int32, (SUBLANES, q), 0)
    for jp in range(n_rows):
        row = v[jp:jp + 1, :]
        for si, slab in enumerate(slabs):
            first = si * SUBLANES
            if first > jp:
                beats = jnp.where(row >= slab, 1.0, 0.0)
            elif first + SUBLANES - 1 < jp:
                beats = jnp.where(row > slab, 1.0, 0.0)
            else:
                ge = jnp.where(row >= slab, 1.0, 0.0)
                gt = jnp.where(row > slab, 1.0, 0.0)
                beats = gt + (ge - gt) * jnp.where(sub > jp - first, 1.0, 0.0)
            ranks[si] = ranks[si] + beats
    return ranks[0] if len(ranks) == 1 else jnp.concatenate(ranks, axis=0)


def _proj_kernel(x_ref, posr_ref, gmix_ref, wtok_ref, wfeat_ref, gfeat_ref, invf8_ref,
                 qn_ref, qm_ref, gn_ref, vm_ref, vs_ref, vw_ref, km_ref, ks_ref, kw_ref,
                 kc_ref, vc_ref):
    tm = x_ref.shape[0]
    h = _rms_rows(x_ref[...], gmix_ref[...]).astype(BF16)

    acc = _dot(h, wtok_ref[...])
    kc_ref[...] = acc[:, :KV_WIDTH]
    vc_ref[...] = acc[:, KV_WIDTH:]

    ang = invf8_ref[...] * posr_ref[0].astype(F32)
    cos_a, sin_a = jnp.cos(ang), jnp.sin(ang)

    def project(lo, hi):
        return _dot_nt(wfeat_ref[lo:hi, :], h)

    def normed_head(acc_g, row0, hh):
        t = acc_g[hh * HEAD_DIM:(hh + 1) * HEAD_DIM, :]
        ss = jnp.sum(t * t, axis=0, keepdims=True)
        gain = gfeat_ref[row0 + hh * HEAD_DIM:row0 + (hh + 1) * HEAD_DIM, :]
        y = t * lax.rsqrt(ss * (1.0 / HEAD_DIM) + NORM_EPS) * gain
        a, b = y[:ROPE_HALF], y[ROPE_HALF:ROPE_DIM]
        return jnp.concatenate([a * cos_a - b * sin_a, b * cos_a + a * sin_a, y[ROPE_DIM:]], axis=0)

    q_rows = 2 * Q_WIDTH
    for mixer, dst in enumerate((qn_ref, qm_ref)):
        acc_q = project(mixer * Q_WIDTH, (mixer + 1) * Q_WIDTH)
        for hh in range(NSA_HEADS):
            dst[0, hh * HEAD_DIM:(hh + 1) * HEAD_DIM, :] = normed_head(acc_q, mixer * Q_WIDTH, hh).astype(BF16)

    acc_k = project(q_rows, q_rows + KEY_ROWS)
    for pair in range(KEY_ROWS // LANES):
        y = jnp.concatenate([normed_head(acc_k, q_rows, 2 * pair), normed_head(acc_k, q_rows, 2 * pair + 1)],
                            axis=0)
        y = y.T.astype(BF16)
        if pair < Q_WIDTH // LANES:
            km_ref[:, pair * LANES:(pair + 1) * LANES] = y
        elif pair == Q_WIDTH // LANES:
            ks_ref[...] = y
        else:
            kw_ref[...] = y

    v_rows = q_rows + KEY_ROWS
    acc_vm = project(v_rows, v_rows + Q_WIDTH)
    acc_v = project(v_rows + Q_WIDTH, FEAT_ROWS)
    for j in range(tm // V_TILE):
        cols = slice(j * V_TILE, (j + 1) * V_TILE)
        vm_ref[j] = acc_vm[:, cols].astype(BF16)
        vs_ref[j] = acc_v[:KV_WIDTH, cols].astype(BF16)
        vw_ref[j] = acc_v[KV_WIDTH:2 * KV_WIDTH, cols].astype(BF16)
    gn_ref[0] = acc_v[2 * KV_WIDTH:]


def _proj_call(x2, posr, gmix, wtok, wfeat, gfeat, invf8, b, s):
    t = x2.shape[0]
    tm = PROJ_TM
    nt = s // tm
    row = lambda w_: pl.BlockSpec((tm, w_), lambda i: (i, 0))
    full = lambda a: pl.BlockSpec(a.shape, lambda i: (0,) * a.ndim, pipeline_mode=pl.Buffered(1))
    feat = lambda r: pl.BlockSpec((1, r, tm), lambda i: (i // nt, 0, i % nt))
    tile = lambda r: pl.BlockSpec((tm // V_TILE, r, V_TILE), lambda i: (i, 0, 0))
    out_shape = [
        jax.ShapeDtypeStruct((b, Q_WIDTH, s), BF16),
        jax.ShapeDtypeStruct((b, Q_WIDTH, s), BF16),
        jax.ShapeDtypeStruct((b, GATE_ROWS, s), F32),
        jax.ShapeDtypeStruct((t // V_TILE, Q_WIDTH, V_TILE), BF16),
        jax.ShapeDtypeStruct((t // V_TILE, KV_WIDTH, V_TILE), BF16),
        jax.ShapeDtypeStruct((t // V_TILE, KV_WIDTH, V_TILE), BF16),
        jax.ShapeDtypeStruct((t, Q_WIDTH), BF16),
        jax.ShapeDtypeStruct((t, KV_WIDTH), BF16),
        jax.ShapeDtypeStruct((t, KV_WIDTH), BF16),
        jax.ShapeDtypeStruct((t, KV_WIDTH), F32),
        jax.ShapeDtypeStruct((t, KV_WIDTH), F32),
    ]
    out_specs = [feat(Q_WIDTH), feat(Q_WIDTH), feat(GATE_ROWS), tile(Q_WIDTH), tile(KV_WIDTH),
                 tile(KV_WIDTH), row(Q_WIDTH), row(KV_WIDTH), row(KV_WIDTH), row(KV_WIDTH), row(KV_WIDTH)]
    return pl.pallas_call(
        _proj_kernel,
        grid=(t // tm,),
        in_specs=[row(D_MODEL), pl.BlockSpec((1, 1, tm), lambda i: (i, 0, 0)), full(gmix),
                  full(wtok), full(wfeat), full(gfeat), full(invf8)],
        out_specs=out_specs,
        out_shape=out_shape,
        compiler_params=pltpu.CompilerParams(dimension_semantics=("parallel",),
                                             vmem_limit_bytes=VMEM_LIMIT),
    )(x2, posr, gmix, wtok, wfeat, gfeat, invf8)


def _compress_kernel(kcr_ref, vcr_ref, pe_ref, wk_ref, wv_ref, w2k_ref, w2v_ref, gain_ref,
                     pose_ref, invf_ref, bd_ref, kc_out, vct_out):
    half = NSA_GROUPS * CMP_HIDDEN
    n_sub = kcr_ref.shape[1] // CMP_STRIDE

    def sub_blocks(ref):
        return jnp.concatenate([ref[0, pl.ds(i, n_sub, stride=CMP_STRIDE), :] for i in range(CMP_STRIDE)],
                               axis=1)

    def comp(x, pe_a, pe_b, w_ref, w2_ref):
        a = _dot((x + pe_a).astype(BF16), w_ref[:, :half])
        b = _dot((x + pe_b).astype(BF16), w_ref[:, half:])
        hid = a + pltpu.roll(b, b.shape[0] - 1, 0)
        act = hid * _sigmoid(hid)
        return _dot(act.astype(BF16), w2_ref[...])

    kc = comp(sub_blocks(kcr_ref), pe_ref[0:1, :], pe_ref[1:2, :], wk_ref, w2k_ref)
    vc = comp(sub_blocks(vcr_ref), pe_ref[2:3, :], pe_ref[3:4, :], wv_ref, w2v_ref)
    kc = _head_norm(kc, bd_ref[...], gain_ref[...])
    kc = _rope(kc, _rope_tables(pose_ref[0], invf_ref[...]))
    kc_out[0] = kc.astype(BF16)
    vct_out[0] = vc.T.astype(BF16)


def _compress_call(kc_raw, vc_raw, pe4, wk, wv, w2k, w2v, gain, pos_end, invf, bd):
    b, s, width = kc_raw.shape
    n_sub = s // CMP_STRIDE
    blk = lambda shp: pl.BlockSpec((1,) + shp, lambda i: (i, 0, 0))
    full = lambda a: pl.BlockSpec(a.shape, lambda i: (0,) * a.ndim)
    return pl.pallas_call(
        _compress_kernel,
        grid=(b,),
        in_specs=[blk((s, width)), blk((s, width)), full(pe4), full(wk), full(wv),
                  full(w2k), full(w2v), full(gain), blk((n_sub, 1)), full(invf), full(bd)],
        out_specs=[blk((n_sub, KV_WIDTH)), blk((KV_WIDTH, n_sub))],
        out_shape=[jax.ShapeDtypeStruct((b, n_sub, KV_WIDTH), BF16),
                   jax.ShapeDtypeStruct((b, KV_WIDTH, n_sub), BF16)],
        compiler_params=pltpu.CompilerParams(dimension_semantics=("parallel",),
                                             vmem_limit_bytes=VMEM_LIMIT),
    )(kc_raw, vc_raw, pe4, wk, wv, w2k, w2v, gain, pos_end, invf, bd)


def _nsa_kernel(q_ref, kc_ref, vct_ref, ks_ref, vs_ref, kw_ref, vw_ref, gate_ref, ovl_ref, out_ref,
                sel_scr, lhs_scr, m_scr, acc_scr, o_scr, *bufs):
    tq = q_ref.shape[2]
    n_cmp = kc_ref.shape[1]
    n_slots = NSA_GROUPS * NSA_REP
    n_units = len(bufs) // 2
    s_scr, c_scr = bufs[:n_units], bufs[n_units:]
    per_unit = n_slots // n_units
    i = pl.program_id(1)
    t0 = i * tq
    t_row = t0 + lax.broadcasted_iota(jnp.int32, (1, tq), 1)

    zero = jnp.zeros((HEAD_DIM, tq), BF16)
    for g in range(NSA_GROUPS):
        for r in range(NSA_REP):
            hs = g * NSA_REP + r
            rows = q_ref[0, r * KV_WIDTH + g * HEAD_DIM:r * KV_WIDTH + (g + 1) * HEAD_DIM, :]
            lhs_scr[:, hs * tq:(hs + 1) * tq] = jnp.concatenate([rows, zero] if g == 0 else [zero, rows],
                                                                axis=0)
    gsig = _sigmoid(gate_ref[0])

    def unit_queries(unit):
        return lhs_scr[:, unit * SCORE_LANES:(unit + 1) * SCORE_LANES]

    def scores_into(unit, k_tile):
        tk = k_tile.shape[0]
        s_scr[unit][0:tk, :] = _dot(k_tile, unit_queries(unit))

    def sel_keys(kt):
        return ks_ref[0, pl.ds(pl.multiple_of(kt * NSA_TK, NSA_TK), NSA_TK), :]

    def win_keys(kt):
        return kw_ref[0, pl.ds(pl.multiple_of(kt * WIN_TK, WIN_TK), WIN_TK), :]

    for unit in range(n_units):
        c_scr[unit][...] = _dot(kc_ref[0], unit_queries(unit))
    for unit in range(n_units):
        scores_into(unit, sel_keys(0))

    c_idx = lax.broadcasted_iota(jnp.int32, (n_cmp, tq), 0)
    valid_c = (c_idx * CMP_STRIDE + (CMP_LEN - 1) <= t_row) & (c_idx < n_cmp - 1)
    bias_c = jnp.where(valid_c, 0.0, NEG_BIG)
    p_sum = []
    for hs in range(n_slots):
        g = hs // NSA_REP
        unit, hh = divmod(hs, per_unit)
        x = c_scr[unit][:, hh * tq:(hh + 1) * tq] + bias_c
        m = jnp.max(x, axis=0, keepdims=True)
        e = jnp.where(valid_c, jnp.exp2(x - m), 0.0)
        p = e / jnp.maximum(jnp.sum(e, axis=0, keepdims=True), 1e-30)
        o_c = _dot(vct_ref[0, g * HEAD_DIM:(g + 1) * HEAD_DIM, :], p.astype(BF16))
        o_scr[hs] = o_c * gsig[3 * hs:3 * hs + 1, :]
        if hs % NSA_REP == 0:
            p_sum.append(p)
        else:
            p_sum[g] = p_sum[g] + p

    cur = t_row // SEL_BLOCK
    j_idx = lax.broadcasted_iota(jnp.int32, (SEL_BLOCKS_MAX, tq), 0)
    forced = (j_idx == 0) | (j_idx == cur) | (j_idx == cur - 1)
    for g in range(NSA_GROUPS):
        hi, lo = _split_bf16(p_sum[g])
        imp = _dot(ovl_ref[...], hi) + _dot(ovl_ref[...], lo)
        imp = jnp.where(forced, jnp.inf, jnp.where(j_idx > cur, -jnp.inf, imp))
        sel_scr[g] = jnp.where(_rank_rows(imp, SEL_BLOCKS_MAX) < SEL_TOPN, 0.0, NEG_BIG)

    n_lane = tq // LANES

    def run_tile(v_tile, plan_of_group, next_keys):
        ones_rows = _ones_rows(v_tile.shape[1])
        plans = [plan_of_group(g) for g in range(NSA_GROUPS)]
        for unit in range(n_units):
            g = (unit * per_unit) // NSA_REP
            sub = v_tile.shape[1] // len(plans[g])
            v_aug = _with_sum_row(v_tile[g * HEAD_DIM:(g + 1) * HEAD_DIM, :], ones_rows)
            for hh in range(per_unit):
                def load(a, h, hh=hh, unit=unit, sub=sub):
                    return s_scr[unit][a * sub:(a + 1) * sub, hh * tq + h * LANES:hh * tq + (h + 1) * LANES]

                _slot_update(load, plans[g], m_scr, acc_scr, unit * per_unit + hh, v_aug)
            if next_keys is not None:
                scores_into(unit, next_keys())

    def fold_branch(branch):
        for hs in range(n_slots):
            o_scr[hs] = o_scr[hs] + _normalised(acc_scr[hs], gsig[3 * hs + branch:3 * hs + branch + 1, :])

    blocks_per_tile = NSA_TK // SEL_BLOCK
    first_win = jnp.maximum((t0 - (WINDOW - 1)) // WIN_TK, 0)

    def sel_row(kt, g, a, h):
        return sel_scr[g, pl.ds(kt * blocks_per_tile + a, 1), :][:, h * LANES:(h + 1) * LANES]

    def sel_plan(kt, g):
        return [[('rows', sel_row(kt, g, a, h)) for h in range(n_lane)] for a in range(blocks_per_tile)]

    def causal(k_off, q_off):
        return k_off <= q_off

    def sel_body(kt, carry):
        run_tile(vs_ref[0, kt], lambda g: sel_plan(kt, g), lambda: sel_keys(kt + 1))
        return carry

    last_sel = (t0 + tq - 1) // NSA_TK
    _reset_state(m_scr, acc_scr)
    lax.fori_loop(0, last_sel, sel_body, 0)
    run_tile(vs_ref[0, last_sel],
             lambda g: _visibility_plan(blocks_per_tile, SEL_BLOCK, n_lane, causal,
                                        lambda a, h: sel_row(last_sel, g, a, h)),
             lambda: win_keys(first_win))
    fold_branch(1)

    def win_tile(kt, visible, next_keys):
        run_tile(vw_ref[0, kt], lambda g: _visibility_plan(blocks_per_tile, SEL_BLOCK, n_lane, visible),
                 next_keys)

    _reset_state(m_scr, acc_scr)

    @pl.when(i >= 2)
    def _():
        win_tile(i - 2, lambda k_off, q_off: q_off < k_off, lambda: win_keys(i - 1))

    @pl.when(i >= 1)
    def _():
        win_tile(i - 1, lambda k_off, q_off: True, lambda: win_keys(i))

    win_tile(i, causal, None)
    fold_branch(2)

    for r in range(NSA_REP):
        out_ref[0, r * KV_WIDTH:r * KV_WIDTH + HEAD_DIM, :] = o_scr[r].astype(BF16)
        out_ref[0, r * KV_WIDTH + HEAD_DIM:(r + 1) * KV_WIDTH, :] = o_scr[NSA_REP + r].astype(BF16)


def _nsa_call(qn, kc, vct, ks, vs, kw, vw, gn, ovl):
    b, _, s = qn.shape
    tq = NSA_TQ
    n_slots = NSA_GROUPS * NSA_REP
    n_units = n_slots * tq // SCORE_LANES
    assert s // SEL_BLOCK <= SEL_BLOCKS_MAX and tq == WIN_TK == NSA_TK and WINDOW == 2 * WIN_TK
    seq = lambda a: pl.BlockSpec((1,) + a.shape[1:], lambda bi, i: (bi,) + (0,) * (a.ndim - 1))
    full = lambda a: pl.BlockSpec(a.shape, lambda bi, i: (0,) * a.ndim)
    return pl.pallas_call(
        _nsa_kernel,
        grid=(b, s // tq),
        in_specs=[pl.BlockSpec((1, Q_WIDTH, tq), lambda bi, i: (bi, 0, i)),
                  seq(kc), seq(vct), seq(ks), seq(vs), seq(kw), seq(vw),
                  pl.BlockSpec((1, GATE_ROWS, tq), lambda bi, i: (bi, 0, i)), full(ovl)],
        out_specs=pl.BlockSpec((1, Q_WIDTH, tq), lambda bi, i: (bi, 0, i)),
        out_shape=jax.ShapeDtypeStruct((b, Q_WIDTH, s), BF16),
        scratch_shapes=[pltpu.VMEM((NSA_GROUPS, SEL_BLOCKS_MAX, tq), F32),
                        pltpu.VMEM((KV_WIDTH, n_slots * tq), BF16),
                        pltpu.VMEM((n_slots, SUBLANES, tq), F32),
                        pltpu.VMEM((n_slots, HEAD_DIM + SUM_ROWS, tq), F32),
                        pltpu.VMEM((n_slots, HEAD_DIM, tq), F32)]
        + [pltpu.VMEM((max(NSA_TK, WIN_TK), SCORE_LANES), F32)] * n_units
        + [pltpu.VMEM((kc.shape[1], SCORE_LANES), F32)] * n_units,
        compiler_params=pltpu.CompilerParams(dimension_semantics=("parallel", "parallel"),
                                             vmem_limit_bytes=VMEM_LIMIT),
    )(qn, kc, vct, ks, vs, kw, vw, gn, ovl)


def _moba_kernel(q_ref, k_ref, v_ref, out_ref, km_scr, sel_scr, lhs_scr, m_scr, acc_scr, *s_scr):
    bs = MOBA_BLOCK
    n_blocks = k_ref.shape[1] // bs
    n_slots = len(s_scr)
    i = pl.program_id(2)

    @pl.when(i == 0)
    def _():
        km_scr[...] = jnp.zeros(km_scr.shape, F32)
        for n in range(n_blocks):
            km_scr[n:n + 1, :] = jnp.mean(k_ref[0, n * bs:(n + 1) * bs, :].astype(F32), axis=0,
                                          keepdims=True)

    def pair_lanes(hs):
        return slice((hs // 2) * LANES, (hs // 2 + 1) * LANES)

    zero = jnp.zeros((HEAD_DIM, bs), BF16)
    n_idx = lax.broadcasted_iota(jnp.int32, (MOBA_BLOCKS_MAX, bs), 0)
    for hs in range(n_slots):
        rows = q_ref[0, hs * HEAD_DIM:(hs + 1) * HEAD_DIM, :]
        lhs_scr[hs] = jnp.concatenate([rows, zero] if hs % 2 == 0 else [zero, rows], axis=0)

    def scores_into(hs, n):
        k_tile = k_ref[0, pl.ds(pl.multiple_of(n * bs, bs), bs), pair_lanes(hs)]
        s_scr[hs][...] = _dot(k_tile, lhs_scr[hs])

    block_scores = []
    for hs in range(n_slots):
        km_hi, km_lo = _split_bf16(km_scr[:, pair_lanes(hs)])
        block_scores.append(_dot(km_hi, lhs_scr[hs]) + _dot(km_lo, lhs_scr[hs]))
    for hs in range(n_slots):
        scores_into(hs, 0)

    for hs in range(n_slots):
        sc = jnp.where(n_idx < i, block_scores[hs], -jnp.inf)
        sel_scr[hs] = jnp.where((_rank_rows(sc, MOBA_BLOCKS_MAX) < MOBA_TOPK) & (n_idx < i), 0.0, NEG_BIG)

    ones_rows = _ones_rows(bs)

    n_lane = bs // LANES

    def run_block(n, plan_of_slot, prefetch):
        for hs in range(n_slots):
            plan = plan_of_slot(hs)
            sub = bs // len(plan)
            v_aug = _with_sum_row(v_ref[0, n, hs * HEAD_DIM:(hs + 1) * HEAD_DIM, :], ones_rows)

            def load(a, h, hs=hs, sub=sub):
                return s_scr[hs][a * sub:(a + 1) * sub, h * LANES:(h + 1) * LANES]

            _slot_update(load, plan, m_scr, acc_scr, hs, v_aug)
            if prefetch:
                scores_into(hs, n + 1)

    def past_body(n, carry):
        run_block(n, lambda hs: [[('rows', sel_scr[hs, pl.ds(n, 1), :][:, h * LANES:(h + 1) * LANES])
                                  for h in range(n_lane)]], True)
        return carry

    _reset_state(m_scr, acc_scr)
    lax.fori_loop(0, i, past_body, 0)
    own_plan = _visibility_plan(bs // OWN_SUB, OWN_SUB, n_lane, lambda k_off, q_off: k_off <= q_off)
    run_block(i, lambda hs: own_plan, False)
    for hs in range(n_slots):
        out_ref[0, hs * HEAD_DIM:(hs + 1) * HEAD_DIM, :] = _normalised(acc_scr[hs]).astype(BF16)


def _moba_call(qm, km, vm):
    b, w, s = qm.shape
    bs = MOBA_BLOCK
    nb = s // bs
    wb = MOBA_STEP_HEADS * HEAD_DIM
    assert nb <= MOBA_BLOCKS_MAX and w % wb == 0
    return pl.pallas_call(
        _moba_kernel,
        grid=(b, w // wb, nb),
        in_specs=[pl.BlockSpec((1, wb, bs), lambda bi, p, i: (bi, p, i)),
                  pl.BlockSpec((1, s, wb), lambda bi, p, i: (bi, 0, p)),
                  pl.BlockSpec((1, nb, wb, bs), lambda bi, p, i: (bi, 0, p, 0))],
        out_specs=pl.BlockSpec((1, wb, bs), lambda bi, p, i: (bi, p, i)),
        out_shape=jax.ShapeDtypeStruct((b, w, s), BF16),
        scratch_shapes=[pltpu.VMEM((MOBA_BLOCKS_MAX, wb), F32),
                        pltpu.VMEM((MOBA_STEP_HEADS, MOBA_BLOCKS_MAX, bs), F32),
                        pltpu.VMEM((MOBA_STEP_HEADS, LANES, bs), BF16),
                        pltpu.VMEM((MOBA_STEP_HEADS, SUBLANES, bs), F32),
                        pltpu.VMEM((MOBA_STEP_HEADS, HEAD_DIM + SUM_ROWS, bs), F32)]
        + [pltpu.VMEM((bs, bs), F32)] * MOBA_STEP_HEADS,
        compiler_params=pltpu.CompilerParams(
            dimension_semantics=("parallel", "parallel", "arbitrary"), vmem_limit_bytes=VMEM_LIMIT),
    )(qm, km, vm)


def _post_kernel(x_ref, yn_ref, ym_ref, p_ref, gmix_ref, wg_ref, wun_ref, wum_ref, wo_ref,
                 gffn_ref, wfi_ref, wfo_ref, gple_ref, wpg_ref, wpp_ref, out_ref):
    x = x_ref[...]
    h = _rms_rows(x, gmix_ref[...]).astype(BF16)
    ga = _sigmoid(_dot(h, wg_ref[:, :D_MODEL]))
    gb = _sigmoid(_dot(h, wg_ref[:, D_MODEL:]))
    merged = ga * _dot_tn(yn_ref[0], wun_ref[...]) + gb * _dot_tn(ym_ref[0], wum_ref[...])
    x = x + _dot(merged.astype(BF16), wo_ref[...])

    h = _rms_rows(x, gffn_ref[...]).astype(BF16)
    y = x
    for lo, hi in FFN_CHUNKS:
        gate = _dot(h, wfi_ref[:, lo:hi])
        up = _dot(h, wfi_ref[:, D_FF + lo:D_FF + hi])
        act = (gate * _sigmoid(gate) * up).astype(BF16)
        y = y + _dot(act, wfo_ref[lo:hi, :])

    h2 = _rms_rows(y, gple_ref[...]).astype(BF16)
    ple_gate = _sigmoid(_dot(h2, wpg_ref[...]))
    out_ref[...] = y + ple_gate * _dot(p_ref[...].astype(BF16), wpp_ref[...])


def _post_call(x2, yn, ym, p2, gmix, wg, wun, wum, wo, gffn, wfi, wfo, gple, wpg, wpp):
    t = x2.shape[0]
    tm = POST_TM
    nt = yn.shape[2] // tm
    row = lambda w_: pl.BlockSpec((tm, w_), lambda i: (i, 0))
    feat = pl.BlockSpec((1, Q_WIDTH, tm), lambda i: (i // nt, 0, i % nt))
    full = lambda a: pl.BlockSpec(a.shape, lambda i: (0, 0), pipeline_mode=pl.Buffered(1))
    weights = (gmix, wg, wun, wum, wo, gffn, wfi, wfo, gple, wpg, wpp)
    return pl.pallas_call(
        _post_kernel,
        grid=(t // tm,),
        in_specs=[row(D_MODEL), feat, feat, row(PLE_DIM)] + [full(w) for w in weights],
        out_specs=row(D_MODEL),
        out_shape=jax.ShapeDtypeStruct((t, D_MODEL), F32),
        compiler_params=pltpu.CompilerParams(dimension_semantics=("parallel",),
                                             vmem_limit_bytes=VMEM_LIMIT),
    )(x2, yn, ym, p2, *weights)


def _block_diag_ones(width):
    idx = np.arange(width) // HEAD_DIM
    return jnp.asarray(idx[:, None] == idx[None, :], dtype=BF16)


def _inv_freq():
    return ROPE_THETA ** (-jnp.arange(ROPE_HALF, dtype=F32) / ROPE_HALF)


def _inv_freq_lanes(width):
    inv_freq = _inv_freq()
    per_head = jnp.concatenate([inv_freq, inv_freq, jnp.zeros((HEAD_DIM - ROPE_DIM,), F32)])
    return jnp.tile(per_head, width // HEAD_DIM)[None, :]


def _split_w_in(w):
    parts = jnp.split(w, IN_CUTS, axis=-1)
    (q_n, kc, vc, ks, vs, kw, vw, gate_n, q_m, k_m, v_m, gate_a, gate_b) = parts
    d = w.shape[0]
    q_n = q_n.reshape(d, NSA_GROUPS, NSA_REP, HEAD_DIM).transpose(0, 2, 1, 3).reshape(d, -1)
    gate_n = jnp.pad(gate_n, ((0, 0), (0, GATE_ROWS - gate_n.shape[1])))
    w_tok = jnp.concatenate([kc, vc], axis=1)
    w_feat = jnp.concatenate([q_n, q_m, k_m, ks, kw, v_m, vs, vw, gate_n], axis=1).T
    w_gate = jnp.concatenate([gate_a, gate_b], axis=1)
    return w_tok.astype(BF16), w_feat.astype(BF16), w_gate.astype(BF16)


def _compress_weights(w1, w2, pe):
    eye = jnp.eye(NSA_GROUPS, dtype=F32)
    halves = []
    for part in (w1[:CMP_STRIDE * HEAD_DIM], w1[CMP_STRIDE * HEAD_DIM:]):
        p3 = part.reshape(CMP_STRIDE, HEAD_DIM, CMP_HIDDEN)
        halves.append(jnp.einsum('idh,gk->igdkh', p3, eye)
                      .reshape(CMP_STRIDE * NSA_GROUPS * HEAD_DIM, NSA_GROUPS * CMP_HIDDEN))
    w_big = jnp.concatenate(halves, axis=1).astype(BF16)
    w2_bd = jnp.einsum('hd,kg->khgd', w2, eye).reshape(NSA_GROUPS * CMP_HIDDEN,
                                                        NSA_GROUPS * HEAD_DIM).astype(BF16)
    pe_rows = [jnp.broadcast_to(pe[a:a + CMP_STRIDE, None, :], (CMP_STRIDE, NSA_GROUPS, HEAD_DIM))
               .reshape(1, -1) for a in (0, CMP_STRIDE)]
    return w_big, w2_bd, pe_rows


def _overlap_matrix_t(n_cmp):
    c = np.arange(n_cmp)
    j = np.arange(SEL_BLOCKS_MAX)
    start, end = c * CMP_STRIDE, c * CMP_STRIDE + CMP_LEN - 1
    ov = (start[None, :] <= j[:, None] * SEL_BLOCK + SEL_BLOCK - 1) & (end[None, :] >= j[:, None] * SEL_BLOCK)
    return jnp.asarray(ov, dtype=BF16)


def kernel(x, p, positions, g_mix, w_in, nsa_q_gain, nsa_kc_gain, nsa_ks_gain, nsa_kw_gain, nsa_pe_k, nsa_pe_v, nsa_ck_w1, nsa_ck_w2, nsa_cv_w1, nsa_cv_w2, moba_q_gain, moba_k_gain, w_up_nsa, w_up_moba, w_out, g_ffn, w_ffn_in, w_ffn_out, g_ple, w_ple_gate, w_ple_proj):
    b, s, d = x.shape
    depth = w_in.shape[0]
    assert s % PAD_MULT == 0 and s % PROJ_TM == 0 and d == D_MODEL and V_TILE == MOBA_BLOCK
    t = b * s
    n_sub = s // CMP_STRIDE

    assert ROPE_HALF == SUBLANES
    posr = positions.reshape(t // PROJ_TM, 1, PROJ_TM)
    pos_end = jnp.concatenate([positions[:, CMP_LEN - 1::CMP_STRIDE], positions[:, -1:]], axis=1)[:, :, None]
    invf128 = _inv_freq_lanes(LANES)
    invf8 = _inv_freq()[:, None]
    bd128 = _block_diag_ones(LANES)
    ovl_t = _overlap_matrix_t(n_sub)
    tile = lambda g, n: jnp.tile(g, n)
    xi = x.reshape(t, d)

    for i in range(depth):
        w_tok, w_feat, w_gate = _split_w_in(w_in[i])
        g_feat = jnp.concatenate([
            jnp.concatenate([tile(nsa_q_gain[i], NSA_HEADS), tile(moba_q_gain[i], MOBA_HEADS)])
            * (ATTN_SCALE * LOG2E),
            tile(moba_k_gain[i], MOBA_HEADS), tile(nsa_ks_gain[i], NSA_GROUPS),
            tile(nsa_kw_gain[i], NSA_GROUPS)])[:, None]
        gmix = g_mix[i][None, :]
        (qn, qm, gn, vm, vs, vw, km, ks, kw, kc_raw, vc_raw) = _proj_call(
            xi, posr, gmix, w_tok, w_feat, g_feat, invf8, b, s)

        wk_big, w2k_bd, pe_k = _compress_weights(nsa_ck_w1[i], nsa_ck_w2[i], nsa_pe_k[i])
        wv_big, w2v_bd, pe_v = _compress_weights(nsa_cv_w1[i], nsa_cv_w2[i], nsa_pe_v[i])
        pe4 = jnp.concatenate(pe_k + pe_v, axis=0)
        kc, vct = _compress_call(kc_raw.reshape(b, s, KV_WIDTH), vc_raw.reshape(b, s, KV_WIDTH), pe4, wk_big, wv_big, w2k_bd, w2v_bd,
                                 tile(nsa_kc_gain[i], NSA_GROUPS)[None, :], pos_end,
                                 invf128, bd128)

        r3 = lambda a: a.reshape(b, s, a.shape[-1])
        y_nsa = _nsa_call(qn, kc, vct, r3(ks), vs.reshape(b, s // NSA_TK, KV_WIDTH, NSA_TK), r3(kw),
                          vw.reshape(b, s // WIN_TK, KV_WIDTH, WIN_TK), gn, ovl_t)
        y_moba = _moba_call(qm, r3(km), vm.reshape(b, s // MOBA_BLOCK, Q_WIDTH, MOBA_BLOCK))

        w_un = (w_up_nsa[i].reshape(NSA_GROUPS, NSA_REP, HEAD_DIM, d).transpose(1, 0, 2, 3)
                .reshape(NSA_HEADS * HEAD_DIM, d).astype(BF16))
        xi = _post_call(xi, y_nsa, y_moba, p[i].reshape(t, PLE_DIM), gmix, w_gate, w_un,
                        w_up_moba[i].astype(BF16), w_out[i].astype(BF16), g_ffn[i][None, :],
                        w_ffn_in[i].astype(BF16), w_ffn_out[i].astype(BF16), g_ple[i][None, :],
                        w_ple_gate[i].astype(BF16), w_ple_proj[i].astype(BF16))
    return xi.reshape(b, s, d)
```

```python
import jax
import jax.numpy as jnp
import numpy as np
from jax import lax
from jax.experimental import pallas as pl
from jax.experimental.pallas import tpu as pltpu

F32 = jnp.float32
BF16 = jnp.bfloat16

D_MODEL = 1024
HEAD_DIM = 64
ROPE_DIM = HEAD_DIM // 4
ROPE_HALF = ROPE_DIM // 2
ROPE_THETA = 500000.0
NORM_EPS = 1e-6
ATTN_SCALE = HEAD_DIM ** -0.5

NSA_HEADS = 8
NSA_GROUPS = 2
NSA_REP = NSA_HEADS // NSA_GROUPS
CMP_LEN = 32
CMP_STRIDE = 16
CMP_HIDDEN = 256
SEL_BLOCK = 64
SEL_TOPN = 8
WINDOW = 512

MOBA_HEADS = 8
MOBA_BLOCK = 256
MOBA_TOPK = 3

PAD_MULT = 256
D_FF = ((-(-8 * D_MODEL // 3)) + 255) // 256 * 256
PLE_DIM = 256

IN_SPLITS = ((NSA_HEADS * HEAD_DIM,) + (NSA_GROUPS * HEAD_DIM,) * 6 + (3 * NSA_HEADS,)
             + (MOBA_HEADS * HEAD_DIM,) * 3 + (D_MODEL, D_MODEL))
IN_CUTS = tuple(int(c) for c in np.cumsum(IN_SPLITS)[:-1])

LANES = 128
SUBLANES = 8
NEG_BIG = -1e30
LOG2E = 1.4426950408889634
SUM_ROWS = 16
VMEM_LIMIT = 56 * 1024 * 1024

Q_WIDTH = NSA_HEADS * HEAD_DIM
KV_WIDTH = NSA_GROUPS * HEAD_DIM
GATE_ROWS = 32
KEY_ROWS = Q_WIDTH + 2 * KV_WIDTH
FEAT_ROWS = 2 * Q_WIDTH + KEY_ROWS + Q_WIDTH + 2 * KV_WIDTH + GATE_ROWS

PROJ_TM = 512
V_TILE = 256
NSA_TQ = 256
NSA_TK = V_TILE
WIN_TK = V_TILE
SCORE_LANES = 256
POST_TM = 512
MXU_TILE = 256
FFN_CHUNKS = ((0, 6 * MXU_TILE), (6 * MXU_TILE, D_FF))
SEL_BLOCKS_MAX = 32
MOBA_BLOCKS_MAX = 8
MOBA_STEP_HEADS = 8
OWN_SUB = 64


def _dot(a, b):
    return jnp.dot(a, b, preferred_element_type=F32)


def _dot_nt(a, b):
    return lax.dot_general(a, b, (((1,), (1,)), ((), ())), preferred_element_type=F32)


def _dot_tn(a, b):
    return lax.dot_general(a, b, (((0,), (0,)), ((), ())), preferred_element_type=F32)


def _split_bf16(a_f32):
    hi = a_f32.astype(BF16)
    return hi, (a_f32 - hi.astype(F32)).astype(BF16)


def _sigmoid(x):
    return 1.0 / (1.0 + jnp.exp(-x))


def _rms_rows(x, g):
    return x * lax.rsqrt(jnp.mean(x * x, axis=-1, keepdims=True) + NORM_EPS) * g


def _head_norm(t, bd, gain):
    hi, lo = _split_bf16(t * t)
    ss = _dot(hi, bd) + _dot(lo, bd)
    return t * lax.rsqrt(ss * (1.0 / HEAD_DIM) + NORM_EPS) * gain


def _rope_tables(pos_col, invf):
    ang = pos_col.astype(F32) * invf
    cos_a, sin_a = jnp.cos(ang), jnp.sin(ang)
    d = lax.broadcasted_iota(jnp.int32, ang.shape, 1) & (HEAD_DIM - 1)
    s_lo = jnp.where(d < ROPE_HALF, -sin_a, 0.0)
    s_hi = jnp.where((d >= ROPE_HALF) & (d < ROPE_DIM), sin_a, 0.0)
    return cos_a, s_lo, s_hi


def _rope(y, tables):
    cos_a, s_lo, s_hi = tables
    w = y.shape[1]
    return (y * cos_a + pltpu.roll(y, w - ROPE_HALF, 1) * s_lo
            + pltpu.roll(y, ROPE_HALF, 1) * s_hi)


def _ones_rows(tk):
    r = lax.broadcasted_iota(jnp.int32, (SUM_ROWS, tk), 0)
    return jnp.where(r == 0, 1.0, 0.0).astype(BF16)


def _with_sum_row(v_rows, ones_rows):
    return jnp.concatenate([v_rows, ones_rows], axis=0)


def _slot_update(load_s, plan, m_ref, acc_ref, hs, v_aug):
    n_sub, n_lane = len(plan), len(plan[0])
    sub = v_aug.shape[1] // n_sub
    e_cols, alphas = [], []
    for h in range(n_lane):
        lanes = slice(h * LANES, (h + 1) * LANES)
        m = m_ref[hs, 0:1, lanes]
        m_new = m
        for a in range(n_sub):
            if plan[a][h] is None:
                continue
            kind, arg = plan[a][h]
            if kind == 'rows':
                top = jnp.max(load_s(a, h), axis=0, keepdims=True)
                m_new = jnp.maximum(m_new, top if arg is None else jnp.where(arg < 0.0, NEG_BIG, top))
            else:
                m_new = jnp.maximum(m_new, jnp.max(load_s(a, h) + arg, axis=0, keepdims=True))
        e = []
        for a in range(n_sub):
            if plan[a][h] is None:
                e.append(jnp.zeros((sub, LANES), BF16))
                continue
            kind, arg = plan[a][h]
            if kind == 'rows':
                shift = -m_new if arg is None else jnp.where(arg < 0.0, NEG_BIG, -m_new)
                e.append(jnp.exp2(load_s(a, h) + shift).astype(BF16))
            else:
                e.append(jnp.exp2(load_s(a, h) + arg - m_new).astype(BF16))
        e_cols.append(e[0] if n_sub == 1 else jnp.concatenate(e, axis=0))
        alphas.append(jnp.exp2(m - m_new))
        m_ref[hs, 0:1, lanes] = m_new
    e_all = e_cols[0] if n_lane == 1 else jnp.concatenate(e_cols, axis=1)
    alpha = alphas[0] if n_lane == 1 else jnp.concatenate(alphas, axis=1)
    acc_ref[hs] = alpha * acc_ref[hs] + _dot(v_aug, e_all)


def _visibility_plan(n_sub, sub, n_lane, visible, kept_row=None):
    plan = []
    for a in range(n_sub):
        row_plan = []
        for h in range(n_lane):
            k0, k1, q0, q1 = a * sub, (a + 1) * sub - 1, h * LANES, (h + 1) * LANES - 1
            corners = [visible(k, q) for k in (k0, k1) for q in (q0, q1)]
            row = None if kept_row is None else kept_row(a, h)
            if not any(corners):
                row_plan.append(None)
            elif all(corners):
                row_plan.append(('rows', row))
            else:
                k_off = k0 + lax.broadcasted_iota(jnp.int32, (sub, LANES), 0)
                q_off = q0 + lax.broadcasted_iota(jnp.int32, (sub, LANES), 1)
                keep = 0.0 if row is None else row
                row_plan.append(('tile', jnp.where(visible(k_off, q_off), keep, NEG_BIG)))
        plan.append(row_plan)
    return plan


def _reset_state(m_ref, acc_ref):
    m_ref[...] = jnp.full(m_ref.shape, NEG_BIG, F32)
    acc_ref[...] = jnp.zeros(acc_ref.shape, F32)


def _normalised(acc, scale_row=1.0):
    return acc[:HEAD_DIM] * (scale_row / jnp.maximum(acc[HEAD_DIM:HEAD_DIM + 1], 1e-30))


def _pick_top_rows(v, n_pick):
    n, q = v.shape
    slabs = [v[a:a + SUBLANES] for a in range(0, n, SUBLANES)]
    sub = lax.broadcasted_iota(jnp.int32, (SUBLANES, q), 0).astype(F32)
    row_id = [sub + float(a) for a in range(0, n, SUBLANES)]
    picked = [jnp.zeros((SUBLANES, q), F32) for _ in slabs]
    for _ in range(n_pick):
        top = slabs[0]
        for slab in slabs[1:]:
            top = jnp.maximum(top, slab)
        top = jnp.max(top, axis=0, keepdims=True)
        first = jnp.where(slabs[0] == top, row_id[0], float(n))
        for slab, rid in zip(slabs[1:], row_id[1:]):
            first = jnp.minimum(first, jnp.where(slab == top, rid, float(n)))
        first = jnp.min(first, axis=0, keepdims=True)
        hit = [rid == first for rid in row_id]
        picked = [jnp.where(h, 1.0, p) for h, p in zip(hit, picked)]
        slabs = [jnp.where(h, -jnp.inf, s) for h, s in zip(hit, slabs)]
    return picked[0] if len(picked) == 1 else jnp.concatenate(picked, axis=0)


def _rank_rows(v, n_rows):
    n, q = v.shape
    slabs = [v[a:a + SUBLANES] for a in range(0, n, SUBLANES)]
    ranks = [jnp.zeros((SUBLANES, q), F32) for _ in slabs]
    sub = lax.broadcasted_iota(jnp.int32, (SUBLANES, q), 0)
    for jp in range(n_rows):
        row = v[jp:jp + 1, :]
        for si, slab in enumerate(slabs):
            first = si * SUBLANES
            if first > jp:
                beats = jnp.where(row >= slab, 1.0, 0.0)
            elif first + SUBLANES - 1 < jp:
                beats = jnp.where(row > slab, 1.0, 0.0)
            else:
                ge = jnp.where(row >= slab, 1.0, 0.0)
                gt = jnp.where(row > slab, 1.0, 0.0)
                beats = gt + (ge - gt) * jnp.where(sub > jp - first, 1.0, 0.0)
            ranks[si] = ranks[si] + beats
    return ranks[0] if len(ranks) == 1 else jnp.concatenate(ranks, axis=0)


def _proj_kernel(x_ref, posr_ref, gmix_ref, wtok_ref, wfeat_ref, gfeat_ref, invf8_ref,
                 qn_ref, qm_ref, gn_ref, vm_ref, vs_ref, vw_ref, km_ref, ks_ref, kw_ref,
                 kc_ref, vc_ref):
    tm = x_ref.shape[0]
    h = _rms_rows(x_ref[...], gmix_ref[...]).astype(BF16)

    acc = _dot(h, wtok_ref[...])
    kc_ref[...] = acc[:, :KV_WIDTH]
    vc_ref[...] = acc[:, KV_WIDTH:]

    ang = invf8_ref[...] * posr_ref[0].astype(F32)
    cos_a, sin_a = jnp.cos(ang), jnp.sin(ang)

    def project(lo, hi):
        return _dot_nt(wfeat_ref[lo:hi, :], h)

    def normed_head(acc_g, row0, hh):
        t = acc_g[hh * HEAD_DIM:(hh + 1) * HEAD_DIM, :]
        ss = jnp.sum(t * t, axis=0, keepdims=True)
        gain = gfeat_ref[row0 + hh * HEAD_DIM:row0 + (hh + 1) * HEAD_DIM, :]
        y = t * lax.rsqrt(ss * (1.0 / HEAD_DIM) + NORM_EPS) * gain
        a, b = y[:ROPE_HALF], y[ROPE_HALF:ROPE_DIM]
        return jnp.concatenate([a * cos_a - b * sin_a, b * cos_a + a * sin_a, y[ROPE_DIM:]], axis=0)

    q_rows = 2 * Q_WIDTH
    for mixer, dst in enumerate((qn_ref, qm_ref)):
        acc_q = project(mixer * Q_WIDTH, (mixer + 1) * Q_WIDTH)
        for hh in range(NSA_HEADS):
            dst[0, hh * HEAD_DIM:(hh + 1) * HEAD_DIM, :] = normed_head(acc_q, mixer * Q_WIDTH, hh).astype(BF16)

    acc_k = project(q_rows, q_rows + KEY_ROWS)
    for pair in range(KEY_ROWS // LANES):
        y = jnp.concatenate([normed_head(acc_k, q_rows, 2 * pair), normed_head(acc_k, q_rows, 2 * pair + 1)],
                            axis=0)
        y = y.T.astype(BF16)
        if pair < Q_WIDTH // LANES:
            km_ref[:, pair * LANES:(pair + 1) * LANES] = y
        elif pair == Q_WIDTH // LANES:
            ks_ref[...] = y
        else:
            kw_ref[...] = y

    v_rows = q_rows + KEY_ROWS
    acc_vm = project(v_rows, v_rows + Q_WIDTH)
    acc_v = project(v_rows + Q_WIDTH, FEAT_ROWS)
    for j in range(tm // V_TILE):
        cols = slice(j * V_TILE, (j + 1) * V_TILE)
        vm_ref[j] = acc_vm[:, cols].astype(BF16)
        vs_ref[j] = acc_v[:KV_WIDTH, cols].astype(BF16)
        vw_ref[j] = acc_v[KV_WIDTH:2 * KV_WIDTH, cols].astype(BF16)
    gn_ref[0] = acc_v[2 * KV_WIDTH:]


def _proj_call(x2, posr, gmix, wtok, wfeat, gfeat, invf8, b, s):
    t = x2.shape[0]
    tm = PROJ_TM
    nt = s // tm
    row = lambda w_: pl.BlockSpec((tm, w_), lambda i: (i, 0))
    full = lambda a: pl.BlockSpec(a.shape, lambda i: (0,) * a.ndim, pipeline_mode=pl.Buffered(1))
    feat = lambda r: pl.BlockSpec((1, r, tm), lambda i: (i // nt, 0, i % nt))
    tile = lambda r: pl.BlockSpec((tm // V_TILE, r, V_TILE), lambda i: (i, 0, 0))
    out_shape = [
        jax.ShapeDtypeStruct((b, Q_WIDTH, s), BF16),
        jax.ShapeDtypeStruct((b, Q_WIDTH, s), BF16),
        jax.ShapeDtypeStruct((b, GATE_ROWS, s), F32),
        jax.ShapeDtypeStruct((t // V_TILE, Q_WIDTH, V_TILE), BF16),
        jax.ShapeDtypeStruct((t // V_TILE, KV_WIDTH, V_TILE), BF16),
        jax.ShapeDtypeStruct((t // V_TILE, KV_WIDTH, V_TILE), BF16),
        jax.ShapeDtypeStruct((t, Q_WIDTH), BF16),
        jax.ShapeDtypeStruct((t, KV_WIDTH), BF16),
        jax.ShapeDtypeStruct((t, KV_WIDTH), BF16),
        jax.ShapeDtypeStruct((t, KV_WIDTH), F32),
        jax.ShapeDtypeStruct((t, KV_WIDTH), F32),
    ]
    out_specs = [feat(Q_WIDTH), feat(Q_WIDTH), feat(GATE_ROWS), tile(Q_WIDTH), tile(KV_WIDTH),
                 tile(KV_WIDTH), row(Q_WIDTH), row(KV_WIDTH), row(KV_WIDTH), row(KV_WIDTH), row(KV_WIDTH)]
    return pl.pallas_call(
        _proj_kernel,
        grid=(t // tm,),
        in_specs=[row(D_MODEL), pl.BlockSpec((1, 1, tm), lambda i: (i, 0, 0)), full(gmix),
                  full(wtok), full(wfeat), full(gfeat), full(invf8)],
        out_specs=out_specs,
        out_shape=out_shape,
        compiler_params=pltpu.CompilerParams(dimension_semantics=("parallel",),
                                             vmem_limit_bytes=VMEM_LIMIT),
    )(x2, posr, gmix, wtok, wfeat, gfeat, invf8)


def _compress_kernel(kcr_ref, vcr_ref, pe_ref, wk_ref, wv_ref, w2k_ref, w2v_ref, gain_ref,
                     pose_ref, invf_ref, bd_ref, kc_out, vct_out):
    half = NSA_GROUPS * CMP_HIDDEN
    n_sub = kcr_ref.shape[1] // CMP_STRIDE

    def sub_blocks(ref):
        return jnp.concatenate([ref[0, pl.ds(i, n_sub, stride=CMP_STRIDE), :] for i in range(CMP_STRIDE)],
                               axis=1)

    def comp(x, pe_a, pe_b, w_ref, w2_ref):
        a = _dot((x + pe_a).astype(BF16), w_ref[:, :half])
        b = _dot((x + pe_b).astype(BF16), w_ref[:, half:])
        hid = a + pltpu.roll(b, b.shape[0] - 1, 0)
        act = hid * _sigmoid(hid)
        return _dot(act.astype(BF16), w2_ref[...])

    kc = comp(sub_blocks(kcr_ref), pe_ref[0:1, :], pe_ref[1:2, :], wk_ref, w2k_ref)
    vc = comp(sub_blocks(vcr_ref), pe_ref[2:3, :], pe_ref[3:4, :], wv_ref, w2v_ref)
    kc = _head_norm(kc, bd_ref[...], gain_ref[...])
    kc = _rope(kc, _rope_tables(pose_ref[0], invf_ref[...]))
    kc_out[0] = kc.astype(BF16)
    vct_out[0] = vc.T.astype(BF16)


def _compress_call(kc_raw, vc_raw, pe4, wk, wv, w2k, w2v, gain, pos_end, invf, bd):
    b, s, width = kc_raw.shape
    n_sub = s // CMP_STRIDE
    blk = lambda shp: pl.BlockSpec((1,) + shp, lambda i: (i, 0, 0))
    full = lambda a: pl.BlockSpec(a.shape, lambda i: (0,) * a.ndim)
    return pl.pallas_call(
        _compress_kernel,
        grid=(b,),
        in_specs=[blk((s, width)), blk((s, width)), full(pe4), full(wk), full(wv),
                  full(w2k), full(w2v), full(gain), blk((n_sub, 1)), full(invf), full(bd)],
        out_specs=[blk((n_sub, KV_WIDTH)), blk((KV_WIDTH, n_sub))],
        out_shape=[jax.ShapeDtypeStruct((b, n_sub, KV_WIDTH), BF16),
                   jax.ShapeDtypeStruct((b, KV_WIDTH, n_sub), BF16)],
        compiler_params=pltpu.CompilerParams(dimension_semantics=("parallel",),
                                             vmem_limit_bytes=VMEM_LIMIT),
    )(kc_raw, vc_raw, pe4, wk, wv, w2k, w2v, gain, pos_end, invf, bd)


def _nsa_kernel(q_ref, kc_ref, vct_ref, ks_ref, vs_ref, kw_ref, vw_ref, gate_ref, ovl_ref, out_ref,
                sel_scr, lhs_scr, m_scr, acc_scr, o_scr, *bufs):
    tq = q_ref.shape[2]
    n_cmp = kc_ref.shape[1]
    n_slots = NSA_GROUPS * NSA_REP
    n_units = len(bufs) // 2
    s_scr, c_scr = bufs[:n_units], bufs[n_units:]
    per_unit = n_slots // n_units
    i = pl.program_id(1)
    t0 = i * tq
    t_row = t0 + lax.broadcasted_iota(jnp.int32, (1, tq), 1)

    zero = jnp.zeros((HEAD_DIM, tq), BF16)
    for g in range(NSA_GROUPS):
        for r in range(NSA_REP):
            hs = g * NSA_REP + r
            rows = q_ref[0, r * KV_WIDTH + g * HEAD_DIM:r * KV_WIDTH + (g + 1) * HEAD_DIM, :]
            lhs_scr[:, hs * tq:(hs + 1) * tq] = jnp.concatenate([rows, zero] if g == 0 else [zero, rows],
                                                                axis=0)
    gsig = _sigmoid(gate_ref[0])

    def unit_queries(unit):
        return lhs_scr[:, unit * SCORE_LANES:(unit + 1) * SCORE_LANES]

    def scores_into(unit, k_tile):
        tk = k_tile.shape[0]
        s_scr[unit][0:tk, :] = _dot(k_tile, unit_queries(unit))

    def sel_keys(kt):
        return ks_ref[0, pl.ds(pl.multiple_of(kt * NSA_TK, NSA_TK), NSA_TK), :]

    def win_keys(kt):
        return kw_ref[0, pl.ds(pl.multiple_of(kt * WIN_TK, WIN_TK), WIN_TK), :]

    n_lane = tq // LANES

    def run_tile(v_tile, plan_of_group, next_keys):
        ones_rows = _ones_rows(v_tile.shape[1])
        plans = [plan_of_group(g) for g in range(NSA_GROUPS)]
        for unit in range(n_units):
            g = (unit * per_unit) // NSA_REP
            sub = v_tile.shape[1] // len(plans[g])
            v_aug = _with_sum_row(v_tile[g * HEAD_DIM:(g + 1) * HEAD_DIM, :], ones_rows)
            for hh in range(per_unit):
                def load(a, h, hh=hh, unit=unit, sub=sub):
                    return s_scr[unit][a * sub:(a + 1) * sub, hh * tq + h * LANES:hh * tq + (h + 1) * LANES]

                _slot_update(load, plans[g], m_scr, acc_scr, unit * per_unit + hh, v_aug)
            if next_keys is not None:
                scores_into(unit, next_keys())

    def fold_branch(branch):
        for hs in range(n_slots):
            o_scr[hs] = o_scr[hs] + _normalised(acc_scr[hs], gsig[3 * hs + branch:3 * hs + branch + 1, :])

    blocks_per_tile = NSA_TK // SEL_BLOCK
    first_win = jnp.maximum((t0 - (WINDOW - 1)) // WIN_TK, 0)

    def sel_row(kt, g, a, h):
        return sel_scr[g, pl.ds(kt * blocks_per_tile + a, 1), :][:, h * LANES:(h + 1) * LANES]

    def sel_plan(kt, g):
        return [[('rows', sel_row(kt, g, a, h)) for h in range(n_lane)] for a in range(blocks_per_tile)]

    def causal(k_off, q_off):
        return k_off <= q_off

    def sel_body(kt, carry):
        run_tile(vs_ref[0, kt], lambda g: sel_plan(kt, g), lambda: sel_keys(kt + 1))
        return carry

    def win_tile(kt, visible, next_keys):
        run_tile(vw_ref[0, kt], lambda g: _visibility_plan(blocks_per_tile, SEL_BLOCK, n_lane, visible),
                 next_keys)

    for unit in range(n_units):
        c_scr[unit][...] = _dot(kc_ref[0], unit_queries(unit))
    for unit in range(n_units):
        scores_into(unit, win_keys(first_win))

    c_idx = lax.broadcasted_iota(jnp.int32, (n_cmp, tq), 0)
    valid_c = (c_idx * CMP_STRIDE + (CMP_LEN - 1) <= t_row) & (c_idx < n_cmp - 1)
    bias_c = jnp.where(valid_c, 0.0, NEG_BIG)
    p_sum = []
    for hs in range(n_slots):
        g = hs // NSA_REP
        unit, hh = divmod(hs, per_unit)
        x = c_scr[unit][:, hh * tq:(hh + 1) * tq] + bias_c
        m = jnp.max(x, axis=0, keepdims=True)
        e = jnp.where(valid_c, jnp.exp2(x - m), 0.0)
        p = e / jnp.maximum(jnp.sum(e, axis=0, keepdims=True), 1e-30)
        o_c = _dot(vct_ref[0, g * HEAD_DIM:(g + 1) * HEAD_DIM, :], p.astype(BF16))
        o_scr[hs] = o_c * gsig[3 * hs:3 * hs + 1, :]
        if hs % NSA_REP == 0:
            p_sum.append(p)
        else:
            p_sum[g] = p_sum[g] + p

    imp = []
    for g in range(NSA_GROUPS):
        hi, lo = _split_bf16(p_sum[g])
        imp.append(_dot(ovl_ref[...], hi) + _dot(ovl_ref[...], lo))

    _reset_state(m_scr, acc_scr)

    @pl.when(i >= 2)
    def _():
        win_tile(i - 2, lambda k_off, q_off: q_off < k_off, lambda: win_keys(i - 1))

    @pl.when(i >= 1)
    def _():
        win_tile(i - 1, lambda k_off, q_off: True, lambda: win_keys(i))

    cur = t_row // SEL_BLOCK
    j_idx = lax.broadcasted_iota(jnp.int32, (SEL_BLOCKS_MAX, tq), 0)
    forced = (j_idx == 0) | (j_idx == cur) | (j_idx == cur - 1)
    excluded = forced | (j_idx > cur)
    for g in range(NSA_GROUPS):
        picked = _pick_top_rows(jnp.where(excluded, -jnp.inf, imp[g]), SEL_TOPN - 3)
        sel_scr[g] = jnp.where(forced | (picked > 0.5), 0.0, NEG_BIG)

    win_tile(i, causal, lambda: sel_keys(0))
    fold_branch(2)

    last_sel = (t0 + tq - 1) // NSA_TK
    _reset_state(m_scr, acc_scr)
    lax.fori_loop(0, last_sel, sel_body, 0)
    run_tile(vs_ref[0, last_sel],
             lambda g: _visibility_plan(blocks_per_tile, SEL_BLOCK, n_lane, causal,
                                        lambda a, h: sel_row(last_sel, g, a, h)),
             None)
    fold_branch(1)

    for r in range(NSA_REP):
        out_ref[0, r * KV_WIDTH:r * KV_WIDTH + HEAD_DIM, :] = o_scr[r].astype(BF16)
        out_ref[0, r * KV_WIDTH + HEAD_DIM:(r + 1) * KV_WIDTH, :] = o_scr[NSA_REP + r].astype(BF16)


def _nsa_call(qn, kc, vct, ks, vs, kw, vw, gn, ovl):
    b, _, s = qn.shape
    tq = NSA_TQ
    n_slots = NSA_GROUPS * NSA_REP
    n_units = n_slots * tq // SCORE_LANES
    assert s // SEL_BLOCK <= SEL_BLOCKS_MAX and tq == WIN_TK == NSA_TK and WINDOW == 2 * WIN_TK
    seq = lambda a: pl.BlockSpec((1,) + a.shape[1:], lambda bi, i: (bi,) + (0,) * (a.ndim - 1))
    full = lambda a: pl.BlockSpec(a.shape, lambda bi, i: (0,) * a.ndim)
    return pl.pallas_call(
        _nsa_kernel,
        grid=(b, s // tq),
        in_specs=[pl.BlockSpec((1, Q_WIDTH, tq), lambda bi, i: (bi, 0, i)),
                  seq(kc), seq(vct), seq(ks), seq(vs), seq(kw), seq(vw),
                  pl.BlockSpec((1, GATE_ROWS, tq), lambda bi, i: (bi, 0, i)), full(ovl)],
        out_specs=pl.BlockSpec((1, Q_WIDTH, tq), lambda bi, i: (bi, 0, i)),
        out_shape=jax.ShapeDtypeStruct((b, Q_WIDTH, s), BF16),
        scratch_shapes=[pltpu.VMEM((NSA_GROUPS, SEL_BLOCKS_MAX, tq), F32),
                        pltpu.VMEM((KV_WIDTH, n_slots * tq), BF16),
                        pltpu.VMEM((n_slots, SUBLANES, tq), F32),
                        pltpu.VMEM((n_slots, HEAD_DIM + SUM_ROWS, tq), F32),
                        pltpu.VMEM((n_slots, HEAD_DIM, tq), F32)]
        + [pltpu.VMEM((max(NSA_TK, WIN_TK), SCORE_LANES), F32)] * n_units
        + [pltpu.VMEM((kc.shape[1], SCORE_LANES), F32)] * n_units,
        compiler_params=pltpu.CompilerParams(dimension_semantics=("parallel", "parallel"),
                                             vmem_limit_bytes=VMEM_LIMIT),
    )(qn, kc, vct, ks, vs, kw, vw, gn, ovl)


def _moba_kernel(q_ref, k_ref, v_ref, out_ref, km_scr, sel_scr, lhs_scr, m_scr, acc_scr, *s_scr):
    bs = MOBA_BLOCK
    n_blocks = k_ref.shape[1] // bs
    n_slots = len(s_scr)
    i = pl.program_id(2)

    @pl.when(i == 0)
    def _():
        km_scr[...] = jnp.zeros(km_scr.shape, F32)
        for n in range(n_blocks):
            km_scr[n:n + 1, :] = jnp.mean(k_ref[0, n * bs:(n + 1) * bs, :].astype(F32), axis=0,
                                          keepdims=True)

    def pair_lanes(hs):
        return slice((hs // 2) * LANES, (hs // 2 + 1) * LANES)

    zero = jnp.zeros((HEAD_DIM, bs), BF16)
    n_idx = lax.broadcasted_iota(jnp.int32, (MOBA_BLOCKS_MAX, bs), 0)
    for hs in range(n_slots):
        rows = q_ref[0, hs * HEAD_DIM:(hs + 1) * HEAD_DIM, :]
        lhs_scr[hs] = jnp.concatenate([rows, zero] if hs % 2 == 0 else [zero, rows], axis=0)

    def scores_into(hs, n):
        k_tile = k_ref[0, pl.ds(pl.multiple_of(n * bs, bs), bs), pair_lanes(hs)]
        s_scr[hs][...] = _dot(k_tile, lhs_scr[hs])

    block_scores = []
    for hs in range(n_slots):
        km_hi, km_lo = _split_bf16(km_scr[:, pair_lanes(hs)])
        block_scores.append(_dot(km_hi, lhs_scr[hs]) + _dot(km_lo, lhs_scr[hs]))
    for hs in range(n_slots):
        scores_into(hs, 0)

    for hs in range(n_slots):
        sc = jnp.where(n_idx < i, block_scores[hs], -jnp.inf)
        sel_scr[hs] = jnp.where((_rank_rows(sc, MOBA_BLOCKS_MAX) < MOBA_TOPK) & (n_idx < i), 0.0, NEG_BIG)

    ones_rows = _ones_rows(bs)

    n_lane = bs // LANES

    def run_block(n, plan_of_slot, prefetch):
        for hs in range(n_slots):
            plan = plan_of_slot(hs)
            sub = bs // len(plan)
            v_aug = _with_sum_row(v_ref[0, n, hs * HEAD_DIM:(hs + 1) * HEAD_DIM, :], ones_rows)

            def load(a, h, hs=hs, sub=sub):
                return s_scr[hs][a * sub:(a + 1) * sub, h * LANES:(h + 1) * LANES]

            _slot_update(load, plan, m_scr, acc_scr, hs, v_aug)
            if prefetch:
                scores_into(hs, n + 1)

    def past_body(n, carry):
        run_block(n, lambda hs: [[('rows', sel_scr[hs, pl.ds(n, 1), :][:, h * LANES:(h + 1) * LANES])
                                  for h in range(n_lane)]], True)
        return carry

    _reset_state(m_scr, acc_scr)
    lax.fori_loop(0, i, past_body, 0)
    own_plan = _visibility_plan(bs // OWN_SUB, OWN_SUB, n_lane, lambda k_off, q_off: k_off <= q_off)
    run_block(i, lambda hs: own_plan, False)
    for hs in range(n_slots):
        out_ref[0, hs * HEAD_DIM:(hs + 1) * HEAD_DIM, :] = _normalised(acc_scr[hs]).astype(BF16)


def _moba_call(qm, km, vm):
    b, w, s = qm.shape
    bs = MOBA_BLOCK
    nb = s // bs
    wb = MOBA_STEP_HEADS * HEAD_DIM
    assert nb <= MOBA_BLOCKS_MAX and w % wb == 0
    return pl.pallas_call(
        _moba_kernel,
        grid=(b, w // wb, nb),
        in_specs=[pl.BlockSpec((1, wb, bs), lambda bi, p, i: (bi, p, i)),
                  pl.BlockSpec((1, s, wb), lambda bi, p, i: (bi, 0, p)),
                  pl.BlockSpec((1, nb, wb, bs), lambda bi, p, i: (bi, 0, p, 0))],
        out_specs=pl.BlockSpec((1, wb, bs), lambda bi, p, i: (bi, p, i)),
        out_shape=jax.ShapeDtypeStruct((b, w, s), BF16),
        scratch_shapes=[pltpu.VMEM((MOBA_BLOCKS_MAX, wb), F32),
                        pltpu.VMEM((MOBA_STEP_HEADS, MOBA_BLOCKS_MAX, bs), F32),
                        pltpu.VMEM((MOBA_STEP_HEADS, LANES, bs), BF16),
                        pltpu.VMEM((MOBA_STEP_HEADS, SUBLANES, bs), F32),
                        pltpu.VMEM((MOBA_STEP_HEADS, HEAD_DIM + SUM_ROWS, bs), F32)]
        + [pltpu.VMEM((bs, bs), F32)] * MOBA_STEP_HEADS,
        compiler_params=pltpu.CompilerParams(
            dimension_semantics=("parallel", "parallel", "arbitrary"), vmem_limit_bytes=VMEM_LIMIT),
    )(qm, km, vm)


def _post_kernel(x_ref, yn_ref, ym_ref, p_ref, gmix_ref, wg_ref, wun_ref, wum_ref, wo_ref,
                 gffn_ref, wfi_ref, wfo_ref, gple_ref, wpg_ref, wpp_ref, out_ref):
    x = x_ref[...]
    h = _rms_rows(x, gmix_ref[...]).astype(BF16)
    ga = _sigmoid(_dot(h, wg_ref[:, :D_MODEL]))
    gb = _sigmoid(_dot(h, wg_ref[:, D_MODEL:]))
    merged = ga * _dot_tn(yn_ref[0], wun_ref[...]) + gb * _dot_tn(ym_ref[0], wum_ref[...])
    x = x + _dot(merged.astype(BF16), wo_ref[...])

    h = _rms_rows(x, gffn_ref[...]).astype(BF16)
    y = x
    for lo, hi in FFN_CHUNKS:
        gate = _dot(h, wfi_ref[:, lo:hi])
        up = _dot(h, wfi_ref[:, D_FF + lo:D_FF + hi])
        act = (gate * _sigmoid(gate) * up).astype(BF16)
        y = y + _dot(act, wfo_ref[lo:hi, :])

    h2 = _rms_rows(y, gple_ref[...]).astype(BF16)
    ple_gate = _sigmoid(_dot(h2, wpg_ref[...]))
    out_ref[...] = y + ple_gate * _dot(p_ref[...].astype(BF16), wpp_ref[...])


def _post_call(x2, yn, ym, p2, gmix, wg, wun, wum, wo, gffn, wfi, wfo, gple, wpg, wpp):
    t = x2.shape[0]
    tm = POST_TM
    nt = yn.shape[2] // tm
    row = lambda w_: pl.BlockSpec((tm, w_), lambda i: (i, 0))
    feat = pl.BlockSpec((1, Q_WIDTH, tm), lambda i: (i // nt, 0, i % nt))
    full = lambda a: pl.BlockSpec(a.shape, lambda i: (0, 0), pipeline_mode=pl.Buffered(1))
    weights = (gmix, wg, wun, wum, wo, gffn, wfi, wfo, gple, wpg, wpp)
    return pl.pallas_call(
        _post_kernel,
        grid=(t // tm,),
        in_specs=[row(D_MODEL), feat, feat, row(PLE_DIM)] + [full(w) for w in weights],
        out_specs=row(D_MODEL),
        out_shape=jax.ShapeDtypeStruct((t, D_MODEL), F32),
        compiler_params=pltpu.CompilerParams(dimension_semantics=("parallel",),
                                             vmem_limit_bytes=VMEM_LIMIT),
    )(x2, yn, ym, p2, *weights)


def _block_diag_ones(width):
    idx = np.arange(width) // HEAD_DIM
    return jnp.asarray(idx[:, None] == idx[None, :], dtype=BF16)


def _inv_freq():
    return ROPE_THETA ** (-jnp.arange(ROPE_HALF, dtype=F32) / ROPE_HALF)


def _inv_freq_lanes(width):
    inv_freq = _inv_freq()
    per_head = jnp.concatenate([inv_freq, inv_freq, jnp.zeros((HEAD_DIM - ROPE_DIM,), F32)])
    return jnp.tile(per_head, width // HEAD_DIM)[None, :]


def _split_w_in(w):
    parts = jnp.split(w, IN_CUTS, axis=-1)
    (q_n, kc, vc, ks, vs, kw, vw, gate_n, q_m, k_m, v_m, gate_a, gate_b) = parts
    d = w.shape[0]
    q_n = q_n.reshape(d, NSA_GROUPS, NSA_REP, HEAD_DIM).transpose(0, 2, 1, 3).reshape(d, -1)
    gate_n = jnp.pad(gate_n, ((0, 0), (0, GATE_ROWS - gate_n.shape[1])))
    w_tok = jnp.concatenate([kc, vc], axis=1)
    w_feat = jnp.concatenate([q_n, q_m, k_m, ks, kw, v_m, vs, vw, gate_n], axis=1).T
    w_gate = jnp.concatenate([gate_a, gate_b], axis=1)
    return w_tok.astype(BF16), w_feat.astype(BF16), w_gate.astype(BF16)


def _compress_weights(w1, w2, pe):
    eye = jnp.eye(NSA_GROUPS, dtype=F32)
    halves = []
    for part in (w1[:CMP_STRIDE * HEAD_DIM], w1[CMP_STRIDE * HEAD_DIM:]):
        p3 = part.reshape(CMP_STRIDE, HEAD_DIM, CMP_HIDDEN)
        halves.append(jnp.einsum('idh,gk->igdkh', p3, eye)
                      .reshape(CMP_STRIDE * NSA_GROUPS * HEAD_DIM, NSA_GROUPS * CMP_HIDDEN))
    w_big = jnp.concatenate(halves, axis=1).astype(BF16)
    w2_bd = jnp.einsum('hd,kg->khgd', w2, eye).reshape(NSA_GROUPS * CMP_HIDDEN,
                                                        NSA_GROUPS * HEAD_DIM).astype(BF16)
    pe_rows = [jnp.broadcast_to(pe[a:a + CMP_STRIDE, None, :], (CMP_STRIDE, NSA_GROUPS, HEAD_DIM))
               .reshape(1, -1) for a in (0, CMP_STRIDE)]
    return w_big, w2_bd, pe_rows


def _overlap_matrix_t(n_cmp):
    c = np.arange(n_cmp)
    j = np.arange(SEL_BLOCKS_MAX)
    start, end = c * CMP_STRIDE, c * CMP_STRIDE + CMP_LEN - 1
    ov = (start[None, :] <= j[:, None] * SEL_BLOCK + SEL_BLOCK - 1) & (end[None, :] >= j[:, None] * SEL_BLOCK)
    return jnp.asarray(ov, dtype=BF16)


def kernel(x, p, positions, g_mix, w_in, nsa_q_gain, nsa_kc_gain, nsa_ks_gain, nsa_kw_gain, nsa_pe_k, nsa_pe_v, nsa_ck_w1, nsa_ck_w2, nsa_cv_w1, nsa_cv_w2, moba_q_gain, moba_k_gain, w_up_nsa, w_up_moba, w_out, g_ffn, w_ffn_in, w_ffn_out, g_ple, w_ple_gate, w_ple_proj):
    b, s, d = x.shape
    depth = w_in.shape[0]
    assert s % PAD_MULT == 0 and s % PROJ_TM == 0 and d == D_MODEL and V_TILE == MOBA_BLOCK
    t = b * s
    n_sub = s // CMP_STRIDE

    assert ROPE_HALF == SUBLANES
    posr = positions.reshape(t // PROJ_TM, 1, PROJ_TM)
    pos_end = jnp.concatenate([positions[:, CMP_LEN - 1::CMP_STRIDE], positions[:, -1:]], axis=1)[:, :, None]
    invf128 = _inv_freq_lanes(LANES)
    invf8 = _inv_freq()[:, None]
    bd128 = _block_diag_ones(LANES)
    ovl_t = _overlap_matrix_t(n_sub)
    tile = lambda g, n: jnp.tile(g, n)
    xi = x.reshape(t, d)

    for i in range(depth):
        w_tok, w_feat, w_gate = _split_w_in(w_in[i])
        g_feat = jnp.concatenate([
            jnp.concatenate([tile(nsa_q_gain[i], NSA_HEADS), tile(moba_q_gain[i], MOBA_HEADS)])
            * (ATTN_SCALE * LOG2E),
            tile(moba_k_gain[i], MOBA_HEADS), tile(nsa_ks_gain[i], NSA_GROUPS),
            tile(nsa_kw_gain[i], NSA_GROUPS)])[:, None]
        gmix = g_mix[i][None, :]
        (qn, qm, gn, vm, vs, vw, km, ks, kw, kc_raw, vc_raw) = _proj_call(
            xi, posr, gmix, w_tok, w_feat, g_feat, invf8, b, s)

        wk_big, w2k_bd, pe_k = _compress_weights(nsa_ck_w1[i], nsa_ck_w2[i], nsa_pe_k[i])
        wv_big, w2v_bd, pe_v = _compress_weights(nsa_cv_w1[i], nsa_cv_w2[i], nsa_pe_v[i])
        pe4 = jnp.concatenate(pe_k + pe_v, axis=0)
        kc, vct = _compress_call(kc_raw.reshape(b, s, KV_WIDTH), vc_raw.reshape(b, s, KV_WIDTH), pe4, wk_big, wv_big, w2k_bd, w2v_bd,
                                 tile(nsa_kc_gain[i], NSA_GROUPS)[None, :], pos_end,
                                 invf128, bd128)

        r3 = lambda a: a.reshape(b, s, a.shape[-1])
        y_nsa = _nsa_call(qn, kc, vct, r3(ks), vs.reshape(b, s // NSA_TK, KV_WIDTH, NSA_TK), r3(kw),
                          vw.reshape(b, s // WIN_TK, KV_WIDTH, WIN_TK), gn, ovl_t)
        y_moba = _moba_call(qm, r3(km), vm.reshape(b, s // MOBA_BLOCK, Q_WIDTH, MOBA_BLOCK))

        w_un = (w_up_nsa[i].reshape(NSA_GROUPS, NSA_REP, HEAD_DIM, d).transpose(1, 0, 2, 3)
                .reshape(NSA_HEADS * HEAD_DIM, d).astype(BF16))
        xi = _post_call(xi, y_nsa, y_moba, p[i].reshape(t, PLE_DIM), gmix, w_gate, w_un,
                        w_up_moba[i].astype(BF16), w_out[i].astype(BF16), g_ffn[i][None, :],
                        w_ffn_in[i].astype(BF16), w_ffn_out[i].astype(BF16), g_ple[i][None, :],
                        w_ple_gate[i].astype(BF16), w_ple_proj[i].astype(BF16))
    return xi.reshape(b, s, d)
```

```python
import jax
import jax.numpy as jnp
import numpy as np
from jax import lax
from jax.experimental import pallas as pl
from jax.experimental.pallas import tpu as pltpu

F32 = jnp.float32
BF16 = jnp.bfloat16

D_MODEL = 1024
HEAD_DIM = 64
ROPE_DIM = HEAD_DIM // 4
ROPE_HALF = ROPE_DIM // 2
ROPE_THETA = 500000.0
NORM_EPS = 1e-6
ATTN_SCALE = HEAD_DIM ** -0.5

NSA_HEADS = 8
NSA_GROUPS = 2
NSA_REP = NSA_HEADS // NSA_GROUPS
CMP_LEN = 32
CMP_STRIDE = 16
CMP_HIDDEN = 256
SEL_BLOCK = 64
SEL_TOPN = 8
WINDOW = 512

MOBA_HEADS = 8
MOBA_BLOCK = 256
MOBA_TOPK = 3

PAD_MULT = 256
D_FF = ((-(-8 * D_MODEL // 3)) + 255) // 256 * 256
PLE_DIM = 256

IN_SPLITS = ((NSA_HEADS * HEAD_DIM,) + (NSA_GROUPS * HEAD_DIM,) * 6 + (3 * NSA_HEADS,)
             + (MOBA_HEADS * HEAD_DIM,) * 3 + (D_MODEL, D_MODEL))
IN_CUTS = tuple(int(c) for c in np.cumsum(IN_SPLITS)[:-1])

LANES = 128
SUBLANES = 8
NEG_BIG = -1e30
LOG2E = 1.4426950408889634
SUM_ROWS = 16
VMEM_LIMIT = 56 * 1024 * 1024

Q_WIDTH = NSA_HEADS * HEAD_DIM
KV_WIDTH = NSA_GROUPS * HEAD_DIM
GATE_ROWS = 32
KEY_ROWS = Q_WIDTH + 2 * KV_WIDTH
FEAT_ROWS = 2 * Q_WIDTH + KEY_ROWS + Q_WIDTH + 2 * KV_WIDTH + GATE_ROWS

PROJ_TM = 512
V_TILE = 256
NSA_TQ = 256
NSA_TK = V_TILE
WIN_TK = V_TILE
SCORE_LANES = 256
POST_TM = 512
MXU_TILE = 256
FFN_CHUNKS = ((0, 6 * MXU_TILE), (6 * MXU_TILE, D_FF))
SEL_BLOCKS_MAX = 32
MOBA_BLOCKS_MAX = 8
MOBA_STEP_HEADS = 8
OWN_SUB = 64


def _dot(a, b):
    return jnp.dot(a, b, preferred_element_type=F32)


def _dot_nt(a, b):
    return lax.dot_general(a, b, (((1,), (1,)), ((), ())), preferred_element_type=F32)


def _dot_tn(a, b):
    return lax.dot_general(a, b, (((0,), (0,)), ((), ())), preferred_element_type=F32)


def _split_bf16(a_f32):
    hi = a_f32.astype(BF16)
    return hi, (a_f32 - hi.astype(F32)).astype(BF16)


def _sigmoid(x):
    return 1.0 / (1.0 + jnp.exp(-x))


def _rms_rows(x, g):
    return x * lax.rsqrt(jnp.mean(x * x, axis=-1, keepdims=True) + NORM_EPS) * g


def _head_norm(t, bd, gain):
    hi, lo = _split_bf16(t * t)
    ss = _dot(hi, bd) + _dot(lo, bd)
    return t * lax.rsqrt(ss * (1.0 / HEAD_DIM) + NORM_EPS) * gain


def _rope_tables(pos_col, invf):
    ang = pos_col.astype(F32) * invf
    cos_a, sin_a = jnp.cos(ang), jnp.sin(ang)
    d = lax.broadcasted_iota(jnp.int32, ang.shape, 1) & (HEAD_DIM - 1)
    s_lo = jnp.where(d < ROPE_HALF, -sin_a, 0.0)
    s_hi = jnp.where((d >= ROPE_HALF) & (d < ROPE_DIM), sin_a, 0.0)
    return cos_a, s_lo, s_hi


def _rope(y, tables):
    cos_a, s_lo, s_hi = tables
    w = y.shape[1]
    return (y * cos_a + pltpu.roll(y, w - ROPE_HALF, 1) * s_lo
            + pltpu.roll(y, ROPE_HALF, 1) * s_hi)


def _ones_rows(tk):
    r = lax.broadcasted_iota(jnp.int32, (SUM_ROWS, tk), 0)
    return jnp.where(r == 0, 1.0, 0.0).astype(BF16)


def _with_sum_row(v_rows, ones_rows):
    return jnp.concatenate([v_rows, ones_rows], axis=0)


def _slot_update(load_s, plan, m_ref, acc_ref, hs, v_aug):
    n_sub, n_lane = len(plan), len(plan[0])
    sub = v_aug.shape[1] // n_sub
    e_cols, alphas = [], []
    for h in range(n_lane):
        lanes = slice(h * LANES, (h + 1) * LANES)
        m = m_ref[hs, 0:1, lanes]
        m_new = m
        for a in range(n_sub):
            if plan[a][h] is None:
                continue
            kind, arg = plan[a][h]
            if kind == 'rows':
                top = jnp.max(load_s(a, h), axis=0, keepdims=True)
                m_new = jnp.maximum(m_new, top if arg is None else jnp.where(arg < 0.0, NEG_BIG, top))
            else:
                m_new = jnp.maximum(m_new, jnp.max(load_s(a, h) + arg, axis=0, keepdims=True))
        e = []
        for a in range(n_sub):
            if plan[a][h] is None:
                e.append(jnp.zeros((sub, LANES), BF16))
                continue
            kind, arg = plan[a][h]
            if kind == 'rows':
                shift = -m_new if arg is None else jnp.where(arg < 0.0, NEG_BIG, -m_new)
                e.append(jnp.exp2(load_s(a, h) + shift).astype(BF16))
            else:
                e.append(jnp.exp2(load_s(a, h) + arg - m_new).astype(BF16))
        e_cols.append(e[0] if n_sub == 1 else jnp.concatenate(e, axis=0))
        alphas.append(jnp.exp2(m - m_new))
        m_ref[hs, 0:1, lanes] = m_new
    e_all = e_cols[0] if n_lane == 1 else jnp.concatenate(e_cols, axis=1)
    alpha = alphas[0] if n_lane == 1 else jnp.concatenate(alphas, axis=1)
    acc_ref[hs] = alpha * acc_ref[hs] + _dot(v_aug, e_all)


def _visibility_plan(n_sub, sub, n_lane, visible, kept_row=None):
    plan = []
    for a in range(n_sub):
        row_plan = []
        for h in range(n_lane):
            k0, k1, q0, q1 = a * sub, (a + 1) * sub - 1, h * LANES, (h + 1) * LANES - 1
            corners = [visible(k, q) for k in (k0, k1) for q in (q0, q1)]
            row = None if kept_row is None else kept_row(a, h)
            if not any(corners):
                row_plan.append(None)
            elif all(corners):
                row_plan.append(('rows', row))
            else:
                k_off = k0 + lax.broadcasted_iota(jnp.int32, (sub, LANES), 0)
                q_off = q0 + lax.broadcasted_iota(jnp.int32, (sub, LANES), 1)
                keep = 0.0 if row is None else row
                row_plan.append(('tile', jnp.where(visible(k_off, q_off), keep, NEG_BIG)))
        plan.append(row_plan)
    return plan


def _reset_state(m_ref, acc_ref):
    m_ref[...] = jnp.full(m_ref.shape, NEG_BIG, F32)
    acc_ref[...] = jnp.zeros(acc_ref.shape, F32)


def _normalised(acc, scale_row=1.0):
    return acc[:HEAD_DIM] * (scale_row / jnp.maximum(acc[HEAD_DIM:HEAD_DIM + 1], 1e-30))


def _pick_top_rows(v, n_pick):
    n, q = v.shape
    slabs = [v[a:a + SUBLANES] for a in range(0, n, SUBLANES)]
    sub = lax.broadcasted_iota(jnp.int32, (SUBLANES, q), 0).astype(F32)
    row_id = [sub + float(a) for a in range(0, n, SUBLANES)]
    picked = [jnp.zeros((SUBLANES, q), F32) for _ in slabs]
    for _ in range(n_pick):
        top = slabs[0]
        for slab in slabs[1:]:
            top = jnp.maximum(top, slab)
        top = jnp.max(top, axis=0, keepdims=True)
        first = jnp.where(slabs[0] == top, row_id[0], float(n))
        for slab, rid in zip(slabs[1:], row_id[1:]):
            first = jnp.minimum(first, jnp.where(slab == top, rid, float(n)))
        first = jnp.min(first, axis=0, keepdims=True)
        hit = [rid == first for rid in row_id]
        picked = [jnp.where(h, 1.0, p) for h, p in zip(hit, picked)]
        slabs = [jnp.where(h, -jnp.inf, s) for h, s in zip(hit, slabs)]
    return picked[0] if len(picked) == 1 else jnp.concatenate(picked, axis=0)


def _rank_rows(v, n_rows):
    n, q = v.shape
    slabs = [v[a:a + SUBLANES] for a in range(0, n, SUBLANES)]
    ranks = [jnp.zeros((SUBLANES, q), F32) for _ in slabs]
    sub = lax.broadcasted_iota(jnp.int32, (SUBLANES, q), 0)
    for jp in range(n_rows):
        row = v[jp:jp + 1, :]
        for si, slab in enumerate(slabs):
            first = si * SUBLANES
            if first > jp:
                beats = jnp.where(row >= slab, 1.0, 0.0)
            elif first + SUBLANES - 1 < jp:
                beats = jnp.where(row > slab, 1.0, 0.0)
            else:
                ge = jnp.where(row >= slab, 1.0, 0.0)
                gt = jnp.where(row > slab, 1.0, 0.0)
                beats = gt + (ge - gt) * jnp.where(sub > jp - first, 1.0, 0.0)
            ranks[si] = ranks[si] + beats
    return ranks[0] if len(ranks) == 1 else jnp.concatenate(ranks, axis=0)


def _proj_kernel(x_ref, posr_ref, gmix_ref, wtok_ref, wfeat_ref, gfeat_ref, invf8_ref,
                 qn_ref, qm_ref, gn_ref, vm_ref, vs_ref, vw_ref, km_ref, ks_ref, kw_ref,
                 kc_ref, vc_ref):
    tm = x_ref.shape[0]
    h = _rms_rows(x_ref[...], gmix_ref[...]).astype(BF16)

    acc = _dot(h, wtok_ref[...])
    kc_ref[...] = acc[:, :KV_WIDTH]
    vc_ref[...] = acc[:, KV_WIDTH:]

    ang = invf8_ref[...] * posr_ref[0].astype(F32)
    cos_a, sin_a = jnp.cos(ang), jnp.sin(ang)

    def project(lo, hi):
        return _dot_nt(wfeat_ref[lo:hi, :], h)

    def normed_head(acc_g, row0, hh):
        t = acc_g[hh * HEAD_DIM:(hh + 1) * HEAD_DIM, :]
        ss = jnp.sum(t * t, axis=0, keepdims=True)
        gain = gfeat_ref[row0 + hh * HEAD_DIM:row0 + (hh + 1) * HEAD_DIM, :]
        y = t * lax.rsqrt(ss * (1.0 / HEAD_DIM) + NORM_EPS) * gain
        a, b = y[:ROPE_HALF], y[ROPE_HALF:ROPE_DIM]
        return jnp.concatenate([a * cos_a - b * sin_a, b * cos_a + a * sin_a, y[ROPE_DIM:]], axis=0)

    q_rows = 2 * Q_WIDTH
    for mixer, dst in enumerate((qn_ref, qm_ref)):
        acc_q = project(mixer * Q_WIDTH, (mixer + 1) * Q_WIDTH)
        for hh in range(NSA_HEADS):
            dst[0, hh * HEAD_DIM:(hh + 1) * HEAD_DIM, :] = normed_head(acc_q, mixer * Q_WIDTH, hh).astype(BF16)

    acc_k = project(q_rows, q_rows + KEY_ROWS)
    for pair in range(KEY_ROWS // LANES):
        y = jnp.concatenate([normed_head(acc_k, q_rows, 2 * pair), normed_head(acc_k, q_rows, 2 * pair + 1)],
                            axis=0)
        y = y.T.astype(BF16)
        if pair < Q_WIDTH // LANES:
            km_ref[:, pair * LANES:(pair + 1) * LANES] = y
        elif pair == Q_WIDTH // LANES:
            ks_ref[...] = y
        else:
            kw_ref[...] = y

    v_rows = q_rows + KEY_ROWS
    acc_vm = project(v_rows, v_rows + Q_WIDTH)
    acc_v = project(v_rows + Q_WIDTH, FEAT_ROWS)
    for j in range(tm // V_TILE):
        cols = slice(j * V_TILE, (j + 1) * V_TILE)
        vm_ref[j] = acc_vm[:, cols].astype(BF16)
        vs_ref[j] = acc_v[:KV_WIDTH, cols].astype(BF16)
        vw_ref[j] = acc_v[KV_WIDTH:2 * KV_WIDTH, cols].astype(BF16)
    gn_ref[0] = acc_v[2 * KV_WIDTH:]


def _proj_call(x2, posr, gmix, wtok, wfeat, gfeat, invf8, b, s):
    t = x2.shape[0]
    tm = PROJ_TM
    nt = s // tm
    row = lambda w_: pl.BlockSpec((tm, w_), lambda i: (i, 0))
    full = lambda a: pl.BlockSpec(a.shape, lambda i: (0,) * a.ndim, pipeline_mode=pl.Buffered(1))
    feat = lambda r: pl.BlockSpec((1, r, tm), lambda i: (i // nt, 0, i % nt))
    tile = lambda r: pl.BlockSpec((tm // V_TILE, r, V_TILE), lambda i: (i, 0, 0))
    out_shape = [
        jax.ShapeDtypeStruct((b, Q_WIDTH, s), BF16),
        jax.ShapeDtypeStruct((b, Q_WIDTH, s), BF16),
        jax.ShapeDtypeStruct((b, GATE_ROWS, s), F32),
        jax.ShapeDtypeStruct((t // V_TILE, Q_WIDTH, V_TILE), BF16),
        jax.ShapeDtypeStruct((t // V_TILE, KV_WIDTH, V_TILE), BF16),
        jax.ShapeDtypeStruct((t // V_TILE, KV_WIDTH, V_TILE), BF16),
        jax.ShapeDtypeStruct((t, Q_WIDTH), BF16),
        jax.ShapeDtypeStruct((t, KV_WIDTH), BF16),
        jax.ShapeDtypeStruct((t, KV_WIDTH), BF16),
        jax.ShapeDtypeStruct((t, KV_WIDTH), F32),
        jax.ShapeDtypeStruct((t, KV_WIDTH), F32),
    ]
    out_specs = [feat(Q_WIDTH), feat(Q_WIDTH), feat(GATE_ROWS), tile(Q_WIDTH), tile(KV_WIDTH),
                 tile(KV_WIDTH), row(Q_WIDTH), row(KV_WIDTH), row(KV_WIDTH), row(KV_WIDTH), row(KV_WIDTH)]
    return pl.pallas_call(
        _proj_kernel,
        grid=(t // tm,),
        in_specs=[row(D_MODEL), pl.BlockSpec((1, 1, tm), lambda i: (i, 0, 0)), full(gmix),
                  full(wtok), full(wfeat), full(gfeat), full(invf8)],
        out_specs=out_specs,
        out_shape=out_shape,
        compiler_params=pltpu.CompilerParams(dimension_semantics=("parallel",),
                                             vmem_limit_bytes=VMEM_LIMIT),
    )(x2, posr, gmix, wtok, wfeat, gfeat, invf8)


def _compress_kernel(kcr_ref, vcr_ref, pe_ref, wk_ref, wv_ref, w2k_ref, w2v_ref, gain_ref,
                     pose_ref, invf_ref, bd_ref, kc_out, vct_out):
    half = NSA_GROUPS * CMP_HIDDEN
    n_sub = kcr_ref.shape[1] // CMP_STRIDE

    def sub_blocks(ref):
        return jnp.concatenate([ref[0, pl.ds(i, n_sub, stride=CMP_STRIDE), :] for i in range(CMP_STRIDE)],
                               axis=1)

    def comp(x, pe_a, pe_b, w_ref, w2_ref):
        a = _dot((x + pe_a).astype(BF16), w_ref[:, :half])
        b = _dot((x + pe_b).astype(BF16), w_ref[:, half:])
        hid = a + pltpu.roll(b, b.shape[0] - 1, 0)
        act = hid * _sigmoid(hid)
        return _dot(act.astype(BF16), w2_ref[...])

    kc = comp(sub_blocks(kcr_ref), pe_ref[0:1, :], pe_ref[1:2, :], wk_ref, w2k_ref)
    vc = comp(sub_blocks(vcr_ref), pe_ref[2:3, :], pe_ref[3:4, :], wv_ref, w2v_ref)
    kc = _head_norm(kc, bd_ref[...], gain_ref[...])
    kc = _rope(kc, _rope_tables(pose_ref[0], invf_ref[...]))
    kc_out[0] = kc.astype(BF16)
    vct_out[0] = vc.T.astype(BF16)


def _compress_call(kc_raw, vc_raw, pe4, wk, wv, w2k, w2v, gain, pos_end, invf, bd):
    b, s, width = kc_raw.shape
    n_sub = s // CMP_STRIDE
    blk = lambda shp: pl.BlockSpec((1,) + shp, lambda i: (i, 0, 0))
    full = lambda a: pl.BlockSpec(a.shape, lambda i: (0,) * a.ndim)
    return pl.pallas_call(
        _compress_kernel,
        grid=(b,),
        in_specs=[blk((s, width)), blk((s, width)), full(pe4), full(wk), full(wv),
                  full(w2k), full(w2v), full(gain), blk((n_sub, 1)), full(invf), full(bd)],
        out_specs=[blk((n_sub, KV_WIDTH)), blk((KV_WIDTH, n_sub))],
        out_shape=[jax.ShapeDtypeStruct((b, n_sub, KV_WIDTH), BF16),
                   jax.ShapeDtypeStruct((b, KV_WIDTH, n_sub), BF16)],
        compiler_params=pltpu.CompilerParams(dimension_semantics=("parallel",),
                                             vmem_limit_bytes=VMEM_LIMIT),
    )(kc_raw, vc_raw, pe4, wk, wv, w2k, w2v, gain, pos_end, invf, bd)


def _nsa_kernel(q_ref, kc_ref, vct_ref, ks_ref, vs_ref, kw_ref, vw_ref, gate_ref, ovl_ref, out_ref,
                sel_scr, lhs_scr, m_scr, acc_scr, o_scr, *bufs):
    tq = q_ref.shape[2]
    n_cmp = kc_ref.shape[1]
    n_slots = NSA_GROUPS * NSA_REP
    n_units = len(bufs) // 2
    s_scr, c_scr = bufs[:n_units], bufs[n_units:]
    per_unit = n_slots // n_units
    i = pl.program_id(1)
    t0 = i * tq
    t_row = t0 + lax.broadcasted_iota(jnp.int32, (1, tq), 1)

    zero = jnp.zeros((HEAD_DIM, tq), BF16)
    for g in range(NSA_GROUPS):
        for r in range(NSA_REP):
            hs = g * NSA_REP + r
            rows = q_ref[0, r * KV_WIDTH + g * HEAD_DIM:r * KV_WIDTH + (g + 1) * HEAD_DIM, :]
            lhs_scr[:, hs * tq:(hs + 1) * tq] = jnp.concatenate([rows, zero] if g == 0 else [zero, rows],
                                                                axis=0)
    gsig = _sigmoid(gate_ref[0])

    def unit_queries(unit):
        return lhs_scr[:, unit * SCORE_LANES:(unit + 1) * SCORE_LANES]

    def scores_into(unit, k_tile):
        tk = k_tile.shape[0]
        s_scr[unit][0:tk, :] = _dot(k_tile, unit_queries(unit))

    def sel_keys(kt):
        return ks_ref[0, pl.ds(pl.multiple_of(kt * NSA_TK, NSA_TK), NSA_TK), :]

    def win_keys(kt):
        return kw_ref[0, pl.ds(pl.multiple_of(kt * WIN_TK, WIN_TK), WIN_TK), :]

    n_lane = tq // LANES

    def run_tile(v_tile, plan_of_group, next_keys):
        ones_rows = _ones_rows(v_tile.shape[1])
        plans = [plan_of_group(g) for g in range(NSA_GROUPS)]
        for unit in range(n_units):
            g = (unit * per_unit) // NSA_REP
            sub = v_tile.shape[1] // len(plans[g])
            v_aug = _with_sum_row(v_tile[g * HEAD_DIM:(g + 1) * HEAD_DIM, :], ones_rows)
            for hh in range(per_unit):
                def load(a, h, hh=hh, unit=unit, sub=sub):
                    return s_scr[unit][a * sub:(a + 1) * sub, hh * tq + h * LANES:hh * tq + (h + 1) * LANES]

                _slot_update(load, plans[g], m_scr, acc_scr, unit * per_unit + hh, v_aug)
            if next_keys is not None:
                scores_into(unit, next_keys())

    def fold_branch(branch):
        for hs in range(n_slots):
            o_scr[hs] = o_scr[hs] + _normalised(acc_scr[hs], gsig[3 * hs + branch:3 * hs + branch + 1, :])

    blocks_per_tile = NSA_TK // SEL_BLOCK
    first_win = jnp.maximum((t0 - (WINDOW - 1)) // WIN_TK, 0)

    def sel_row(kt, g, a, h):
        return sel_scr[g, pl.ds(kt * blocks_per_tile + a, 1), :][:, h * LANES:(h + 1) * LANES]

    def sel_plan(kt, g):
        return [[('rows', sel_row(kt, g, a, h)) for h in range(n_lane)] for a in range(blocks_per_tile)]

    def causal(k_off, q_off):
        return k_off <= q_off

    def sel_body(kt, carry):
        run_tile(vs_ref[0, kt], lambda g: sel_plan(kt, g), lambda: sel_keys(kt + 1))
        return carry

    def win_tile(kt, visible, next_keys):
        run_tile(vw_ref[0, kt], lambda g: _visibility_plan(blocks_per_tile, SEL_BLOCK, n_lane, visible),
                 next_keys)

    for unit in range(n_units):
        c_scr[unit][...] = _dot(kc_ref[0], unit_queries(unit))
    for unit in range(n_units):
        scores_into(unit, win_keys(first_win))

    c_idx = lax.broadcasted_iota(jnp.int32, (n_cmp, tq), 0)
    valid_c = (c_idx * CMP_STRIDE + (CMP_LEN - 1) <= t_row) & (c_idx < n_cmp - 1)
    bias_c = jnp.where(valid_c, 0.0, NEG_BIG)
    p_sum = []
    for hs in range(n_slots):
        g = hs // NSA_REP
        unit, hh = divmod(hs, per_unit)
        x = c_scr[unit][:, hh * tq:(hh + 1) * tq] + bias_c
        m = jnp.max(x, axis=0, keepdims=True)
        e = jnp.where(valid_c, jnp.exp2(x - m), 0.0)
        p = e / jnp.maximum(jnp.sum(e, axis=0, keepdims=True), 1e-30)
        o_c = _dot(vct_ref[0, g * HEAD_DIM:(g + 1) * HEAD_DIM, :], p.astype(BF16))
        o_scr[hs] = o_c * gsig[3 * hs:3 * hs + 1, :]
        if hs % NSA_REP == 0:
            p_sum.append(p)
        else:
            p_sum[g] = p_sum[g] + p

    imp = []
    for g in range(NSA_GROUPS):
        hi, lo = _split_bf16(p_sum[g])
        imp.append(_dot(ovl_ref[...], hi) + _dot(ovl_ref[...], lo))

    _reset_state(m_scr, acc_scr)

    @pl.when(i >= 2)
    def _():
        win_tile(i - 2, lambda k_off, q_off: q_off < k_off, lambda: win_keys(i - 1))

    @pl.when(i >= 1)
    def _():
        win_tile(i - 1, lambda k_off, q_off: True, lambda: win_keys(i))

    cur = t_row // SEL_BLOCK
    j_idx = lax.broadcasted_iota(jnp.int32, (SEL_BLOCKS_MAX, tq), 0)
    forced = (j_idx == 0) | (j_idx == cur) | (j_idx == cur - 1)
    excluded = forced | (j_idx > cur)
    for g in range(NSA_GROUPS):
        picked = _pick_top_rows(jnp.where(excluded, -jnp.inf, imp[g]), SEL_TOPN - 3)
        sel_scr[g] = jnp.where(forced | (picked > 0.5), 0.0, NEG_BIG)

    win_tile(i, causal, lambda: sel_keys(0))
    fold_branch(2)

    last_sel = (t0 + tq - 1) // NSA_TK
    _reset_state(m_scr, acc_scr)
    lax.fori_loop(0, last_sel, sel_body, 0)
    run_tile(vs_ref[0, last_sel],
             lambda g: _visibility_plan(blocks_per_tile, SEL_BLOCK, n_lane, causal,
                                        lambda a, h: sel_row(last_sel, g, a, h)),
             None)
    fold_branch(1)

    for r in range(NSA_REP):
        out_ref[0, r * KV_WIDTH:r * KV_WIDTH + HEAD_DIM, :] = o_scr[r].astype(BF16)
        out_ref[0, r * KV_WIDTH + HEAD_DIM:(r + 1) * KV_WIDTH, :] = o_scr[NSA_REP + r].astype(BF16)


def _nsa_call(qn, kc, vct, ks, vs, kw, vw, gn, ovl):
    b, _, s = qn.shape
    tq = NSA_TQ
    n_slots = NSA_GROUPS * NSA_REP
    n_units = n_slots * tq // SCORE_LANES
    assert s // SEL_BLOCK <= SEL_BLOCKS_MAX and tq == WIN_TK == NSA_TK and WINDOW == 2 * WIN_TK
    seq = lambda a: pl.BlockSpec((1,) + a.shape[1:], lambda bi, i: (bi,) + (0,) * (a.ndim - 1))
    full = lambda a: pl.BlockSpec(a.shape, lambda bi, i: (0,) * a.ndim)
    return pl.pallas_call(
        _nsa_kernel,
        grid=(b, s // tq),
        in_specs=[pl.BlockSpec((1, Q_WIDTH, tq), lambda bi, i: (bi, 0, i)),
                  seq(kc), seq(vct), seq(ks), seq(vs), seq(kw), seq(vw),
                  pl.BlockSpec((1, GATE_ROWS, tq), lambda bi, i: (bi, 0, i)), full(ovl)],
        out_specs=pl.BlockSpec((1, Q_WIDTH, tq), lambda bi, i: (bi, 0, i)),
        out_shape=jax.ShapeDtypeStruct((b, Q_WIDTH, s), BF16),
        scratch_shapes=[pltpu.VMEM((NSA_GROUPS, SEL_BLOCKS_MAX, tq), F32),
                        pltpu.VMEM((KV_WIDTH, n_slots * tq), BF16),
                        pltpu.VMEM((n_slots, SUBLANES, tq), F32),
                        pltpu.VMEM((n_slots, HEAD_DIM + SUM_ROWS, tq), F32),
                        pltpu.VMEM((n_slots, HEAD_DIM, tq), F32)]
        + [pltpu.VMEM((max(NSA_TK, WIN_TK), SCORE_LANES), F32)] * n_units
        + [pltpu.VMEM((kc.shape[1], SCORE_LANES), F32)] * n_units,
        compiler_params=pltpu.CompilerParams(dimension_semantics=("parallel", "parallel"),
                                             vmem_limit_bytes=VMEM_LIMIT),
    )(qn, kc, vct, ks, vs, kw, vw, gn, ovl)


def _moba_kernel(q_ref, k_ref, v_ref, out_ref, km_scr, sel_scr, lhs_scr, m_scr, acc_scr, *s_scr):
    bs = MOBA_BLOCK
    n_blocks = k_ref.shape[1] // bs
    n_slots = len(s_scr)
    i = pl.program_id(2)

    @pl.when(i == 0)
    def _():
        km_scr[...] = jnp.zeros(km_scr.shape, F32)
        for n in range(n_blocks):
            km_scr[n:n + 1, :] = jnp.mean(k_ref[0, n * bs:(n + 1) * bs, :].astype(F32), axis=0,
                                          keepdims=True)

    def pair_lanes(hs):
        return slice((hs // 2) * LANES, (hs // 2 + 1) * LANES)

    zero = jnp.zeros((HEAD_DIM, bs), BF16)
    n_idx = lax.broadcasted_iota(jnp.int32, (MOBA_BLOCKS_MAX, bs), 0)
    for hs in range(n_slots):
        rows = q_ref[0, hs * HEAD_DIM:(hs + 1) * HEAD_DIM, :]
        lhs_scr[hs] = jnp.concatenate([rows, zero] if hs % 2 == 0 else [zero, rows], axis=0)

    def scores_into(hs, n):
        k_tile = k_ref[0, pl.ds(pl.multiple_of(n * bs, bs), bs), pair_lanes(hs)]
        s_scr[hs][...] = _dot(k_tile, lhs_scr[hs])

    block_scores = []
    for hs in range(n_slots):
        km_hi, km_lo = _split_bf16(km_scr[:, pair_lanes(hs)])
        block_scores.append(_dot(km_hi, lhs_scr[hs]) + _dot(km_lo, lhs_scr[hs]))
    for hs in range(n_slots):
        scores_into(hs, i)

    ones_rows = _ones_rows(bs)
    n_lane = bs // LANES

    def run_block(n, plan_of_slot, next_block):
        for hs in range(n_slots):
            plan = plan_of_slot(hs)
            sub = bs // len(plan)
            v_aug = _with_sum_row(v_ref[0, n, hs * HEAD_DIM:(hs + 1) * HEAD_DIM, :], ones_rows)

            def load(a, h, hs=hs, sub=sub):
                return s_scr[hs][a * sub:(a + 1) * sub, h * LANES:(h + 1) * LANES]

            _slot_update(load, plan, m_scr, acc_scr, hs, v_aug)
            if next_block is not None:
                scores_into(hs, next_block)

    _reset_state(m_scr, acc_scr)
    for hs in range(n_slots):
        sc = jnp.where(n_idx < i, block_scores[hs], -jnp.inf)
        sel_scr[hs] = jnp.where((_rank_rows(sc, MOBA_BLOCKS_MAX) < MOBA_TOPK) & (n_idx < i), 0.0, NEG_BIG)
    own_plan = _visibility_plan(bs // OWN_SUB, OWN_SUB, n_lane, lambda k_off, q_off: k_off <= q_off)
    run_block(i, lambda hs: own_plan, i * 0)

    def past_plan(n):
        return lambda hs: [[('rows', sel_scr[hs, pl.ds(n, 1), :][:, h * LANES:(h + 1) * LANES])
                            for h in range(n_lane)]]

    def past_body(n, carry):
        run_block(n, past_plan(n), n + 1)
        return carry

    lax.fori_loop(0, i - 1, past_body, 0)

    @pl.when(i >= 1)
    def _():
        run_block(i - 1, past_plan(i - 1), None)

    for hs in range(n_slots):
        out_ref[0, hs * HEAD_DIM:(hs + 1) * HEAD_DIM, :] = _normalised(acc_scr[hs]).astype(BF16)


def _moba_call(qm, km, vm):
    b, w, s = qm.shape
    bs = MOBA_BLOCK
    nb = s // bs
    wb = MOBA_STEP_HEADS * HEAD_DIM
    assert nb <= MOBA_BLOCKS_MAX and w % wb == 0
    return pl.pallas_call(
        _moba_kernel,
        grid=(b, w // wb, nb),
        in_specs=[pl.BlockSpec((1, wb, bs), lambda bi, p, i: (bi, p, i)),
                  pl.BlockSpec((1, s, wb), lambda bi, p, i: (bi, 0, p)),
                  pl.BlockSpec((1, nb, wb, bs), lambda bi, p, i: (bi, 0, p, 0))],
        out_specs=pl.BlockSpec((1, wb, bs), lambda bi, p, i: (bi, p, i)),
        out_shape=jax.ShapeDtypeStruct((b, w, s), BF16),
        scratch_shapes=[pltpu.VMEM((MOBA_BLOCKS_MAX, wb), F32),
                        pltpu.VMEM((MOBA_STEP_HEADS, MOBA_BLOCKS_MAX, bs), F32),
                        pltpu.VMEM((MOBA_STEP_HEADS, LANES, bs), BF16),
                        pltpu.VMEM((MOBA_STEP_HEADS, SUBLANES, bs), F32),
                        pltpu.VMEM((MOBA_STEP_HEADS, HEAD_DIM + SUM_ROWS, bs), F32)]
        + [pltpu.VMEM((bs, bs), F32)] * MOBA_STEP_HEADS,
        compiler_params=pltpu.CompilerParams(
            dimension_semantics=("parallel", "parallel", "arbitrary"), vmem_limit_bytes=VMEM_LIMIT),
    )(qm, km, vm)


def _post_kernel(x_ref, yn_ref, ym_ref, p_ref, gmix_ref, wg_ref, wun_ref, wum_ref, wo_ref,
                 gffn_ref, wfi_ref, wfo_ref, gple_ref, wpg_ref, wpp_ref, out_ref):
    x = x_ref[...]
    h = _rms_rows(x, gmix_ref[...]).astype(BF16)
    ga = _sigmoid(_dot(h, wg_ref[:, :D_MODEL]))
    gb = _sigmoid(_dot(h, wg_ref[:, D_MODEL:]))
    merged = ga * _dot_tn(yn_ref[0], wun_ref[...]) + gb * _dot_tn(ym_ref[0], wum_ref[...])
    x = x + _dot(merged.astype(BF16), wo_ref[...])

    h = _rms_rows(x, gffn_ref[...]).astype(BF16)
    y = x
    for lo, hi in FFN_CHUNKS:
        gate = _dot(h, wfi_ref[:, lo:hi])
        up = _dot(h, wfi_ref[:, D_FF + lo:D_FF + hi])
        act = (gate * _sigmoid(gate) * up).astype(BF16)
        y = y + _dot(act, wfo_ref[lo:hi, :])

    h2 = _rms_rows(y, gple_ref[...]).astype(BF16)
    ple_gate = _sigmoid(_dot(h2, wpg_ref[...]))
    out_ref[...] = y + ple_gate * _dot(p_ref[...].astype(BF16), wpp_ref[...])


def _post_call(x2, yn, ym, p2, gmix, wg, wun, wum, wo, gffn, wfi, wfo, gple, wpg, wpp):
    t = x2.shape[0]
    tm = POST_TM
    nt = yn.shape[2] // tm
    row = lambda w_: pl.BlockSpec((tm, w_), lambda i: (i, 0))
    feat = pl.BlockSpec((1, Q_WIDTH, tm), lambda i: (i // nt, 0, i % nt))
    full = lambda a: pl.BlockSpec(a.shape, lambda i: (0, 0), pipeline_mode=pl.Buffered(1))
    weights = (gmix, wg, wun, wum, wo, gffn, wfi, wfo, gple, wpg, wpp)
    return pl.pallas_call(
        _post_kernel,
        grid=(t // tm,),
        in_specs=[row(D_MODEL), feat, feat, row(PLE_DIM)] + [full(w) for w in weights],
        out_specs=row(D_MODEL),
        out_shape=jax.ShapeDtypeStruct((t, D_MODEL), F32),
        compiler_params=pltpu.CompilerParams(dimension_semantics=("parallel",),
                                             vmem_limit_bytes=VMEM_LIMIT),
    )(x2, yn, ym, p2, *weights)


def _block_diag_ones(width):
    idx = np.arange(width) // HEAD_DIM
    return jnp.asarray(idx[:, None] == idx[None, :], dtype=BF16)


def _inv_freq():
    return ROPE_THETA ** (-jnp.arange(ROPE_HALF, dtype=F32) / ROPE_HALF)


def _inv_freq_lanes(width):
    inv_freq = _inv_freq()
    per_head = jnp.concatenate([inv_freq, inv_freq, jnp.zeros((HEAD_DIM - ROPE_DIM,), F32)])
    return jnp.tile(per_head, width // HEAD_DIM)[None, :]


def _split_w_in(w):
    parts = jnp.split(w, IN_CUTS, axis=-1)
    (q_n, kc, vc, ks, vs, kw, vw, gate_n, q_m, k_m, v_m, gate_a, gate_b) = parts
    d = w.shape[0]
    q_n = q_n.reshape(d, NSA_GROUPS, NSA_REP, HEAD_DIM).transpose(0, 2, 1, 3).reshape(d, -1)
    gate_n = jnp.pad(gate_n, ((0, 0), (0, GATE_ROWS - gate_n.shape[1])))
    w_tok = jnp.concatenate([kc, vc], axis=1)
    w_feat = jnp.concatenate([q_n, q_m, k_m, ks, kw, v_m, vs, vw, gate_n], axis=1).T
    w_gate = jnp.concatenate([gate_a, gate_b], axis=1)
    return w_tok.astype(BF16), w_feat.astype(BF16), w_gate.astype(BF16)


def _compress_weights(w1, w2, pe):
    eye = jnp.eye(NSA_GROUPS, dtype=F32)
    halves = []
    for part in (w1[:CMP_STRIDE * HEAD_DIM], w1[CMP_STRIDE * HEAD_DIM:]):
        p3 = part.reshape(CMP_STRIDE, HEAD_DIM, CMP_HIDDEN)
        halves.append(jnp.einsum('idh,gk->igdkh', p3, eye)
                      .reshape(CMP_STRIDE * NSA_GROUPS * HEAD_DIM, NSA_GROUPS * CMP_HIDDEN))
    w_big = jnp.concatenate(halves, axis=1).astype(BF16)
    w2_bd = jnp.einsum('hd,kg->khgd', w2, eye).reshape(NSA_GROUPS * CMP_HIDDEN,
                                                        NSA_GROUPS * HEAD_DIM).astype(BF16)
    pe_rows = [jnp.broadcast_to(pe[a:a + CMP_STRIDE, None, :], (CMP_STRIDE, NSA_GROUPS, HEAD_DIM))
               .reshape(1, -1) for a in (0, CMP_STRIDE)]
    return w_big, w2_bd, pe_rows


def _overlap_matrix_t(n_cmp):
    c = np.arange(n_cmp)
    j = np.arange(SEL_BLOCKS_MAX)
    start, end = c * CMP_STRIDE, c * CMP_STRIDE + CMP_LEN - 1
    ov = (start[None, :] <= j[:, None] * SEL_BLOCK + SEL_BLOCK - 1) & (end[None, :] >= j[:, None] * SEL_BLOCK)
    return jnp.asarray(ov, dtype=BF16)


def kernel(x, p, positions, g_mix, w_in, nsa_q_gain, nsa_kc_gain, nsa_ks_gain, nsa_kw_gain, nsa_pe_k, nsa_pe_v, nsa_ck_w1, nsa_ck_w2, nsa_cv_w1, nsa_cv_w2, moba_q_gain, moba_k_gain, w_up_nsa, w_up_moba, w_out, g_ffn, w_ffn_in, w_ffn_out, g_ple, w_ple_gate, w_ple_proj):
    b, s, d = x.shape
    depth = w_in.shape[0]
    assert s % PAD_MULT == 0 and s % PROJ_TM == 0 and d == D_MODEL and V_TILE == MOBA_BLOCK
    t = b * s
    n_sub = s // CMP_STRIDE

    assert ROPE_HALF == SUBLANES
    posr = positions.reshape(t // PROJ_TM, 1, PROJ_TM)
    pos_end = jnp.concatenate([positions[:, CMP_LEN - 1::CMP_STRIDE], positions[:, -1:]], axis=1)[:, :, None]
    invf128 = _inv_freq_lanes(LANES)
    invf8 = _inv_freq()[:, None]
    bd128 = _block_diag_ones(LANES)
    ovl_t = _overlap_matrix_t(n_sub)
    tile = lambda g, n: jnp.tile(g, n)
    xi = x.reshape(t, d)

    for i in range(depth):
        w_tok, w_feat, w_gate = _split_w_in(w_in[i])
        g_feat = jnp.concatenate([
            jnp.concatenate([tile(nsa_q_gain[i], NSA_HEADS), tile(moba_q_gain[i], MOBA_HEADS)])
            * (ATTN_SCALE * LOG2E),
            tile(moba_k_gain[i], MOBA_HEADS), tile(nsa_ks_gain[i], NSA_GROUPS),
            tile(nsa_kw_gain[i], NSA_GROUPS)])[:, None]
        gmix = g_mix[i][None, :]
        (qn, qm, gn, vm, vs, vw, km, ks, kw, kc_raw, vc_raw) = _proj_call(
            xi, posr, gmix, w_tok, w_feat, g_feat, invf8, b, s)

        wk_big, w2k_bd, pe_k = _compress_weights(nsa_ck_w1[i], nsa_ck_w2[i], nsa_pe_k[i])
        wv_big, w2v_bd, pe_v = _compress_weights(nsa_cv_w1[i], nsa_cv_w2[i], nsa_pe_v[i])
        pe4 = jnp.concatenate(pe_k + pe_v, axis=0)
        kc, vct = _compress_call(kc_raw.reshape(b, s, KV_WIDTH), vc_raw.reshape(b, s, KV_WIDTH), pe4, wk_big, wv_big, w2k_bd, w2v_bd,
                                 tile(nsa_kc_gain[i], NSA_GROUPS)[None, :], pos_end,
                                 invf128, bd128)

        r3 = lambda a: a.reshape(b, s, a.shape[-1])
        y_nsa = _nsa_call(qn, kc, vct, r3(ks), vs.reshape(b, s // NSA_TK, KV_WIDTH, NSA_TK), r3(kw),
                          vw.reshape(b, s // WIN_TK, KV_WIDTH, WIN_TK), gn, ovl_t)
        y_moba = _moba_call(qm, r3(km), vm.reshape(b, s // MOBA_BLOCK, Q_WIDTH, MOBA_BLOCK))

        w_un = (w_up_nsa[i].reshape(NSA_GROUPS, NSA_REP, HEAD_DIM, d).transpose(1, 0, 2, 3)
                .reshape(NSA_HEADS * HEAD_DIM, d).astype(BF16))
        xi = _post_call(xi, y_nsa, y_moba, p[i].reshape(t, PLE_DIM), gmix, w_gate, w_un,
                        w_up_moba[i].astype(BF16), w_out[i].astype(BF16), g_ffn[i][None, :],
                        w_ffn_in[i].astype(BF16), w_ffn_out[i].astype(BF16), g_ple[i][None, :],
                        w_ple_gate[i].astype(BF16), w_ple_proj[i].astype(BF16))
    return xi.reshape(b, s, d)
```

```python
import jax
import jax.numpy as jnp
import numpy as np
from jax import lax
from jax.experimental import pallas as pl
from jax.experimental.pallas import tpu as pltpu

F32 = jnp.float32
BF16 = jnp.bfloat16

D_MODEL = 1024
HEAD_DIM = 64
ROPE_DIM = HEAD_DIM // 4
ROPE_HALF = ROPE_DIM // 2
ROPE_THETA = 500000.0
NORM_EPS = 1e-6
ATTN_SCALE = HEAD_DIM ** -0.5

NSA_HEADS = 8
NSA_GROUPS = 2
NSA_REP = NSA_HEADS // NSA_GROUPS
CMP_LEN = 32
CMP_STRIDE = 16
CMP_HIDDEN = 256
SEL_BLOCK = 64
SEL_TOPN = 8
WINDOW = 512

MOBA_HEADS = 8
MOBA_BLOCK = 256
MOBA_TOPK = 3

PAD_MULT = 256
D_FF = ((-(-8 * D_MODEL // 3)) + 255) // 256 * 256
PLE_DIM = 256

IN_SPLITS = ((NSA_HEADS * HEAD_DIM,) + (NSA_GROUPS * HEAD_DIM,) * 6 + (3 * NSA_HEADS,)
             + (MOBA_HEADS * HEAD_DIM,) * 3 + (D_MODEL, D_MODEL))
IN_CUTS = tuple(int(c) for c in np.cumsum(IN_SPLITS)[:-1])

LANES = 128
SUBLANES = 8
NEG_BIG = -1e30
LOG2E = 1.4426950408889634
SUM_ROWS = 16
VMEM_LIMIT = 56 * 1024 * 1024

Q_WIDTH = NSA_HEADS * HEAD_DIM
KV_WIDTH = NSA_GROUPS * HEAD_DIM
GATE_ROWS = 32
KEY_ROWS = Q_WIDTH + 2 * KV_WIDTH
FEAT_ROWS = 2 * Q_WIDTH + KEY_ROWS + Q_WIDTH + 2 * KV_WIDTH + GATE_ROWS

PROJ_TM = 512
V_TILE = 256
NSA_TQ = 256
NSA_TK = V_TILE
WIN_TK = V_TILE
SCORE_LANES = 256
POST_TM = 512
MXU_TILE = 256
FFN_CHUNKS = ((0, 6 * MXU_TILE), (6 * MXU_TILE, D_FF))
SEL_BLOCKS_MAX = 32
MOBA_BLOCKS_MAX = 8
MOBA_STEP_HEADS = 8
OWN_SUB = 64


def _dot(a, b):
    return jnp.dot(a, b, preferred_element_type=F32)


def _dot_nt(a, b):
    return lax.dot_general(a, b, (((1,), (1,)), ((), ())), preferred_element_type=F32)


def _dot_tn(a, b):
    return lax.dot_general(a, b, (((0,), (0,)), ((), ())), preferred_element_type=F32)


def _split_bf16(a_f32):
    hi = a_f32.astype(BF16)
    return hi, (a_f32 - hi.astype(F32)).astype(BF16)


def _sigmoid(x):
    return 1.0 / (1.0 + jnp.exp(-x))


def _rms_rows(x, g):
    return x * lax.rsqrt(jnp.mean(x * x, axis=-1, keepdims=True) + NORM_EPS) * g


def _head_norm(t, bd, gain):
    hi, lo = _split_bf16(t * t)
    ss = _dot(hi, bd) + _dot(lo, bd)
    return t * lax.rsqrt(ss * (1.0 / HEAD_DIM) + NORM_EPS) * gain


def _rope_tables(pos_col, invf):
    ang = pos_col.astype(F32) * invf
    cos_a, sin_a = jnp.cos(ang), jnp.sin(ang)
    d = lax.broadcasted_iota(jnp.int32, ang.shape, 1) & (HEAD_DIM - 1)
    s_lo = jnp.where(d < ROPE_HALF, -sin_a, 0.0)
    s_hi = jnp.where((d >= ROPE_HALF) & (d < ROPE_DIM), sin_a, 0.0)
    return cos_a, s_lo, s_hi


def _rope(y, tables):
    cos_a, s_lo, s_hi = tables
    w = y.shape[1]
    return (y * cos_a + pltpu.roll(y, w - ROPE_HALF, 1) * s_lo
            + pltpu.roll(y, ROPE_HALF, 1) * s_hi)


def _ones_rows(tk):
    r = lax.broadcasted_iota(jnp.int32, (SUM_ROWS, tk), 0)
    return jnp.where(r == 0, 1.0, 0.0).astype(BF16)


def _with_sum_row(v_rows, ones_rows):
    return jnp.concatenate([v_rows, ones_rows], axis=0)


def _slot_update(load_s, plan, m_ref, acc_ref, hs, v_aug):
    n_sub, n_lane = len(plan), len(plan[0])
    sub = v_aug.shape[1] // n_sub
    e_cols, alphas = [], []
    for h in range(n_lane):
        lanes = slice(h * LANES, (h + 1) * LANES)
        m = m_ref[hs, 0:1, lanes]
        m_new = m
        for a in range(n_sub):
            if plan[a][h] is None:
                continue
            kind, arg = plan[a][h]
            if kind == 'rows':
                top = jnp.max(load_s(a, h), axis=0, keepdims=True)
                m_new = jnp.maximum(m_new, top if arg is None else jnp.where(arg < 0.0, NEG_BIG, top))
            else:
                m_new = jnp.maximum(m_new, jnp.max(load_s(a, h) + arg, axis=0, keepdims=True))
        e = []
        for a in range(n_sub):
            if plan[a][h] is None:
                e.append(jnp.zeros((sub, LANES), BF16))
                continue
            kind, arg = plan[a][h]
            if kind == 'rows':
                shift = -m_new if arg is None else jnp.where(arg < 0.0, NEG_BIG, -m_new)
                e.append(jnp.exp2(load_s(a, h) + shift).astype(BF16))
            else:
                e.append(jnp.exp2(load_s(a, h) + arg - m_new).astype(BF16))
        e_cols.append(e[0] if n_sub == 1 else jnp.concatenate(e, axis=0))
        alphas.append(jnp.exp2(m - m_new))
        m_ref[hs, 0:1, lanes] = m_new
    e_all = e_cols[0] if n_lane == 1 else jnp.concatenate(e_cols, axis=1)
    alpha = alphas[0] if n_lane == 1 else jnp.concatenate(alphas, axis=1)
    acc_ref[hs] = alpha * acc_ref[hs] + _dot(v_aug, e_all)


def _visibility_plan(n_sub, sub, n_lane, visible, kept_row=None):
    plan = []
    for a in range(n_sub):
        row_plan = []
        for h in range(n_lane):
            k0, k1, q0, q1 = a * sub, (a + 1) * sub - 1, h * LANES, (h + 1) * LANES - 1
            corners = [visible(k, q) for k in (k0, k1) for q in (q0, q1)]
            row = None if kept_row is None else kept_row(a, h)
            if not any(corners):
                row_plan.append(None)
            elif all(corners):
                row_plan.append(('rows', row))
            else:
                k_off = k0 + lax.broadcasted_iota(jnp.int32, (sub, LANES), 0)
                q_off = q0 + lax.broadcasted_iota(jnp.int32, (sub, LANES), 1)
                keep = 0.0 if row is None else row
                row_plan.append(('tile', jnp.where(visible(k_off, q_off), keep, NEG_BIG)))
        plan.append(row_plan)
    return plan


def _reset_state(m_ref, acc_ref):
    m_ref[...] = jnp.full(m_ref.shape, NEG_BIG, F32)
    acc_ref[...] = jnp.zeros(acc_ref.shape, F32)


def _normalised(acc, scale_row=1.0):
    return acc[:HEAD_DIM] * (scale_row / jnp.maximum(acc[HEAD_DIM:HEAD_DIM + 1], 1e-30))


def _pick_top_rows(v, n_pick):
    n, q = v.shape
    slabs = [v[a:a + SUBLANES] for a in range(0, n, SUBLANES)]
    sub = lax.broadcasted_iota(jnp.int32, (SUBLANES, q), 0).astype(F32)
    row_id = [sub + float(a) for a in range(0, n, SUBLANES)]
    picked = [jnp.zeros((SUBLANES, q), F32) for _ in slabs]
    for _ in range(n_pick):
        top = slabs[0]
        for slab in slabs[1:]:
            top = jnp.maximum(top, slab)
        top = jnp.max(top, axis=0, keepdims=True)
        first = jnp.where(slabs[0] == top, row_id[0], float(n))
        for slab, rid in zip(slabs[1:], row_id[1:]):
            first = jnp.minimum(first, jnp.where(slab == top, rid, float(n)))
        first = jnp.min(first, axis=0, keepdims=True)
        hit = [rid == first for rid in row_id]
        picked = [jnp.where(h, 1.0, p) for h, p in zip(hit, picked)]
        slabs = [jnp.where(h, -jnp.inf, s) for h, s in zip(hit, slabs)]
    return picked[0] if len(picked) == 1 else jnp.concatenate(picked, axis=0)


def _rank_rows(v, n_rows):
    n, q = v.shape
    slabs = [v[a:a + SUBLANES] for a in range(0, n, SUBLANES)]
    ranks = [jnp.zeros((SUBLANES, q), F32) for _ in slabs]
    sub = lax.broadcasted_iota(jnp.int32, (SUBLANES, q), 0)
    for jp in range(n_rows):
        row = v[jp:jp + 1, :]
        for si, slab in enumerate(slabs):
            first = si * SUBLANES
            if first > jp:
                beats = jnp.where(row >= slab, 1.0, 0.0)
            elif first + SUBLANES - 1 < jp:
                beats = jnp.where(row > slab, 1.0, 0.0)
            else:
                ge = jnp.where(row >= slab, 1.0, 0.0)
                gt = jnp.where(row > slab, 1.0, 0.0)
                beats = gt + (ge - gt) * jnp.where(sub > jp - first, 1.0, 0.0)
            ranks[si] = ranks[si] + beats
    return ranks[0] if len(ranks) == 1 else jnp.concatenate(ranks, axis=0)


def _proj_kernel(x_ref, posr_ref, gmix_ref, wtok_ref, wfeat_ref, gfeat_ref, invf8_ref,
                 qn_ref, qm_ref, gn_ref, vm_ref, vs_ref, vw_ref, km_ref, ks_ref, kw_ref,
                 kc_ref, vc_ref):
    tm = x_ref.shape[0]
    h = _rms_rows(x_ref[...], gmix_ref[...]).astype(BF16)

    acc = _dot(h, wtok_ref[...])
    kc_ref[...] = acc[:, :KV_WIDTH]
    vc_ref[...] = acc[:, KV_WIDTH:]

    ang = invf8_ref[...] * posr_ref[0].astype(F32)
    cos_a, sin_a = jnp.cos(ang), jnp.sin(ang)

    def project(lo, hi):
        return _dot_nt(wfeat_ref[lo:hi, :], h)

    def normed_head(acc_g, row0, hh):
        t = acc_g[hh * HEAD_DIM:(hh + 1) * HEAD_DIM, :]
        ss = jnp.sum(t * t, axis=0, keepdims=True)
        gain = gfeat_ref[row0 + hh * HEAD_DIM:row0 + (hh + 1) * HEAD_DIM, :]
        y = t * lax.rsqrt(ss * (1.0 / HEAD_DIM) + NORM_EPS) * gain
        a, b = y[:ROPE_HALF], y[ROPE_HALF:ROPE_DIM]
        return jnp.concatenate([a * cos_a - b * sin_a, b * cos_a + a * sin_a, y[ROPE_DIM:]], axis=0)

    q_rows = 2 * Q_WIDTH
    for mixer, dst in enumerate((qn_ref, qm_ref)):
        acc_q = project(mixer * Q_WIDTH, (mixer + 1) * Q_WIDTH)
        for hh in range(NSA_HEADS):
            dst[0, hh * HEAD_DIM:(hh + 1) * HEAD_DIM, :] = normed_head(acc_q, mixer * Q_WIDTH, hh).astype(BF16)

    acc_k = project(q_rows, q_rows + KEY_ROWS)
    for pair in range(KEY_ROWS // LANES):
        y = jnp.concatenate([normed_head(acc_k, q_rows, 2 * pair), normed_head(acc_k, q_rows, 2 * pair + 1)],
                            axis=0)
        y = y.T.astype(BF16)
        if pair < Q_WIDTH // LANES:
            km_ref[:, pair * LANES:(pair + 1) * LANES] = y
        elif pair == Q_WIDTH // LANES:
            ks_ref[...] = y
        else:
            kw_ref[...] = y

    v_rows = q_rows + KEY_ROWS
    acc_vm = project(v_rows, v_rows + Q_WIDTH)
    acc_v = project(v_rows + Q_WIDTH, FEAT_ROWS)
    for j in range(tm // V_TILE):
        cols = slice(j * V_TILE, (j + 1) * V_TILE)
        vm_ref[j] = acc_vm[:, cols].astype(BF16)
        vs_ref[j] = acc_v[:KV_WIDTH, cols].astype(BF16)
        vw_ref[j] = acc_v[KV_WIDTH:2 * KV_WIDTH, cols].astype(BF16)
    gn_ref[0] = acc_v[2 * KV_WIDTH:]


def _proj_call(x2, posr, gmix, wtok, wfeat, gfeat, invf8, b, s):
    t = x2.shape[0]
    tm = PROJ_TM
    nt = s // tm
    row = lambda w_: pl.BlockSpec((tm, w_), lambda i: (i, 0))
    full = lambda a: pl.BlockSpec(a.shape, lambda i: (0,) * a.ndim, pipeline_mode=pl.Buffered(1))
    feat = lambda r: pl.BlockSpec((1, r, tm), lambda i: (i // nt, 0, i % nt))
    tile = lambda r: pl.BlockSpec((tm // V_TILE, r, V_TILE), lambda i: (i, 0, 0))
    out_shape = [
        jax.ShapeDtypeStruct((b, Q_WIDTH, s), BF16),
        jax.ShapeDtypeStruct((b, Q_WIDTH, s), BF16),
        jax.ShapeDtypeStruct((b, GATE_ROWS, s), F32),
        jax.ShapeDtypeStruct((t // V_TILE, Q_WIDTH, V_TILE), BF16),
        jax.ShapeDtypeStruct((t // V_TILE, KV_WIDTH, V_TILE), BF16),
        jax.ShapeDtypeStruct((t // V_TILE, KV_WIDTH, V_TILE), BF16),
        jax.ShapeDtypeStruct((t, Q_WIDTH), BF16),
        jax.ShapeDtypeStruct((t, KV_WIDTH), BF16),
        jax.ShapeDtypeStruct((t, KV_WIDTH), BF16),
        jax.ShapeDtypeStruct((t, KV_WIDTH), F32),
        jax.ShapeDtypeStruct((t, KV_WIDTH), F32),
    ]
    out_specs = [feat(Q_WIDTH), feat(Q_WIDTH), feat(GATE_ROWS), tile(Q_WIDTH), tile(KV_WIDTH),
                 tile(KV_WIDTH), row(Q_WIDTH), row(KV_WIDTH), row(KV_WIDTH), row(KV_WIDTH), row(KV_WIDTH)]
    return pl.pallas_call(
        _proj_kernel,
        grid=(t // tm,),
        in_specs=[row(D_MODEL), pl.BlockSpec((1, 1, tm), lambda i: (i, 0, 0)), full(gmix),
                  full(wtok), full(wfeat), full(gfeat), full(invf8)],
        out_specs=out_specs,
        out_shape=out_shape,
        compiler_params=pltpu.CompilerParams(dimension_semantics=("parallel",),
                                             vmem_limit_bytes=VMEM_LIMIT),
    )(x2, posr, gmix, wtok, wfeat, gfeat, invf8)


def _compress_kernel(kcr_ref, vcr_ref, pe_ref, wk_ref, wv_ref, w2k_ref, w2v_ref, gain_ref,
                     pose_ref, invf_ref, bd_ref, kc_out, vct_out):
    half = NSA_GROUPS * CMP_HIDDEN
    n_sub = kcr_ref.shape[1] // CMP_STRIDE

    def sub_blocks(ref):
        return jnp.concatenate([ref[0, pl.ds(i, n_sub, stride=CMP_STRIDE), :] for i in range(CMP_STRIDE)],
                               axis=1)

    def comp(x, pe_a, pe_b, w_ref, w2_ref):
        a = _dot((x + pe_a).astype(BF16), w_ref[:, :half])
        b = _dot((x + pe_b).astype(BF16), w_ref[:, half:])
        hid = a + pltpu.roll(b, b.shape[0] - 1, 0)
        act = hid * _sigmoid(hid)
        return _dot(act.astype(BF16), w2_ref[...])

    kc = comp(sub_blocks(kcr_ref), pe_ref[0:1, :], pe_ref[1:2, :], wk_ref, w2k_ref)
    vc = comp(sub_blocks(vcr_ref), pe_ref[2:3, :], pe_ref[3:4, :], wv_ref, w2v_ref)
    kc = _head_norm(kc, bd_ref[...], gain_ref[...])
    kc = _rope(kc, _rope_tables(pose_ref[0], invf_ref[...]))
    kc_out[0] = kc.astype(BF16)
    vct_out[0] = vc.T.astype(BF16)


def _compress_call(kc_raw, vc_raw, pe4, wk, wv, w2k, w2v, gain, pos_end, invf, bd):
    b, s, width = kc_raw.shape
    n_sub = s // CMP_STRIDE
    blk = lambda shp: pl.BlockSpec((1,) + shp, lambda i: (i, 0, 0))
    full = lambda a: pl.BlockSpec(a.shape, lambda i: (0,) * a.ndim)
    return pl.pallas_call(
        _compress_kernel,
        grid=(b,),
        in_specs=[blk((s, width)), blk((s, width)), full(pe4), full(wk), full(wv),
                  full(w2k), full(w2v), full(gain), blk((n_sub, 1)), full(invf), full(bd)],
        out_specs=[blk((n_sub, KV_WIDTH)), blk((KV_WIDTH, n_sub))],
        out_shape=[jax.ShapeDtypeStruct((b, n_sub, KV_WIDTH), BF16),
                   jax.ShapeDtypeStruct((b, KV_WIDTH, n_sub), BF16)],
        compiler_params=pltpu.CompilerParams(dimension_semantics=("parallel",),
                                             vmem_limit_bytes=VMEM_LIMIT),
    )(kc_raw, vc_raw, pe4, wk, wv, w2k, w2v, gain, pos_end, invf, bd)


def _nsa_kernel(q_ref, kc_ref, vct_ref, ks_ref, vs_ref, kw_ref, vw_ref, gate_ref, ovl_ref, out_ref,
                sel_scr, lhs_scr, m_scr, acc_scr, o_scr, *bufs):
    tq = q_ref.shape[2]
    n_cmp = kc_ref.shape[1]
    n_slots = NSA_GROUPS * NSA_REP
    n_units = len(bufs) // 2
    s_scr, c_scr = bufs[:n_units], bufs[n_units:]
    per_unit = n_slots // n_units
    i = pl.program_id(1)
    t0 = i * tq
    t_row = t0 + lax.broadcasted_iota(jnp.int32, (1, tq), 1)

    zero = jnp.zeros((HEAD_DIM, tq), BF16)
    for g in range(NSA_GROUPS):
        for r in range(NSA_REP):
            hs = g * NSA_REP + r
            rows = q_ref[0, r * KV_WIDTH + g * HEAD_DIM:r * KV_WIDTH + (g + 1) * HEAD_DIM, :]
            lhs_scr[:, hs * tq:(hs + 1) * tq] = jnp.concatenate([rows, zero] if g == 0 else [zero, rows],
                                                                axis=0)
    gsig = _sigmoid(gate_ref[0])

    def unit_queries(unit):
        return lhs_scr[:, unit * SCORE_LANES:(unit + 1) * SCORE_LANES]

    def scores_into(unit, k_tile):
        tk = k_tile.shape[0]
        s_scr[unit][0:tk, :] = _dot(k_tile, unit_queries(unit))

    def sel_keys(kt):
        return ks_ref[0, pl.ds(pl.multiple_of(kt * NSA_TK, NSA_TK), NSA_TK), :]

    def win_keys(kt):
        return kw_ref[0, pl.ds(pl.multiple_of(kt * WIN_TK, WIN_TK), WIN_TK), :]

    n_lane = tq // LANES

    def run_tile(v_tile, plan_of_group, next_keys):
        ones_rows = _ones_rows(v_tile.shape[1])
        plans = [plan_of_group(g) for g in range(NSA_GROUPS)]
        for unit in range(n_units):
            g = (unit * per_unit) // NSA_REP
            sub = v_tile.shape[1] // len(plans[g])
            v_aug = _with_sum_row(v_tile[g * HEAD_DIM:(g + 1) * HEAD_DIM, :], ones_rows)
            for hh in range(per_unit):
                def load(a, h, hh=hh, unit=unit, sub=sub):
                    return s_scr[unit][a * sub:(a + 1) * sub, hh * tq + h * LANES:hh * tq + (h + 1) * LANES]

                _slot_update(load, plans[g], m_scr, acc_scr, unit * per_unit + hh, v_aug)
            if next_keys is not None:
                scores_into(unit, next_keys())

    def fold_branch(branch):
        for hs in range(n_slots):
            o_scr[hs] = o_scr[hs] + _normalised(acc_scr[hs], gsig[3 * hs + branch:3 * hs + branch + 1, :])

    blocks_per_tile = NSA_TK // SEL_BLOCK
    first_win = jnp.maximum((t0 - (WINDOW - 1)) // WIN_TK, 0)

    def sel_row(kt, g, a, h):
        return sel_scr[g, pl.ds(kt * blocks_per_tile + a, 1), :][:, h * LANES:(h + 1) * LANES]

    def sel_plan(kt, g):
        return [[('rows', sel_row(kt, g, a, h)) for h in range(n_lane)] for a in range(blocks_per_tile)]

    def causal(k_off, q_off):
        return k_off <= q_off

    def sel_body(kt, carry):
        run_tile(vs_ref[0, kt], lambda g: sel_plan(kt, g), lambda: sel_keys(kt + 1))
        return carry

    def win_tile(kt, visible, next_keys):
        run_tile(vw_ref[0, kt], lambda g: _visibility_plan(blocks_per_tile, SEL_BLOCK, n_lane, visible),
                 next_keys)

    for unit in range(n_units):
        c_scr[unit][...] = _dot(kc_ref[0], unit_queries(unit))
    for unit in range(n_units):
        scores_into(unit, win_keys(first_win))

    c_idx = lax.broadcasted_iota(jnp.int32, (n_cmp, tq), 0)
    valid_c = (c_idx * CMP_STRIDE + (CMP_LEN - 1) <= t_row) & (c_idx < n_cmp - 1)
    bias_c = jnp.where(valid_c, 0.0, NEG_BIG)
    p_sum = []
    for hs in range(n_slots):
        g = hs // NSA_REP
        unit, hh = divmod(hs, per_unit)
        x = c_scr[unit][:, hh * tq:(hh + 1) * tq] + bias_c
        m = jnp.max(x, axis=0, keepdims=True)
        e = jnp.where(valid_c, jnp.exp2(x - m), 0.0)
        p = e / jnp.maximum(jnp.sum(e, axis=0, keepdims=True), 1e-30)
        o_c = _dot(vct_ref[0, g * HEAD_DIM:(g + 1) * HEAD_DIM, :], p.astype(BF16))
        o_scr[hs] = o_c * gsig[3 * hs:3 * hs + 1, :]
        if hs % NSA_REP == 0:
            p_sum.append(p)
        else:
            p_sum[g] = p_sum[g] + p

    imp = []
    for g in range(NSA_GROUPS):
        hi, lo = _split_bf16(p_sum[g])
        imp.append(_dot(ovl_ref[...], hi) + _dot(ovl_ref[...], lo))

    _reset_state(m_scr, acc_scr)

    @pl.when(i >= 2)
    def _():
        win_tile(i - 2, lambda k_off, q_off: q_off < k_off, lambda: win_keys(i - 1))

    @pl.when(i >= 1)
    def _():
        win_tile(i - 1, lambda k_off, q_off: True, lambda: win_keys(i))

    cur = t_row // SEL_BLOCK
    j_idx = lax.broadcasted_iota(jnp.int32, (SEL_BLOCKS_MAX, tq), 0)
    forced = (j_idx == 0) | (j_idx == cur) | (j_idx == cur - 1)
    excluded = forced | (j_idx > cur)
    for g in range(NSA_GROUPS):
        picked = _pick_top_rows(jnp.where(excluded, -jnp.inf, imp[g]), SEL_TOPN - 3)
        sel_scr[g] = jnp.where(forced | (picked > 0.5), 0.0, NEG_BIG)

    win_tile(i, causal, lambda: sel_keys(0))
    fold_branch(2)

    last_sel = (t0 + tq - 1) // NSA_TK
    _reset_state(m_scr, acc_scr)
    lax.fori_loop(0, last_sel, sel_body, 0)
    run_tile(vs_ref[0, last_sel],
             lambda g: _visibility_plan(blocks_per_tile, SEL_BLOCK, n_lane, causal,
                                        lambda a, h: sel_row(last_sel, g, a, h)),
             None)
    fold_branch(1)

    for r in range(NSA_REP):
        out_ref[0, r * KV_WIDTH:r * KV_WIDTH + HEAD_DIM, :] = o_scr[r].astype(BF16)
        out_ref[0, r * KV_WIDTH + HEAD_DIM:(r + 1) * KV_WIDTH, :] = o_scr[NSA_REP + r].astype(BF16)


def _nsa_call(qn, kc, vct, ks, vs, kw, vw, gn, ovl):
    b, _, s = qn.shape
    tq = NSA_TQ
    n_slots = NSA_GROUPS * NSA_REP
    n_units = n_slots * tq // SCORE_LANES
    assert s // SEL_BLOCK <= SEL_BLOCKS_MAX and tq == WIN_TK == NSA_TK and WINDOW == 2 * WIN_TK
    seq = lambda a: pl.BlockSpec((1,) + a.shape[1:], lambda bi, i: (bi,) + (0,) * (a.ndim - 1))
    full = lambda a: pl.BlockSpec(a.shape, lambda bi, i: (0,) * a.ndim)
    return pl.pallas_call(
        _nsa_kernel,
        grid=(b, s // tq),
        in_specs=[pl.BlockSpec((1, Q_WIDTH, tq), lambda bi, i: (bi, 0, i)),
                  seq(kc), seq(vct), seq(ks), seq(vs), seq(kw), seq(vw),
                  pl.BlockSpec((1, GATE_ROWS, tq), lambda bi, i: (bi, 0, i)), full(ovl)],
        out_specs=pl.BlockSpec((1, Q_WIDTH, tq), lambda bi, i: (bi, 0, i)),
        out_shape=jax.ShapeDtypeStruct((b, Q_WIDTH, s), BF16),
        scratch_shapes=[pltpu.VMEM((NSA_GROUPS, SEL_BLOCKS_MAX, tq), F32),
                        pltpu.VMEM((KV_WIDTH, n_slots * tq), BF16),
                        pltpu.VMEM((n_slots, SUBLANES, tq), F32),
                        pltpu.VMEM((n_slots, HEAD_DIM + SUM_ROWS, tq), F32),
                        pltpu.VMEM((n_slots, HEAD_DIM, tq), F32)]
        + [pltpu.VMEM((max(NSA_TK, WIN_TK), SCORE_LANES), F32)] * n_units
        + [pltpu.VMEM((kc.shape[1], SCORE_LANES), F32)] * n_units,
        compiler_params=pltpu.CompilerParams(dimension_semantics=("parallel", "parallel"),
                                             vmem_limit_bytes=VMEM_LIMIT),
    )(qn, kc, vct, ks, vs, kw, vw, gn, ovl)


def _moba_kernel(q_ref, k_ref, v_ref, out_ref, km_scr, sel_scr, lhs_scr, m_scr, acc_scr, *s_scr):
    bs = MOBA_BLOCK
    n_blocks = k_ref.shape[1] // bs
    n_slots = len(s_scr)
    i = pl.program_id(2)

    @pl.when(i == 0)
    def _():
        km_scr[...] = jnp.zeros(km_scr.shape, F32)
        for n in range(n_blocks):
            km_scr[n:n + 1, :] = jnp.mean(k_ref[0, n * bs:(n + 1) * bs, :].astype(F32), axis=0,
                                          keepdims=True)

    def pair_lanes(hs):
        return slice((hs // 2) * LANES, (hs // 2 + 1) * LANES)

    zero = jnp.zeros((HEAD_DIM, bs), BF16)
    n_idx = lax.broadcasted_iota(jnp.int32, (MOBA_BLOCKS_MAX, bs), 0)
    for hs in range(n_slots):
        rows = q_ref[0, hs * HEAD_DIM:(hs + 1) * HEAD_DIM, :]
        lhs_scr[hs] = jnp.concatenate([rows, zero] if hs % 2 == 0 else [zero, rows], axis=0)

    def scores_into(hs, n):
        k_tile = k_ref[0, pl.ds(pl.multiple_of(n * bs, bs), bs), pair_lanes(hs)]
        s_scr[hs][...] = _dot(k_tile, lhs_scr[hs])

    block_scores = []
    for hs in range(n_slots):
        km_hi, km_lo = _split_bf16(km_scr[:, pair_lanes(hs)])
        block_scores.append(_dot(km_hi, lhs_scr[hs]) + _dot(km_lo, lhs_scr[hs]))
    for hs in range(n_slots):
        scores_into(hs, i)

    ones_rows = _ones_rows(bs)
    n_lane = bs // LANES

    def run_block(n, plan_of_slot, next_block):
        for hs in range(n_slots):
            plan = plan_of_slot(hs)
            sub = bs // len(plan)
            v_aug = _with_sum_row(v_ref[0, n, hs * HEAD_DIM:(hs + 1) * HEAD_DIM, :], ones_rows)

            def load(a, h, hs=hs, sub=sub):
                return s_scr[hs][a * sub:(a + 1) * sub, h * LANES:(h + 1) * LANES]

            _slot_update(load, plan, m_scr, acc_scr, hs, v_aug)
            if next_block is not None:
                scores_into(hs, next_block)

    _reset_state(m_scr, acc_scr)
    for hs in range(n_slots):
        sc = jnp.where(n_idx < i, block_scores[hs], -jnp.inf)
        sel_scr[hs] = jnp.where((_rank_rows(sc, MOBA_BLOCKS_MAX) < MOBA_TOPK) & (n_idx < i), 0.0, NEG_BIG)
    own_plan = _visibility_plan(bs // OWN_SUB, OWN_SUB, n_lane, lambda k_off, q_off: k_off <= q_off)
    run_block(i, lambda hs: own_plan, i * 0)

    def past_plan(n):
        return lambda hs: [[('rows', sel_scr[hs, pl.ds(n, 1), :][:, h * LANES:(h + 1) * LANES])
                            for h in range(n_lane)]]

    def past_body(n, carry):
        run_block(n, past_plan(n), n + 1)
        return carry

    lax.fori_loop(0, i - 1, past_body, 0)

    @pl.when(i >= 1)
    def _():
        run_block(i - 1, past_plan(i - 1), None)

    for hs in range(n_slots):
        out_ref[0, hs * HEAD_DIM:(hs + 1) * HEAD_DIM, :] = _normalised(acc_scr[hs]).astype(BF16)


def _moba_call(qm, km, vm):
    b, w, s = qm.shape
    bs = MOBA_BLOCK
    nb = s // bs
    wb = MOBA_STEP_HEADS * HEAD_DIM
    assert nb <= MOBA_BLOCKS_MAX and w % wb == 0
    return pl.pallas_call(
        _moba_kernel,
        grid=(b, w // wb, nb),
        in_specs=[pl.BlockSpec((1, wb, bs), lambda bi, p, i: (bi, p, i)),
                  pl.BlockSpec((1, s, wb), lambda bi, p, i: (bi, 0, p)),
                  pl.BlockSpec((1, nb, wb, bs), lambda bi, p, i: (bi, 0, p, 0))],
        out_specs=pl.BlockSpec((1, wb, bs), lambda bi, p, i: (bi, p, i)),
        out_shape=jax.ShapeDtypeStruct((b, w, s), BF16),
        scratch_shapes=[pltpu.VMEM((MOBA_BLOCKS_MAX, wb), F32),
                        pltpu.VMEM((MOBA_STEP_HEADS, MOBA_BLOCKS_MAX, bs), F32),
                        pltpu.VMEM((MOBA_STEP_HEADS, LANES, bs), BF16),
                        pltpu.VMEM((MOBA_STEP_HEADS, SUBLANES, bs), F32),
                        pltpu.VMEM((MOBA_STEP_HEADS, HEAD_DIM + SUM_ROWS, bs), F32)]
        + [pltpu.VMEM((bs, bs), F32)] * MOBA_STEP_HEADS,
        compiler_params=pltpu.CompilerParams(
            dimension_semantics=("parallel", "parallel", "arbitrary"), vmem_limit_bytes=VMEM_LIMIT),
    )(qm, km, vm)


def _post_kernel(x_ref, yn_ref, ym_ref, p_ref, gmix_ref, wg_ref, wun_ref, wum_ref, wo_ref,
                 gffn_ref, wfi_ref, wfo_ref, gple_ref, wpg_ref, wpp_ref, out_ref):
    half = x_ref.shape[0] // 2
    halves = [slice(r * half, (r + 1) * half) for r in range(2)]

    xs = []
    for rows in halves:
        x = x_ref[rows, :]
        h = _rms_rows(x, gmix_ref[...]).astype(BF16)
        ga = _sigmoid(_dot(h, wg_ref[:, :D_MODEL]))
        gb = _sigmoid(_dot(h, wg_ref[:, D_MODEL:]))
        merged = ga * _dot_tn(yn_ref[0, :, rows], wun_ref[...]) + gb * _dot_tn(ym_ref[0, :, rows], wum_ref[...])
        xs.append(x + _dot(merged.astype(BF16), wo_ref[...]))

    ys = []
    for x in xs:
        h = _rms_rows(x, gffn_ref[...]).astype(BF16)
        y = x
        for lo, hi in FFN_CHUNKS:
            gate = _dot(h, wfi_ref[:, lo:hi])
            up = _dot(h, wfi_ref[:, D_FF + lo:D_FF + hi])
            act = (gate * _sigmoid(gate) * up).astype(BF16)
            y = y + _dot(act, wfo_ref[lo:hi, :])
        ys.append(y)

    for rows, y in zip(halves, ys):
        h2 = _rms_rows(y, gple_ref[...]).astype(BF16)
        ple_gate = _sigmoid(_dot(h2, wpg_ref[...]))
        out_ref[rows, :] = y + ple_gate * _dot(p_ref[rows, :].astype(BF16), wpp_ref[...])


def _post_call(x2, yn, ym, p2, gmix, wg, wun, wum, wo, gffn, wfi, wfo, gple, wpg, wpp):
    t = x2.shape[0]
    tm = POST_TM
    nt = yn.shape[2] // tm
    row = lambda w_: pl.BlockSpec((tm, w_), lambda i: (i, 0))
    feat = pl.BlockSpec((1, Q_WIDTH, tm), lambda i: (i // nt, 0, i % nt))
    full = lambda a: pl.BlockSpec(a.shape, lambda i: (0, 0), pipeline_mode=pl.Buffered(1))
    weights = (gmix, wg, wun, wum, wo, gffn, wfi, wfo, gple, wpg, wpp)
    return pl.pallas_call(
        _post_kernel,
        grid=(t // tm,),
        in_specs=[row(D_MODEL), feat, feat, row(PLE_DIM)] + [full(w) for w in weights],
        out_specs=row(D_MODEL),
        out_shape=jax.ShapeDtypeStruct((t, D_MODEL), F32),
        compiler_params=pltpu.CompilerParams(dimension_semantics=("parallel",),
                                             vmem_limit_bytes=VMEM_LIMIT),
    )(x2, yn, ym, p2, *weights)


def _block_diag_ones(width):
    idx = np.arange(width) // HEAD_DIM
    return jnp.asarray(idx[:, None] == idx[None, :], dtype=BF16)


def _inv_freq():
    return ROPE_THETA ** (-jnp.arange(ROPE_HALF, dtype=F32) / ROPE_HALF)


def _inv_freq_lanes(width):
    inv_freq = _inv_freq()
    per_head = jnp.concatenate([inv_freq, inv_freq, jnp.zeros((HEAD_DIM - ROPE_DIM,), F32)])
    return jnp.tile(per_head, width // HEAD_DIM)[None, :]


def _split_w_in(w):
    parts = jnp.split(w, IN_CUTS, axis=-1)
    (q_n, kc, vc, ks, vs, kw, vw, gate_n, q_m, k_m, v_m, gate_a, gate_b) = parts
    d = w.shape[0]
    q_n = q_n.reshape(d, NSA_GROUPS, NSA_REP, HEAD_DIM).transpose(0, 2, 1, 3).reshape(d, -1)
    gate_n = jnp.pad(gate_n, ((0, 0), (0, GATE_ROWS - gate_n.shape[1])))
    w_tok = jnp.concatenate([kc, vc], axis=1)
    w_feat = jnp.concatenate([q_n, q_m, k_m, ks, kw, v_m, vs, vw, gate_n], axis=1).T
    w_gate = jnp.concatenate([gate_a, gate_b], axis=1)
    return w_tok.astype(BF16), w_feat.astype(BF16), w_gate.astype(BF16)


def _compress_weights(w1, w2, pe):
    eye = jnp.eye(NSA_GROUPS, dtype=F32)
    halves = []
    for part in (w1[:CMP_STRIDE * HEAD_DIM], w1[CMP_STRIDE * HEAD_DIM:]):
        p3 = part.reshape(CMP_STRIDE, HEAD_DIM, CMP_HIDDEN)
        halves.append(jnp.einsum('idh,gk->igdkh', p3, eye)
                      .reshape(CMP_STRIDE * NSA_GROUPS * HEAD_DIM, NSA_GROUPS * CMP_HIDDEN))
    w_big = jnp.concatenate(halves, axis=1).astype(BF16)
    w2_bd = jnp.einsum('hd,kg->khgd', w2, eye).reshape(NSA_GROUPS * CMP_HIDDEN,
                                                        NSA_GROUPS * HEAD_DIM).astype(BF16)
    pe_rows = [jnp.broadcast_to(pe[a:a + CMP_STRIDE, None, :], (CMP_STRIDE, NSA_GROUPS, HEAD_DIM))
               .reshape(1, -1) for a in (0, CMP_STRIDE)]
    return w_big, w2_bd, pe_rows


def _overlap_matrix_t(n_cmp):
    c = np.arange(n_cmp)
    j = np.arange(SEL_BLOCKS_MAX)
    start, end = c * CMP_STRIDE, c * CMP_STRIDE + CMP_LEN - 1
    ov = (start[None, :] <= j[:, None] * SEL_BLOCK + SEL_BLOCK - 1) & (end[None, :] >= j[:, None] * SEL_BLOCK)
    return jnp.asarray(ov, dtype=BF16)


def kernel(x, p, positions, g_mix, w_in, nsa_q_gain, nsa_kc_gain, nsa_ks_gain, nsa_kw_gain, nsa_pe_k, nsa_pe_v, nsa_ck_w1, nsa_ck_w2, nsa_cv_w1, nsa_cv_w2, moba_q_gain, moba_k_gain, w_up_nsa, w_up_moba, w_out, g_ffn, w_ffn_in, w_ffn_out, g_ple, w_ple_gate, w_ple_proj):
    b, s, d = x.shape
    depth = w_in.shape[0]
    assert s % PAD_MULT == 0 and s % PROJ_TM == 0 and d == D_MODEL and V_TILE == MOBA_BLOCK
    t = b * s
    n_sub = s // CMP_STRIDE

    assert ROPE_HALF == SUBLANES
    posr = positions.reshape(t // PROJ_TM, 1, PROJ_TM)
    pos_end = jnp.concatenate([positions[:, CMP_LEN - 1::CMP_STRIDE], positions[:, -1:]], axis=1)[:, :, None]
    invf128 = _inv_freq_lanes(LANES)
    invf8 = _inv_freq()[:, None]
    bd128 = _block_diag_ones(LANES)
    ovl_t = _overlap_matrix_t(n_sub)
    tile = lambda g, n: jnp.tile(g, n)
    xi = x.reshape(t, d)

    for i in range(depth):
        w_tok, w_feat, w_gate = _split_w_in(w_in[i])
        g_feat = jnp.concatenate([
            jnp.concatenate([tile(nsa_q_gain[i], NSA_HEADS), tile(moba_q_gain[i], MOBA_HEADS)])
            * (ATTN_SCALE * LOG2E),
            tile(moba_k_gain[i], MOBA_HEADS), tile(nsa_ks_gain[i], NSA_GROUPS),
            tile(nsa_kw_gain[i], NSA_GROUPS)])[:, None]
        gmix = g_mix[i][None, :]
        (qn, qm, gn, vm, vs, vw, km, ks, kw, kc_raw, vc_raw) = _proj_call(
            xi, posr, gmix, w_tok, w_feat, g_feat, invf8, b, s)

        wk_big, w2k_bd, pe_k = _compress_weights(nsa_ck_w1[i], nsa_ck_w2[i], nsa_pe_k[i])
        wv_big, w2v_bd, pe_v = _compress_weights(nsa_cv_w1[i], nsa_cv_w2[i], nsa_pe_v[i])
        pe4 = jnp.concatenate(pe_k + pe_v, axis=0)
        kc, vct = _compress_call(kc_raw.reshape(b, s, KV_WIDTH), vc_raw.reshape(b, s, KV_WIDTH), pe4, wk_big, wv_big, w2k_bd, w2v_bd,
                                 tile(nsa_kc_gain[i], NSA_GROUPS)[None, :], pos_end,
                                 invf128, bd128)

        r3 = lambda a: a.reshape(b, s, a.shape[-1])
        y_nsa = _nsa_call(qn, kc, vct, r3(ks), vs.reshape(b, s // NSA_TK, KV_WIDTH, NSA_TK), r3(kw),
                          vw.reshape(b, s // WIN_TK, KV_WIDTH, WIN_TK), gn, ovl_t)
        y_moba = _moba_call(qm, r3(km), vm.reshape(b, s // MOBA_BLOCK, Q_WIDTH, MOBA_BLOCK))

        w_un = (w_up_nsa[i].reshape(NSA_GROUPS, NSA_REP, HEAD_DIM, d).transpose(1, 0, 2, 3)
                .reshape(NSA_HEADS * HEAD_DIM, d).astype(BF16))
        xi = _post_call(xi, y_nsa, y_moba, p[i].reshape(t, PLE_DIM), gmix, w_gate, w_un,
                        w_up_moba[i].astype(BF16), w_out[i].astype(BF16), g_ffn[i][None, :],
                        w_ffn_in[i].astype(BF16), w_ffn_out[i].astype(BF16), g_ple[i][None, :],
                        w_ple_gate[i].astype(BF16), w_ple_proj[i].astype(BF16))
    return xi.reshape(b, s, d)
```

```python
import jax
import jax.numpy as jnp
import numpy as np
from jax import lax
from jax.experimental import pallas as pl
from jax.experimental.pallas import tpu as pltpu

F32 = jnp.float32
BF16 = jnp.bfloat16

D_MODEL = 1024
HEAD_DIM = 64
ROPE_DIM = HEAD_DIM // 4
ROPE_HALF = ROPE_DIM // 2
ROPE_THETA = 500000.0
NORM_EPS = 1e-6
ATTN_SCALE = HEAD_DIM ** -0.5

NSA_HEADS = 8
NSA_GROUPS = 2
NSA_REP = NSA_HEADS // NSA_GROUPS
CMP_LEN = 32
CMP_STRIDE = 16
CMP_HIDDEN = 256
SEL_BLOCK = 64
SEL_TOPN = 8
WINDOW = 512

MOBA_HEADS = 8
MOBA_BLOCK = 256
MOBA_TOPK = 3

PAD_MULT = 256
D_FF = ((-(-8 * D_MODEL // 3)) + 255) // 256 * 256
PLE_DIM = 256

IN_SPLITS = ((NSA_HEADS * HEAD_DIM,) + (NSA_GROUPS * HEAD_DIM,) * 6 + (3 * NSA_HEADS,)
             + (MOBA_HEADS * HEAD_DIM,) * 3 + (D_MODEL, D_MODEL))
IN_CUTS = tuple(int(c) for c in np.cumsum(IN_SPLITS)[:-1])

LANES = 128
SUBLANES = 8
NEG_BIG = -1e30
LOG2E = 1.4426950408889634
SUM_ROWS = 16
VMEM_LIMIT = 56 * 1024 * 1024

Q_WIDTH = NSA_HEADS * HEAD_DIM
KV_WIDTH = NSA_GROUPS * HEAD_DIM
GATE_ROWS = 32
KEY_ROWS = Q_WIDTH + 2 * KV_WIDTH
FEAT_ROWS = 2 * Q_WIDTH + KEY_ROWS + Q_WIDTH + 2 * KV_WIDTH + GATE_ROWS

PROJ_TM = 512
V_TILE = 256
NSA_TQ = 256
NSA_STEP_BLOCKS = 2
NSA_TK = V_TILE
WIN_TK = V_TILE
SCORE_LANES = 256
POST_TM = 512
MXU_TILE = 256
FFN_CHUNKS = ((0, 6 * MXU_TILE), (6 * MXU_TILE, D_FF))
SEL_BLOCKS_MAX = 32
MOBA_BLOCKS_MAX = 8
MOBA_STEP_HEADS = 8
OWN_SUB = 64


def _dot(a, b):
    return jnp.dot(a, b, preferred_element_type=F32)


def _dot_nt(a, b):
    return lax.dot_general(a, b, (((1,), (1,)), ((), ())), preferred_element_type=F32)


def _dot_tn(a, b):
    return lax.dot_general(a, b, (((0,), (0,)), ((), ())), preferred_element_type=F32)


def _split_bf16(a_f32):
    hi = a_f32.astype(BF16)
    return hi, (a_f32 - hi.astype(F32)).astype(BF16)


def _sigmoid(x):
    return 1.0 / (1.0 + jnp.exp(-x))


def _rms_rows(x, g):
    return x * lax.rsqrt(jnp.mean(x * x, axis=-1, keepdims=True) + NORM_EPS) * g


def _head_norm(t, bd, gain):
    hi, lo = _split_bf16(t * t)
    ss = _dot(hi, bd) + _dot(lo, bd)
    return t * lax.rsqrt(ss * (1.0 / HEAD_DIM) + NORM_EPS) * gain


def _rope_tables(pos_col, invf):
    ang = pos_col.astype(F32) * invf
    cos_a, sin_a = jnp.cos(ang), jnp.sin(ang)
    d = lax.broadcasted_iota(jnp.int32, ang.shape, 1) & (HEAD_DIM - 1)
    s_lo = jnp.where(d < ROPE_HALF, -sin_a, 0.0)
    s_hi = jnp.where((d >= ROPE_HALF) & (d < ROPE_DIM), sin_a, 0.0)
    return cos_a, s_lo, s_hi


def _rope(y, tables):
    cos_a, s_lo, s_hi = tables
    w = y.shape[1]
    return (y * cos_a + pltpu.roll(y, w - ROPE_HALF, 1) * s_lo
            + pltpu.roll(y, ROPE_HALF, 1) * s_hi)


def _ones_rows(tk):
    r = lax.broadcasted_iota(jnp.int32, (SUM_ROWS, tk), 0)
    return jnp.where(r == 0, 1.0, 0.0).astype(BF16)


def _with_sum_row(v_rows, ones_rows):
    return jnp.concatenate([v_rows, ones_rows], axis=0)


def _slot_update(load_s, plan, m_ref, acc_ref, hs, v_aug):
    n_sub, n_lane = len(plan), len(plan[0])
    sub = v_aug.shape[1] // n_sub
    e_cols, alphas = [], []
    for h in range(n_lane):
        lanes = slice(h * LANES, (h + 1) * LANES)
        m = m_ref[hs, 0:1, lanes]
        m_new = m
        for a in range(n_sub):
            if plan[a][h] is None:
                continue
            kind, arg = plan[a][h]
            if kind == 'rows':
                top = jnp.max(load_s(a, h), axis=0, keepdims=True)
                m_new = jnp.maximum(m_new, top if arg is None else jnp.where(arg < 0.0, NEG_BIG, top))
            else:
                m_new = jnp.maximum(m_new, jnp.max(load_s(a, h) + arg, axis=0, keepdims=True))
        e = []
        for a in range(n_sub):
            if plan[a][h] is None:
                e.append(jnp.zeros((sub, LANES), BF16))
                continue
            kind, arg = plan[a][h]
            if kind == 'rows':
                shift = -m_new if arg is None else jnp.where(arg < 0.0, NEG_BIG, -m_new)
                e.append(jnp.exp2(load_s(a, h) + shift).astype(BF16))
            else:
                e.append(jnp.exp2(load_s(a, h) + arg - m_new).astype(BF16))
        e_cols.append(e[0] if n_sub == 1 else jnp.concatenate(e, axis=0))
        alphas.append(jnp.exp2(m - m_new))
        m_ref[hs, 0:1, lanes] = m_new
    e_all = e_cols[0] if n_lane == 1 else jnp.concatenate(e_cols, axis=1)
    alpha = alphas[0] if n_lane == 1 else jnp.concatenate(alphas, axis=1)
    acc_ref[hs] = alpha * acc_ref[hs] + _dot(v_aug, e_all)


def _visibility_plan(n_sub, sub, n_lane, visible, kept_row=None):
    plan = []
    for a in range(n_sub):
        row_plan = []
        for h in range(n_lane):
            k0, k1, q0, q1 = a * sub, (a + 1) * sub - 1, h * LANES, (h + 1) * LANES - 1
            corners = [visible(k, q) for k in (k0, k1) for q in (q0, q1)]
            row = None if kept_row is None else kept_row(a, h)
            if not any(corners):
                row_plan.append(None)
            elif all(corners):
                row_plan.append(('rows', row))
            else:
                k_off = k0 + lax.broadcasted_iota(jnp.int32, (sub, LANES), 0)
                q_off = q0 + lax.broadcasted_iota(jnp.int32, (sub, LANES), 1)
                keep = 0.0 if row is None else row
                row_plan.append(('tile', jnp.where(visible(k_off, q_off), keep, NEG_BIG)))
        plan.append(row_plan)
    return plan


def _reset_state(m_ref, acc_ref):
    m_ref[...] = jnp.full(m_ref.shape, NEG_BIG, F32)
    acc_ref[...] = jnp.zeros(acc_ref.shape, F32)


def _normalised(acc, scale_row=1.0):
    return acc[:HEAD_DIM] * (scale_row / jnp.maximum(acc[HEAD_DIM:HEAD_DIM + 1], 1e-30))


def _pick_top_rows(v, n_pick):
    n, q = v.shape
    slabs = [v[a:a + SUBLANES] for a in range(0, n, SUBLANES)]
    sub = lax.broadcasted_iota(jnp.int32, (SUBLANES, q), 0).astype(F32)
    row_id = [sub + float(a) for a in range(0, n, SUBLANES)]
    picked = [jnp.zeros((SUBLANES, q), F32) for _ in slabs]
    for _ in range(n_pick):
        top = slabs[0]
        for slab in slabs[1:]:
            top = jnp.maximum(top, slab)
        top = jnp.max(top, axis=0, keepdims=True)
        first = jnp.where(slabs[0] == top, row_id[0], float(n))
        for slab, rid in zip(slabs[1:], row_id[1:]):
            first = jnp.minimum(first, jnp.where(slab == top, rid, float(n)))
        first = jnp.min(first, axis=0, keepdims=True)
        hit = [rid == first for rid in row_id]
        picked = [jnp.where(h, 1.0, p) for h, p in zip(hit, picked)]
        slabs = [jnp.where(h, -jnp.inf, s) for h, s in zip(hit, slabs)]
    return picked[0] if len(picked) == 1 else jnp.concatenate(picked, axis=0)


def _rank_rows(v, n_rows):
    n, q = v.shape
    slabs = [v[a:a + SUBLANES] for a in range(0, n, SUBLANES)]
    ranks = [jnp.zeros((SUBLANES, q), F32) for _ in slabs]
    sub = lax.broadcasted_iota(jnp.int32, (SUBLANES, q), 0)
    for jp in range(n_rows):
        row = v[jp:jp + 1, :]
        for si, slab in enumerate(slabs):
            first = si * SUBLANES
            if first > jp:
                beats = jnp.where(row >= slab, 1.0, 0.0)
            elif first + SUBLANES - 1 < jp:
                beats = jnp.where(row > slab, 1.0, 0.0)
            else:
                ge = jnp.where(row >= slab, 1.0, 0.0)
                gt = jnp.where(row > slab, 1.0, 0.0)
                beats = gt + (ge - gt) * jnp.where(sub > jp - first, 1.0, 0.0)
            ranks[si] = ranks[si] + beats
    return ranks[0] if len(ranks) == 1 else jnp.concatenate(ranks, axis=0)


def _proj_kernel(x_ref, posr_ref, gmix_ref, wtok_ref, wfeat_ref, gfeat_ref, invf8_ref,
                 qn_ref, qm_ref, gn_ref, vm_ref, vs_ref, vw_ref, km_ref, ks_ref, kw_ref,
                 kc_ref, vc_ref):
    tm = x_ref.shape[0]
    h = _rms_rows(x_ref[...], gmix_ref[...]).astype(BF16)

    acc = _dot(h, wtok_ref[...])
    kc_ref[...] = acc[:, :KV_WIDTH]
    vc_ref[...] = acc[:, KV_WIDTH:]

    ang = invf8_ref[...] * posr_ref[0].astype(F32)
    cos_a, sin_a = jnp.cos(ang), jnp.sin(ang)

    def project(lo, hi):
        return _dot_nt(wfeat_ref[lo:hi, :], h)

    def normed_head(acc_g, row0, hh):
        t = acc_g[hh * HEAD_DIM:(hh + 1) * HEAD_DIM, :]
        ss = jnp.sum(t * t, axis=0, keepdims=True)
        gain = gfeat_ref[row0 + hh * HEAD_DIM:row0 + (hh + 1) * HEAD_DIM, :]
        y = t * lax.rsqrt(ss * (1.0 / HEAD_DIM) + NORM_EPS) * gain
        a, b = y[:ROPE_HALF], y[ROPE_HALF:ROPE_DIM]
        return jnp.concatenate([a * cos_a - b * sin_a, b * cos_a + a * sin_a, y[ROPE_DIM:]], axis=0)

    q_rows = 2 * Q_WIDTH
    for mixer, dst in enumerate((qn_ref, qm_ref)):
        acc_q = project(mixer * Q_WIDTH, (mixer + 1) * Q_WIDTH)
        for hh in range(NSA_HEADS):
            dst[0, hh * HEAD_DIM:(hh + 1) * HEAD_DIM, :] = normed_head(acc_q, mixer * Q_WIDTH, hh).astype(BF16)

    acc_k = project(q_rows, q_rows + KEY_ROWS)
    for pair in range(KEY_ROWS // LANES):
        y = jnp.concatenate([normed_head(acc_k, q_rows, 2 * pair), normed_head(acc_k, q_rows, 2 * pair + 1)],
                            axis=0)
        y = y.T.astype(BF16)
        if pair < Q_WIDTH // LANES:
            km_ref[:, pair * LANES:(pair + 1) * LANES] = y
        elif pair == Q_WIDTH // LANES:
            ks_ref[...] = y
        else:
            kw_ref[...] = y

    v_rows = q_rows + KEY_ROWS
    acc_vm = project(v_rows, v_rows + Q_WIDTH)
    acc_v = project(v_rows + Q_WIDTH, FEAT_ROWS)
    for j in range(tm // V_TILE):
        cols = slice(j * V_TILE, (j + 1) * V_TILE)
        vm_ref[j] = acc_vm[:, cols].astype(BF16)
        vs_ref[j] = acc_v[:KV_WIDTH, cols].astype(BF16)
        vw_ref[j] = acc_v[KV_WIDTH:2 * KV_WIDTH, cols].astype(BF16)
    gn_ref[0] = acc_v[2 * KV_WIDTH:]


def _proj_call(x2, posr, gmix, wtok, wfeat, gfeat, invf8, b, s):
    t = x2.shape[0]
    tm = PROJ_TM
    nt = s // tm
    row = lambda w_: pl.BlockSpec((tm, w_), lambda i: (i, 0))
    full = lambda a: pl.BlockSpec(a.shape, lambda i: (0,) * a.ndim, pipeline_mode=pl.Buffered(1))
    feat = lambda r: pl.BlockSpec((1, r, tm), lambda i: (i // nt, 0, i % nt))
    tile = lambda r: pl.BlockSpec((tm // V_TILE, r, V_TILE), lambda i: (i, 0, 0))
    out_shape = [
        jax.ShapeDtypeStruct((b, Q_WIDTH, s), BF16),
        jax.ShapeDtypeStruct((b, Q_WIDTH, s), BF16),
        jax.ShapeDtypeStruct((b, GATE_ROWS, s), F32),
        jax.ShapeDtypeStruct((t // V_TILE, Q_WIDTH, V_TILE), BF16),
        jax.ShapeDtypeStruct((t // V_TILE, KV_WIDTH, V_TILE), BF16),
        jax.ShapeDtypeStruct((t // V_TILE, KV_WIDTH, V_TILE), BF16),
        jax.ShapeDtypeStruct((t, Q_WIDTH), BF16),
        jax.ShapeDtypeStruct((t, KV_WIDTH), BF16),
        jax.ShapeDtypeStruct((t, KV_WIDTH), BF16),
        jax.ShapeDtypeStruct((t, KV_WIDTH), F32),
        jax.ShapeDtypeStruct((t, KV_WIDTH), F32),
    ]
    out_specs = [feat(Q_WIDTH), feat(Q_WIDTH), feat(GATE_ROWS), tile(Q_WIDTH), tile(KV_WIDTH),
                 tile(KV_WIDTH), row(Q_WIDTH), row(KV_WIDTH), row(KV_WIDTH), row(KV_WIDTH), row(KV_WIDTH)]
    return pl.pallas_call(
        _proj_kernel,
        grid=(t // tm,),
        in_specs=[row(D_MODEL), pl.BlockSpec((1, 1, tm), lambda i: (i, 0, 0)), full(gmix),
                  full(wtok), full(wfeat), full(gfeat), full(invf8)],
        out_specs=out_specs,
        out_shape=out_shape,
        compiler_params=pltpu.CompilerParams(dimension_semantics=("parallel",),
                                             vmem_limit_bytes=VMEM_LIMIT),
    )(x2, posr, gmix, wtok, wfeat, gfeat, invf8)


def _compress_kernel(kcr_ref, vcr_ref, pe_ref, wk_ref, wv_ref, w2k_ref, w2v_ref, gain_ref,
                     pose_ref, invf_ref, bd_ref, kc_out, vct_out):
    half = NSA_GROUPS * CMP_HIDDEN
    n_sub = kcr_ref.shape[1] // CMP_STRIDE

    def sub_blocks(ref):
        return jnp.concatenate([ref[0, pl.ds(i, n_sub, stride=CMP_STRIDE), :] for i in range(CMP_STRIDE)],
                               axis=1)

    def comp(x, pe_a, pe_b, w_ref, w2_ref):
        a = _dot((x + pe_a).astype(BF16), w_ref[:, :half])
        b = _dot((x + pe_b).astype(BF16), w_ref[:, half:])
        hid = a + pltpu.roll(b, b.shape[0] - 1, 0)
        act = hid * _sigmoid(hid)
        return _dot(act.astype(BF16), w2_ref[...])

    kc = comp(sub_blocks(kcr_ref), pe_ref[0:1, :], pe_ref[1:2, :], wk_ref, w2k_ref)
    vc = comp(sub_blocks(vcr_ref), pe_ref[2:3, :], pe_ref[3:4, :], wv_ref, w2v_ref)
    kc = _head_norm(kc, bd_ref[...], gain_ref[...])
    kc = _rope(kc, _rope_tables(pose_ref[0], invf_ref[...]))
    kc_out[0] = kc.astype(BF16)
    vct_out[0] = vc.T.astype(BF16)


def _compress_call(kc_raw, vc_raw, pe4, wk, wv, w2k, w2v, gain, pos_end, invf, bd):
    b, s, width = kc_raw.shape
    n_sub = s // CMP_STRIDE
    blk = lambda shp: pl.BlockSpec((1,) + shp, lambda i: (i, 0, 0))
    full = lambda a: pl.BlockSpec(a.shape, lambda i: (0,) * a.ndim)
    return pl.pallas_call(
        _compress_kernel,
        grid=(b,),
        in_specs=[blk((s, width)), blk((s, width)), full(pe4), full(wk), full(wv),
                  full(w2k), full(w2v), full(gain), blk((n_sub, 1)), full(invf), full(bd)],
        out_specs=[blk((n_sub, KV_WIDTH)), blk((KV_WIDTH, n_sub))],
        out_shape=[jax.ShapeDtypeStruct((b, n_sub, KV_WIDTH), BF16),
                   jax.ShapeDtypeStruct((b, KV_WIDTH, n_sub), BF16)],
        compiler_params=pltpu.CompilerParams(dimension_semantics=("parallel",),
                                             vmem_limit_bytes=VMEM_LIMIT),
    )(kc_raw, vc_raw, pe4, wk, wv, w2k, w2v, gain, pos_end, invf, bd)


def _nsa_kernel(q_ref, *refs):
    for sub in range(q_ref.shape[2] // NSA_TQ):
        _nsa_block(pl.program_id(1) * NSA_STEP_BLOCKS + sub, slice(sub * NSA_TQ, (sub + 1) * NSA_TQ),
                   q_ref, *refs)


def _nsa_block(i, qs, q_ref, kc_ref, vct_ref, ks_ref, vs_ref, kw_ref, vw_ref, gate_ref, ovl_ref, out_ref,
               sel_scr, lhs_scr, m_scr, acc_scr, o_scr, *bufs):
    tq = NSA_TQ
    n_cmp = kc_ref.shape[1]
    n_slots = NSA_GROUPS * NSA_REP
    n_units = len(bufs) // 2
    s_scr, c_scr = bufs[:n_units], bufs[n_units:]
    per_unit = n_slots // n_units
    t0 = i * tq
    t_row = t0 + lax.broadcasted_iota(jnp.int32, (1, tq), 1)

    zero = jnp.zeros((HEAD_DIM, tq), BF16)
    for g in range(NSA_GROUPS):
        for r in range(NSA_REP):
            hs = g * NSA_REP + r
            rows = q_ref[0, r * KV_WIDTH + g * HEAD_DIM:r * KV_WIDTH + (g + 1) * HEAD_DIM, qs]
            lhs_scr[:, hs * tq:(hs + 1) * tq] = jnp.concatenate([rows, zero] if g == 0 else [zero, rows],
                                                                axis=0)
    gsig = _sigmoid(gate_ref[0, :, qs])

    def unit_queries(unit):
        return lhs_scr[:, unit * SCORE_LANES:(unit + 1) * SCORE_LANES]

    def scores_into(unit, k_tile):
        tk = k_tile.shape[0]
        s_scr[unit][0:tk, :] = _dot(k_tile, unit_queries(unit))

    def sel_keys(kt):
        return ks_ref[0, pl.ds(pl.multiple_of(kt * NSA_TK, NSA_TK), NSA_TK), :]

    def win_keys(kt):
        return kw_ref[0, pl.ds(pl.multiple_of(kt * WIN_TK, WIN_TK), WIN_TK), :]

    n_lane = tq // LANES

    def run_tile(v_tile, plan_of_group, next_keys):
        ones_rows = _ones_rows(v_tile.shape[1])
        plans = [plan_of_group(g) for g in range(NSA_GROUPS)]
        for unit in range(n_units):
            g = (unit * per_unit) // NSA_REP
            sub = v_tile.shape[1] // len(plans[g])
            v_aug = _with_sum_row(v_tile[g * HEAD_DIM:(g + 1) * HEAD_DIM, :], ones_rows)
            for hh in range(per_unit):
                def load(a, h, hh=hh, unit=unit, sub=sub):
                    return s_scr[unit][a * sub:(a + 1) * sub, hh * tq + h * LANES:hh * tq + (h + 1) * LANES]

                _slot_update(load, plans[g], m_scr, acc_scr, unit * per_unit + hh, v_aug)
            if next_keys is not None:
                scores_into(unit, next_keys())

    def fold_branch(branch):
        for hs in range(n_slots):
            o_scr[hs] = o_scr[hs] + _normalised(acc_scr[hs], gsig[3 * hs + branch:3 * hs + branch + 1, :])

    blocks_per_tile = NSA_TK // SEL_BLOCK
    first_win = jnp.maximum((t0 - (WINDOW - 1)) // WIN_TK, 0)

    def sel_row(kt, g, a, h):
        return sel_scr[g, pl.ds(kt * blocks_per_tile + a, 1), :][:, h * LANES:(h + 1) * LANES]

    def sel_plan(kt, g):
        return [[('rows', sel_row(kt, g, a, h)) for h in range(n_lane)] for a in range(blocks_per_tile)]

    def causal(k_off, q_off):
        return k_off <= q_off

    def sel_body(kt, carry):
        run_tile(vs_ref[0, kt], lambda g: sel_plan(kt, g), lambda: sel_keys(kt + 1))
        return carry

    def win_tile(kt, visible, next_keys):
        run_tile(vw_ref[0, kt], lambda g: _visibility_plan(blocks_per_tile, SEL_BLOCK, n_lane, visible),
                 next_keys)

    for unit in range(n_units):
        c_scr[unit][...] = _dot(kc_ref[0], unit_queries(unit))
    for unit in range(n_units):
        scores_into(unit, win_keys(first_win))

    c_idx = lax.broadcasted_iota(jnp.int32, (n_cmp, tq), 0)
    valid_c = (c_idx * CMP_STRIDE + (CMP_LEN - 1) <= t_row) & (c_idx < n_cmp - 1)
    bias_c = jnp.where(valid_c, 0.0, NEG_BIG)
    p_sum = []
    for hs in range(n_slots):
        g = hs // NSA_REP
        unit, hh = divmod(hs, per_unit)
        x = c_scr[unit][:, hh * tq:(hh + 1) * tq] + bias_c
        m = jnp.max(x, axis=0, keepdims=True)
        e = jnp.where(valid_c, jnp.exp2(x - m), 0.0)
        p = e / jnp.maximum(jnp.sum(e, axis=0, keepdims=True), 1e-30)
        o_c = _dot(vct_ref[0, g * HEAD_DIM:(g + 1) * HEAD_DIM, :], p.astype(BF16))
        o_scr[hs] = o_c * gsig[3 * hs:3 * hs + 1, :]
        if hs % NSA_REP == 0:
            p_sum.append(p)
        else:
            p_sum[g] = p_sum[g] + p

    imp = []
    for g in range(NSA_GROUPS):
        hi, lo = _split_bf16(p_sum[g])
        imp.append(_dot(ovl_ref[...], hi) + _dot(ovl_ref[...], lo))

    _reset_state(m_scr, acc_scr)

    @pl.when(i >= 2)
    def _():
        win_tile(i - 2, lambda k_off, q_off: q_off < k_off, lambda: win_keys(i - 1))

    @pl.when(i >= 1)
    def _():
        win_tile(i - 1, lambda k_off, q_off: True, lambda: win_keys(i))

    cur = t_row // SEL_BLOCK
    j_idx = lax.broadcasted_iota(jnp.int32, (SEL_BLOCKS_MAX, tq), 0)
    forced = (j_idx == 0) | (j_idx == cur) | (j_idx == cur - 1)
    excluded = forced | (j_idx > cur)
    for g in range(NSA_GROUPS):
        picked = _pick_top_rows(jnp.where(excluded, -jnp.inf, imp[g]), SEL_TOPN - 3)
        sel_scr[g] = jnp.where(forced | (picked > 0.5), 0.0, NEG_BIG)

    win_tile(i, causal, lambda: sel_keys(0))
    fold_branch(2)

    last_sel = (t0 + tq - 1) // NSA_TK
    _reset_state(m_scr, acc_scr)
    lax.fori_loop(0, last_sel, sel_body, 0)
    run_tile(vs_ref[0, last_sel],
             lambda g: _visibility_plan(blocks_per_tile, SEL_BLOCK, n_lane, causal,
                                        lambda a, h: sel_row(last_sel, g, a, h)),
             None)
    fold_branch(1)

    for r in range(NSA_REP):
        out_ref[0, r * KV_WIDTH:r * KV_WIDTH + HEAD_DIM, qs] = o_scr[r].astype(BF16)
        out_ref[0, r * KV_WIDTH + HEAD_DIM:(r + 1) * KV_WIDTH, qs] = o_scr[NSA_REP + r].astype(BF16)


def _nsa_call(qn, kc, vct, ks, vs, kw, vw, gn, ovl):
    b, _, s = qn.shape
    tq = NSA_TQ
    n_slots = NSA_GROUPS * NSA_REP
    n_units = n_slots * tq // SCORE_LANES
    tstep = tq * NSA_STEP_BLOCKS
    assert s % tstep == 0
    assert s // SEL_BLOCK <= SEL_BLOCKS_MAX and tq == WIN_TK == NSA_TK and WINDOW == 2 * WIN_TK
    seq = lambda a: pl.BlockSpec((1,) + a.shape[1:], lambda bi, i: (bi,) + (0,) * (a.ndim - 1))
    full = lambda a: pl.BlockSpec(a.shape, lambda bi, i: (0,) * a.ndim)
    return pl.pallas_call(
        _nsa_kernel,
        grid=(b, s // tstep),
        in_specs=[pl.BlockSpec((1, Q_WIDTH, tstep), lambda bi, i: (bi, 0, i)),
                  seq(kc), seq(vct), seq(ks), seq(vs), seq(kw), seq(vw),
                  pl.BlockSpec((1, GATE_ROWS, tstep), lambda bi, i: (bi, 0, i)), full(ovl)],
        out_specs=pl.BlockSpec((1, Q_WIDTH, tstep), lambda bi, i: (bi, 0, i)),
        out_shape=jax.ShapeDtypeStruct((b, Q_WIDTH, s), BF16),
        scratch_shapes=[pltpu.VMEM((NSA_GROUPS, SEL_BLOCKS_MAX, tq), F32),
                        pltpu.VMEM((KV_WIDTH, n_slots * tq), BF16),
                        pltpu.VMEM((n_slots, SUBLANES, tq), F32),
                        pltpu.VMEM((n_slots, HEAD_DIM + SUM_ROWS, tq), F32),
                        pltpu.VMEM((n_slots, HEAD_DIM, tq), F32)]
        + [pltpu.VMEM((max(NSA_TK, WIN_TK), SCORE_LANES), F32)] * n_units
        + [pltpu.VMEM((kc.shape[1], SCORE_LANES), F32)] * n_units,
        compiler_params=pltpu.CompilerParams(dimension_semantics=("parallel", "parallel"),
                                             vmem_limit_bytes=VMEM_LIMIT),
    )(qn, kc, vct, ks, vs, kw, vw, gn, ovl)


def _moba_kernel(q_ref, k_ref, v_ref, out_ref, km_scr, sel_scr, lhs_scr, m_scr, acc_scr, *s_scr):
    bs = MOBA_BLOCK
    n_blocks = k_ref.shape[1] // bs
    n_slots = len(s_scr)
    i = pl.program_id(2)

    @pl.when(i == 0)
    def _():
        km_scr[...] = jnp.zeros(km_scr.shape, F32)
        for n in range(n_blocks):
            km_scr[n:n + 1, :] = jnp.mean(k_ref[0, n * bs:(n + 1) * bs, :].astype(F32), axis=0,
                                          keepdims=True)

    def pair_lanes(hs):
        return slice((hs // 2) * LANES, (hs // 2 + 1) * LANES)

    zero = jnp.zeros((HEAD_DIM, bs), BF16)
    n_idx = lax.broadcasted_iota(jnp.int32, (MOBA_BLOCKS_MAX, bs), 0)
    for hs in range(n_slots):
        rows = q_ref[0, hs * HEAD_DIM:(hs + 1) * HEAD_DIM, :]
        lhs_scr[hs] = jnp.concatenate([rows, zero] if hs % 2 == 0 else [zero, rows], axis=0)

    def scores_into(hs, n):
        k_tile = k_ref[0, pl.ds(pl.multiple_of(n * bs, bs), bs), pair_lanes(hs)]
        s_scr[hs][...] = _dot(k_tile, lhs_scr[hs])

    block_scores = []
    for hs in range(n_slots):
        km_hi, km_lo = _split_bf16(km_scr[:, pair_lanes(hs)])
        block_scores.append(_dot(km_hi, lhs_scr[hs]) + _dot(km_lo, lhs_scr[hs]))
    for hs in range(n_slots):
        scores_into(hs, i)

    ones_rows = _ones_rows(bs)
    n_lane = bs // LANES

    def run_block(n, plan_of_slot, next_block):
        for hs in range(n_slots):
            plan = plan_of_slot(hs)
            sub = bs // len(plan)
            v_aug = _with_sum_row(v_ref[0, n, hs * HEAD_DIM:(hs + 1) * HEAD_DIM, :], ones_rows)

            def load(a, h, hs=hs, sub=sub):
                return s_scr[hs][a * sub:(a + 1) * sub, h * LANES:(h + 1) * LANES]

            _slot_update(load, plan, m_scr, acc_scr, hs, v_aug)
            if next_block is not None:
                scores_into(hs, next_block)

    _reset_state(m_scr, acc_scr)
    for hs in range(n_slots):
        sc = jnp.where(n_idx < i, block_scores[hs], -jnp.inf)
        sel_scr[hs] = jnp.where((_rank_rows(sc, MOBA_BLOCKS_MAX) < MOBA_TOPK) & (n_idx < i), 0.0, NEG_BIG)
    own_plan = _visibility_plan(bs // OWN_SUB, OWN_SUB, n_lane, lambda k_off, q_off: k_off <= q_off)
    run_block(i, lambda hs: own_plan, i * 0)

    def past_plan(n):
        return lambda hs: [[('rows', sel_scr[hs, pl.ds(n, 1), :][:, h * LANES:(h + 1) * LANES])
                            for h in range(n_lane)]]

    def past_body(n, carry):
        run_block(n, past_plan(n), n + 1)
        return carry

    lax.fori_loop(0, i - 1, past_body, 0)

    @pl.when(i >= 1)
    def _():
        run_block(i - 1, past_plan(i - 1), None)

    for hs in range(n_slots):
        out_ref[0, hs * HEAD_DIM:(hs + 1) * HEAD_DIM, :] = _normalised(acc_scr[hs]).astype(BF16)


def _moba_call(qm, km, vm):
    b, w, s = qm.shape
    bs = MOBA_BLOCK
    nb = s // bs
    wb = MOBA_STEP_HEADS * HEAD_DIM
    assert nb <= MOBA_BLOCKS_MAX and w % wb == 0
    return pl.pallas_call(
        _moba_kernel,
        grid=(b, w // wb, nb),
        in_specs=[pl.BlockSpec((1, wb, bs), lambda bi, p, i: (bi, p, i)),
                  pl.BlockSpec((1, s, wb), lambda bi, p, i: (bi, 0, p)),
                  pl.BlockSpec((1, nb, wb, bs), lambda bi, p, i: (bi, 0, p, 0))],
        out_specs=pl.BlockSpec((1, wb, bs), lambda bi, p, i: (bi, p, i)),
        out_shape=jax.ShapeDtypeStruct((b, w, s), BF16),
        scratch_shapes=[pltpu.VMEM((MOBA_BLOCKS_MAX, wb), F32),
                        pltpu.VMEM((MOBA_STEP_HEADS, MOBA_BLOCKS_MAX, bs), F32),
                        pltpu.VMEM((MOBA_STEP_HEADS, LANES, bs), BF16),
                        pltpu.VMEM((MOBA_STEP_HEADS, SUBLANES, bs), F32),
                        pltpu.VMEM((MOBA_STEP_HEADS, HEAD_DIM + SUM_ROWS, bs), F32)]
        + [pltpu.VMEM((bs, bs), F32)] * MOBA_STEP_HEADS,
        compiler_params=pltpu.CompilerParams(
            dimension_semantics=("parallel", "parallel", "arbitrary"), vmem_limit_bytes=VMEM_LIMIT),
    )(qm, km, vm)


def _post_kernel(x_ref, yn_ref, ym_ref, p_ref, gmix_ref, wg_ref, wun_ref, wum_ref, wo_ref,
                 gffn_ref, wfi_ref, wfo_ref, gple_ref, wpg_ref, wpp_ref, out_ref):
    half = x_ref.shape[0] // 2
    halves = [slice(r * half, (r + 1) * half) for r in range(2)]

    xs = []
    for rows in halves:
        x = x_ref[rows, :]
        h = _rms_rows(x, gmix_ref[...]).astype(BF16)
        ga = _sigmoid(_dot(h, wg_ref[:, :D_MODEL]))
        gb = _sigmoid(_dot(h, wg_ref[:, D_MODEL:]))
        merged = ga * _dot_tn(yn_ref[0, :, rows], wun_ref[...]) + gb * _dot_tn(ym_ref[0, :, rows], wum_ref[...])
        xs.append(x + _dot(merged.astype(BF16), wo_ref[...]))

    ys = []
    for x in xs:
        h = _rms_rows(x, gffn_ref[...]).astype(BF16)
        y = x
        for lo, hi in FFN_CHUNKS:
            gate = _dot(h, wfi_ref[:, lo:hi])
            up = _dot(h, wfi_ref[:, D_FF + lo:D_FF + hi])
            act = (gate * _sigmoid(gate) * up).astype(BF16)
            y = y + _dot(act, wfo_ref[lo:hi, :])
        ys.append(y)

    for rows, y in zip(halves, ys):
        h2 = _rms_rows(y, gple_ref[...]).astype(BF16)
        ple_gate = _sigmoid(_dot(h2, wpg_ref[...]))
        out_ref[rows, :] = y + ple_gate * _dot(p_ref[rows, :].astype(BF16), wpp_ref[...])


def _post_call(x2, yn, ym, p2, gmix, wg, wun, wum, wo, gffn, wfi, wfo, gple, wpg, wpp):
    t = x2.shape[0]
    tm = POST_TM
    nt = yn.shape[2] // tm
    row = lambda w_: pl.BlockSpec((tm, w_), lambda i: (i, 0))
    feat = pl.BlockSpec((1, Q_WIDTH, tm), lambda i: (i // nt, 0, i % nt))
    full = lambda a: pl.BlockSpec(a.shape, lambda i: (0, 0), pipeline_mode=pl.Buffered(1))
    weights = (gmix, wg, wun, wum, wo, gffn, wfi, wfo, gple, wpg, wpp)
    return pl.pallas_call(
        _post_kernel,
        grid=(t // tm,),
        in_specs=[row(D_MODEL), feat, feat, row(PLE_DIM)] + [full(w) for w in weights],
        out_specs=row(D_MODEL),
        out_shape=jax.ShapeDtypeStruct((t, D_MODEL), F32),
        compiler_params=pltpu.CompilerParams(dimension_semantics=("parallel",),
                                             vmem_limit_bytes=VMEM_LIMIT),
    )(x2, yn, ym, p2, *weights)


def _block_diag_ones(width):
    idx = np.arange(width) // HEAD_DIM
    return jnp.asarray(idx[:, None] == idx[None, :], dtype=BF16)


def _inv_freq():
    return ROPE_THETA ** (-jnp.arange(ROPE_HALF, dtype=F32) / ROPE_HALF)


def _inv_freq_lanes(width):
    inv_freq = _inv_freq()
    per_head = jnp.concatenate([inv_freq, inv_freq, jnp.zeros((HEAD_DIM - ROPE_DIM,), F32)])
    return jnp.tile(per_head, width // HEAD_DIM)[None, :]


def _split_w_in(w):
    parts = jnp.split(w, IN_CUTS, axis=-1)
    (q_n, kc, vc, ks, vs, kw, vw, gate_n, q_m, k_m, v_m, gate_a, gate_b) = parts
    d = w.shape[0]
    q_n = q_n.reshape(d, NSA_GROUPS, NSA_REP, HEAD_DIM).transpose(0, 2, 1, 3).reshape(d, -1)
    gate_n = jnp.pad(gate_n, ((0, 0), (0, GATE_ROWS - gate_n.shape[1])))
    w_tok = jnp.concatenate([kc, vc], axis=1)
    w_feat = jnp.concatenate([q_n, q_m, k_m, ks, kw, v_m, vs, vw, gate_n], axis=1).T
    w_gate = jnp.concatenate([gate_a, gate_b], axis=1)
    return w_tok.astype(BF16), w_feat.astype(BF16), w_gate.astype(BF16)


def _compress_weights(w1, w2, pe):
    eye = jnp.eye(NSA_GROUPS, dtype=F32)
    halves = []
    for part in (w1[:CMP_STRIDE * HEAD_DIM], w1[CMP_STRIDE * HEAD_DIM:]):
        p3 = part.reshape(CMP_STRIDE, HEAD_DIM, CMP_HIDDEN)
        halves.append(jnp.einsum('idh,gk->igdkh', p3, eye)
                      .reshape(CMP_STRIDE * NSA_GROUPS * HEAD_DIM, NSA_GROUPS * CMP_HIDDEN))
    w_big = jnp.concatenate(halves, axis=1).astype(BF16)
    w2_bd = jnp.einsum('hd,kg->khgd', w2, eye).reshape(NSA_GROUPS * CMP_HIDDEN,
                                                        NSA_GROUPS * HEAD_DIM).astype(BF16)
    pe_rows = [jnp.broadcast_to(pe[a:a + CMP_STRIDE, None, :], (CMP_STRIDE, NSA_GROUPS, HEAD_DIM))
               .reshape(1, -1) for a in (0, CMP_STRIDE)]
    return w_big, w2_bd, pe_rows


def _overlap_matrix_t(n_cmp):
    c = np.arange(n_cmp)
    j = np.arange(SEL_BLOCKS_MAX)
    start, end = c * CMP_STRIDE, c * CMP_STRIDE + CMP_LEN - 1
    ov = (start[None, :] <= j[:, None] * SEL_BLOCK + SEL_BLOCK - 1) & (end[None, :] >= j[:, None] * SEL_BLOCK)
    return jnp.asarray(ov, dtype=BF16)


def kernel(x, p, positions, g_mix, w_in, nsa_q_gain, nsa_kc_gain, nsa_ks_gain, nsa_kw_gain, nsa_pe_k, nsa_pe_v, nsa_ck_w1, nsa_ck_w2, nsa_cv_w1, nsa_cv_w2, moba_q_gain, moba_k_gain, w_up_nsa, w_up_moba, w_out, g_ffn, w_ffn_in, w_ffn_out, g_ple, w_ple_gate, w_ple_proj):
    b, s, d = x.shape
    depth = w_in.shape[0]
    assert s % PAD_MULT == 0 and s % PROJ_TM == 0 and d == D_MODEL and V_TILE == MOBA_BLOCK
    t = b * s
    n_sub = s // CMP_STRIDE

    assert ROPE_HALF == SUBLANES
    posr = positions.reshape(t // PROJ_TM, 1, PROJ_TM)
    pos_end = jnp.concatenate([positions[:, CMP_LEN - 1::CMP_STRIDE], positions[:, -1:]], axis=1)[:, :, None]
    invf128 = _inv_freq_lanes(LANES)
    invf8 = _inv_freq()[:, None]
    bd128 = _block_diag_ones(LANES)
    ovl_t = _overlap_matrix_t(n_sub)
    tile = lambda g, n: jnp.tile(g, n)
    xi = x.reshape(t, d)

    for i in range(depth):
        w_tok, w_feat, w_gate = _split_w_in(w_in[i])
        g_feat = jnp.concatenate([
            jnp.concatenate([tile(nsa_q_gain[i], NSA_HEADS), tile(moba_q_gain[i], MOBA_HEADS)])
            * (ATTN_SCALE * LOG2E),
            tile(moba_k_gain[i], MOBA_HEADS), tile(nsa_ks_gain[i], NSA_GROUPS),
            tile(nsa_kw_gain[i], NSA_GROUPS)])[:, None]
        gmix = g_mix[i][None, :]
        (qn, qm, gn, vm, vs, vw, km, ks, kw, kc_raw, vc_raw) = _proj_call(
            xi, posr, gmix, w_tok, w_feat, g_feat, invf8, b, s)

        wk_big, w2k_bd, pe_k = _compress_weights(nsa_ck_w1[i], nsa_ck_w2[i], nsa_pe_k[i])
        wv_big, w2v_bd, pe_v = _compress_weights(nsa_cv_w1[i], nsa_cv_w2[i], nsa_pe_v[i])
        pe4 = jnp.concatenate(pe_k + pe_v, axis=0)
        kc, vct = _compress_call(kc_raw.reshape(b, s, KV_WIDTH), vc_raw.reshape(b, s, KV_WIDTH), pe4, wk_big, wv_big, w2k_bd, w2v_bd,
                                 tile(nsa_kc_gain[i], NSA_GROUPS)[None, :], pos_end,
                                 invf128, bd128)

        r3 = lambda a: a.reshape(b, s, a.shape[-1])
        y_nsa = _nsa_call(qn, kc, vct, r3(ks), vs.reshape(b, s // NSA_TK, KV_WIDTH, NSA_TK), r3(kw),
                          vw.reshape(b, s // WIN_TK, KV_WIDTH, WIN_TK), gn, ovl_t)
        y_moba = _moba_call(qm, r3(km), vm.reshape(b, s // MOBA_BLOCK, Q_WIDTH, MOBA_BLOCK))

        w_un = (w_up_nsa[i].reshape(NSA_GROUPS, NSA_REP, HEAD_DIM, d).transpose(1, 0, 2, 3)
                .reshape(NSA_HEADS * HEAD_DIM, d).astype(BF16))
        xi = _post_call(xi, y_nsa, y_moba, p[i].reshape(t, PLE_DIM), gmix, w_gate, w_un,
                        w_up_moba[i].astype(BF16), w_out[i].astype(BF16), g_ffn[i][None, :],
                        w_ffn_in[i].astype(BF16), w_ffn_out[i].astype(BF16), g_ple[i][None, :],
                        w_ple_gate[i].astype(BF16), w_ple_proj[i].astype(BF16))
    return xi.reshape(b, s, d)
```
